```python
import math
import jax, jax.numpy as jnp
from jax import lax
import numpy as np

D_MODEL = 1024
BATCH = 16
SEQ = 256
DEPTH = 2
DEC_BATCH = 8
DEC_SEQ = 2048
PAST_LEN = 256

GRID_W = 64
N_EVEN = (DEPTH + 1) // 2
N_ODD = DEPTH // 2
LRU_WIDTH = D_MODEL // 2
LRU_HEADS = 8
LRU_HEAD_DIM = LRU_WIDTH // LRU_HEADS
LRU_C = 8.0
CONV_W = 4
CONV_PAD_L = 2
S5_WIDTH = D_MODEL // 2
S5_CH = 16
S5_GROUPS = S5_WIDTH // S5_CH
S5_STATE = 64
MLA_HEADS = 8
QK_NOPE = 128
QK_ROPE = 64
V_DIM = 128
Q_LORA = 384
KV_LORA = 256
ROPE_THETA = 10000.0
ROPE_AXIS_FREQS = QK_ROPE // 4
Q_BLOCK = 128
D_FF = 2816
N_EXPERTS = 8
TOP_K = 2
D_FF_EXPERT = 2816
EPS = 1e-6

kernel_name = 'hybrid_lru_s5_mla_diffusion_step'


def rms_norm(x, g):
    xf = x.astype(jnp.float32)
    y = xf * lax.rsqrt(jnp.mean(xf * xf, axis=-1, keepdims=True) + EPS)
    return (y * g.astype(jnp.float32)).astype(x.dtype)


def modulation(cond, w_mod, b_mod):
    m = (jax.nn.silu(cond) @ w_mod + b_mod)[:, None, :]
    return jnp.split(m, 6, axis=-1)


def adaln_pre(x, g, shift, scale):
    return (rms_norm(x, g) * (1.0 + scale) + shift).astype(x.dtype)


def residual_post(x, y, g, gate):
    return (x + gate * rms_norm(y, g)).astype(x.dtype)


def centred_dwconv(x, w, b):
    y = lax.conv_general_dilated(x, w[:, None, :].astype(x.dtype), window_strides=(1,),
                                 padding=[(CONV_PAD_L, CONV_W - 1 - CONV_PAD_L)],
                                 dimension_numbers=('NWC', 'WIO', 'NWC'),
                                 feature_group_count=x.shape[-1])
    return y + b


def linear_scan(a, b, reverse):
    def combine(l, r):
        a_l, b_l = l
        a_r, b_r = r
        return a_r * a_l, a_r * b_l + b_r
    _, h = lax.associative_scan(combine, (a, b), reverse=reverse, axis=1)
    return h


def rg_lru(x, h0, wa, ba, wx, bx, lam, reverse):
    B_, L, C = x.shape
    f32 = jnp.float32
    xh = x.reshape(B_, L, LRU_HEADS, LRU_HEAD_DIM)
    r = jax.nn.sigmoid(jnp.einsum('blhi,hij->blhj', xh, wa.astype(f32)).reshape(B_, L, C) + ba.astype(f32))
    i = jax.nn.sigmoid(jnp.einsum('blhi,hij->blhj', xh, wx.astype(f32)).reshape(B_, L, C) + bx.astype(f32))
    log_a = LRU_C * r * jax.nn.log_sigmoid(lam.astype(f32))
    a = jnp.exp(log_a)
    b = jnp.sqrt(-jnp.expm1(2.0 * log_a)) * (i * x)
    start = -1 if reverse else 0
    b = b.at[:, start].add(a[:, start] * h0.astype(f32))
    h = linear_scan(a, b, reverse)
    return h, h[:, 0 if reverse else -1]


def s5_ssm(u, h0, a_re, a_im, log_dt, b_re, b_im, c_re, c_im, reverse):
    f = lambda t: t.astype(jnp.float32)
    A = lax.complex(f(a_re), f(a_im))
    dt = jnp.exp(f(log_dt))[:, None]
    a_bar = jnp.exp(A * dt)
    b_bar = ((a_bar - 1.0) / A)[..., None] * lax.complex(f(b_re), f(b_im))
    bu = jnp.einsum('blgc,gpc->blgp', u.astype(jnp.complex64), b_bar)
    start = -1 if reverse else 0
    bu = bu.at[:, start].add(a_bar * h0)
    h = linear_scan(jnp.broadcast_to(a_bar, bu.shape), bu, reverse)
    y = jnp.einsum('blgp,gcp->blgc', h, lax.complex(f(c_re), f(c_im))).real
    return y, h[:, 0 if reverse else -1]


def ab_mixer(h, lru_h0, s5_h0_re, s5_h0_im, w_in, conv_w, conv_b, lru_wa, lru_ba, lru_wx, lru_bx,
             lru_lambda, s5_a_re, s5_a_im, s5_log_dt, s5_b_re, s5_b_im, s5_c_re, s5_c_im, s5_d,
             s5_w_glu, s5_b_glu, w_out):
    B_, L, _ = h.shape
    f32 = jnp.float32
    xa, ga, xb = jnp.split(h @ w_in, [LRU_WIDTH, 2 * LRU_WIDTH], axis=-1)
    xa = centred_dwconv(xa, conv_w, conv_b).astype(f32)
    ha_f, la_f = rg_lru(xa, lru_h0[:, 0], lru_wa[0], lru_ba[0], lru_wx[0], lru_bx[0], lru_lambda[0], False)
    ha_b, la_b = rg_lru(xa, lru_h0[:, 1], lru_wa[1], lru_ba[1], lru_wx[1], lru_bx[1], lru_lambda[1], True)
    ya = (ha_f + ha_b) * jax.nn.gelu(ga.astype(f32))
    xb = xb.astype(f32)
    u = xb.reshape(B_, L, S5_GROUPS, S5_CH)
    h0 = lax.complex(s5_h0_re.astype(f32), s5_h0_im.astype(f32))
    yf, sf = s5_ssm(u, h0[:, 0], s5_a_re[0], s5_a_im[0], s5_log_dt[0], s5_b_re[0], s5_b_im[0],
                    s5_c_re[0], s5_c_im[0], False)
    yr, sr = s5_ssm(u, h0[:, 1], s5_a_re[1], s5_a_im[1], s5_log_dt[1], s5_b_re[1], s5_b_im[1],
                    s5_c_re[1], s5_c_im[1], True)
    yb = jax.nn.gelu((yf + yr).reshape(B_, L, S5_WIDTH) + s5_d.astype(f32) * xb)
    yb = yb * jax.nn.sigmoid(yb @ s5_w_glu.astype(f32) + s5_b_glu.astype(f32))
    out = jnp.concatenate([ya, yb], axis=-1) @ w_out.astype(f32)
    s5_state = jnp.stack([sf, sr], axis=1)
    return (out.astype(h.dtype), jnp.stack([la_f, la_b], axis=1).astype(h.dtype),
            s5_state.real.astype(h.dtype), s5_state.imag.astype(h.dtype))


def axial_rope(rows):
    row = jnp.repeat(jnp.arange(rows, dtype=jnp.float32), GRID_W)
    col = jnp.tile(jnp.arange(GRID_W, dtype=jnp.float32), rows)
    inv = ROPE_THETA ** (-jnp.arange(ROPE_AXIS_FREQS, dtype=jnp.float32) / ROPE_AXIS_FREQS)
    ang = jnp.concatenate([row[:, None] * inv, col[:, None] * inv], axis=-1)
    return jnp.cos(ang), jnp.sin(ang)


def apply_rope(x, cos, sin):
    xf = x.astype(jnp.float32)
    x1, x2 = xf[..., 0::2], xf[..., 1::2]
    y = jnp.stack([x1 * cos - x2 * sin, x1 * sin + x2 * cos], axis=-1).reshape(x.shape)
    return y.astype(x.dtype)


def mla_down(h, w_in, g_q, g_kv):
    cq, ckv, kr = jnp.split(h @ w_in, [Q_LORA, Q_LORA + KV_LORA], axis=-1)
    return rms_norm(cq, g_q), rms_norm(ckv, g_kv), kr


def mla_queries(cq, w_uq):
    B_, L, _ = cq.shape
    return (cq @ w_uq).reshape(B_, L, MLA_HEADS, QK_NOPE + QK_ROPE)


def mla_keys_values(ckv, kr, w_ukv):
    B_, L, _ = ckv.shape
    kv = (ckv @ w_ukv).reshape(B_, L, MLA_HEADS, QK_NOPE + V_DIM)
    k_nope, v = jnp.split(kv, [QK_NOPE], axis=-1)
    k_rope = jnp.broadcast_to(kr[:, :, None, :], (B_, L, MLA_HEADS, QK_ROPE)).astype(k_nope.dtype)
    return jnp.concatenate([k_nope, k_rope], axis=-1), v


def block_attention(q, k, v):
    B_, Lq, H, Dk = q.shape
    nb = Lq // Q_BLOCK
    qb = q.reshape(B_, nb, Q_BLOCK, H, Dk).transpose(1, 0, 2, 3, 4)
    scale = Dk ** -0.5

    def one_block(q_blk):
        s = jnp.einsum('bqhd,bkhd->bhqk', q_blk, k).astype(jnp.float32) * scale
        p = jax.nn.softmax(s, axis=-1).astype(v.dtype)
        return jnp.einsum('bhqk,bkhd->bqhd', p, v)

    o = lax.map(one_block, qb)
    return o.transpose(1, 0, 2, 3, 4).reshape(B_, Lq, H, v.shape[-1])


def mla_output(o, w_out):
    B_, L = o.shape[:2]
    return o.reshape(B_, L, MLA_HEADS * V_DIM) @ w_out


def swiglu(h, w_gate_up, w_down):
    g, u = jnp.split(h @ w_gate_up, 2, axis=-1)
    return (jax.nn.silu(g) * u) @ w_down


def moe_swiglu(h, w_router, w_gate_up, w_down):
    logits = (h @ w_router).astype(jnp.float32)
    top_v, top_i = lax.top_k(logits, TOP_K)
    weights = jax.nn.softmax(top_v, axis=-1)
    combine = jnp.einsum('blk,blke->ble', weights, jax.nn.one_hot(top_i, N_EXPERTS, dtype=jnp.float32))
    out = jnp.zeros(h.shape, jnp.float32)
    for e in range(N_EXPERTS):
        out = out + combine[..., e:e + 1] * swiglu(h, w_gate_up[e], w_down[e]).astype(jnp.float32)
    return out.astype(h.dtype)


def setup_inputs(seed: int = 0) -> dict:
    key = jax.random.key(seed)
    ks = iter(jax.random.split(key, 64))
    f32 = jnp.float32

    def nrm(shape, scale=1.0):
        return scale * jax.random.normal(next(ks), shape, f32)

    def dense(shape, fan_in):
        return nrm(shape, fan_in ** -0.5)

    D = D_MODEL
    lam_u = jax.random.uniform(next(ks), (N_EVEN, 2, LRU_WIDTH), f32, 0.9, 0.999)
    lam_s = lam_u ** (1.0 / LRU_C)
    n = jnp.arange(S5_STATE, dtype=f32)
    s5_shape = (N_EVEN, 2, S5_GROUPS, S5_STATE)
    log_dt = math.log(0.001) + jax.random.uniform(next(ks), (N_EVEN, 2, S5_GROUPS), f32) * (
        math.log(0.1) - math.log(0.001))
    return {
        'x_prompt': nrm((BATCH, SEQ, D)),
        'x_sample': nrm((DEC_BATCH, DEC_SEQ, D)),
        'c': nrm((DEC_BATCH, D)),
        'state_lru': nrm((DEC_BATCH, N_EVEN, 2, LRU_WIDTH)),
        'state_s5_re': nrm((DEC_BATCH, N_EVEN, 2, S5_GROUPS, S5_STATE)),
        'state_s5_im': nrm((DEC_BATCH, N_EVEN, 2, S5_GROUPS, S5_STATE)),
        'cache_kv_latent': nrm((DEC_BATCH, N_ODD, PAST_LEN, KV_LORA)),
        'cache_k_rope': nrm((DEC_BATCH, N_ODD, PAST_LEN, QK_ROPE)),
        'c_ctx': nrm((D,)),
        'w_mod': dense((DEPTH, D, 6 * D), D) * 0.5,
        'b_mod': nrm((DEPTH, 6 * D), 0.01),
        'norm_gains': 1.0 + nrm((DEPTH, 4, D), 0.02),
        'ab_w_in': dense((N_EVEN, D, 2 * LRU_WIDTH + S5_WIDTH), D),
        'ab_conv_w': dense((N_EVEN, CONV_W, LRU_WIDTH), CONV_W),
        'ab_conv_b': nrm((N_EVEN, LRU_WIDTH), 0.01),
        'lru_wa': dense((N_EVEN, 2, LRU_HEADS, LRU_HEAD_DIM, LRU_HEAD_DIM), LRU_HEAD_DIM),
        'lru_ba': nrm((N_EVEN, 2, LRU_WIDTH), 0.01),
        'lru_wx': dense((N_EVEN, 2, LRU_HEADS, LRU_HEAD_DIM, LRU_HEAD_DIM), LRU_HEAD_DIM),
        'lru_bx': nrm((N_EVEN, 2, LRU_WIDTH), 0.01),
        'lru_lambda': jnp.log(lam_s) - jnp.log1p(-lam_s),
        's5_a_re': -0.5 + nrm(s5_shape, 0.01),
        's5_a_im': jnp.pi * n + nrm(s5_shape, 0.01),
        's5_log_dt': log_dt,
        's5_b_re': dense((N_EVEN, 2, S5_GROUPS, S5_STATE, S5_CH), 2 * S5_CH),
        's5_b_im': dense((N_EVEN, 2, S5_GROUPS, S5_STATE, S5_CH), 2 * S5_CH),
        's5_c_re': dense((N_EVEN, 2, S5_GROUPS, S5_CH, S5_STATE), 2 * S5_STATE),
        's5_c_im': dense((N_EVEN, 2, S5_GROUPS, S5_CH, S5_STATE), 2 * S5_STATE),
        's5_d': nrm((N_EVEN, S5_WIDTH)),
        's5_w_glu': dense((N_EVEN, S5_WIDTH, S5_WIDTH), S5_WIDTH),
        's5_b_glu': nrm((N_EVEN, S5_WIDTH), 0.01),
        'ab_w_out': dense((N_EVEN, LRU_WIDTH + S5_WIDTH, D), LRU_WIDTH + S5_WIDTH),
        'ffn_w_gate_up': dense((N_EVEN, D, 2 * D_FF), D),
        'ffn_w_down': dense((N_EVEN, D_FF, D), D_FF),
        'mla_w_in': dense((N_ODD, D, Q_LORA + KV_LORA + QK_ROPE), D),
        'mla_g_q': 1.0 + nrm((N_ODD, Q_LORA), 0.02),
        'mla_g_kv': 1.0 + nrm((N_ODD, KV_LORA), 0.02),
        'mla_w_uq': dense((N_ODD, Q_LORA, MLA_HEADS * (QK_NOPE + QK_ROPE)), Q_LORA),
        'mla_w_ukv': dense((N_ODD, KV_LORA, MLA_HEADS * (QK_NOPE + V_DIM)), KV_LORA),
        'mla_w_out': dense((N_ODD, MLA_HEADS * V_DIM, D), MLA_HEADS * V_DIM),
        'moe_w_router': dense((N_ODD, D, N_EXPERTS), D),
        'moe_w_gate_up': dense((N_ODD, N_EXPERTS, D, 2 * D_FF_EXPERT), D),
        'moe_w_down': dense((N_ODD, N_EXPERTS, D_FF_EXPERT, D), D_FF_EXPERT),
    }


def reference(x_prompt, x_sample, c, state_lru, state_s5_re, state_s5_im, cache_kv_latent, cache_k_rope,
              c_ctx, w_mod, b_mod, norm_gains, ab_w_in, ab_conv_w, ab_conv_b, lru_wa, lru_ba, lru_wx, lru_bx,
              lru_lambda, s5_a_re, s5_a_im, s5_log_dt, s5_b_re, s5_b_im, s5_c_re, s5_c_im, s5_d, s5_w_glu,
              s5_b_glu, ab_w_out, ffn_w_gate_up, ffn_w_down, mla_w_in, mla_g_q, mla_g_kv, mla_w_uq, mla_w_ukv,
              mla_w_out, moe_w_router, moe_w_gate_up, moe_w_down):
    xp, xs = x_prompt, x_sample
    Bp = xp.shape[0]
    rows = xs.shape[1] // GRID_W
    cos, sin = axial_rope(rows)
    lru_list, s5re_list, s5im_list, kv_list, kr_list = [], [], [], [], []
    for layer in range(DEPTH):
        j = layer // 2
        ng = norm_gains[layer]
        p_sh1, p_sc1, p_g1, p_sh2, p_sc2, p_g2 = modulation(c_ctx[None, :], w_mod[layer], b_mod[layer])
        s_sh1, s_sc1, s_g1, s_sh2, s_sc2, s_g2 = modulation(c, w_mod[layer], b_mod[layer])
        hp = adaln_pre(xp, ng[0], p_sh1, p_sc1)
        hs = adaln_pre(xs, ng[0], s_sh1, s_sc1)
        if layer % 2 == 0:
            ab_params = (ab_w_in[j], ab_conv_w[j], ab_conv_b[j], lru_wa[j], lru_ba[j], lru_wx[j], lru_bx[j],
                         lru_lambda[j], s5_a_re[j], s5_a_im[j], s5_log_dt[j], s5_b_re[j], s5_b_im[j],
                         s5_c_re[j], s5_c_im[j], s5_d[j], s5_w_glu[j], s5_b_glu[j], ab_w_out[j])
            zero_lru = jnp.zeros((Bp, 2, LRU_WIDTH), xp.dtype)
            zero_s5 = jnp.zeros((Bp, 2, S5_GROUPS, S5_STATE), xp.dtype)
            op, lru_p, s5re_p, s5im_p = ab_mixer(hp, zero_lru, zero_s5, zero_s5, *ab_params)
            os_, _, _, _ = ab_mixer(hs, state_lru[:, j], state_s5_re[:, j], state_s5_im[:, j], *ab_params)
            lru_list.append(lru_p)
            s5re_list.append(s5re_p)
            s5im_list.append(s5im_p)
        else:
            cq_p, ckv_p, kr_p = mla_down(hp, mla_w_in[j], mla_g_q[j], mla_g_kv[j])
            q_p = mla_queries(cq_p, mla_w_uq[j])
            k_p, v_p = mla_keys_values(ckv_p, kr_p, mla_w_ukv[j])
            op = mla_output(block_attention(q_p, k_p, v_p), mla_w_out[j])
            cq_s, ckv_s, kr_s = mla_down(hs, mla_w_in[j], mla_g_q[j], mla_g_kv[j])
            q_s = mla_queries(cq_s, mla_w_uq[j])
            q_s = jnp.concatenate([q_s[..., :QK_NOPE],
                                   apply_rope(q_s[..., QK_NOPE:], cos[None, :, None, :], sin[None, :, None, :])],
                                  axis=-1)
            kr_s = apply_rope(kr_s, cos[None], sin[None])
            k_c, v_c = mla_keys_values(cache_kv_latent[:, j], cache_k_rope[:, j], mla_w_ukv[j])
            k_l, v_l = mla_keys_values(ckv_s, kr_s, mla_w_ukv[j])
            k_all = jnp.concatenate([k_c.astype(k_l.dtype), k_l], axis=1)
            v_all = jnp.concatenate([v_c.astype(v_l.dtype), v_l], axis=1)
            os_ = mla_output(block_attention(q_s, k_all, v_all), mla_w_out[j])
            kv_list.append(ckv_p)
            kr_list.append(kr_p)
        xp = residual_post(xp, op, ng[1], p_g1)
        xs = residual_post(xs, os_, ng[1], s_g1)
        hp = adaln_pre(xp, ng[2], p_sh2, p_sc2)
        hs = adaln_pre(xs, ng[2], s_sh2, s_sc2)
        if layer % 2 == 0:
            fp = swiglu(hp, ffn_w_gate_up[j], ffn_w_down[j])
            fs = swiglu(hs, ffn_w_gate_up[j], ffn_w_down[j])
        else:
            fp = moe_swiglu(hp, moe_w_router[j], moe_w_gate_up[j], moe_w_down[j])
            fs = moe_swiglu(hs, moe_w_router[j], moe_w_gate_up[j], moe_w_down[j])
        xp = residual_post(xp, fp, ng[3], p_g2)
        xs = residual_post(xs, fs, ng[3], s_g2)
    new_state_lru = jnp.stack(lru_list, axis=1)
    new_state_s5_re = jnp.stack(s5re_list, axis=1)
    new_state_s5_im = jnp.stack(s5im_list, axis=1)
    new_cache_kv_latent = jnp.stack(kv_list, axis=1)
    new_cache_k_rope = jnp.stack(kr_list, axis=1)
    return (xp, xs, new_state_lru, new_state_s5_re, new_state_s5_im, new_cache_kv_latent, new_cache_k_rope)
```

```python
import functools
import math

import numpy as np
import jax
import jax.numpy as jnp
from jax import lax
from jax.experimental import pallas as pl
from jax.experimental.pallas import tpu as pltpu

F32 = jnp.float32
BF16 = jnp.bfloat16
I32 = jnp.int32

D = 1024
BATCH, SEQ = 16, 256
DEC_BATCH, DEC_SEQ = 8, 2048
PAST_LEN = 256
GRID_W = 64
LRU_W = 512
LRU_HEADS = 8
LRU_C = 8.0
CONV_W = 4
S5_W = 512
S5_CH = 16
S5_G = 32
S5_P = 64
S5_N = S5_G * S5_P
HEADS = 8
QK_NOPE, QK_ROPE, V_DIM = 128, 64, 128
Q_LORA, KV_LORA = 384, 256
ROPE_THETA = 10000.0
D_FF = 2816
N_EXP = 8
EPS = 1e-6

NP = BATCH * SEQ
NS = DEC_BATCH * DEC_SEQ
N = NP + NS
SUB = 8
LANE = 128
T_CHUNK = 32
R_CHUNK = T_CHUNK * SUB
FF_BLK = 256
N_FF = D_FF // FF_BLK
TM_E = 512
P_ROWS = 2 * N + N_EXP * TM_E
N_TILES = P_ROWS // TM_E
VMEM_LIMIT = 56 * 1024 * 1024

NT_DIMS = (((1,), (1,)), ((), ()))


def _cparams(sem):
    return pltpu.CompilerParams(dimension_semantics=sem, vmem_limit_bytes=VMEM_LIMIT)


def _dot(a, b):
    return jnp.dot(a, b, preferred_element_type=F32)


def _sigmoid(x):
    return 1.0 / (1.0 + jnp.exp(-x))


def _neg_expm1(z):
    series = -z * (1.0 + z / 2.0 * (1.0 + z / 3.0 * (1.0 + z / 4.0 * (1.0 + z / 5.0 * (1.0 + z / 6.0 * (1.0 + z / 7.0))))))
    return jnp.where(z > -0.1, series, 1.0 - jnp.exp(z))


def _gelu(x):
    return x * (0.5 * (1.0 + jnp.tanh(math.sqrt(2.0 / math.pi) * (x + 0.044715 * (x * x * x)))))


def _rms(x, g):
    ms = jnp.mean(x * x, axis=-1, keepdims=True)
    return x * lax.rsqrt(ms + EPS) * g


def _rows8(y, fn):
    r, c = y.shape
    return fn(y.reshape(r // SUB, SUB, c)).reshape(r, c)


def _adaln(x, g, scale, shift):
    return _rows8(_rms(x, g), lambda y: y * (1.0 + scale)[None] + shift[None])


def _gated(x, y, g, gate):
    return x + _rows8(_rms(y, g), lambda z: z * gate[None])


def _full(shape):
    nd = len(shape)
    return pl.BlockSpec(shape, lambda *_: (0,) * nd)


def _mod_spec(k, grp):
    return pl.BlockSpec((1, SUB, D), lambda i, *_: (grp(i), 0, k))


def _grp_tm(tm):
    return lambda i: (i * tm >= NP).astype(I32)


def _grp_bm(tm):
    return lambda i: jnp.where(i * tm < NP, 0, 1 + (i * tm - NP) // DEC_SEQ)


def _mod_kernel(c_ref, w_ref, b_ref, o_ref):
    c = c_ref[...]
    s = c * _sigmoid(c)
    o_ref[0] = _dot(s.astype(BF16), w_ref[0].astype(BF16)) + b_ref[0]


def _modulation(cond, w_mod, b_mod):
    depth = w_mod.shape[0]
    rows = cond.shape[0]
    return pl.pallas_call(
        _mod_kernel,
        grid=(depth, 6),
        in_specs=[_full((rows, D)),
                  pl.BlockSpec((1, D, D), lambda l, j: (l, 0, j)),
                  pl.BlockSpec((1, 1, D), lambda l, j: (l, 0, j))],
        out_specs=pl.BlockSpec((1, rows, D), lambda l, j: (l, 0, j)),
        out_shape=jax.ShapeDtypeStruct((depth, rows, 6 * D), F32),
        compiler_params=_cparams(("arbitrary", "arbitrary")),
        name="modulation",
    )(cond, w_mod, b_mod.reshape(depth, 1, 6 * D))


def _inproj_kernel(x_ref, g_ref, sh_ref, sc_ref, w_ref, o_ref):
    h = _adaln(x_ref[...], g_ref[...], sc_ref[0], sh_ref[0])
    o_ref[...] = _dot(h.astype(BF16), w_ref[...])


def _ab_inproj(x, gain, mods, w_in):
    tm = 512
    nout = w_in.shape[1]
    grp = _grp_tm(tm)
    return pl.pallas_call(
        _inproj_kernel,
        grid=(N // tm,),
        in_specs=[pl.BlockSpec((tm, D), lambda i: (i, 0)),
                  _full((1, D)),
                  _mod_spec(0, grp), _mod_spec(1, grp),
                  _full((D, nout))],
        out_specs=pl.BlockSpec((tm, nout), lambda i: (i, 0)),
        out_shape=jax.ShapeDtypeStruct((N, nout), F32),
        compiler_params=_cparams(("arbitrary",)),
        name="ab_inproj",
    )(x, gain, mods, mods, w_in)


def _scan_table(reverse):
    cols = []
    groups = [(0, SEQ // T_CHUNK, 0), (1, SEQ // T_CHUNK, SEQ // T_CHUNK),
              (2, DEC_SEQ // T_CHUNK, NP // R_CHUNK)]
    for g, nc, base in groups:
        order = range(nc - 1, -1, -1) if reverse else range(nc)
        for k, c in enumerate(order):
            cols.append((base + c, g, int(k == 0), int(c > 0), int(c < nc - 1)))
    return np.asarray(cols, np.int32).T.copy()


def _scan_kernel(tbl, xa_ref, xp_ref, xn_ref, xb_ref, h0l_ref, h0r_ref, h0i_ref,
                 cw_ref, cb_ref, wg_ref, bg_ref, lam_ref, bm_ref, ar_ref, ai_ref, cm_ref,
                 ha_ref, y_ref, ll_ref, lr_ref, li_ref,
                 ext, abuf, bbuf, hre, him, hl, sre, sim, *, reverse):
    s = pl.program_id(0)

    @pl.when(tbl[2, s] == 1)
    def _():
        hl[...] = h0l_ref[0]
        sre[...] = h0r_ref[0]
        sim[...] = h0i_ref[0]

    ext[0:2 * SUB] = jnp.where(tbl[3, s] == 1, xp_ref[...], 0.0)
    ext[2 * SUB:2 * SUB + R_CHUNK] = xa_ref[...]
    ext[2 * SUB + R_CHUNK:3 * SUB + R_CHUNK] = jnp.where(tbl[4, s] == 1, xn_ref[...], 0.0)
    xa = cb_ref[...] + cw_ref[0:1] * ext[0:R_CHUNK]
    for k in range(1, CONV_W):
        xa = xa + cw_ref[k:k + 1] * ext[k * SUB:k * SUB + R_CHUNK]

    gz = _dot(xa.astype(BF16), wg_ref[...]) + bg_ref[...]
    r = _sigmoid(gz[:, :LRU_W])
    i = _sigmoid(gz[:, LRU_W:])
    lam = lam_ref[...]
    log_sig = jnp.minimum(lam, 0.0) - jnp.log1p(jnp.exp(-jnp.abs(lam)))
    log_a = LRU_C * r * log_sig
    abuf[...] = jnp.exp(log_a)
    bbuf[...] = jnp.sqrt(_neg_expm1(2.0 * log_a)) * (i * xa)

    order = range(T_CHUNK - 1, -1, -1) if reverse else range(T_CHUNK)
    h = hl[...]
    for t in order:
        rows = slice(t * SUB, (t + 1) * SUB)
        h = abuf[rows] * h + bbuf[rows]
        ha_ref[rows, :] = h
    hl[...] = h
    ll_ref[0] = h

    ub = xb_ref[...].astype(BF16)
    half = S5_N // 2
    for j in range(2):
        bu = _dot(ub[:, j * 256:(j + 1) * 256], bm_ref[j])
        hre[:, j * half:(j + 1) * half] = bu[:, :half]
        him[:, j * half:(j + 1) * half] = bu[:, half:]
    cblk = 4 * LANE
    for cb in range(S5_N // cblk):
        cols = slice(cb * cblk, (cb + 1) * cblk)
        ar = jnp.broadcast_to(ar_ref[:, cols], (SUB, cblk))
        ai = jnp.broadcast_to(ai_ref[:, cols], (SUB, cblk))
        hr = sre[:, cols]
        hi = sim[:, cols]
        for t in order:
            rows = slice(t * SUB, (t + 1) * SUB)
            nr = ar * hr - ai * hi + hre[rows, cols]
            ni = ar * hi + ai * hr + him[rows, cols]
            hr, hi = nr, ni
            hre[rows, cols] = hr
            him[rows, cols] = hi
        sre[:, cols] = hr
        sim[:, cols] = hi
    lr_ref[0] = sre[...]
    li_ref[0] = sim[...]
    for j in range(2):
        hc = jnp.concatenate([hre[:, j * half:(j + 1) * half], him[:, j * half:(j + 1) * half]],
                             axis=1).astype(BF16)
        y_ref[:, j * 256:(j + 1) * 256] = _dot(hc, cm_ref[j])


def _ab_scan(xz, h0l, h0r, h0i, cw, cb, wg, bg, lam, bm, ar, ai, cm, reverse):
    tbl = jnp.asarray(_scan_table(reverse))
    n_steps = tbl.shape[1]
    blk = lambda s, t: t[0, s]
    grp = lambda s, t: t[1, s]
    state_spec = lambda w: pl.BlockSpec((1, SUB, w), lambda s, t: (grp(s, t), 0, 0))
    const = lambda shape: pl.BlockSpec(shape, lambda s, t: (0,) * len(shape))
    grid_spec = pltpu.PrefetchScalarGridSpec(
        num_scalar_prefetch=1,
        grid=(n_steps,),
        in_specs=[
            pl.BlockSpec((R_CHUNK, LRU_W), lambda s, t: (blk(s, t), 0)),
            pl.BlockSpec((2 * SUB, LRU_W), lambda s, t: (jnp.maximum(blk(s, t) * (T_CHUNK // 2) - 1, 0), 0)),
            pl.BlockSpec((SUB, LRU_W), lambda s, t: (jnp.minimum((blk(s, t) + 1) * T_CHUNK, N // SUB - 1), 0)),
            pl.BlockSpec((R_CHUNK, S5_W), lambda s, t: (blk(s, t), 2)),
            state_spec(LRU_W), state_spec(S5_N), state_spec(S5_N),
            const((CONV_W, LRU_W)), const((1, LRU_W)),
            const((LRU_W, 2 * LRU_W)), const((1, 2 * LRU_W)), const((1, LRU_W)),
            const((2, 256, S5_N)), const((1, S5_N)), const((1, S5_N)), const((2, S5_N, 256)),
        ],
        out_specs=[
            pl.BlockSpec((R_CHUNK, LRU_W), lambda s, t: (blk(s, t), 0)),
            pl.BlockSpec((R_CHUNK, S5_W), lambda s, t: (blk(s, t), 0)),
            state_spec(LRU_W), state_spec(S5_N), state_spec(S5_N),
        ],
        scratch_shapes=[
            pltpu.VMEM((R_CHUNK + 3 * SUB, LRU_W), F32),
            pltpu.VMEM((R_CHUNK, LRU_W), F32), pltpu.VMEM((R_CHUNK, LRU_W), F32),
            pltpu.VMEM((R_CHUNK, S5_N), F32), pltpu.VMEM((R_CHUNK, S5_N), F32),
            pltpu.VMEM((SUB, LRU_W), F32), pltpu.VMEM((SUB, S5_N), F32), pltpu.VMEM((SUB, S5_N), F32),
        ],
    )
    return pl.pallas_call(
        functools.partial(_scan_kernel, reverse=reverse),
        grid_spec=grid_spec,
        out_shape=[jax.ShapeDtypeStruct((N, LRU_W), F32), jax.ShapeDtypeStruct((N, S5_W), F32),
                   jax.ShapeDtypeStruct((3, SUB, LRU_W), F32),
                   jax.ShapeDtypeStruct((3, SUB, S5_N), F32), jax.ShapeDtypeStruct((3, SUB, S5_N), F32)],
        compiler_params=_cparams(("arbitrary",)),
        name="ab_scan_bwd" if reverse else "ab_scan_fwd",
    )(tbl, xz, xz, xz, xz, h0l, h0r, h0i, cw, cb, wg, bg, lam, bm, ar, ai, cm)


def _about_kernel(haf, hab, yf, yr, ga, xb, x_ref, gate, g1, d_ref, wglu, bglu, wout, o_ref):
    ya = (haf[...] + hab[...]) * _gelu(ga[...])
    yb0 = _gelu(yf[...] + yr[...] + d_ref[...] * xb[...])
    yb = yb0 * _sigmoid(_dot(yb0.astype(BF16), wglu[...]) + bglu[...])
    out = _dot(ya.astype(BF16), wout[0:LRU_W]) + _dot(yb.astype(BF16), wout[LRU_W:LRU_W + S5_W])
    o_ref[...] = _gated(x_ref[...], out, g1[...], gate[0])


def _ab_out(haf, hab, yf, yr, xz, x, mods, g1, s5_d, wglu, bglu, wout):
    tm = 512
    grp = _grp_tm(tm)
    half = lambda c: pl.BlockSpec((tm, LRU_W), lambda i: (i, c))
    return pl.pallas_call(
        _about_kernel,
        grid=(N // tm,),
        in_specs=[half(0), half(0), half(0), half(0), half(1), half(2),
                  pl.BlockSpec((tm, D), lambda i: (i, 0)),
                  _mod_spec(2, grp), _full((1, D)), _full((1, S5_W)),
                  _full((S5_W, S5_W)), _full((1, S5_W)), _full((LRU_W + S5_W, D))],
        out_specs=pl.BlockSpec((tm, D), lambda i: (i, 0)),
        out_shape=jax.ShapeDtypeStruct((N, D), F32),
        compiler_params=_cparams(("arbitrary",)),
        name="ab_out",
    )(haf, hab, yf, yr, xz, xz, x, mods, g1, s5_d, wglu, bglu, wout)


def _swiglu_step(h, wg, wu, wd):
    g = _dot(h, wg.astype(BF16))
    u = _dot(h, wu.astype(BF16))
    act = (g * _sigmoid(g)) * u
    return _dot(act.astype(BF16), wd.astype(BF16))


def _ffn_kernel(x_ref, sh, sc, gt, g2, g3, wg, wu, wd, o_ref, hbf, acc):
    j = pl.program_id(1)

    @pl.when(j == 0)
    def _():
        hbf[...] = _adaln(x_ref[...], g2[...], sc[0], sh[0]).astype(BF16)
        acc[...] = jnp.zeros_like(acc)

    acc[...] += _swiglu_step(hbf[...], wg[...], wu[...], wd[...])

    @pl.when(j == N_FF - 1)
    def _():
        o_ref[...] = _gated(x_ref[...], acc[...], g3[...], gt[0])


def _ffn(x, mods, g2, g3, w_gate_up, w_down):
    tm = 1024
    grp = _grp_tm(tm)
    return pl.pallas_call(
        _ffn_kernel,
        grid=(N // tm, N_FF),
        in_specs=[pl.BlockSpec((tm, D), lambda i, j: (i, 0)),
                  _mod_spec(3, grp), _mod_spec(4, grp), _mod_spec(5, grp),
                  _full((1, D)), _full((1, D)),
                  pl.BlockSpec((D, FF_BLK), lambda i, j: (0, j)),
                  pl.BlockSpec((D, FF_BLK), lambda i, j: (0, j + N_FF)),
                  pl.BlockSpec((FF_BLK, D), lambda i, j: (j, 0))],
        out_specs=pl.BlockSpec((tm, D), lambda i, j: (i, 0)),
        out_shape=jax.ShapeDtypeStruct((N, D), F32),
        scratch_shapes=[pltpu.VMEM((tm, D), BF16), pltpu.VMEM((tm, D), F32)],
        compiler_params=_cparams(("arbitrary", "arbitrary")),
        name="ffn",
    )(x, mods, mods, mods, g2, g3, w_gate_up, w_gate_up, w_down)


def _pair_swap(x):
    outs = []
    for c in range(x.shape[1] // LANE):
        xc = x[:, c * LANE:(c + 1) * LANE]
        even = lax.broadcasted_iota(I32, xc.shape, 1) % 2 == 0
        outs.append(jnp.where(even, pltpu.roll(xc, LANE - 1, 1), pltpu.roll(xc, 1, 1)))
    return outs[0] if len(outs) == 1 else jnp.concatenate(outs, axis=1)


def _mlaproj_kernel(x_ref, g0, sh, sc, w1, gq, gkv, wuq, wukv, cq_ref, sq_ref, ck_ref, sk_ref,
                    qn_ref, qr_ref, kn_ref, v_ref, kr2_ref, ckv_ref, krr_ref):
    h = _adaln(x_ref[...], g0[...], sc[0], sh[0])
    dn = _dot(h.astype(BF16), w1[...])
    cq = _rms(dn[:, :Q_LORA], gq[...])
    ckv = _rms(dn[:, Q_LORA:Q_LORA + KV_LORA], gkv[...])
    krp = dn[:, Q_LORA + KV_LORA:]
    ckv_ref[...] = ckv
    krr_ref[...] = krp
    q = _dot(cq.astype(BF16), wuq[...])
    qn_ref[...] = q[:, :HEADS * QK_NOPE].astype(BF16)
    qr = q[:, HEADS * QK_NOPE:]
    qr_ref[...] = (qr * cq_ref[...] + _pair_swap(qr) * sq_ref[...]).astype(BF16)
    kv = _dot(ckv.astype(BF16), wukv[...])
    kn_ref[...] = kv[:, :HEADS * QK_NOPE].astype(BF16)
    v_ref[...] = kv[:, HEADS * QK_NOPE:].astype(BF16)
    kr = krp * ck_ref[...] + _pair_swap(krp) * sk_ref[...]
    kr2_ref[...] = jnp.concatenate([kr, pltpu.roll(kr, QK_ROPE, 1)], axis=1).astype(BF16)


def _mla_proj(x, g0, mods, w1, gq, gkv, wuq, wukv, cos_q, sin_q, cos_k, sin_k):
    tm = 256
    grp = _grp_bm(tm)
    n_pos = DEC_SEQ // tm
    tab = lambda w: pl.BlockSpec((tm, w), lambda i: (jnp.where(i * tm < NP, n_pos, (i - NP // tm) % n_pos), 0))
    row = lambda w: pl.BlockSpec((tm, w), lambda i: (i, 0))
    shp = lambda w, dt: jax.ShapeDtypeStruct((N, w), dt)
    return pl.pallas_call(
        _mlaproj_kernel,
        grid=(N // tm,),
        in_specs=[row(D), _full((1, D)), _mod_spec(0, grp), _mod_spec(1, grp),
                  _full(w1.shape), _full((1, Q_LORA)), _full((1, KV_LORA)),
                  _full(wuq.shape), _full(wukv.shape),
                  tab(HEADS * QK_ROPE), tab(HEADS * QK_ROPE), tab(LANE), tab(LANE)],
        out_specs=[row(HEADS * QK_NOPE), row(HEADS * QK_ROPE), row(HEADS * QK_NOPE), row(HEADS * V_DIM),
                   row(2 * LANE), row(KV_LORA), row(LANE)],
        out_shape=[shp(HEADS * QK_NOPE, BF16), shp(HEADS * QK_ROPE, BF16), shp(HEADS * QK_NOPE, BF16),
                   shp(HEADS * V_DIM, BF16), shp(2 * LANE, BF16), shp(KV_LORA, F32), shp(LANE, F32)],
        compiler_params=_cparams(("arbitrary",)),
        name="mla_proj",
    )(x, g0, mods, mods, w1, gq, gkv, wuq, wukv, cos_q, sin_q, cos_k, sin_k)


def _cachekv_kernel(c_ref, w_ref, kn_ref, v_ref):
    kv = _dot(c_ref[...].astype(BF16), w_ref[...])
    kn_ref[...] = kv[:, :HEADS * QK_NOPE].astype(BF16)
    v_ref[...] = kv[:, HEADS * QK_NOPE:].astype(BF16)


def _cache_kv(ckv_cache, wukv):
    rows = ckv_cache.shape[0]
    tm = 512
    return pl.pallas_call(
        _cachekv_kernel,
        grid=(rows // tm,),
        in_specs=[pl.BlockSpec((tm, KV_LORA), lambda i: (i, 0)), _full(wukv.shape)],
        out_specs=[pl.BlockSpec((tm, HEADS * QK_NOPE), lambda i: (i, 0)),
                   pl.BlockSpec((tm, HEADS * V_DIM), lambda i: (i, 0))],
        out_shape=[jax.ShapeDtypeStruct((rows, HEADS * QK_NOPE), BF16),
                   jax.ShapeDtypeStruct((rows, HEADS * V_DIM), BF16)],
        compiler_params=_cparams(("arbitrary",)),
        name="cache_kv",
    )(ckv_cache, wukv)


def _attn_kernel(*refs, has_cache):
    if has_cache:
        qn, qr, kn, kr, v, knc, krc, vc, o_ref = refs
    else:
        qn, qr, kn, kr, v, o_ref = refs
    scale = (QK_NOPE + QK_ROPE) ** -0.5
    q = jnp.concatenate([qn[...], qr[...]], axis=1)
    k = jnp.concatenate([kn[...], kr[...]], axis=1)
    s = lax.dot_general(q, k, NT_DIMS, preferred_element_type=F32) * scale
    m = jnp.max(s, axis=-1, keepdims=True)
    if has_cache:
        kc = jnp.concatenate([knc[...], krc[...]], axis=1)
        sc = lax.dot_general(q, kc, NT_DIMS, preferred_element_type=F32) * scale
        m = jnp.maximum(m, jnp.max(sc, axis=-1, keepdims=True))
        pc = jnp.exp(sc - m)
    p = jnp.exp(s - m)
    den = jnp.sum(p, axis=-1, keepdims=True)
    o = _dot(p.astype(BF16), v[...])
    if has_cache:
        den = den + jnp.sum(pc, axis=-1, keepdims=True)
        o = o + _dot(pc.astype(BF16), vc[...])
    o_ref[...] = (o / den).astype(BF16)


def _attention(qn, qr, kn, kr2, v, *, row0, n_seq, seq, tq, cache=None):
    nq = seq // tq
    qblk = lambda b, h, i: row0 // tq + b * nq + i
    kblk = lambda b: row0 // seq + b
    in_specs = [pl.BlockSpec((tq, LANE), lambda b, h, i: (qblk(b, h, i), h)),
                pl.BlockSpec((tq, LANE), lambda b, h, i: (qblk(b, h, i), h // 2)),
                pl.BlockSpec((seq, LANE), lambda b, h, i: (kblk(b), h)),
                pl.BlockSpec((seq, LANE), lambda b, h, i: (kblk(b), h % 2)),
                pl.BlockSpec((seq, LANE), lambda b, h, i: (kblk(b), h))]
    args = [qn, qr, kn, kr2, v]
    if cache is not None:
        knc, kr2c, vc = cache
        in_specs += [pl.BlockSpec((PAST_LEN, LANE), lambda b, h, i: (b, h)),
                     pl.BlockSpec((PAST_LEN, LANE), lambda b, h, i: (b, h % 2)),
                     pl.BlockSpec((PAST_LEN, LANE), lambda b, h, i: (b, h))]
        args += [knc, kr2c, vc]
    return pl.pallas_call(
        functools.partial(_attn_kernel, has_cache=cache is not None),
        grid=(n_seq, HEADS, nq),
        in_specs=in_specs,
        out_specs=pl.BlockSpec((tq, LANE), lambda b, h, i: (b * nq + i, h)),
        out_shape=jax.ShapeDtypeStruct((n_seq * seq, HEADS * V_DIM), BF16),
        compiler_params=_cparams(("arbitrary", "arbitrary", "arbitrary")),
        name="attn_latent" if cache is not None else "attn_context",
    )(*args)


def _router_kernel(op_ref, os_ref, x_ref, gate1, g1, sh2, sc2, g2, wout, wr_hi, wr_lo, tri,
                   x3_ref, h_ref, ri_ref, rf_ref, cnt_ref, carry, *, prompt_steps):
    step = pl.program_id(0)

    @pl.when(step == 0)
    def _():
        carry[...] = jnp.zeros_like(carry)

    o = jnp.where(step < prompt_steps, op_ref[...], os_ref[...])
    x3 = _gated(x_ref[...], _dot(o, wout[...]), g1[...], gate1[0])
    x3_ref[...] = x3
    h = _adaln(x3, g2[...], sc2[0], sh2[0])
    h_ref[...] = h
    h_hi = h.astype(BF16)
    h_lo = (h - h_hi.astype(F32)).astype(BF16)
    dg = lambda a, b: lax.dot_general(a, b, NT_DIMS, preferred_element_type=F32)
    lg = dg(wr_hi[...], h_hi) + dg(wr_hi[...], h_lo) + dg(wr_lo[...], h_hi)
    eidx = lax.broadcasted_iota(I32, lg.shape, 0).astype(F32)
    m1 = jnp.max(lg, axis=0, keepdims=True)
    i1 = jnp.min(jnp.where(lg == m1, eidx, float(N_EXP)), axis=0, keepdims=True)
    sel1 = eidx == i1
    lg2 = jnp.where(sel1, -jnp.inf, lg)
    m2 = jnp.max(lg2, axis=0, keepdims=True)
    i2 = jnp.min(jnp.where(lg2 == m2, eidx, float(N_EXP)), axis=0, keepdims=True)
    sel2 = eidx == i2
    e = jnp.exp(m2 - m1)
    w1 = 1.0 / (1.0 + e)
    w2 = e / (1.0 + e)
    picked = jnp.where(sel1 | sel2, 1.0, 0.0)
    rank = _dot(picked.astype(BF16), tri[...]) + carry[:, 0:1]
    r1 = jnp.sum(jnp.where(sel1, rank, 0.0), axis=0, keepdims=True)
    r2 = jnp.sum(jnp.where(sel2, rank, 0.0), axis=0, keepdims=True)
    carry[...] = carry[...] + jnp.sum(picked, axis=1, keepdims=True)
    cnt_ref[...] = carry[...]
    ri_ref[...] = jnp.where(eidx == 0.0, i1, jnp.where(eidx == 1.0, i2, jnp.where(eidx == 2.0, r1, r2))).astype(I32)
    rf_ref[...] = jnp.where(eidx == 0.0, w1, w2)


def _attn_out_router(o_p, o_s, x, mods, g1, g2, wout, wr_hi, wr_lo):
    tm = 512
    grp = _grp_bm(tm)
    n_p = NP // tm
    o_specs = [pl.BlockSpec((tm, HEADS * V_DIM), lambda i: (jnp.minimum(i, n_p - 1), 0)),
               pl.BlockSpec((tm, HEADS * V_DIM), lambda i: (jnp.maximum(i - n_p, 0), 0))]
    tri = jnp.asarray(np.triu(np.ones((tm, tm), np.float32), 1), BF16)
    row = lambda w: pl.BlockSpec((tm, w), lambda i: (i, 0))
    col = pl.BlockSpec((N_EXP, tm), lambda i: (0, i))
    return pl.pallas_call(
        functools.partial(_router_kernel, prompt_steps=n_p),
        grid=(N // tm,),
        in_specs=o_specs + [row(D), _mod_spec(2, grp), _full((1, D)),
                  _mod_spec(3, grp), _mod_spec(4, grp), _full((1, D)),
                  _full((HEADS * V_DIM, D)), _full((N_EXP, D)), _full((N_EXP, D)), _full((tm, tm))],
        out_specs=[row(D), row(D), col, col, _full((N_EXP, LANE))],
        out_shape=[jax.ShapeDtypeStruct((N, D), F32), jax.ShapeDtypeStruct((N, D), F32),
                   jax.ShapeDtypeStruct((N_EXP, N), I32), jax.ShapeDtypeStruct((N_EXP, N), F32),
                   jax.ShapeDtypeStruct((N_EXP, LANE), F32)],
        scratch_shapes=[pltpu.VMEM((N_EXP, LANE), F32)],
        compiler_params=_cparams(("arbitrary",)),
        name="attn_out_router",
    )(o_p, o_s, x, mods, g1, mods, mods, g2, wout, wr_hi, wr_lo, tri)


def _row_copy(src_hbm, dst, i_src, i_dst, sem):
    return pltpu.make_async_copy(src_hbm.at[pl.ds(i_src, 1), :], dst.at[pl.ds(i_dst, 1), :], sem)


def _gather_rows(idx_ref, src_hbm, dst, sem):
    n = dst.shape[0]

    def start(r, carry):
        _row_copy(src_hbm, dst, idx_ref[0, 0, r], r, sem).start()
        return carry

    lax.fori_loop(0, n, start, 0)
    pltpu.make_async_copy(src_hbm.at[pl.ds(0, n), :], dst, sem).wait()


def _dispatch_kernel(src_ref, h_hbm, o_ref, sem):
    _gather_rows(src_ref, h_hbm, o_ref, sem)


def _dispatch(src, h):
    tm = TM_E
    return pl.pallas_call(
        _dispatch_kernel,
        grid=(N_TILES,),
        in_specs=[pl.BlockSpec((1, 1, tm), lambda i: (i, 0, 0), memory_space=pltpu.SMEM),
                  pl.BlockSpec(memory_space=pl.ANY)],
        out_specs=pl.BlockSpec((tm, D), lambda i: (i, 0)),
        out_shape=jax.ShapeDtypeStruct((P_ROWS, D), F32),
        scratch_shapes=[pltpu.SemaphoreType.DMA],
        compiler_params=_cparams(("arbitrary",)),
        name="moe_dispatch",
    )(src.reshape(N_TILES, 1, tm), h)


def _moe_kernel(te, nu, hs_ref, wg, wu, wd, y_ref, hbf, acc):
    i = pl.program_id(0)
    j = pl.program_id(1)
    used = i < nu[0]

    @pl.when(used & (j == 0))
    def _():
        hbf[...] = hs_ref[...].astype(BF16)
        acc[...] = jnp.zeros_like(acc)

    @pl.when(used)
    def _():
        acc[...] += _swiglu_step(hbf[...], wg[0], wu[0], wd[0])

    @pl.when(j == N_FF - 1)
    def _():
        y_ref[...] = jnp.where(used, acc[...], 0.0)


def _moe_experts(tile_expert, n_used, hs, w_gate_up, w_down):
    tm = TM_E
    jj = lambda i, j, te, nu: jnp.where(i < nu[0], j, N_FF - 1)
    grid_spec = pltpu.PrefetchScalarGridSpec(
        num_scalar_prefetch=2,
        grid=(N_TILES, N_FF),
        in_specs=[pl.BlockSpec((tm, D), lambda i, j, te, nu: (i, 0)),
                  pl.BlockSpec((1, D, FF_BLK), lambda i, j, te, nu: (te[i], 0, jj(i, j, te, nu))),
                  pl.BlockSpec((1, D, FF_BLK), lambda i, j, te, nu: (te[i], 0, jj(i, j, te, nu) + N_FF)),
                  pl.BlockSpec((1, FF_BLK, D), lambda i, j, te, nu: (te[i], jj(i, j, te, nu), 0))],
        out_specs=pl.BlockSpec((tm, D), lambda i, j, te, nu: (i, 0)),
        scratch_shapes=[pltpu.VMEM((tm, D), BF16), pltpu.VMEM((tm, D), F32)],
    )
    return pl.pallas_call(
        _moe_kernel,
        grid_spec=grid_spec,
        out_shape=jax.ShapeDtypeStruct((P_ROWS, D), F32),
        compiler_params=_cparams(("arbitrary", "arbitrary")),
        name="moe_experts",
    )(tile_expert, n_used, hs, w_gate_up, w_gate_up, w_down)


def _combine_kernel(p1_ref, p2_ref, y_hbm, x_ref, w1_ref, w2_ref, gate2, g3, o_ref, b1, b2, sem):
    _gather_rows(p1_ref, y_hbm, b1, sem.at[0])
    _gather_rows(p2_ref, y_hbm, b2, sem.at[1])
    f = w1_ref[...] * b1[...] + w2_ref[...] * b2[...]
    o_ref[...] = _gated(x_ref[...], f, g3[...], gate2[0])


def _moe_combine(pos1, pos2, y, x, w1, w2, mods, g3):
    tm = 256
    grp = _grp_bm(tm)
    nb = N // tm
    smem = pl.BlockSpec((1, 1, tm), lambda i: (i, 0, 0), memory_space=pltpu.SMEM)
    return pl.pallas_call(
        _combine_kernel,
        grid=(nb,),
        in_specs=[smem, smem, pl.BlockSpec(memory_space=pl.ANY),
                  pl.BlockSpec((tm, D), lambda i: (i, 0)),
                  pl.BlockSpec((tm, 1), lambda i: (i, 0)), pl.BlockSpec((tm, 1), lambda i: (i, 0)),
                  _mod_spec(5, grp), _full((1, D))],
        out_specs=pl.BlockSpec((tm, D), lambda i: (i, 0)),
        out_shape=jax.ShapeDtypeStruct((N, D), F32),
        scratch_shapes=[pltpu.VMEM((tm, D), F32), pltpu.VMEM((tm, D), F32), pltpu.SemaphoreType.DMA((2,))],
        compiler_params=_cparams(("arbitrary",)),
        name="moe_combine",
    )(pos1.reshape(nb, 1, tm), pos2.reshape(nb, 1, tm), y, x, w1, w2, mods, g3)


def _block_diag(w):
    hh, a, b = w.shape
    eye = jnp.eye(hh, dtype=w.dtype)
    return jnp.einsum('hab,hk->hakb', w, eye).reshape(hh * a, hh * b)


def _s5_matrices(a_re, a_im, log_dt, b_re, b_im, c_re, c_im):
    dt = jnp.exp(log_dt)[:, None]
    mag = jnp.exp(a_re * dt)
    abr = mag * jnp.cos(a_im * dt)
    abi = mag * jnp.sin(a_im * dt)
    den = a_re * a_re + a_im * a_im
    cr = ((abr - 1.0) * a_re + abi * a_im) / den
    ci = (abi * a_re - (abr - 1.0) * a_im) / den
    bbr = cr[..., None] * b_re - ci[..., None] * b_im
    bbi = cr[..., None] * b_im + ci[..., None] * b_re
    hg = S5_G // 2
    eye = jnp.eye(hg, dtype=F32)
    bms, cms = [], []
    for j in range(2):
        sl = slice(j * hg, (j + 1) * hg)
        bd = lambda m: jnp.einsum('gpc,gh->gchp', m[sl], eye).reshape(hg * S5_CH, hg * S5_P)
        bms.append(jnp.concatenate([bd(bbr), bd(bbi)], axis=1))
        cd = lambda m: jnp.einsum('gcp,gh->gphc', m[sl], eye).reshape(hg * S5_P, hg * S5_CH)
        cms.append(jnp.concatenate([cd(c_re), cd(-c_im)], axis=0))
    return (jnp.stack(bms).astype(BF16), abr.reshape(1, S5_N), abi.reshape(1, S5_N),
            jnp.stack(cms).astype(BF16))


def _rope_tables(tm):
    rows = DEC_SEQ // GRID_W
    row = jnp.repeat(jnp.arange(rows, dtype=F32), GRID_W)
    col = jnp.tile(jnp.arange(GRID_W, dtype=F32), rows)
    nf = QK_ROPE // 4
    inv = ROPE_THETA ** (-jnp.arange(nf, dtype=F32) / nf)
    ang = jnp.concatenate([row[:, None] * inv, col[:, None] * inv], axis=-1)
    cos = jnp.repeat(jnp.cos(ang), 2, axis=-1)
    sin = jnp.stack([-jnp.sin(ang), jnp.sin(ang)], axis=-1).reshape(DEC_SEQ, QK_ROPE)
    ident = lambda t, one: jnp.concatenate([t, jnp.full((tm, t.shape[1]), one, F32)], axis=0)
    cos_q = ident(jnp.tile(cos, (1, HEADS)), 1.0)
    sin_q = ident(jnp.tile(sin, (1, HEADS)), 0.0)
    pad = lambda t, one: jnp.concatenate([t, jnp.full((DEC_SEQ, LANE - QK_ROPE), one, F32)], axis=1)
    return cos_q, sin_q, ident(pad(cos, 1.0), 1.0), ident(pad(sin, 0.0), 0.0)


def _to_time_major(xp, xs):
    p = xp.reshape(BATCH // SUB, SUB, SEQ, -1).transpose(0, 2, 1, 3).reshape(NP, -1)
    s = xs.transpose(1, 0, 2).reshape(NS, -1)
    return jnp.concatenate([p, s], axis=0)


def _to_batch_major(x):
    p = x[:NP].reshape(BATCH // SUB, SEQ, SUB, -1).transpose(0, 2, 1, 3).reshape(NP, -1)
    s = x[NP:].reshape(DEC_SEQ, DEC_BATCH, -1).transpose(1, 0, 2).reshape(NS, -1)
    return jnp.concatenate([p, s], axis=0)


def _group_states(prompt_state, sample_state):
    w = sample_state.shape[-1]
    return jnp.concatenate([prompt_state.reshape(2, SUB, w), sample_state.reshape(1, SUB, w)], axis=0)


def _layer_ab(x, m, ng, j, state_lru, state_s5_re, state_s5_im, p):
    xz = _ab_inproj(x, ng[0:1], m, p['ab_w_in'][j].astype(BF16))
    zeros = lambda w: jnp.zeros((BATCH, w), F32)
    outs = []
    for d in range(2):
        wg = jnp.concatenate([_block_diag(p['lru_wa'][j, d]), _block_diag(p['lru_wx'][j, d])], axis=1).astype(BF16)
        bg = jnp.concatenate([p['lru_ba'][j, d], p['lru_bx'][j, d]])[None]
        bm, ar, ai, cm = _s5_matrices(p['s5_a_re'][j, d], p['s5_a_im'][j, d], p['s5_log_dt'][j, d],
                                      p['s5_b_re'][j, d], p['s5_b_im'][j, d], p['s5_c_re'][j, d], p['s5_c_im'][j, d])
        h0l = _group_states(zeros(LRU_W), state_lru[:, j, d])
        h0r = _group_states(zeros(S5_N), state_s5_re[:, j, d].reshape(DEC_BATCH, S5_N))
        h0i = _group_states(zeros(S5_N), state_s5_im[:, j, d].reshape(DEC_BATCH, S5_N))
        outs.append(_ab_scan(xz, h0l, h0r, h0i, p['ab_conv_w'][j], p['ab_conv_b'][j][None], wg, bg,
                             p['lru_lambda'][j, d][None], bm, ar, ai, cm, reverse=(d == 1)))
    (haf, yf, llf, lrf, lif), (hab, yr, llb, lrb, lib) = outs
    x = _ab_out(haf, hab, yf, yr, xz, x, m, ng[1:2], p['s5_d'][j][None], p['s5_w_glu'][j].astype(BF16),
                p['s5_b_glu'][j][None], p['ab_w_out'][j].astype(BF16))
    x = _ffn(x, m, ng[2:3], ng[3:4], p['ffn_w_gate_up'][j], p['ffn_w_down'][j])
    prompt = lambda f, b, w: jnp.stack([f[:2].reshape(BATCH, w), b[:2].reshape(BATCH, w)], axis=1)
    lru = prompt(llf, llb, LRU_W)
    s5r = prompt(lrf, lrb, S5_N).reshape(BATCH, 2, S5_G, S5_P)
    s5i = prompt(lif, lib, S5_N).reshape(BATCH, 2, S5_G, S5_P)
    return x, lru, s5r, s5i


def _head_major(w, parts):
    k = w.shape[0]
    per_head = w.reshape(k, HEADS, -1)
    out, start = [], 0
    for width in parts:
        out.append(per_head[:, :, start:start + width].reshape(k, HEADS * width))
        start += width
    return jnp.concatenate(out, axis=1)


def _layer_mla_moe(x, m, ng, j, cache_kv_latent, cache_k_rope, p):
    w1 = jnp.concatenate([p['mla_w_in'][j], jnp.zeros((D, LANE - QK_ROPE), F32)], axis=1).astype(BF16)
    wuq = _head_major(p['mla_w_uq'][j], (QK_NOPE, QK_ROPE)).astype(BF16)
    wukv = _head_major(p['mla_w_ukv'][j], (QK_NOPE, V_DIM)).astype(BF16)
    tables = _rope_tables(256)
    qn, qr, kn, v, kr2, ckv, krr = _mla_proj(x, ng[0:1], m, w1, p['mla_g_q'][j][None], p['mla_g_kv'][j][None],
                                              wuq, wukv, *tables)
    knc, vc = _cache_kv(cache_kv_latent[:, j].reshape(DEC_BATCH * PAST_LEN, KV_LORA), wukv)
    krc = cache_k_rope[:, j].reshape(DEC_BATCH * PAST_LEN, QK_ROPE)
    z = jnp.zeros_like(krc)
    kr2c = jnp.concatenate([krc, z, z, krc], axis=1).astype(BF16)
    o_p = _attention(qn, qr, kn, kr2, v, row0=0, n_seq=BATCH, seq=SEQ, tq=SEQ)
    o_s = _attention(qn, qr, kn, kr2, v, row0=NP, n_seq=DEC_BATCH, seq=DEC_SEQ, tq=512, cache=(knc, kr2c, vc))
    wr_t = p['moe_w_router'][j].T
    wr_hi = wr_t.astype(BF16)
    wr_lo = (wr_t - wr_hi.astype(F32)).astype(BF16)
    x3, h, ri, rf, cnt = _attn_out_router(o_p, o_s, x, m, ng[1:2], ng[2:3], p['mla_w_out'][j].astype(BF16), wr_hi, wr_lo)
    counts = cnt[:, 0].astype(I32)
    padded = ((counts + TM_E - 1) // TM_E) * TM_E
    ends = jnp.cumsum(padded)
    offs = ends - padded
    pos1 = offs[ri[0]] + ri[2]
    pos2 = offs[ri[1]] + ri[3]
    tok = jnp.arange(N, dtype=I32)
    src = jnp.zeros((P_ROWS,), I32).at[pos1].set(tok).at[pos2].set(tok)
    tile_expert = jnp.minimum(jnp.searchsorted(ends, jnp.arange(N_TILES, dtype=I32) * TM_E, side='right'),
                              N_EXP - 1).astype(I32)
    n_used = (ends[-1] // TM_E).astype(I32)[None]
    hs = _dispatch(src, h)
    y = _moe_experts(tile_expert, n_used, hs, p['moe_w_gate_up'][j], p['moe_w_down'][j])
    x4 = _moe_combine(pos1, pos2, y, x3, rf[0][:, None], rf[1][:, None], m, ng[3:4])
    kv_new = ckv[:NP].reshape(BATCH, SEQ, KV_LORA)
    kr_new = krr[:NP, :QK_ROPE].reshape(BATCH, SEQ, QK_ROPE)
    return x4, kv_new, kr_new


def kernel(x_prompt, x_sample, c, state_lru, state_s5_re, state_s5_im, cache_kv_latent, cache_k_rope, c_ctx, w_mod, b_mod, norm_gains, ab_w_in, ab_conv_w, ab_conv_b, lru_wa, lru_ba, lru_wx, lru_bx, lru_lambda, s5_a_re, s5_a_im, s5_log_dt, s5_b_re, s5_b_im, s5_c_re, s5_c_im, s5_d, s5_w_glu, s5_b_glu, ab_w_out, ffn_w_gate_up, ffn_w_down, mla_w_in, mla_g_q, mla_g_kv, mla_w_uq, mla_w_ukv, mla_w_out, moe_w_router, moe_w_gate_up, moe_w_down):
    p = dict(ab_w_in=ab_w_in, ab_conv_w=ab_conv_w, ab_conv_b=ab_conv_b, lru_wa=lru_wa, lru_ba=lru_ba,
             lru_wx=lru_wx, lru_bx=lru_bx, lru_lambda=lru_lambda, s5_a_re=s5_a_re, s5_a_im=s5_a_im,
             s5_log_dt=s5_log_dt, s5_b_re=s5_b_re, s5_b_im=s5_b_im, s5_c_re=s5_c_re, s5_c_im=s5_c_im,
             s5_d=s5_d, s5_w_glu=s5_w_glu, s5_b_glu=s5_b_glu, ab_w_out=ab_w_out, ffn_w_gate_up=ffn_w_gate_up,
             ffn_w_down=ffn_w_down, mla_w_in=mla_w_in, mla_g_q=mla_g_q, mla_g_kv=mla_g_kv, mla_w_uq=mla_w_uq,
             mla_w_ukv=mla_w_ukv, mla_w_out=mla_w_out, moe_w_router=moe_w_router, moe_w_gate_up=moe_w_gate_up,
             moe_w_down=moe_w_down)
    depth = w_mod.shape[0]
    cond = jnp.concatenate([c_ctx[None], c, jnp.zeros((2 * SUB - 1 - DEC_BATCH, D), F32)], axis=0)
    mod = _modulation(cond, w_mod, b_mod)
    ctx_tile = lambda l: jnp.broadcast_to(mod[l, 0:1], (SUB, 6 * D))
    x = _to_time_major(x_prompt, x_sample)
    time_major = True
    lru_l, s5r_l, s5i_l, kv_l, kr_l = [], [], [], [], []
    for layer in range(depth):
        j = layer // 2
        ng = norm_gains[layer]
        if layer % 2 == 0:
            if not time_major:
                x = _to_time_major(x[:NP].reshape(BATCH, SEQ, D), x[NP:].reshape(DEC_BATCH, DEC_SEQ, D))
                time_major = True
            m = jnp.stack([ctx_tile(layer), mod[layer, 1:1 + DEC_BATCH]])
            x, lru, s5r, s5i = _layer_ab(x, m, ng, j, state_lru, state_s5_re, state_s5_im, p)
            lru_l.append(lru)
            s5r_l.append(s5r)
            s5i_l.append(s5i)
        else:
            if time_major:
                x = _to_batch_major(x)
                time_major = False
            lat = jnp.broadcast_to(mod[layer, 1:1 + DEC_BATCH, None, :], (DEC_BATCH, SUB, 6 * D))
            m = jnp.concatenate([ctx_tile(layer)[None], lat], axis=0)
            x, kv_new, kr_new = _layer_mla_moe(x, m, ng, j, cache_kv_latent, cache_k_rope, p)
            kv_l.append(kv_new)
            kr_l.append(kr_new)
    if time_major:
        x = _to_batch_major(x)
    return (x[:NP].reshape(BATCH, SEQ, D), x[NP:].reshape(DEC_BATCH, DEC_SEQ, D),
            jnp.stack(lru_l, axis=1), jnp.stack(s5r_l, axis=1), jnp.stack(s5i_l, axis=1),
            jnp.stack(kv_l, axis=1), jnp.stack(kr_l, axis=1))
```

```python
import functools
import math

import numpy as np
import jax
import jax.numpy as jnp
from jax import lax
from jax.experimental import pallas as pl
from jax.experimental.pallas import tpu as pltpu

F32 = jnp.float32
BF16 = jnp.bfloat16
I32 = jnp.int32

D = 1024
BATCH, SEQ = 16, 256
DEC_BATCH, DEC_SEQ = 8, 2048
PAST_LEN = 256
GRID_W = 64
LRU_W = 512
LRU_HEADS = 8
LRU_C = 8.0
CONV_W = 4
S5_W = 512
S5_CH = 16
S5_G = 32
S5_P = 64
S5_N = S5_G * S5_P
HEADS = 8
QK_NOPE, QK_ROPE, V_DIM = 128, 64, 128
Q_LORA, KV_LORA = 384, 256
ROPE_THETA = 10000.0
D_FF = 2816
N_EXP = 8
EPS = 1e-6

NP = BATCH * SEQ
NS = DEC_BATCH * DEC_SEQ
N = NP + NS
SUB = 8
LANE = 128
ROW_TILE = D // LANE
T_CHUNK = 32
R_CHUNK = T_CHUNK * SUB
FF_BLK = 256
N_FF = D_FF // FF_BLK
TM_E = 512
P_ROWS = 2 * N + N_EXP * TM_E
N_TILES = P_ROWS // TM_E
VMEM_LIMIT = 56 * 1024 * 1024

NT_DIMS = (((1,), (1,)), ((), ()))


def _cparams(sem):
    return pltpu.CompilerParams(dimension_semantics=sem, vmem_limit_bytes=VMEM_LIMIT)


def _dot(a, b):
    return jnp.dot(a, b, preferred_element_type=F32)


def _sigmoid(x):
    return 1.0 / (1.0 + jnp.exp(-x))


def _neg_expm1(z):
    series = -z * (1.0 + z / 2.0 * (1.0 + z / 3.0 * (1.0 + z / 4.0 * (1.0 + z / 5.0 * (1.0 + z / 6.0 * (1.0 + z / 7.0))))))
    return jnp.where(z > -0.1, series, 1.0 - jnp.exp(z))


def _gelu(x):
    return x * (0.5 * (1.0 + jnp.tanh(math.sqrt(2.0 / math.pi) * (x + 0.044715 * (x * x * x)))))


def _rms(x, g):
    ms = jnp.mean(x * x, axis=-1, keepdims=True)
    return x * lax.rsqrt(ms + EPS) * g


def _rows8(y, fn):
    r, c = y.shape
    return fn(y.reshape(r // SUB, SUB, c)).reshape(r, c)


def _adaln(x, g, scale, shift):
    return _rows8(_rms(x, g), lambda y: y * (1.0 + scale)[None] + shift[None])


def _gated(x, y, g, gate):
    return x + _rows8(_rms(y, g), lambda z: z * gate[None])


def _store_row_tiles(ref, piece):
    rows = ref.shape[0] // ROW_TILE
    for c in range(ROW_TILE):
        ref[pl.ds(c, rows, stride=ROW_TILE), :] = piece(slice(c * LANE, (c + 1) * LANE))


def _load_row_tiles(ref, c, rows):
    return ref[pl.ds(c, rows, stride=ROW_TILE), :]


def _full(shape):
    nd = len(shape)
    return pl.BlockSpec(shape, lambda *_: (0,) * nd)


def _mod_spec(k, grp):
    return pl.BlockSpec((1, SUB, D), lambda i, *_: (grp(i), 0, k))


def _grp_tm(tm):
    return lambda i: (i * tm >= NP).astype(I32)


def _grp_bm(tm):
    return lambda i: jnp.where(i * tm < NP, 0, 1 + (i * tm - NP) // DEC_SEQ)


def _mod_kernel(c_ref, w_ref, b_ref, o_ref):
    c = c_ref[...]
    s = c * _sigmoid(c)
    o_ref[0] = _dot(s.astype(BF16), w_ref[0].astype(BF16)) + b_ref[0]


def _modulation(cond, w_mod, b_mod):
    depth = w_mod.shape[0]
    rows = cond.shape[0]
    return pl.pallas_call(
        _mod_kernel,
        grid=(depth, 6),
        in_specs=[_full((rows, D)),
                  pl.BlockSpec((1, D, D), lambda l, j: (l, 0, j)),
                  pl.BlockSpec((1, 1, D), lambda l, j: (l, 0, j))],
        out_specs=pl.BlockSpec((1, rows, D), lambda l, j: (l, 0, j)),
        out_shape=jax.ShapeDtypeStruct((depth, rows, 6 * D), F32),
        compiler_params=_cparams(("arbitrary", "arbitrary")),
        name="modulation",
    )(cond, w_mod, b_mod.reshape(depth, 1, 6 * D))


def _inproj_kernel(x_ref, g_ref, sh_ref, sc_ref, w_ref, o_ref):
    h = _adaln(x_ref[...], g_ref[...], sc_ref[0], sh_ref[0])
    o_ref[...] = _dot(h.astype(BF16), w_ref[...])


def _ab_inproj(x, gain, mods, w_in):
    tm = 512
    nout = w_in.shape[1]
    grp = _grp_tm(tm)
    return pl.pallas_call(
        _inproj_kernel,
        grid=(N // tm,),
        in_specs=[pl.BlockSpec((tm, D), lambda i: (i, 0)),
                  _full((1, D)),
                  _mod_spec(0, grp), _mod_spec(1, grp),
                  _full((D, nout))],
        out_specs=pl.BlockSpec((tm, nout), lambda i: (i, 0)),
        out_shape=jax.ShapeDtypeStruct((N, nout), F32),
        compiler_params=_cparams(("arbitrary",)),
        name="ab_inproj",
    )(x, gain, mods, mods, w_in)


def _scan_table(reverse):
    cols = []
    groups = [(0, SEQ // T_CHUNK, 0), (1, SEQ // T_CHUNK, SEQ // T_CHUNK),
              (2, DEC_SEQ // T_CHUNK, NP // R_CHUNK)]
    for g, nc, base in groups:
        order = range(nc - 1, -1, -1) if reverse else range(nc)
        for k, c in enumerate(order):
            cols.append((base + c, g, int(k == 0), int(c > 0), int(c < nc - 1)))
    return np.asarray(cols, np.int32).T.copy()


def _scan_kernel(tbl, xa_ref, xp_ref, xn_ref, xb_ref, h0l_ref, h0r_ref, h0i_ref,
                 cw_ref, cb_ref, wg_ref, bg_ref, lam_ref, bm_ref, ar_ref, ai_ref, cm_ref,
                 ha_ref, y_ref, ll_ref, lr_ref, li_ref,
                 ext, abuf, bbuf, hre, him, hl, sre, sim, *, reverse):
    s = pl.program_id(0)

    @pl.when(tbl[2, s] == 1)
    def _():
        hl[...] = h0l_ref[0]
        sre[...] = h0r_ref[0]
        sim[...] = h0i_ref[0]

    ext[0:2 * SUB] = jnp.where(tbl[3, s] == 1, xp_ref[...], 0.0)
    ext[2 * SUB:2 * SUB + R_CHUNK] = xa_ref[...]
    ext[2 * SUB + R_CHUNK:3 * SUB + R_CHUNK] = jnp.where(tbl[4, s] == 1, xn_ref[...], 0.0)
    xa = cb_ref[...] + cw_ref[0:1] * ext[0:R_CHUNK]
    for k in range(1, CONV_W):
        xa = xa + cw_ref[k:k + 1] * ext[k * SUB:k * SUB + R_CHUNK]

    gz = _dot(xa.astype(BF16), wg_ref[...]) + bg_ref[...]
    r = _sigmoid(gz[:, :LRU_W])
    i = _sigmoid(gz[:, LRU_W:])
    lam = lam_ref[...]
    log_sig = jnp.minimum(lam, 0.0) - jnp.log1p(jnp.exp(-jnp.abs(lam)))
    log_a = LRU_C * r * log_sig
    abuf[...] = jnp.exp(log_a)
    bbuf[...] = jnp.sqrt(_neg_expm1(2.0 * log_a)) * (i * xa)

    order = range(T_CHUNK - 1, -1, -1) if reverse else range(T_CHUNK)
    h = hl[...]
    for t in order:
        rows = slice(t * SUB, (t + 1) * SUB)
        h = abuf[rows] * h + bbuf[rows]
        ha_ref[rows, :] = h
    hl[...] = h
    ll_ref[0] = h

    ub = xb_ref[...].astype(BF16)
    half = S5_N // 2
    for j in range(2):
        bu = _dot(ub[:, j * 256:(j + 1) * 256], bm_ref[j])
        hre[:, j * half:(j + 1) * half] = bu[:, :half]
        him[:, j * half:(j + 1) * half] = bu[:, half:]
    cblk = 4 * LANE
    for cb in range(S5_N // cblk):
        cols = slice(cb * cblk, (cb + 1) * cblk)
        ar = jnp.broadcast_to(ar_ref[:, cols], (SUB, cblk))
        ai = jnp.broadcast_to(ai_ref[:, cols], (SUB, cblk))
        hr = sre[:, cols]
        hi = sim[:, cols]
        for t in order:
            rows = slice(t * SUB, (t + 1) * SUB)
            nr = ar * hr - ai * hi + hre[rows, cols]
            ni = ar * hi + ai * hr + him[rows, cols]
            hr, hi = nr, ni
            hre[rows, cols] = hr
            him[rows, cols] = hi
        sre[:, cols] = hr
        sim[:, cols] = hi
    lr_ref[0] = sre[...]
    li_ref[0] = sim[...]
    for j in range(2):
        hc = jnp.concatenate([hre[:, j * half:(j + 1) * half], him[:, j * half:(j + 1) * half]],
                             axis=1).astype(BF16)
        y_ref[:, j * 256:(j + 1) * 256] = _dot(hc, cm_ref[j])


def _ab_scan(xz, h0l, h0r, h0i, cw, cb, wg, bg, lam, bm, ar, ai, cm, reverse):
    tbl = jnp.asarray(_scan_table(reverse))
    n_steps = tbl.shape[1]
    blk = lambda s, t: t[0, s]
    grp = lambda s, t: t[1, s]
    state_spec = lambda w: pl.BlockSpec((1, SUB, w), lambda s, t: (grp(s, t), 0, 0))
    const = lambda shape: pl.BlockSpec(shape, lambda s, t: (0,) * len(shape))
    grid_spec = pltpu.PrefetchScalarGridSpec(
        num_scalar_prefetch=1,
        grid=(n_steps,),
        in_specs=[
            pl.BlockSpec((R_CHUNK, LRU_W), lambda s, t: (blk(s, t), 0)),
            pl.BlockSpec((2 * SUB, LRU_W), lambda s, t: (jnp.maximum(blk(s, t) * (T_CHUNK // 2) - 1, 0), 0)),
            pl.BlockSpec((SUB, LRU_W), lambda s, t: (jnp.minimum((blk(s, t) + 1) * T_CHUNK, N // SUB - 1), 0)),
            pl.BlockSpec((R_CHUNK, S5_W), lambda s, t: (blk(s, t), 2)),
            state_spec(LRU_W), state_spec(S5_N), state_spec(S5_N),
            const((CONV_W, LRU_W)), const((1, LRU_W)),
            const((LRU_W, 2 * LRU_W)), const((1, 2 * LRU_W)), const((1, LRU_W)),
            const((2, 256, S5_N)), const((1, S5_N)), const((1, S5_N)), const((2, S5_N, 256)),
        ],
        out_specs=[
            pl.BlockSpec((R_CHUNK, LRU_W), lambda s, t: (blk(s, t), 0)),
            pl.BlockSpec((R_CHUNK, S5_W), lambda s, t: (blk(s, t), 0)),
            state_spec(LRU_W), state_spec(S5_N), state_spec(S5_N),
        ],
        scratch_shapes=[
            pltpu.VMEM((R_CHUNK + 3 * SUB, LRU_W), F32),
            pltpu.VMEM((R_CHUNK, LRU_W), F32), pltpu.VMEM((R_CHUNK, LRU_W), F32),
            pltpu.VMEM((R_CHUNK, S5_N), F32), pltpu.VMEM((R_CHUNK, S5_N), F32),
            pltpu.VMEM((SUB, LRU_W), F32), pltpu.VMEM((SUB, S5_N), F32), pltpu.VMEM((SUB, S5_N), F32),
        ],
    )
    return pl.pallas_call(
        functools.partial(_scan_kernel, reverse=reverse),
        grid_spec=grid_spec,
        out_shape=[jax.ShapeDtypeStruct((N, LRU_W), F32), jax.ShapeDtypeStruct((N, S5_W), F32),
                   jax.ShapeDtypeStruct((3, SUB, LRU_W), F32),
                   jax.ShapeDtypeStruct((3, SUB, S5_N), F32), jax.ShapeDtypeStruct((3, SUB, S5_N), F32)],
        compiler_params=_cparams(("arbitrary",)),
        name="ab_scan_bwd" if reverse else "ab_scan_fwd",
    )(tbl, xz, xz, xz, xz, h0l, h0r, h0i, cw, cb, wg, bg, lam, bm, ar, ai, cm)


def _about_kernel(haf, hab, yf, yr, ga, xb, x_ref, gate, g1, d_ref, wglu, bglu, wout, o_ref):
    ya = (haf[...] + hab[...]) * _gelu(ga[...])
    yb0 = _gelu(yf[...] + yr[...] + d_ref[...] * xb[...])
    yb = yb0 * _sigmoid(_dot(yb0.astype(BF16), wglu[...]) + bglu[...])
    out = _dot(ya.astype(BF16), wout[0:LRU_W]) + _dot(yb.astype(BF16), wout[LRU_W:LRU_W + S5_W])
    o_ref[...] = _gated(x_ref[...], out, g1[...], gate[0])


def _ab_out(haf, hab, yf, yr, xz, x, mods, g1, s5_d, wglu, bglu, wout):
    tm = 512
    grp = _grp_tm(tm)
    half = lambda c: pl.BlockSpec((tm, LRU_W), lambda i: (i, c))
    return pl.pallas_call(
        _about_kernel,
        grid=(N // tm,),
        in_specs=[half(0), half(0), half(0), half(0), half(1), half(2),
                  pl.BlockSpec((tm, D), lambda i: (i, 0)),
                  _mod_spec(2, grp), _full((1, D)), _full((1, S5_W)),
                  _full((S5_W, S5_W)), _full((1, S5_W)), _full((LRU_W + S5_W, D))],
        out_specs=pl.BlockSpec((tm, D), lambda i: (i, 0)),
        out_shape=jax.ShapeDtypeStruct((N, D), F32),
        compiler_params=_cparams(("arbitrary",)),
        name="ab_out",
    )(haf, hab, yf, yr, xz, xz, x, mods, g1, s5_d, wglu, bglu, wout)


def _blocked_swiglu_weights(w_gate_up, w_down):
    lead = w_gate_up.shape[:-2]
    nl = len(lead)
    wgu = w_gate_up.reshape(*lead, D, 2 * N_FF, FF_BLK)
    wgu = jnp.moveaxis(wgu, nl + 1, nl).astype(BF16)
    return wgu, w_down.reshape(*lead, N_FF, FF_BLK, D).astype(BF16)


def _swiglu_resident(hbf, wgu, wd, acc):
    acc[...] = jnp.zeros_like(acc)

    def body(j, carry):
        h = hbf[...]
        g = _dot(h, wgu[j])
        u = _dot(h, wgu[j + N_FF])
        act = (g * _sigmoid(g)) * u
        acc[...] += _dot(act.astype(BF16), wd[j])
        return carry

    lax.fori_loop(0, N_FF, body, 0)


def _ffn_kernel(x_ref, sh, sc, gt, g2, g3, wgu, wd, o_ref, hbf, acc):
    hbf[...] = _adaln(x_ref[...], g2[...], sc[0], sh[0]).astype(BF16)
    _swiglu_resident(hbf, wgu, wd, acc)
    o_ref[...] = _gated(x_ref[...], acc[...], g3[...], gt[0])


def _ffn(x, mods, g2, g3, wgu, wd):
    tm = 1024
    grp = _grp_tm(tm)
    once = lambda shape: pl.BlockSpec(shape, lambda i: (0,) * len(shape), pipeline_mode=pl.Buffered(1))
    return pl.pallas_call(
        _ffn_kernel,
        grid=(N // tm,),
        in_specs=[pl.BlockSpec((tm, D), lambda i: (i, 0)),
                  _mod_spec(3, grp), _mod_spec(4, grp), _mod_spec(5, grp),
                  _full((1, D)), _full((1, D)),
                  once(wgu.shape), once(wd.shape)],
        out_specs=pl.BlockSpec((tm, D), lambda i: (i, 0)),
        out_shape=jax.ShapeDtypeStruct((N, D), F32),
        scratch_shapes=[pltpu.VMEM((tm, D), BF16), pltpu.VMEM((tm, D), F32)],
        compiler_params=_cparams(("arbitrary",)),
        name="ffn",
    )(x, mods, mods, mods, g2, g3, wgu, wd)


def _pair_swap(x):
    outs = []
    for c in range(x.shape[1] // LANE):
        xc = x[:, c * LANE:(c + 1) * LANE]
        even = lax.broadcasted_iota(I32, xc.shape, 1) % 2 == 0
        outs.append(jnp.where(even, pltpu.roll(xc, LANE - 1, 1), pltpu.roll(xc, 1, 1)))
    return outs[0] if len(outs) == 1 else jnp.concatenate(outs, axis=1)


def _mlaproj_kernel(x_ref, g0, sh, sc, w1, gq, gkv, wuq, wukv, cq_ref, sq_ref, ck_ref, sk_ref,
                    qn_ref, qr_ref, kn_ref, v_ref, kr2_ref, ckv_ref, krr_ref):
    h = _adaln(x_ref[...], g0[...], sc[0], sh[0])
    dn = _dot(h.astype(BF16), w1[...])
    cq = _rms(dn[:, :Q_LORA], gq[...])
    ckv = _rms(dn[:, Q_LORA:Q_LORA + KV_LORA], gkv[...])
    krp = dn[:, Q_LORA + KV_LORA:]
    ckv_ref[...] = ckv
    krr_ref[...] = krp
    q = _dot(cq.astype(BF16), wuq[...])
    qn_ref[...] = q[:, :HEADS * QK_NOPE].astype(BF16)
    qr = q[:, HEADS * QK_NOPE:]
    qr_ref[...] = (qr * cq_ref[...] + _pair_swap(qr) * sq_ref[...]).astype(BF16)
    kv = _dot(ckv.astype(BF16), wukv[...])
    kn_ref[...] = kv[:, :HEADS * QK_NOPE].astype(BF16)
    v_ref[...] = kv[:, HEADS * QK_NOPE:].astype(BF16)
    kr = krp * ck_ref[...] + _pair_swap(krp) * sk_ref[...]
    kr2_ref[...] = jnp.concatenate([kr, pltpu.roll(kr, QK_ROPE, 1)], axis=1).astype(BF16)


def _mla_proj(x, g0, mods, w1, gq, gkv, wuq, wukv, cos_q, sin_q, cos_k, sin_k):
    tm = 256
    grp = _grp_bm(tm)
    n_pos = DEC_SEQ // tm
    tab = lambda w: pl.BlockSpec((tm, w), lambda i: (jnp.where(i * tm < NP, n_pos, (i - NP // tm) % n_pos), 0))
    row = lambda w: pl.BlockSpec((tm, w), lambda i: (i, 0))
    shp = lambda w, dt: jax.ShapeDtypeStruct((N, w), dt)
    return pl.pallas_call(
        _mlaproj_kernel,
        grid=(N // tm,),
        in_specs=[row(D), _full((1, D)), _mod_spec(0, grp), _mod_spec(1, grp),
                  _full(w1.shape), _full((1, Q_LORA)), _full((1, KV_LORA)),
                  _full(wuq.shape), _full(wukv.shape),
                  tab(HEADS * QK_ROPE), tab(HEADS * QK_ROPE), tab(LANE), tab(LANE)],
        out_specs=[row(HEADS * QK_NOPE), row(HEADS * QK_ROPE), row(HEADS * QK_NOPE), row(HEADS * V_DIM),
                   row(2 * LANE), row(KV_LORA), row(LANE)],
        out_shape=[shp(HEADS * QK_NOPE, BF16), shp(HEADS * QK_ROPE, BF16), shp(HEADS * QK_NOPE, BF16),
                   shp(HEADS * V_DIM, BF16), shp(2 * LANE, BF16), shp(KV_LORA, F32), shp(LANE, F32)],
        compiler_params=_cparams(("arbitrary",)),
        name="mla_proj",
    )(x, g0, mods, mods, w1, gq, gkv, wuq, wukv, cos_q, sin_q, cos_k, sin_k)


def _cachekv_kernel(c_ref, w_ref, kn_ref, v_ref):
    kv = _dot(c_ref[...].astype(BF16), w_ref[...])
    kn_ref[...] = kv[:, :HEADS * QK_NOPE].astype(BF16)
    v_ref[...] = kv[:, HEADS * QK_NOPE:].astype(BF16)


def _cache_kv(ckv_cache, wukv):
    rows = ckv_cache.shape[0]
    tm = 512
    return pl.pallas_call(
        _cachekv_kernel,
        grid=(rows // tm,),
        in_specs=[pl.BlockSpec((tm, KV_LORA), lambda i: (i, 0)), _full(wukv.shape)],
        out_specs=[pl.BlockSpec((tm, HEADS * QK_NOPE), lambda i: (i, 0)),
                   pl.BlockSpec((tm, HEADS * V_DIM), lambda i: (i, 0))],
        out_shape=[jax.ShapeDtypeStruct((rows, HEADS * QK_NOPE), BF16),
                   jax.ShapeDtypeStruct((rows, HEADS * V_DIM), BF16)],
        compiler_params=_cparams(("arbitrary",)),
        name="cache_kv",
    )(ckv_cache, wukv)


def _attn_kernel(*refs, has_cache):
    if has_cache:
        qn, qr, kn, kr, v, knc, krc, vc, o_ref = refs
    else:
        qn, qr, kn, kr, v, o_ref = refs
    scale = (QK_NOPE + QK_ROPE) ** -0.5
    q = jnp.concatenate([qn[...], qr[...]], axis=1)
    k = jnp.concatenate([kn[...], kr[...]], axis=1)
    s = lax.dot_general(q, k, NT_DIMS, preferred_element_type=F32) * scale
    m = jnp.max(s, axis=-1, keepdims=True)
    if has_cache:
        kc = jnp.concatenate([knc[...], krc[...]], axis=1)
        sc = lax.dot_general(q, kc, NT_DIMS, preferred_element_type=F32) * scale
        m = jnp.maximum(m, jnp.max(sc, axis=-1, keepdims=True))
        pc = jnp.exp(sc - m)
    p = jnp.exp(s - m)
    den = jnp.sum(p, axis=-1, keepdims=True)
    o = _dot(p.astype(BF16), v[...])
    if has_cache:
        den = den + jnp.sum(pc, axis=-1, keepdims=True)
        o = o + _dot(pc.astype(BF16), vc[...])
    o_ref[...] = (o / den).astype(BF16)


def _attention(qn, qr, kn, kr2, v, *, row0, n_seq, seq, tq, cache=None):
    nq = seq // tq
    qblk = lambda b, h, i: row0 // tq + b * nq + i
    kblk = lambda b: row0 // seq + b
    in_specs = [pl.BlockSpec((tq, LANE), lambda b, h, i: (qblk(b, h, i), h)),
                pl.BlockSpec((tq, LANE), lambda b, h, i: (qblk(b, h, i), h // 2)),
                pl.BlockSpec((seq, LANE), lambda b, h, i: (kblk(b), h)),
                pl.BlockSpec((seq, LANE), lambda b, h, i: (kblk(b), h % 2)),
                pl.BlockSpec((seq, LANE), lambda b, h, i: (kblk(b), h))]
    args = [qn, qr, kn, kr2, v]
    if cache is not None:
        knc, kr2c, vc = cache
        in_specs += [pl.BlockSpec((PAST_LEN, LANE), lambda b, h, i: (b, h)),
                     pl.BlockSpec((PAST_LEN, LANE), lambda b, h, i: (b, h % 2)),
                     pl.BlockSpec((PAST_LEN, LANE), lambda b, h, i: (b, h))]
        args += [knc, kr2c, vc]
    return pl.pallas_call(
        functools.partial(_attn_kernel, has_cache=cache is not None),
        grid=(n_seq, HEADS, nq),
        in_specs=in_specs,
        out_specs=pl.BlockSpec((tq, LANE), lambda b, h, i: (b * nq + i, h)),
        out_shape=jax.ShapeDtypeStruct((n_seq * seq, HEADS * V_DIM), BF16),
        compiler_params=_cparams(("arbitrary", "arbitrary", "arbitrary")),
        name="attn_latent" if cache is not None else "attn_context",
    )(*args)


def _router_kernel(op_ref, os_ref, x_ref, gate1, g1, sh2, sc2, g2, wout, wr_hi, wr_lo, tri,
                   x3_ref, h_ref, ri_ref, rf_ref, cnt_ref, carry, *, prompt_steps):
    step = pl.program_id(0)

    @pl.when(step == 0)
    def _():
        carry[...] = jnp.zeros_like(carry)

    o = jnp.where(step < prompt_steps, op_ref[...], os_ref[...])
    x3 = _gated(x_ref[...], _dot(o, wout[...]), g1[...], gate1[0])
    x3_ref[...] = x3
    h = _adaln(x3, g2[...], sc2[0], sh2[0])
    _store_row_tiles(h_ref, lambda cols: h[:, cols])
    h_hi = h.astype(BF16)
    h_lo = (h - h_hi.astype(F32)).astype(BF16)
    dg = lambda a, b: lax.dot_general(a, b, NT_DIMS, preferred_element_type=F32)
    lg = dg(wr_hi[...], h_hi) + dg(wr_hi[...], h_lo) + dg(wr_lo[...], h_hi)
    eidx = lax.broadcasted_iota(I32, lg.shape, 0).astype(F32)
    m1 = jnp.max(lg, axis=0, keepdims=True)
    i1 = jnp.min(jnp.where(lg == m1, eidx, float(N_EXP)), axis=0, keepdims=True)
    sel1 = eidx == i1
    lg2 = jnp.where(sel1, -jnp.inf, lg)
    m2 = jnp.max(lg2, axis=0, keepdims=True)
    i2 = jnp.min(jnp.where(lg2 == m2, eidx, float(N_EXP)), axis=0, keepdims=True)
    sel2 = eidx == i2
    e = jnp.exp(m2 - m1)
    w1 = 1.0 / (1.0 + e)
    w2 = e / (1.0 + e)
    picked = jnp.where(sel1 | sel2, 1.0, 0.0)
    rank = _dot(picked.astype(BF16), tri[...]) + carry[:, 0:1]
    r1 = jnp.sum(jnp.where(sel1, rank, 0.0), axis=0, keepdims=True)
    r2 = jnp.sum(jnp.where(sel2, rank, 0.0), axis=0, keepdims=True)
    carry[...] = carry[...] + jnp.sum(picked, axis=1, keepdims=True)
    cnt_ref[...] = carry[...]
    ri_ref[...] = jnp.where(eidx == 0.0, i1, jnp.where(eidx == 1.0, i2, jnp.where(eidx == 2.0, r1, r2))).astype(I32)
    rf_ref[...] = jnp.where(eidx == 0.0, w1, w2)


def _attn_out_router(o_p, o_s, x, mods, g1, g2, wout, wr_hi, wr_lo):
    tm = 512
    grp = _grp_bm(tm)
    n_p = NP // tm
    o_specs = [pl.BlockSpec((tm, HEADS * V_DIM), lambda i: (jnp.minimum(i, n_p - 1), 0)),
               pl.BlockSpec((tm, HEADS * V_DIM), lambda i: (jnp.maximum(i - n_p, 0), 0))]
    tri = jnp.asarray(np.triu(np.ones((tm, tm), np.float32), 1), BF16)
    row = lambda w: pl.BlockSpec((tm, w), lambda i: (i, 0))
    col = pl.BlockSpec((N_EXP, tm), lambda i: (0, i))
    return pl.pallas_call(
        functools.partial(_router_kernel, prompt_steps=n_p),
        grid=(N // tm,),
        in_specs=o_specs + [row(D), _mod_spec(2, grp), _full((1, D)),
                  _mod_spec(3, grp), _mod_spec(4, grp), _full((1, D)),
                  _full((HEADS * V_DIM, D)), _full((N_EXP, D)), _full((N_EXP, D)), _full((tm, tm))],
        out_specs=[row(D), pl.BlockSpec((tm * ROW_TILE, LANE), lambda i: (i, 0)), col, col, _full((N_EXP, LANE))],
        out_shape=[jax.ShapeDtypeStruct((N, D), F32), jax.ShapeDtypeStruct((N * ROW_TILE, LANE), F32),
                   jax.ShapeDtypeStruct((N_EXP, N), I32), jax.ShapeDtypeStruct((N_EXP, N), F32),
                   jax.ShapeDtypeStruct((N_EXP, LANE), F32)],
        scratch_shapes=[pltpu.VMEM((N_EXP, LANE), F32)],
        compiler_params=_cparams(("arbitrary",)),
        name="attn_out_router",
    )(o_p, o_s, x, mods, g1, mods, mods, g2, wout, wr_hi, wr_lo, tri)


def _gather_row_tiles(idx_ref, src_hbm, dst, sem):
    n = dst.shape[0] // ROW_TILE

    def start(r, carry):
        i_src = pl.multiple_of(idx_ref[0, 0, r] * ROW_TILE, ROW_TILE)
        i_dst = pl.multiple_of(r * ROW_TILE, ROW_TILE)
        pltpu.make_async_copy(src_hbm.at[pl.ds(i_src, ROW_TILE), :], dst.at[pl.ds(i_dst, ROW_TILE), :], sem).start()
        return carry

    lax.fori_loop(0, n, start, 0, unroll=8)
    pltpu.make_async_copy(src_hbm.at[pl.ds(0, n * ROW_TILE), :], dst, sem).wait()


def _dispatch_kernel(src_ref, h_hbm, o_ref, sem):
    _gather_row_tiles(src_ref, h_hbm, o_ref, sem)


def _dispatch(src, h):
    tm = TM_E
    return pl.pallas_call(
        _dispatch_kernel,
        grid=(N_TILES,),
        in_specs=[pl.BlockSpec((1, 1, tm), lambda i: (i, 0, 0), memory_space=pltpu.SMEM),
                  pl.BlockSpec(memory_space=pl.ANY)],
        out_specs=pl.BlockSpec((tm * ROW_TILE, LANE), lambda i: (i, 0)),
        out_shape=jax.ShapeDtypeStruct((P_ROWS * ROW_TILE, LANE), F32),
        scratch_shapes=[pltpu.SemaphoreType.DMA],
        compiler_params=_cparams(("arbitrary",)),
        name="moe_dispatch",
    )(src.reshape(N_TILES, 1, tm), h)


def _moe_kernel(te, nu, hs_ref, wgu, wd, y_ref, hbf, acc):
    i = pl.program_id(0)

    @pl.when(i < nu[0])
    def _():
        for c in range(ROW_TILE):
            hbf[:, c * LANE:(c + 1) * LANE] = _load_row_tiles(hs_ref, c, TM_E).astype(BF16)
        _swiglu_resident(hbf, wgu.at[0], wd.at[0], acc)
        _store_row_tiles(y_ref, lambda cols: acc[:, cols])

    @pl.when(i >= nu[0])
    def _():
        y_ref[...] = jnp.zeros_like(y_ref)


def _moe_experts(tile_expert, n_used, hs, wgu, wd):
    tm = TM_E
    grid_spec = pltpu.PrefetchScalarGridSpec(
        num_scalar_prefetch=2,
        grid=(N_TILES,),
        in_specs=[pl.BlockSpec((tm * ROW_TILE, LANE), lambda i, te, nu: (i, 0)),
                  pl.BlockSpec((1,) + wgu.shape[1:], lambda i, te, nu: (te[i], 0, 0, 0)),
                  pl.BlockSpec((1,) + wd.shape[1:], lambda i, te, nu: (te[i], 0, 0, 0))],
        out_specs=pl.BlockSpec((tm * ROW_TILE, LANE), lambda i, te, nu: (i, 0)),
        scratch_shapes=[pltpu.VMEM((tm, D), BF16), pltpu.VMEM((tm, D), F32)],
    )
    return pl.pallas_call(
        _moe_kernel,
        grid_spec=grid_spec,
        out_shape=jax.ShapeDtypeStruct((P_ROWS * ROW_TILE, LANE), F32),
        compiler_params=_cparams(("arbitrary",)),
        name="moe_experts",
    )(tile_expert, n_used, hs, wgu, wd)


def _combine_kernel(p1_ref, p2_ref, y_hbm, x_ref, w1_ref, w2_ref, gate2, g3, o_ref, b1, b2, sem):
    _gather_row_tiles(p1_ref, y_hbm, b1, sem.at[0])
    _gather_row_tiles(p2_ref, y_hbm, b2, sem.at[1])
    rows = x_ref.shape[0]
    w1 = w1_ref[...]
    w2 = w2_ref[...]
    f = jnp.concatenate([w1 * _load_row_tiles(b1, c, rows) + w2 * _load_row_tiles(b2, c, rows)
                         for c in range(ROW_TILE)], axis=1)
    o_ref[...] = _gated(x_ref[...], f, g3[...], gate2[0])


def _moe_combine(pos1, pos2, y, x, w1, w2, mods, g3):
    tm = 256
    grp = _grp_bm(tm)
    nb = N // tm
    smem = pl.BlockSpec((1, 1, tm), lambda i: (i, 0, 0), memory_space=pltpu.SMEM)
    return pl.pallas_call(
        _combine_kernel,
        grid=(nb,),
        in_specs=[smem, smem, pl.BlockSpec(memory_space=pl.ANY),
                  pl.BlockSpec((tm, D), lambda i: (i, 0)),
                  pl.BlockSpec((tm, 1), lambda i: (i, 0)), pl.BlockSpec((tm, 1), lambda i: (i, 0)),
                  _mod_spec(5, grp), _full((1, D))],
        out_specs=pl.BlockSpec((tm, D), lambda i: (i, 0)),
        out_shape=jax.ShapeDtypeStruct((N, D), F32),
        scratch_shapes=[pltpu.VMEM((tm * ROW_TILE, LANE), F32), pltpu.VMEM((tm * ROW_TILE, LANE), F32),
                        pltpu.SemaphoreType.DMA((2,))],
        compiler_params=_cparams(("arbitrary",)),
        name="moe_combine",
    )(pos1.reshape(nb, 1, tm), pos2.reshape(nb, 1, tm), y, x, w1, w2, mods, g3)


def _block_diag(w):
    hh, a, b = w.shape
    eye = jnp.eye(hh, dtype=w.dtype)
    return jnp.einsum('hab,hk->hakb', w, eye).reshape(hh * a, hh * b)


def _s5_matrices(a_re, a_im, log_dt, b_re, b_im, c_re, c_im):
    dt = jnp.exp(log_dt)[:, None]
    mag = jnp.exp(a_re * dt)
    abr = mag * jnp.cos(a_im * dt)
    abi = mag * jnp.sin(a_im * dt)
    den = a_re * a_re + a_im * a_im
    cr = ((abr - 1.0) * a_re + abi * a_im) / den
    ci = (abi * a_re - (abr - 1.0) * a_im) / den
    bbr = cr[..., None] * b_re - ci[..., None] * b_im
    bbi = cr[..., None] * b_im + ci[..., None] * b_re
    hg = S5_G // 2
    eye = jnp.eye(hg, dtype=F32)
    bms, cms = [], []
    for j in range(2):
        sl = slice(j * hg, (j + 1) * hg)
        bd = lambda m: jnp.einsum('gpc,gh->gchp', m[sl], eye).reshape(hg * S5_CH, hg * S5_P)
        bms.append(jnp.concatenate([bd(bbr), bd(bbi)], axis=1))
        cd = lambda m: jnp.einsum('gcp,gh->gphc', m[sl], eye).reshape(hg * S5_P, hg * S5_CH)
        cms.append(jnp.concatenate([cd(c_re), cd(-c_im)], axis=0))
    return (jnp.stack(bms).astype(BF16), abr.reshape(1, S5_N), abi.reshape(1, S5_N),
            jnp.stack(cms).astype(BF16))


def _rope_tables(tm):
    rows = DEC_SEQ // GRID_W
    row = jnp.repeat(jnp.arange(rows, dtype=F32), GRID_W)
    col = jnp.tile(jnp.arange(GRID_W, dtype=F32), rows)
    nf = QK_ROPE // 4
    inv = ROPE_THETA ** (-jnp.arange(nf, dtype=F32) / nf)
    ang = jnp.concatenate([row[:, None] * inv, col[:, None] * inv], axis=-1)
    cos = jnp.repeat(jnp.cos(ang), 2, axis=-1)
    sin = jnp.stack([-jnp.sin(ang), jnp.sin(ang)], axis=-1).reshape(DEC_SEQ, QK_ROPE)
    ident = lambda t, one: jnp.concatenate([t, jnp.full((tm, t.shape[1]), one, F32)], axis=0)
    cos_q = ident(jnp.tile(cos, (1, HEADS)), 1.0)
    sin_q = ident(jnp.tile(sin, (1, HEADS)), 0.0)
    pad = lambda t, one: jnp.concatenate([t, jnp.full((DEC_SEQ, LANE - QK_ROPE), one, F32)], axis=1)
    return cos_q, sin_q, ident(pad(cos, 1.0), 1.0), ident(pad(sin, 0.0), 0.0)


def _to_time_major(xp, xs):
    p = xp.reshape(BATCH // SUB, SUB, SEQ, -1).transpose(0, 2, 1, 3).reshape(NP, -1)
    s = xs.transpose(1, 0, 2).reshape(NS, -1)
    return jnp.concatenate([p, s], axis=0)


def _to_batch_major(x):
    p = x[:NP].reshape(BATCH // SUB, SEQ, SUB, -1).transpose(0, 2, 1, 3).reshape(NP, -1)
    s = x[NP:].reshape(DEC_SEQ, DEC_BATCH, -1).transpose(1, 0, 2).reshape(NS, -1)
    return jnp.concatenate([p, s], axis=0)


def _group_states(prompt_state, sample_state):
    w = sample_state.shape[-1]
    return jnp.concatenate([prompt_state.reshape(2, SUB, w), sample_state.reshape(1, SUB, w)], axis=0)


def _layer_ab(x, m, ng, j, state_lru, state_s5_re, state_s5_im, p):
    xz = _ab_inproj(x, ng[0:1], m, p['ab_w_in'][j].astype(BF16))
    zeros = lambda w: jnp.zeros((BATCH, w), F32)
    outs = []
    for d in range(2):
        wg = jnp.concatenate([_block_diag(p['lru_wa'][j, d]), _block_diag(p['lru_wx'][j, d])], axis=1).astype(BF16)
        bg = jnp.concatenate([p['lru_ba'][j, d], p['lru_bx'][j, d]])[None]
        bm, ar, ai, cm = _s5_matrices(p['s5_a_re'][j, d], p['s5_a_im'][j, d], p['s5_log_dt'][j, d],
                                      p['s5_b_re'][j, d], p['s5_b_im'][j, d], p['s5_c_re'][j, d], p['s5_c_im'][j, d])
        h0l = _group_states(zeros(LRU_W), state_lru[:, j, d])
        h0r = _group_states(zeros(S5_N), state_s5_re[:, j, d].reshape(DEC_BATCH, S5_N))
        h0i = _group_states(zeros(S5_N), state_s5_im[:, j, d].reshape(DEC_BATCH, S5_N))
        outs.append(_ab_scan(xz, h0l, h0r, h0i, p['ab_conv_w'][j], p['ab_conv_b'][j][None], wg, bg,
                             p['lru_lambda'][j, d][None], bm, ar, ai, cm, reverse=(d == 1)))
    (haf, yf, llf, lrf, lif), (hab, yr, llb, lrb, lib) = outs
    x = _ab_out(haf, hab, yf, yr, xz, x, m, ng[1:2], p['s5_d'][j][None], p['s5_w_glu'][j].astype(BF16),
                p['s5_b_glu'][j][None], p['ab_w_out'][j].astype(BF16))
    x = _ffn(x, m, ng[2:3], ng[3:4], *_blocked_swiglu_weights(p['ffn_w_gate_up'][j], p['ffn_w_down'][j]))
    prompt = lambda f, b, w: jnp.stack([f[:2].reshape(BATCH, w), b[:2].reshape(BATCH, w)], axis=1)
    lru = prompt(llf, llb, LRU_W)
    s5r = prompt(lrf, lrb, S5_N).reshape(BATCH, 2, S5_G, S5_P)
    s5i = prompt(lif, lib, S5_N).reshape(BATCH, 2, S5_G, S5_P)
    return x, lru, s5r, s5i


def _head_major(w, parts):
    k = w.shape[0]
    per_head = w.reshape(k, HEADS, -1)
    out, start = [], 0
    for width in parts:
        out.append(per_head[:, :, start:start + width].reshape(k, HEADS * width))
        start += width
    return jnp.concatenate(out, axis=1)


def _layer_mla_moe(x, m, ng, j, cache_kv_latent, cache_k_rope, p):
    w1 = jnp.concatenate([p['mla_w_in'][j], jnp.zeros((D, LANE - QK_ROPE), F32)], axis=1).astype(BF16)
    wuq = _head_major(p['mla_w_uq'][j], (QK_NOPE, QK_ROPE)).astype(BF16)
    wukv = _head_major(p['mla_w_ukv'][j], (QK_NOPE, V_DIM)).astype(BF16)
    tables = _rope_tables(256)
    qn, qr, kn, v, kr2, ckv, krr = _mla_proj(x, ng[0:1], m, w1, p['mla_g_q'][j][None], p['mla_g_kv'][j][None],
                                              wuq, wukv, *tables)
    knc, vc = _cache_kv(cache_kv_latent[:, j].reshape(DEC_BATCH * PAST_LEN, KV_LORA), wukv)
    krc = cache_k_rope[:, j].reshape(DEC_BATCH * PAST_LEN, QK_ROPE)
    z = jnp.zeros_like(krc)
    kr2c = jnp.concatenate([krc, z, z, krc], axis=1).astype(BF16)
    o_p = _attention(qn, qr, kn, kr2, v, row0=0, n_seq=BATCH, seq=SEQ, tq=SEQ)
    o_s = _attention(qn, qr, kn, kr2, v, row0=NP, n_seq=DEC_BATCH, seq=DEC_SEQ, tq=512, cache=(knc, kr2c, vc))
    wr_t = p['moe_w_router'][j].T
    wr_hi = wr_t.astype(BF16)
    wr_lo = (wr_t - wr_hi.astype(F32)).astype(BF16)
    x3, h, ri, rf, cnt = _attn_out_router(o_p, o_s, x, m, ng[1:2], ng[2:3], p['mla_w_out'][j].astype(BF16), wr_hi, wr_lo)
    counts = cnt[:, 0].astype(I32)
    padded = ((counts + TM_E - 1) // TM_E) * TM_E
    ends = jnp.cumsum(padded)
    offs = ends - padded
    pos1 = offs[ri[0]] + ri[2]
    pos2 = offs[ri[1]] + ri[3]
    tok = jnp.arange(N, dtype=I32)
    src = jnp.zeros((P_ROWS,), I32).at[pos1].set(tok).at[pos2].set(tok)
    n_used = (ends[-1] // TM_E).astype(I32)[None]
    tile_row = jnp.minimum(jnp.arange(N_TILES, dtype=I32), n_used - 1) * TM_E
    tile_expert = jnp.sum((tile_row[:, None] >= ends[None, :]).astype(I32), axis=1)
    hs = _dispatch(src, h)
    y = _moe_experts(tile_expert, n_used, hs, *_blocked_swiglu_weights(p['moe_w_gate_up'][j], p['moe_w_down'][j]))
    x4 = _moe_combine(pos1, pos2, y, x3, rf[0][:, None], rf[1][:, None], m, ng[3:4])
    kv_new = ckv[:NP].reshape(BATCH, SEQ, KV_LORA)
    kr_new = krr[:NP, :QK_ROPE].reshape(BATCH, SEQ, QK_ROPE)
    return x4, kv_new, kr_new


def kernel(x_prompt, x_sample, c, state_lru, state_s5_re, state_s5_im, cache_kv_latent, cache_k_rope, c_ctx, w_mod, b_mod, norm_gains, ab_w_in, ab_conv_w, ab_conv_b, lru_wa, lru_ba, lru_wx, lru_bx, lru_lambda, s5_a_re, s5_a_im, s5_log_dt, s5_b_re, s5_b_im, s5_c_re, s5_c_im, s5_d, s5_w_glu, s5_b_glu, ab_w_out, ffn_w_gate_up, ffn_w_down, mla_w_in, mla_g_q, mla_g_kv, mla_w_uq, mla_w_ukv, mla_w_out, moe_w_router, moe_w_gate_up, moe_w_down):
    p = dict(ab_w_in=ab_w_in, ab_conv_w=ab_conv_w, ab_conv_b=ab_conv_b, lru_wa=lru_wa, lru_ba=lru_ba,
             lru_wx=lru_wx, lru_bx=lru_bx, lru_lambda=lru_lambda, s5_a_re=s5_a_re, s5_a_im=s5_a_im,
             s5_log_dt=s5_log_dt, s5_b_re=s5_b_re, s5_b_im=s5_b_im, s5_c_re=s5_c_re, s5_c_im=s5_c_im,
             s5_d=s5_d, s5_w_glu=s5_w_glu, s5_b_glu=s5_b_glu, ab_w_out=ab_w_out, ffn_w_gate_up=ffn_w_gate_up,
             ffn_w_down=ffn_w_down, mla_w_in=mla_w_in, mla_g_q=mla_g_q, mla_g_kv=mla_g_kv, mla_w_uq=mla_w_uq,
             mla_w_ukv=mla_w_ukv, mla_w_out=mla_w_out, moe_w_router=moe_w_router, moe_w_gate_up=moe_w_gate_up,
             moe_w_down=moe_w_down)
    depth = w_mod.shape[0]
    cond = jnp.concatenate([c_ctx[None], c, jnp.zeros((2 * SUB - 1 - DEC_BATCH, D), F32)], axis=0)
    mod = _modulation(cond, w_mod, b_mod)
    ctx_tile = lambda l: jnp.broadcast_to(mod[l, 0:1], (SUB, 6 * D))
    x = _to_time_major(x_prompt, x_sample)
    time_major = True
    lru_l, s5r_l, s5i_l, kv_l, kr_l = [], [], [], [], []
    for layer in range(depth):
        j = layer // 2
        ng = norm_gains[layer]
        if layer % 2 == 0:
            if not time_major:
                x = _to_time_major(x[:NP].reshape(BATCH, SEQ, D), x[NP:].reshape(DEC_BATCH, DEC_SEQ, D))
                time_major = True
            m = jnp.stack([ctx_tile(layer), mod[layer, 1:1 + DEC_BATCH]])
            x, lru, s5r, s5i = _layer_ab(x, m, ng, j, state_lru, state_s5_re, state_s5_im, p)
            lru_l.append(lru)
            s5r_l.append(s5r)
            s5i_l.append(s5i)
        else:
            if time_major:
                x = _to_batch_major(x)
                time_major = False
            lat = jnp.broadcast_to(mod[layer, 1:1 + DEC_BATCH, None, :], (DEC_BATCH, SUB, 6 * D))
            m = jnp.concatenate([ctx_tile(layer)[None], lat], axis=0)
            x, kv_new, kr_new = _layer_mla_moe(x, m, ng, j, cache_kv_latent, cache_k_rope, p)
            kv_l.append(kv_new)
            kr_l.append(kr_new)
    if time_major:
        x = _to_batch_major(x)
    return (x[:NP].reshape(BATCH, SEQ, D), x[NP:].reshape(DEC_BATCH, DEC_SEQ, D),
            jnp.stack(lru_l, axis=1), jnp.stack(s5r_l, axis=1), jnp.stack(s5i_l, axis=1),
            jnp.stack(kv_l, axis=1), jnp.stack(kr_l, axis=1))
```

```python
import functools
import math

import numpy as np
import jax
import jax.numpy as jnp
from jax import lax
from jax.experimental import pallas as pl
from jax.experimental.pallas import tpu as pltpu

F32 = jnp.float32
BF16 = jnp.bfloat16
I32 = jnp.int32

D = 1024
BATCH, SEQ = 16, 256
DEC_BATCH, DEC_SEQ = 8, 2048
PAST_LEN = 256
GRID_W = 64
LRU_W = 512
LRU_HEADS = 8
LRU_C = 8.0
CONV_W = 4
S5_W = 512
S5_CH = 16
S5_G = 32
S5_P = 64
S5_N = S5_G * S5_P
HEADS = 8
QK_NOPE, QK_ROPE, V_DIM = 128, 64, 128
Q_LORA, KV_LORA = 384, 256
ROPE_THETA = 10000.0
D_FF = 2816
N_EXP = 8
EPS = 1e-6

NP = BATCH * SEQ
NS = DEC_BATCH * DEC_SEQ
N = NP + NS
SUB = 8
LANE = 128
ROW_TILE = D // LANE
T_CHUNK = 32
R_CHUNK = T_CHUNK * SUB
FF_BLK = 256
N_FF = D_FF // FF_BLK
TM_E = 512
P_ROWS = 2 * N + N_EXP * TM_E
N_TILES = P_ROWS // TM_E
DUMP_ROWS = P_ROWS - 2 * N + TM_E
VMEM_LIMIT = 56 * 1024 * 1024

NT_DIMS = (((1,), (1,)), ((), ()))


def _cparams(sem):
    return pltpu.CompilerParams(dimension_semantics=sem, vmem_limit_bytes=VMEM_LIMIT)


def _dot(a, b):
    return jnp.dot(a, b, preferred_element_type=F32)


def _sigmoid(x):
    return 1.0 / (1.0 + jnp.exp(-x))


def _neg_expm1(z):
    series = -z * (1.0 + z / 2.0 * (1.0 + z / 3.0 * (1.0 + z / 4.0 * (1.0 + z / 5.0 * (1.0 + z / 6.0 * (1.0 + z / 7.0))))))
    return jnp.where(z > -0.1, series, 1.0 - jnp.exp(z))


def _gelu(x):
    return x * (0.5 * (1.0 + jnp.tanh(math.sqrt(2.0 / math.pi) * (x + 0.044715 * (x * x * x)))))


def _rms(x, g):
    ms = jnp.mean(x * x, axis=-1, keepdims=True)
    return x * lax.rsqrt(ms + EPS) * g


def _rows8(y, fn):
    r, c = y.shape
    return fn(y.reshape(r // SUB, SUB, c)).reshape(r, c)


def _adaln(x, g, scale, shift):
    return _rows8(_rms(x, g), lambda y: y * (1.0 + scale)[None] + shift[None])


def _gated(x, y, g, gate):
    return x + _rows8(_rms(y, g), lambda z: z * gate[None])


def _store_row_tiles(ref, piece):
    rows = ref.shape[0] // ROW_TILE
    for c in range(ROW_TILE):
        ref[pl.ds(c, rows, stride=ROW_TILE), :] = piece(slice(c * LANE, (c + 1) * LANE))


def _load_row_tiles(ref, c, rows):
    return ref[pl.ds(c, rows, stride=ROW_TILE), :]


def _full(shape):
    nd = len(shape)
    return pl.BlockSpec(shape, lambda *_: (0,) * nd)


def _mod_spec(k, grp):
    return pl.BlockSpec((1, SUB, D), lambda i, *_: (grp(i), 0, k))


def _grp_tm(tm):
    return lambda i: (i * tm >= NP).astype(I32)


def _grp_bm(tm):
    return lambda i: jnp.where(i * tm < NP, 0, 1 + (i * tm - NP) // DEC_SEQ)


def _mod_kernel(c_ref, w_ref, b_ref, o_ref):
    c = c_ref[...]
    s = c * _sigmoid(c)
    o_ref[0] = _dot(s.astype(BF16), w_ref[0].astype(BF16)) + b_ref[0]


def _modulation(cond, w_mod, b_mod):
    depth = w_mod.shape[0]
    rows = cond.shape[0]
    return pl.pallas_call(
        _mod_kernel,
        grid=(depth, 6),
        in_specs=[_full((rows, D)),
                  pl.BlockSpec((1, D, D), lambda l, j: (l, 0, j)),
                  pl.BlockSpec((1, 1, D), lambda l, j: (l, 0, j))],
        out_specs=pl.BlockSpec((1, rows, D), lambda l, j: (l, 0, j)),
        out_shape=jax.ShapeDtypeStruct((depth, rows, 6 * D), F32),
        compiler_params=_cparams(("arbitrary", "arbitrary")),
        name="modulation",
    )(cond, w_mod, b_mod.reshape(depth, 1, 6 * D))


def _inproj_kernel(x_ref, g_ref, sh_ref, sc_ref, w_ref, o_ref):
    h = _adaln(x_ref[...], g_ref[...], sc_ref[0], sh_ref[0])
    o_ref[...] = _dot(h.astype(BF16), w_ref[...])


def _ab_inproj(x, gain, mods, w_in):
    tm = 512
    nout = w_in.shape[1]
    grp = _grp_tm(tm)
    return pl.pallas_call(
        _inproj_kernel,
        grid=(N // tm,),
        in_specs=[pl.BlockSpec((tm, D), lambda i: (i, 0)),
                  _full((1, D)),
                  _mod_spec(0, grp), _mod_spec(1, grp),
                  _full((D, nout))],
        out_specs=pl.BlockSpec((tm, nout), lambda i: (i, 0)),
        out_shape=jax.ShapeDtypeStruct((N, nout), F32),
        compiler_params=_cparams(("arbitrary",)),
        name="ab_inproj",
    )(x, gain, mods, mods, w_in)


def _scan_table(reverse):
    cols = []
    groups = [(0, SEQ // T_CHUNK, 0), (1, SEQ // T_CHUNK, SEQ // T_CHUNK),
              (2, DEC_SEQ // T_CHUNK, NP // R_CHUNK)]
    for g, nc, base in groups:
        order = range(nc - 1, -1, -1) if reverse else range(nc)
        for k, c in enumerate(order):
            cols.append((base + c, g, int(k == 0), int(c > 0), int(c < nc - 1)))
    return np.asarray(cols, np.int32).T.copy()


def _scan_kernel(tbl, xa_ref, xp_ref, xn_ref, xb_ref, h0l_ref, h0r_ref, h0i_ref,
                 cw_ref, cb_ref, wg_ref, bg_ref, lam_ref, bm_ref, ar_ref, ai_ref, cm_ref,
                 ha_ref, y_ref, ll_ref, lr_ref, li_ref,
                 ext, abuf, bbuf, hre, him, hl, sre, sim, *, reverse):
    s = pl.program_id(0)

    @pl.when(tbl[2, s] == 1)
    def _():
        hl[...] = h0l_ref[0]
        sre[...] = h0r_ref[0]
        sim[...] = h0i_ref[0]

    ext[0:2 * SUB] = jnp.where(tbl[3, s] == 1, xp_ref[...], 0.0)
    ext[2 * SUB:2 * SUB + R_CHUNK] = xa_ref[...]
    ext[2 * SUB + R_CHUNK:3 * SUB + R_CHUNK] = jnp.where(tbl[4, s] == 1, xn_ref[...], 0.0)
    xa = cb_ref[...] + cw_ref[0:1] * ext[0:R_CHUNK]
    for k in range(1, CONV_W):
        xa = xa + cw_ref[k:k + 1] * ext[k * SUB:k * SUB + R_CHUNK]

    gz = _dot(xa.astype(BF16), wg_ref[...]) + bg_ref[...]
    r = _sigmoid(gz[:, :LRU_W])
    i = _sigmoid(gz[:, LRU_W:])
    lam = lam_ref[...]
    log_sig = jnp.minimum(lam, 0.0) - jnp.log1p(jnp.exp(-jnp.abs(lam)))
    log_a = LRU_C * r * log_sig
    abuf[...] = jnp.exp(log_a)
    bbuf[...] = jnp.sqrt(_neg_expm1(2.0 * log_a)) * (i * xa)

    order = range(T_CHUNK - 1, -1, -1) if reverse else range(T_CHUNK)
    h = hl[...]
    for t in order:
        rows = slice(t * SUB, (t + 1) * SUB)
        h = abuf[rows] * h + bbuf[rows]
        ha_ref[rows, :] = h
    hl[...] = h
    ll_ref[0] = h

    ub = xb_ref[...].astype(BF16)
    half = S5_N // 2
    for j in range(2):
        bu = _dot(ub[:, j * 256:(j + 1) * 256], bm_ref[j])
        hre[:, j * half:(j + 1) * half] = bu[:, :half]
        him[:, j * half:(j + 1) * half] = bu[:, half:]
    cblk = 4 * LANE
    for cb in range(S5_N // cblk):
        cols = slice(cb * cblk, (cb + 1) * cblk)
        ar = jnp.broadcast_to(ar_ref[:, cols], (SUB, cblk))
        ai = jnp.broadcast_to(ai_ref[:, cols], (SUB, cblk))
        hr = sre[:, cols]
        hi = sim[:, cols]
        for t in order:
            rows = slice(t * SUB, (t + 1) * SUB)
            nr = ar * hr - ai * hi + hre[rows, cols]
            ni = ar * hi + ai * hr + him[rows, cols]
            hr, hi = nr, ni
            hre[rows, cols] = hr
            him[rows, cols] = hi
        sre[:, cols] = hr
        sim[:, cols] = hi
    lr_ref[0] = sre[...]
    li_ref[0] = sim[...]
    for j in range(2):
        hc = jnp.concatenate([hre[:, j * half:(j + 1) * half], him[:, j * half:(j + 1) * half]],
                             axis=1).astype(BF16)
        y_ref[:, j * 256:(j + 1) * 256] = _dot(hc, cm_ref[j])


def _ab_scan(xz, h0l, h0r, h0i, cw, cb, wg, bg, lam, bm, ar, ai, cm, reverse):
    tbl = jnp.asarray(_scan_table(reverse))
    n_steps = tbl.shape[1]
    blk = lambda s, t: t[0, s]
    grp = lambda s, t: t[1, s]
    state_spec = lambda w: pl.BlockSpec((1, SUB, w), lambda s, t: (grp(s, t), 0, 0))
    const = lambda shape: pl.BlockSpec(shape, lambda s, t: (0,) * len(shape))
    grid_spec = pltpu.PrefetchScalarGridSpec(
        num_scalar_prefetch=1,
        grid=(n_steps,),
        in_specs=[
            pl.BlockSpec((R_CHUNK, LRU_W), lambda s, t: (blk(s, t), 0)),
            pl.BlockSpec((2 * SUB, LRU_W), lambda s, t: (jnp.maximum(blk(s, t) * (T_CHUNK // 2) - 1, 0), 0)),
            pl.BlockSpec((SUB, LRU_W), lambda s, t: (jnp.minimum((blk(s, t) + 1) * T_CHUNK, N // SUB - 1), 0)),
            pl.BlockSpec((R_CHUNK, S5_W), lambda s, t: (blk(s, t), 2)),
            state_spec(LRU_W), state_spec(S5_N), state_spec(S5_N),
            const((CONV_W, LRU_W)), const((1, LRU_W)),
            const((LRU_W, 2 * LRU_W)), const((1, 2 * LRU_W)), const((1, LRU_W)),
            const((2, 256, S5_N)), const((1, S5_N)), const((1, S5_N)), const((2, S5_N, 256)),
        ],
        out_specs=[
            pl.BlockSpec((R_CHUNK, LRU_W), lambda s, t: (blk(s, t), 0)),
            pl.BlockSpec((R_CHUNK, S5_W), lambda s, t: (blk(s, t), 0)),
            state_spec(LRU_W), state_spec(S5_N), state_spec(S5_N),
        ],
        scratch_shapes=[
            pltpu.VMEM((R_CHUNK + 3 * SUB, LRU_W), F32),
            pltpu.VMEM((R_CHUNK, LRU_W), F32), pltpu.VMEM((R_CHUNK, LRU_W), F32),
            pltpu.VMEM((R_CHUNK, S5_N), F32), pltpu.VMEM((R_CHUNK, S5_N), F32),
            pltpu.VMEM((SUB, LRU_W), F32), pltpu.VMEM((SUB, S5_N), F32), pltpu.VMEM((SUB, S5_N), F32),
        ],
    )
    return pl.pallas_call(
        functools.partial(_scan_kernel, reverse=reverse),
        grid_spec=grid_spec,
        out_shape=[jax.ShapeDtypeStruct((N, LRU_W), F32), jax.ShapeDtypeStruct((N, S5_W), F32),
                   jax.ShapeDtypeStruct((3, SUB, LRU_W), F32),
                   jax.ShapeDtypeStruct((3, SUB, S5_N), F32), jax.ShapeDtypeStruct((3, SUB, S5_N), F32)],
        compiler_params=_cparams(("arbitrary",)),
        name="ab_scan_bwd" if reverse else "ab_scan_fwd",
    )(tbl, xz, xz, xz, xz, h0l, h0r, h0i, cw, cb, wg, bg, lam, bm, ar, ai, cm)


def _about_kernel(haf, hab, yf, yr, ga, xb, x_ref, gate, g1, d_ref, wglu, bglu, wout, o_ref):
    ya = (haf[...] + hab[...]) * _gelu(ga[...])
    yb0 = _gelu(yf[...] + yr[...] + d_ref[...] * xb[...])
    yb = yb0 * _sigmoid(_dot(yb0.astype(BF16), wglu[...]) + bglu[...])
    out = _dot(ya.astype(BF16), wout[0:LRU_W]) + _dot(yb.astype(BF16), wout[LRU_W:LRU_W + S5_W])
    o_ref[...] = _gated(x_ref[...], out, g1[...], gate[0])


def _ab_out(haf, hab, yf, yr, xz, x, mods, g1, s5_d, wglu, bglu, wout):
    tm = 512
    grp = _grp_tm(tm)
    half = lambda c: pl.BlockSpec((tm, LRU_W), lambda i: (i, c))
    return pl.pallas_call(
        _about_kernel,
        grid=(N // tm,),
        in_specs=[half(0), half(0), half(0), half(0), half(1), half(2),
                  pl.BlockSpec((tm, D), lambda i: (i, 0)),
                  _mod_spec(2, grp), _full((1, D)), _full((1, S5_W)),
                  _full((S5_W, S5_W)), _full((1, S5_W)), _full((LRU_W + S5_W, D))],
        out_specs=pl.BlockSpec((tm, D), lambda i: (i, 0)),
        out_shape=jax.ShapeDtypeStruct((N, D), F32),
        compiler_params=_cparams(("arbitrary",)),
        name="ab_out",
    )(haf, hab, yf, yr, xz, xz, x, mods, g1, s5_d, wglu, bglu, wout)


def _swiglu_resident(hbf, wgu, wd, acc, between=None):
    for j in range(N_FF):
        h = hbf[...]
        g = _dot(h, wgu[:, j * FF_BLK:(j + 1) * FF_BLK])
        u = _dot(h, wgu[:, D_FF + j * FF_BLK:D_FF + (j + 1) * FF_BLK])
        act = (g * _sigmoid(g)) * u
        part = _dot(act.astype(BF16), wd[j * FF_BLK:(j + 1) * FF_BLK, :])
        if j == 0:
            acc[...] = part
        else:
            acc[...] += part
        if between is not None:
            between(j)


def _ffn_kernel(x_ref, sh, sc, gt, g2, g3, wgu, wd, o_ref, hbf, acc):
    hbf[...] = _adaln(x_ref[...], g2[...], sc[0], sh[0]).astype(BF16)
    _swiglu_resident(hbf, wgu, wd, acc)
    o_ref[...] = _gated(x_ref[...], acc[...], g3[...], gt[0])


def _ffn(x, mods, g2, g3, wgu, wd):
    tm = 1024
    grp = _grp_tm(tm)
    once = lambda shape: pl.BlockSpec(shape, lambda i: (0,) * len(shape), pipeline_mode=pl.Buffered(1))
    return pl.pallas_call(
        _ffn_kernel,
        grid=(N // tm,),
        in_specs=[pl.BlockSpec((tm, D), lambda i: (i, 0)),
                  _mod_spec(3, grp), _mod_spec(4, grp), _mod_spec(5, grp),
                  _full((1, D)), _full((1, D)),
                  once(wgu.shape), once(wd.shape)],
        out_specs=pl.BlockSpec((tm, D), lambda i: (i, 0)),
        out_shape=jax.ShapeDtypeStruct((N, D), F32),
        scratch_shapes=[pltpu.VMEM((tm, D), BF16), pltpu.VMEM((tm, D), F32)],
        compiler_params=_cparams(("arbitrary",)),
        name="ffn",
    )(x, mods, mods, mods, g2, g3, wgu, wd)


def _pair_swap(x):
    outs = []
    for c in range(x.shape[1] // LANE):
        xc = x[:, c * LANE:(c + 1) * LANE]
        even = lax.broadcasted_iota(I32, xc.shape, 1) % 2 == 0
        outs.append(jnp.where(even, pltpu.roll(xc, LANE - 1, 1), pltpu.roll(xc, 1, 1)))
    return outs[0] if len(outs) == 1 else jnp.concatenate(outs, axis=1)


def _mlaproj_kernel(x_ref, g0, sh, sc, w1, gq, gkv, wuq, wukv, cq_ref, sq_ref, ck_ref, sk_ref,
                    qn_ref, qr_ref, kn_ref, v_ref, kr2_ref, ckv_ref, krr_ref):
    h = _adaln(x_ref[...], g0[...], sc[0], sh[0])
    dn = _dot(h.astype(BF16), w1[...])
    cq = _rms(dn[:, :Q_LORA], gq[...])
    ckv = _rms(dn[:, Q_LORA:Q_LORA + KV_LORA], gkv[...])
    krp = dn[:, Q_LORA + KV_LORA:]
    ckv_ref[...] = ckv
    krr_ref[...] = krp
    q = _dot(cq.astype(BF16), wuq[...])
    qn_ref[...] = q[:, :HEADS * QK_NOPE].astype(BF16)
    qr = q[:, HEADS * QK_NOPE:]
    qr_ref[...] = (qr * cq_ref[...] + _pair_swap(qr) * sq_ref[...]).astype(BF16)
    kv = _dot(ckv.astype(BF16), wukv[...])
    kn_ref[...] = kv[:, :HEADS * QK_NOPE].astype(BF16)
    v_ref[...] = kv[:, HEADS * QK_NOPE:].astype(BF16)
    kr = krp * ck_ref[...] + _pair_swap(krp) * sk_ref[...]
    kr2_ref[...] = jnp.concatenate([kr, pltpu.roll(kr, QK_ROPE, 1)], axis=1).astype(BF16)


def _mla_proj(x, g0, mods, w1, gq, gkv, wuq, wukv, cos_q, sin_q, cos_k, sin_k):
    tm = 256
    grp = _grp_bm(tm)
    n_pos = DEC_SEQ // tm
    tab = lambda w: pl.BlockSpec((tm, w), lambda i: (jnp.where(i * tm < NP, n_pos, (i - NP // tm) % n_pos), 0))
    row = lambda w: pl.BlockSpec((tm, w), lambda i: (i, 0))
    shp = lambda w, dt: jax.ShapeDtypeStruct((N, w), dt)
    return pl.pallas_call(
        _mlaproj_kernel,
        grid=(N // tm,),
        in_specs=[row(D), _full((1, D)), _mod_spec(0, grp), _mod_spec(1, grp),
                  _full(w1.shape), _full((1, Q_LORA)), _full((1, KV_LORA)),
                  _full(wuq.shape), _full(wukv.shape),
                  tab(HEADS * QK_ROPE), tab(HEADS * QK_ROPE), tab(LANE), tab(LANE)],
        out_specs=[row(HEADS * QK_NOPE), row(HEADS * QK_ROPE), row(HEADS * QK_NOPE), row(HEADS * V_DIM),
                   row(2 * LANE), row(KV_LORA), row(LANE)],
        out_shape=[shp(HEADS * QK_NOPE, BF16), shp(HEADS * QK_ROPE, BF16), shp(HEADS * QK_NOPE, BF16),
                   shp(HEADS * V_DIM, BF16), shp(2 * LANE, BF16), shp(KV_LORA, F32), shp(LANE, F32)],
        compiler_params=_cparams(("arbitrary",)),
        name="mla_proj",
    )(x, g0, mods, mods, w1, gq, gkv, wuq, wukv, cos_q, sin_q, cos_k, sin_k)


def _cachekv_kernel(c_ref, w_ref, kn_ref, v_ref):
    kv = _dot(c_ref[...].astype(BF16), w_ref[...])
    kn_ref[...] = kv[:, :HEADS * QK_NOPE].astype(BF16)
    v_ref[...] = kv[:, HEADS * QK_NOPE:].astype(BF16)


def _cache_kv(ckv_cache, wukv):
    rows = ckv_cache.shape[0]
    tm = 512
    return pl.pallas_call(
        _cachekv_kernel,
        grid=(rows // tm,),
        in_specs=[pl.BlockSpec((tm, KV_LORA), lambda i: (i, 0)), _full(wukv.shape)],
        out_specs=[pl.BlockSpec((tm, HEADS * QK_NOPE), lambda i: (i, 0)),
                   pl.BlockSpec((tm, HEADS * V_DIM), lambda i: (i, 0))],
        out_shape=[jax.ShapeDtypeStruct((rows, HEADS * QK_NOPE), BF16),
                   jax.ShapeDtypeStruct((rows, HEADS * V_DIM), BF16)],
        compiler_params=_cparams(("arbitrary",)),
        name="cache_kv",
    )(ckv_cache, wukv)


def _attn_kernel(*refs, has_cache):
    if has_cache:
        qn, qr, kn, kr, v, knc, krc, vc, o_ref = refs
    else:
        qn, qr, kn, kr, v, o_ref = refs
    scale = (QK_NOPE + QK_ROPE) ** -0.5
    q = jnp.concatenate([qn[...], qr[...]], axis=1)
    k = jnp.concatenate([kn[...], kr[...]], axis=1)
    s = lax.dot_general(q, k, NT_DIMS, preferred_element_type=F32) * scale
    m = jnp.max(s, axis=-1, keepdims=True)
    if has_cache:
        kc = jnp.concatenate([knc[...], krc[...]], axis=1)
        sc = lax.dot_general(q, kc, NT_DIMS, preferred_element_type=F32) * scale
        m = jnp.maximum(m, jnp.max(sc, axis=-1, keepdims=True))
        pc = jnp.exp(sc - m)
    p = jnp.exp(s - m)
    den = jnp.sum(p, axis=-1, keepdims=True)
    o = _dot(p.astype(BF16), v[...])
    if has_cache:
        den = den + jnp.sum(pc, axis=-1, keepdims=True)
        o = o + _dot(pc.astype(BF16), vc[...])
    o_ref[...] = (o / den).astype(BF16)


def _attention(qn, qr, kn, kr2, v, *, row0, n_seq, seq, tq, cache=None):
    nq = seq // tq
    qblk = lambda b, h, i: row0 // tq + b * nq + i
    kblk = lambda b: row0 // seq + b
    in_specs = [pl.BlockSpec((tq, LANE), lambda b, h, i: (qblk(b, h, i), h)),
                pl.BlockSpec((tq, LANE), lambda b, h, i: (qblk(b, h, i), h // 2)),
                pl.BlockSpec((seq, LANE), lambda b, h, i: (kblk(b), h)),
                pl.BlockSpec((seq, LANE), lambda b, h, i: (kblk(b), h % 2)),
                pl.BlockSpec((seq, LANE), lambda b, h, i: (kblk(b), h))]
    args = [qn, qr, kn, kr2, v]
    if cache is not None:
        knc, kr2c, vc = cache
        in_specs += [pl.BlockSpec((PAST_LEN, LANE), lambda b, h, i: (b, h)),
                     pl.BlockSpec((PAST_LEN, LANE), lambda b, h, i: (b, h % 2)),
                     pl.BlockSpec((PAST_LEN, LANE), lambda b, h, i: (b, h))]
        args += [knc, kr2c, vc]
    return pl.pallas_call(
        functools.partial(_attn_kernel, has_cache=cache is not None),
        grid=(n_seq, HEADS, nq),
        in_specs=in_specs,
        out_specs=pl.BlockSpec((tq, LANE), lambda b, h, i: (b * nq + i, h)),
        out_shape=jax.ShapeDtypeStruct((n_seq * seq, HEADS * V_DIM), BF16),
        compiler_params=_cparams(("arbitrary", "arbitrary", "arbitrary")),
        name="attn_latent" if cache is not None else "attn_context",
    )(*args)


def _router_kernel(op_ref, os_ref, x_ref, gate1, g1, sh2, sc2, g2, wout, wr_hi, wr_lo, tri,
                   x3_ref, h_ref, ri_ref, rf_ref, cnt_ref, carry, *, prompt_steps):
    step = pl.program_id(0)

    @pl.when(step == 0)
    def _():
        carry[...] = jnp.zeros_like(carry)

    o = jnp.where(step < prompt_steps, op_ref[...], os_ref[...])
    x3 = _gated(x_ref[...], _dot(o, wout[...]), g1[...], gate1[0])
    x3_ref[...] = x3
    h = _adaln(x3, g2[...], sc2[0], sh2[0])
    _store_row_tiles(h_ref, lambda cols: h[:, cols])
    h_hi = h.astype(BF16)
    h_lo = (h - h_hi.astype(F32)).astype(BF16)
    dg = lambda a, b: lax.dot_general(a, b, NT_DIMS, preferred_element_type=F32)
    lg = dg(wr_hi[...], h_hi) + dg(wr_hi[...], h_lo) + dg(wr_lo[...], h_hi)
    eidx = lax.broadcasted_iota(I32, lg.shape, 0).astype(F32)
    m1 = jnp.max(lg, axis=0, keepdims=True)
    i1 = jnp.min(jnp.where(lg == m1, eidx, float(N_EXP)), axis=0, keepdims=True)
    sel1 = eidx == i1
    lg2 = jnp.where(sel1, -jnp.inf, lg)
    m2 = jnp.max(lg2, axis=0, keepdims=True)
    i2 = jnp.min(jnp.where(lg2 == m2, eidx, float(N_EXP)), axis=0, keepdims=True)
    sel2 = eidx == i2
    e = jnp.exp(m2 - m1)
    w1 = 1.0 / (1.0 + e)
    w2 = e / (1.0 + e)
    picked = jnp.where(sel1 | sel2, 1.0, 0.0)
    rank = _dot(picked.astype(BF16), tri[...]) + carry[:, 0:1]
    r1 = jnp.sum(jnp.where(sel1, rank, 0.0), axis=0, keepdims=True)
    r2 = jnp.sum(jnp.where(sel2, rank, 0.0), axis=0, keepdims=True)
    carry[...] = carry[...] + jnp.sum(picked, axis=1, keepdims=True)
    cnt_ref[...] = carry[...]
    ri_ref[...] = jnp.where(eidx == 0.0, i1, jnp.where(eidx == 1.0, i2, jnp.where(eidx == 2.0, r1, r2))).astype(I32)
    rf_ref[...] = jnp.where(eidx == 0.0, w1, w2)


def _attn_out_router(o_p, o_s, x, mods, g1, g2, wout, wr_hi, wr_lo):
    tm = 512
    grp = _grp_bm(tm)
    n_p = NP // tm
    o_specs = [pl.BlockSpec((tm, HEADS * V_DIM), lambda i: (jnp.minimum(i, n_p - 1), 0)),
               pl.BlockSpec((tm, HEADS * V_DIM), lambda i: (jnp.maximum(i - n_p, 0), 0))]
    tri = jnp.asarray(np.triu(np.ones((tm, tm), np.float32), 1), BF16)
    row = lambda w: pl.BlockSpec((tm, w), lambda i: (i, 0))
    col = pl.BlockSpec((N_EXP, tm), lambda i: (0, i))
    return pl.pallas_call(
        functools.partial(_router_kernel, prompt_steps=n_p),
        grid=(N // tm,),
        in_specs=o_specs + [row(D), _mod_spec(2, grp), _full((1, D)),
                  _mod_spec(3, grp), _mod_spec(4, grp), _full((1, D)),
                  _full((HEADS * V_DIM, D)), _full((N_EXP, D)), _full((N_EXP, D)), _full((tm, tm))],
        out_specs=[row(D), pl.BlockSpec((tm * ROW_TILE, LANE), lambda i: (i, 0)), col, col, _full((N_EXP, LANE))],
        out_shape=[jax.ShapeDtypeStruct((N, D), F32), jax.ShapeDtypeStruct((N * ROW_TILE, LANE), F32),
                   jax.ShapeDtypeStruct((N_EXP, N), I32), jax.ShapeDtypeStruct((N_EXP, N), F32),
                   jax.ShapeDtypeStruct((N_EXP, LANE), F32)],
        scratch_shapes=[pltpu.VMEM((N_EXP, LANE), F32)],
        compiler_params=_cparams(("arbitrary",)),
        name="attn_out_router",
    )(o_p, o_s, x, mods, g1, mods, mods, g2, wout, wr_hi, wr_lo, tri)


def _tile_rows(r):
    return r * ROW_TILE if isinstance(r, int) else pl.multiple_of(r * ROW_TILE, ROW_TILE)


def _moe_kernel(te, nu, src0_ref, src1_ref, dstp_ref, dstc_ref, h_hbm, wgu, wd, y_hbm,
                hsbuf, ybuf, hbf, acc, sem_g, sem_s):
    i = pl.program_id(0)
    last = pl.num_programs(0) - 1
    cur = i % 2
    nxt = 1 - cur
    buf_rows = TM_E * ROW_TILE

    def gather(idx_ref, r, slot):
        i_src = pl.multiple_of(idx_ref[0, 0, r] * ROW_TILE, ROW_TILE)
        return pltpu.make_async_copy(h_hbm.at[pl.ds(i_src, ROW_TILE), :],
                                     hsbuf.at[slot, pl.ds(_tile_rows(r), ROW_TILE), :], sem_g.at[slot])

    def scatter(idx_ref, r, slot):
        i_dst = pl.multiple_of(idx_ref[0, 0, r] * ROW_TILE, ROW_TILE)
        return pltpu.make_async_copy(ybuf.at[slot, pl.ds(_tile_rows(r), ROW_TILE), :],
                                     y_hbm.at[pl.ds(i_dst, ROW_TILE), :], sem_s.at[slot])

    def wait_gather(slot):
        pltpu.make_async_copy(h_hbm.at[pl.ds(0, buf_rows), :], hsbuf.at[slot], sem_g.at[slot]).wait()

    def wait_scatter(slot):
        pltpu.make_async_copy(ybuf.at[slot], y_hbm.at[pl.ds(0, buf_rows), :], sem_s.at[slot]).wait()

    def for_rows(fn):
        def body(r, carry):
            fn(r)
            return carry
        lax.fori_loop(0, TM_E, body, 0, unroll=8)

    def side_traffic(r):
        gather(src1_ref, r, nxt).start()
        scatter(dstp_ref, r, nxt).start()

    @pl.when(i == 0)
    def _():
        ybuf[1] = jnp.zeros((buf_rows, LANE), F32)
        for_rows(lambda r: gather(src0_ref, r, 0).start())

    wait_gather(cur)

    @pl.when(i < nu[0])
    def _():
        for c in range(ROW_TILE):
            hbf[:, c * LANE:(c + 1) * LANE] = hsbuf[cur, pl.ds(c, TM_E, stride=ROW_TILE), :].astype(BF16)
        bounds = [(TM_E * j) // N_FF for j in range(N_FF + 1)]

        def between(j):
            for r in range(bounds[j], bounds[j + 1]):
                side_traffic(r)

        _swiglu_resident(hbf, wgu.at[0], wd.at[0], acc, between)
        _store_row_tiles(ybuf.at[cur], lambda cols: acc[:, cols])

    @pl.when(i >= nu[0])
    def _():
        for_rows(side_traffic)
        ybuf[cur] = jnp.zeros((buf_rows, LANE), F32)

    wait_scatter(nxt)

    @pl.when(i == last)
    def _():
        for_rows(lambda r: scatter(dstc_ref, r, cur).start())
        wait_scatter(cur)
        wait_gather(nxt)


def _moe_experts(tile_expert, n_used, src_tbl, dst_tbl, h, wgu, wd):
    smem = lambda off: pl.BlockSpec((1, 1, TM_E), lambda i, te, nu: (i + off, 0, 0), memory_space=pltpu.SMEM)
    grid_spec = pltpu.PrefetchScalarGridSpec(
        num_scalar_prefetch=2,
        grid=(N_TILES,),
        in_specs=[smem(0), smem(1), smem(0), smem(1),
                  pl.BlockSpec(memory_space=pl.ANY),
                  pl.BlockSpec((1,) + wgu.shape[1:], lambda i, te, nu: (te[i], 0, 0)),
                  pl.BlockSpec((1,) + wd.shape[1:], lambda i, te, nu: (te[i], 0, 0))],
        out_specs=pl.BlockSpec(memory_space=pl.ANY),
        scratch_shapes=[pltpu.VMEM((2, TM_E * ROW_TILE, LANE), F32), pltpu.VMEM((2, TM_E * ROW_TILE, LANE), F32),
                        pltpu.VMEM((TM_E, D), BF16), pltpu.VMEM((TM_E, D), F32),
                        pltpu.SemaphoreType.DMA((2,)), pltpu.SemaphoreType.DMA((2,))],
    )
    return pl.pallas_call(
        _moe_kernel,
        grid_spec=grid_spec,
        out_shape=jax.ShapeDtypeStruct(((2 * N + DUMP_ROWS) * ROW_TILE, LANE), F32),
        compiler_params=_cparams(("arbitrary",)),
        name="moe_experts",
    )(tile_expert, n_used, src_tbl, src_tbl, dst_tbl, dst_tbl, h, wgu, wd)


def _combine_kernel(y1_ref, y2_ref, x_ref, w1_ref, w2_ref, gate2, g3, o_ref):
    rows = x_ref.shape[0]
    w1 = w1_ref[...]
    w2 = w2_ref[...]
    f = jnp.concatenate([w1 * _load_row_tiles(y1_ref, c, rows) + w2 * _load_row_tiles(y2_ref, c, rows)
                         for c in range(ROW_TILE)], axis=1)
    o_ref[...] = _gated(x_ref[...], f, g3[...], gate2[0])


def _moe_combine(y, x, w1, w2, mods, g3):
    tm = 512
    grp = _grp_bm(tm)
    nb = N // tm
    return pl.pallas_call(
        _combine_kernel,
        grid=(nb,),
        in_specs=[pl.BlockSpec((tm * ROW_TILE, LANE), lambda i: (i, 0)),
                  pl.BlockSpec((tm * ROW_TILE, LANE), lambda i: (nb + i, 0)),
                  pl.BlockSpec((tm, D), lambda i: (i, 0)),
                  pl.BlockSpec((tm, 1), lambda i: (i, 0)), pl.BlockSpec((tm, 1), lambda i: (i, 0)),
                  _mod_spec(5, grp), _full((1, D))],
        out_specs=pl.BlockSpec((tm, D), lambda i: (i, 0)),
        out_shape=jax.ShapeDtypeStruct((N, D), F32),
        compiler_params=_cparams(("arbitrary",)),
        name="moe_combine",
    )(y, y, x, w1, w2, mods, g3)


def _block_diag(w):
    hh, a, b = w.shape
    eye = jnp.eye(hh, dtype=w.dtype)
    return jnp.einsum('hab,hk->hakb', w, eye).reshape(hh * a, hh * b)


def _s5_matrices(a_re, a_im, log_dt, b_re, b_im, c_re, c_im):
    dt = jnp.exp(log_dt)[:, None]
    mag = jnp.exp(a_re * dt)
    abr = mag * jnp.cos(a_im * dt)
    abi = mag * jnp.sin(a_im * dt)
    den = a_re * a_re + a_im * a_im
    cr = ((abr - 1.0) * a_re + abi * a_im) / den
    ci = (abi * a_re - (abr - 1.0) * a_im) / den
    bbr = cr[..., None] * b_re - ci[..., None] * b_im
    bbi = cr[..., None] * b_im + ci[..., None] * b_re
    hg = S5_G // 2
    eye = jnp.eye(hg, dtype=F32)
    bms, cms = [], []
    for j in range(2):
        sl = slice(j * hg, (j + 1) * hg)
        bd = lambda m: jnp.einsum('gpc,gh->gchp', m[sl], eye).reshape(hg * S5_CH, hg * S5_P)
        bms.append(jnp.concatenate([bd(bbr), bd(bbi)], axis=1))
        cd = lambda m: jnp.einsum('gcp,gh->gphc', m[sl], eye).reshape(hg * S5_P, hg * S5_CH)
        cms.append(jnp.concatenate([cd(c_re), cd(-c_im)], axis=0))
    return (jnp.stack(bms).astype(BF16), abr.reshape(1, S5_N), abi.reshape(1, S5_N),
            jnp.stack(cms).astype(BF16))


def _rope_tables(tm):
    rows = DEC_SEQ // GRID_W
    row = jnp.repeat(jnp.arange(rows, dtype=F32), GRID_W)
    col = jnp.tile(jnp.arange(GRID_W, dtype=F32), rows)
    nf = QK_ROPE // 4
    inv = ROPE_THETA ** (-jnp.arange(nf, dtype=F32) / nf)
    ang = jnp.concatenate([row[:, None] * inv, col[:, None] * inv], axis=-1)
    cos = jnp.repeat(jnp.cos(ang), 2, axis=-1)
    sin = jnp.stack([-jnp.sin(ang), jnp.sin(ang)], axis=-1).reshape(DEC_SEQ, QK_ROPE)
    ident = lambda t, one: jnp.concatenate([t, jnp.full((tm, t.shape[1]), one, F32)], axis=0)
    cos_q = ident(jnp.tile(cos, (1, HEADS)), 1.0)
    sin_q = ident(jnp.tile(sin, (1, HEADS)), 0.0)
    pad = lambda t, one: jnp.concatenate([t, jnp.full((DEC_SEQ, LANE - QK_ROPE), one, F32)], axis=1)
    return cos_q, sin_q, ident(pad(cos, 1.0), 1.0), ident(pad(sin, 0.0), 0.0)


def _to_time_major(xp, xs):
    p = xp.reshape(BATCH // SUB, SUB, SEQ, -1).transpose(0, 2, 1, 3).reshape(NP, -1)
    s = xs.transpose(1, 0, 2).reshape(NS, -1)
    return jnp.concatenate([p, s], axis=0)


def _to_batch_major(x):
    p = x[:NP].reshape(BATCH // SUB, SEQ, SUB, -1).transpose(0, 2, 1, 3).reshape(NP, -1)
    s = x[NP:].reshape(DEC_SEQ, DEC_BATCH, -1).transpose(1, 0, 2).reshape(NS, -1)
    return jnp.concatenate([p, s], axis=0)


def _group_states(prompt_state, sample_state):
    w = sample_state.shape[-1]
    return jnp.concatenate([prompt_state.reshape(2, SUB, w), sample_state.reshape(1, SUB, w)], axis=0)


def _layer_ab(x, m, ng, j, state_lru, state_s5_re, state_s5_im, p):
    xz = _ab_inproj(x, ng[0:1], m, p['ab_w_in'][j].astype(BF16))
    zeros = lambda w: jnp.zeros((BATCH, w), F32)
    outs = []
    for d in range(2):
        wg = jnp.concatenate([_block_diag(p['lru_wa'][j, d]), _block_diag(p['lru_wx'][j, d])], axis=1).astype(BF16)
        bg = jnp.concatenate([p['lru_ba'][j, d], p['lru_bx'][j, d]])[None]
        bm, ar, ai, cm = _s5_matrices(p['s5_a_re'][j, d], p['s5_a_im'][j, d], p['s5_log_dt'][j, d],
                                      p['s5_b_re'][j, d], p['s5_b_im'][j, d], p['s5_c_re'][j, d], p['s5_c_im'][j, d])
        h0l = _group_states(zeros(LRU_W), state_lru[:, j, d])
        h0r = _group_states(zeros(S5_N), state_s5_re[:, j, d].reshape(DEC_BATCH, S5_N))
        h0i = _group_states(zeros(S5_N), state_s5_im[:, j, d].reshape(DEC_BATCH, S5_N))
        outs.append(_ab_scan(xz, h0l, h0r, h0i, p['ab_conv_w'][j], p['ab_conv_b'][j][None], wg, bg,
                             p['lru_lambda'][j, d][None], bm, ar, ai, cm, reverse=(d == 1)))
    (haf, yf, llf, lrf, lif), (hab, yr, llb, lrb, lib) = outs
    x = _ab_out(haf, hab, yf, yr, xz, x, m, ng[1:2], p['s5_d'][j][None], p['s5_w_glu'][j].astype(BF16),
                p['s5_b_glu'][j][None], p['ab_w_out'][j].astype(BF16))
    x = _ffn(x, m, ng[2:3], ng[3:4], p['ffn_w_gate_up'][j].astype(BF16), p['ffn_w_down'][j].astype(BF16))
    prompt = lambda f, b, w: jnp.stack([f[:2].reshape(BATCH, w), b[:2].reshape(BATCH, w)], axis=1)
    lru = prompt(llf, llb, LRU_W)
    s5r = prompt(lrf, lrb, S5_N).reshape(BATCH, 2, S5_G, S5_P)
    s5i = prompt(lif, lib, S5_N).reshape(BATCH, 2, S5_G, S5_P)
    return x, lru, s5r, s5i


def _head_major(w, parts):
    k = w.shape[0]
    per_head = w.reshape(k, HEADS, -1)
    out, start = [], 0
    for width in parts:
        out.append(per_head[:, :, start:start + width].reshape(k, HEADS * width))
        start += width
    return jnp.concatenate(out, axis=1)


def _layer_mla_moe(x, m, ng, j, cache_kv_latent, cache_k_rope, p):
    w1 = jnp.concatenate([p['mla_w_in'][j], jnp.zeros((D, LANE - QK_ROPE), F32)], axis=1).astype(BF16)
    wuq = _head_major(p['mla_w_uq'][j], (QK_NOPE, QK_ROPE)).astype(BF16)
    wukv = _head_major(p['mla_w_ukv'][j], (QK_NOPE, V_DIM)).astype(BF16)
    tables = _rope_tables(256)
    qn, qr, kn, v, kr2, ckv, krr = _mla_proj(x, ng[0:1], m, w1, p['mla_g_q'][j][None], p['mla_g_kv'][j][None],
                                              wuq, wukv, *tables)
    knc, vc = _cache_kv(cache_kv_latent[:, j].reshape(DEC_BATCH * PAST_LEN, KV_LORA), wukv)
    krc = cache_k_rope[:, j].reshape(DEC_BATCH * PAST_LEN, QK_ROPE)
    z = jnp.zeros_like(krc)
    kr2c = jnp.concatenate([krc, z, z, krc], axis=1).astype(BF16)
    o_p = _attention(qn, qr, kn, kr2, v, row0=0, n_seq=BATCH, seq=SEQ, tq=SEQ)
    o_s = _attention(qn, qr, kn, kr2, v, row0=NP, n_seq=DEC_BATCH, seq=DEC_SEQ, tq=512, cache=(knc, kr2c, vc))
    wr_t = p['moe_w_router'][j].T
    wr_hi = wr_t.astype(BF16)
    wr_lo = (wr_t - wr_hi.astype(F32)).astype(BF16)
    x3, h, ri, rf, cnt = _attn_out_router(o_p, o_s, x, m, ng[1:2], ng[2:3], p['mla_w_out'][j].astype(BF16), wr_hi, wr_lo)
    counts = cnt[:, 0].astype(I32)
    padded = ((counts + TM_E - 1) // TM_E) * TM_E
    ends = jnp.cumsum(padded)
    offs = ends - padded
    pos1 = offs[ri[0]] + ri[2]
    pos2 = offs[ri[1]] + ri[3]
    pick_tok = jnp.arange(2 * N, dtype=I32)
    dest = jnp.full((P_ROWS,), -1, I32).at[jnp.concatenate([pos1, pos2])].set(pick_tok)
    is_pad = dest < 0
    pad_row = 2 * N + TM_E + jnp.cumsum(is_pad.astype(I32)) - 1
    src_tbl = jnp.concatenate([jnp.where(is_pad, 0, dest % N), jnp.zeros((TM_E,), I32)])
    dst_tbl = jnp.concatenate([2 * N + jnp.arange(TM_E, dtype=I32), jnp.where(is_pad, pad_row, dest)])
    n_used = (ends[-1] // TM_E).astype(I32)[None]
    tile_row = jnp.minimum(jnp.arange(N_TILES, dtype=I32), n_used - 1) * TM_E
    tile_expert = jnp.sum((tile_row[:, None] >= ends[None, :]).astype(I32), axis=1)
    y = _moe_experts(tile_expert, n_used, src_tbl.reshape(N_TILES + 1, 1, TM_E), dst_tbl.reshape(N_TILES + 1, 1, TM_E),
                     h, p['moe_w_gate_up'][j].astype(BF16), p['moe_w_down'][j].astype(BF16))
    x4 = _moe_combine(y, x3, rf[0][:, None], rf[1][:, None], m, ng[3:4])
    kv_new = ckv[:NP].reshape(BATCH, SEQ, KV_LORA)
    kr_new = krr[:NP, :QK_ROPE].reshape(BATCH, SEQ, QK_ROPE)
    return x4, kv_new, kr_new


def kernel(x_prompt, x_sample, c, state_lru, state_s5_re, state_s5_im, cache_kv_latent, cache_k_rope, c_ctx, w_mod, b_mod, norm_gains, ab_w_in, ab_conv_w, ab_conv_b, lru_wa, lru_ba, lru_wx, lru_bx, lru_lambda, s5_a_re, s5_a_im, s5_log_dt, s5_b_re, s5_b_im, s5_c_re, s5_c_im, s5_d, s5_w_glu, s5_b_glu, ab_w_out, ffn_w_gate_up, ffn_w_down, mla_w_in, mla_g_q, mla_g_kv, mla_w_uq, mla_w_ukv, mla_w_out, moe_w_router, moe_w_gate_up, moe_w_down):
    p = dict(ab_w_in=ab_w_in, ab_conv_w=ab_conv_w, ab_conv_b=ab_conv_b, lru_wa=lru_wa, lru_ba=lru_ba,
             lru_wx=lru_wx, lru_bx=lru_bx, lru_lambda=lru_lambda, s5_a_re=s5_a_re, s5_a_im=s5_a_im,
             s5_log_dt=s5_log_dt, s5_b_re=s5_b_re, s5_b_im=s5_b_im, s5_c_re=s5_c_re, s5_c_im=s5_c_im,
             s5_d=s5_d, s5_w_glu=s5_w_glu, s5_b_glu=s5_b_glu, ab_w_out=ab_w_out, ffn_w_gate_up=ffn_w_gate_up,
             ffn_w_down=ffn_w_down, mla_w_in=mla_w_in, mla_g_q=mla_g_q, mla_g_kv=mla_g_kv, mla_w_uq=mla_w_uq,
             mla_w_ukv=mla_w_ukv, mla_w_out=mla_w_out, moe_w_router=moe_w_router, moe_w_gate_up=moe_w_gate_up,
             moe_w_down=moe_w_down)
    depth = w_mod.shape[0]
    cond = jnp.concatenate([c_ctx[None], c, jnp.zeros((2 * SUB - 1 - DEC_BATCH, D), F32)], axis=0)
    mod = _modulation(cond, w_mod, b_mod)
    ctx_tile = lambda l: jnp.broadcast_to(mod[l, 0:1], (SUB, 6 * D))
    x = _to_time_major(x_prompt, x_sample)
    time_major = True
    lru_l, s5r_l, s5i_l, kv_l, kr_l = [], [], [], [], []
    for layer in range(depth):
        j = layer // 2
        ng = norm_gains[layer]
        if layer % 2 == 0:
            if not time_major:
                x = _to_time_major(x[:NP].reshape(BATCH, SEQ, D), x[NP:].reshape(DEC_BATCH, DEC_SEQ, D))
                time_major = True
            m = jnp.stack([ctx_tile(layer), mod[layer, 1:1 + DEC_BATCH]])
            x, lru, s5r, s5i = _layer_ab(x, m, ng, j, state_lru, state_s5_re, state_s5_im, p)
            lru_l.append(lru)
            s5r_l.append(s5r)
            s5i_l.append(s5i)
        else:
            if time_major:
                x = _to_batch_major(x)
                time_major = False
            lat = jnp.broadcast_to(mod[layer, 1:1 + DEC_BATCH, None, :], (DEC_BATCH, SUB, 6 * D))
            m = jnp.concatenate([ctx_tile(layer)[None], lat], axis=0)
            x, kv_new, kr_new = _layer_mla_moe(x, m, ng, j, cache_kv_latent, cache_k_rope, p)
            kv_l.append(kv_new)
            kr_l.append(kr_new)
    if time_major:
        x = _to_batch_major(x)
    return (x[:NP].reshape(BATCH, SEQ, D), x[NP:].reshape(DEC_BATCH, DEC_SEQ, D),
            jnp.stack(lru_l, axis=1), jnp.stack(s5r_l, axis=1), jnp.stack(s5i_l, axis=1),
            jnp.stack(kv_l, axis=1), jnp.stack(kr_l, axis=1))
```

```python
import functools
import math

import numpy as np
import jax
import jax.numpy as jnp
from jax import lax
from jax.experimental import pallas as pl
from jax.experimental.pallas import tpu as pltpu

F32 = jnp.float32
BF16 = jnp.bfloat16
I32 = jnp.int32

D = 1024
BATCH, SEQ = 16, 256
DEC_BATCH, DEC_SEQ = 8, 2048
PAST_LEN = 256
GRID_W = 64
LRU_W = 512
LRU_HEADS = 8
LRU_C = 8.0
CONV_W = 4
S5_W = 512
S5_CH = 16
S5_G = 32
S5_P = 64
S5_N = S5_G * S5_P
HEADS = 8
QK_NOPE, QK_ROPE, V_DIM = 128, 64, 128
Q_LORA, KV_LORA = 384, 256
ROPE_THETA = 10000.0
D_FF = 2816
N_EXP = 8
EPS = 1e-6

NP = BATCH * SEQ
NS = DEC_BATCH * DEC_SEQ
N = NP + NS
SUB = 8
LANE = 128
ROW_TILE = D // LANE
T_CHUNK = 32
R_CHUNK = T_CHUNK * SUB
FF_BLK = 256
KEY_BLK = 256
HEAD_GRP = 4
SIDE_BLOCKS = 8
N_FF = D_FF // FF_BLK
TM_E = 512
P_ROWS = 2 * N + N_EXP * TM_E
N_TILES = P_ROWS // TM_E
DUMP_ROWS = P_ROWS - 2 * N + TM_E
VMEM_LIMIT = 56 * 1024 * 1024

NT_DIMS = (((1,), (1,)), ((), ()))


def _cparams(sem):
    return pltpu.CompilerParams(dimension_semantics=sem, vmem_limit_bytes=VMEM_LIMIT)


def _dot(a, b):
    return jnp.dot(a, b, preferred_element_type=F32)


def _sigmoid(x):
    return 1.0 / (1.0 + jnp.exp(-x))


def _neg_expm1(z):
    series = -z * (1.0 + z / 2.0 * (1.0 + z / 3.0 * (1.0 + z / 4.0 * (1.0 + z / 5.0 * (1.0 + z / 6.0 * (1.0 + z / 7.0))))))
    return jnp.where(z > -0.1, series, 1.0 - jnp.exp(z))


def _gelu(x):
    return x * (0.5 * (1.0 + jnp.tanh(math.sqrt(2.0 / math.pi) * (x + 0.044715 * (x * x * x)))))


def _rms(x, g):
    ms = jnp.mean(x * x, axis=-1, keepdims=True)
    return x * lax.rsqrt(ms + EPS) * g


def _rows8(y, fn):
    r, c = y.shape
    return fn(y.reshape(r // SUB, SUB, c)).reshape(r, c)


def _adaln(x, g, scale, shift):
    return _rows8(_rms(x, g), lambda y: y * (1.0 + scale)[None] + shift[None])


def _gated(x, y, g, gate):
    return x + _rows8(_rms(y, g), lambda z: z * gate[None])


def _store_row_tiles(ref, piece):
    rows = ref.shape[0] // ROW_TILE
    for c in range(ROW_TILE):
        ref[pl.ds(c, rows, stride=ROW_TILE), :] = piece(slice(c * LANE, (c + 1) * LANE))


def _load_row_tiles(ref, c, rows):
    return ref[pl.ds(c, rows, stride=ROW_TILE), :]


def _full(shape):
    nd = len(shape)
    return pl.BlockSpec(shape, lambda *_: (0,) * nd)


def _mod_spec(k, grp):
    return pl.BlockSpec((1, SUB, D), lambda i, *_: (grp(i), 0, k))


def _grp_tm(tm):
    return lambda i: (i * tm >= NP).astype(I32)


def _grp_bm(tm):
    return lambda i: jnp.where(i * tm < NP, 0, 1 + (i * tm - NP) // DEC_SEQ)


def _mod_kernel(c_ref, w_ref, b_ref, o_ref):
    c = c_ref[...]
    s = c * _sigmoid(c)
    o_ref[0] = _dot(s.astype(BF16), w_ref[0].astype(BF16)) + b_ref[0]


def _modulation(cond, w_mod, b_mod):
    depth = w_mod.shape[0]
    rows = cond.shape[0]
    return pl.pallas_call(
        _mod_kernel,
        grid=(depth, 6),
        in_specs=[_full((rows, D)),
                  pl.BlockSpec((1, D, D), lambda l, j: (l, 0, j)),
                  pl.BlockSpec((1, 1, D), lambda l, j: (l, 0, j))],
        out_specs=pl.BlockSpec((1, rows, D), lambda l, j: (l, 0, j)),
        out_shape=jax.ShapeDtypeStruct((depth, rows, 6 * D), F32),
        compiler_params=_cparams(("arbitrary", "arbitrary")),
        name="modulation",
    )(cond, w_mod, b_mod.reshape(depth, 1, 6 * D))


def _inproj_kernel(x_ref, g_ref, sh_ref, sc_ref, w_ref, o_ref):
    h = _adaln(x_ref[...], g_ref[...], sc_ref[0], sh_ref[0])
    o_ref[...] = _dot(h.astype(BF16), w_ref[...])


def _ab_inproj(x, gain, mods, w_in):
    tm = 512
    nout = w_in.shape[1]
    grp = _grp_tm(tm)
    return pl.pallas_call(
        _inproj_kernel,
        grid=(N // tm,),
        in_specs=[pl.BlockSpec((tm, D), lambda i: (i, 0)),
                  _full((1, D)),
                  _mod_spec(0, grp), _mod_spec(1, grp),
                  _full((D, nout))],
        out_specs=pl.BlockSpec((tm, nout), lambda i: (i, 0)),
        out_shape=jax.ShapeDtypeStruct((N, nout), F32),
        compiler_params=_cparams(("arbitrary",)),
        name="ab_inproj",
    )(x, gain, mods, mods, w_in)


def _scan_table(reverse):
    cols = []
    groups = [(0, SEQ // T_CHUNK, 0), (1, SEQ // T_CHUNK, SEQ // T_CHUNK),
              (2, DEC_SEQ // T_CHUNK, NP // R_CHUNK)]
    for g, nc, base in groups:
        order = range(nc - 1, -1, -1) if reverse else range(nc)
        for k, c in enumerate(order):
            cols.append((base + c, g, int(k == 0), int(c > 0), int(c < nc - 1)))
    return np.asarray(cols, np.int32).T.copy()


def _scan_kernel(tbl, xa_ref, xp_ref, xn_ref, xb_ref, h0l_ref, h0r_ref, h0i_ref,
                 cw_ref, cb_ref, wg_ref, bg_ref, lam_ref, bm_ref, ar_ref, ai_ref, cm_ref,
                 ha_ref, y_ref, ll_ref, lr_ref, li_ref,
                 ext, abuf, bbuf, hre, him, hl, sre, sim, *, reverse):
    s = pl.program_id(0)

    @pl.when(tbl[2, s] == 1)
    def _():
        hl[...] = h0l_ref[0]
        sre[...] = h0r_ref[0]
        sim[...] = h0i_ref[0]

    ext[0:2 * SUB] = jnp.where(tbl[3, s] == 1, xp_ref[...], 0.0)
    ext[2 * SUB:2 * SUB + R_CHUNK] = xa_ref[...]
    ext[2 * SUB + R_CHUNK:3 * SUB + R_CHUNK] = jnp.where(tbl[4, s] == 1, xn_ref[...], 0.0)
    xa = cb_ref[...] + cw_ref[0:1] * ext[0:R_CHUNK]
    for k in range(1, CONV_W):
        xa = xa + cw_ref[k:k + 1] * ext[k * SUB:k * SUB + R_CHUNK]

    gz = _dot(xa.astype(BF16), wg_ref[...]) + bg_ref[...]
    r = _sigmoid(gz[:, :LRU_W])
    i = _sigmoid(gz[:, LRU_W:])
    lam = lam_ref[...]
    log_sig = jnp.minimum(lam, 0.0) - jnp.log1p(jnp.exp(-jnp.abs(lam)))
    log_a = LRU_C * r * log_sig
    abuf[...] = jnp.exp(log_a)
    bbuf[...] = jnp.sqrt(_neg_expm1(2.0 * log_a)) * (i * xa)

    order = range(T_CHUNK - 1, -1, -1) if reverse else range(T_CHUNK)
    h = hl[...]
    for t in order:
        rows = slice(t * SUB, (t + 1) * SUB)
        h = abuf[rows] * h + bbuf[rows]
        ha_ref[rows, :] = h
    hl[...] = h
    ll_ref[0] = h

    ub = xb_ref[...].astype(BF16)
    half = S5_N // 2
    for j in range(2):
        bu = _dot(ub[:, j * 256:(j + 1) * 256], bm_ref[j])
        hre[:, j * half:(j + 1) * half] = bu[:, :half]
        him[:, j * half:(j + 1) * half] = bu[:, half:]
    cblk = 4 * LANE
    for cb in range(S5_N // cblk):
        cols = slice(cb * cblk, (cb + 1) * cblk)
        ar = jnp.broadcast_to(ar_ref[:, cols], (SUB, cblk))
        ai = jnp.broadcast_to(ai_ref[:, cols], (SUB, cblk))
        hr = sre[:, cols]
        hi = sim[:, cols]
        for t in order:
            rows = slice(t * SUB, (t + 1) * SUB)
            nr = ar * hr - ai * hi + hre[rows, cols]
            ni = ar * hi + ai * hr + him[rows, cols]
            hr, hi = nr, ni
            hre[rows, cols] = hr
            him[rows, cols] = hi
        sre[:, cols] = hr
        sim[:, cols] = hi
    lr_ref[0] = sre[...]
    li_ref[0] = sim[...]
    for j in range(2):
        hc = jnp.concatenate([hre[:, j * half:(j + 1) * half], him[:, j * half:(j + 1) * half]],
                             axis=1).astype(BF16)
        y_ref[:, j * 256:(j + 1) * 256] = _dot(hc, cm_ref[j])


def _ab_scan(xz, h0l, h0r, h0i, cw, cb, wg, bg, lam, bm, ar, ai, cm, reverse):
    tbl = jnp.asarray(_scan_table(reverse))
    n_steps = tbl.shape[1]
    blk = lambda s, t: t[0, s]
    grp = lambda s, t: t[1, s]
    state_spec = lambda w: pl.BlockSpec((1, SUB, w), lambda s, t: (grp(s, t), 0, 0))
    const = lambda shape: pl.BlockSpec(shape, lambda s, t: (0,) * len(shape))
    grid_spec = pltpu.PrefetchScalarGridSpec(
        num_scalar_prefetch=1,
        grid=(n_steps,),
        in_specs=[
            pl.BlockSpec((R_CHUNK, LRU_W), lambda s, t: (blk(s, t), 0)),
            pl.BlockSpec((2 * SUB, LRU_W), lambda s, t: (jnp.maximum(blk(s, t) * (T_CHUNK // 2) - 1, 0), 0)),
            pl.BlockSpec((SUB, LRU_W), lambda s, t: (jnp.minimum((blk(s, t) + 1) * T_CHUNK, N // SUB - 1), 0)),
            pl.BlockSpec((R_CHUNK, S5_W), lambda s, t: (blk(s, t), 2)),
            state_spec(LRU_W), state_spec(S5_N), state_spec(S5_N),
            const((CONV_W, LRU_W)), const((1, LRU_W)),
            const((LRU_W, 2 * LRU_W)), const((1, 2 * LRU_W)), const((1, LRU_W)),
            const((2, 256, S5_N)), const((1, S5_N)), const((1, S5_N)), const((2, S5_N, 256)),
        ],
        out_specs=[
            pl.BlockSpec((R_CHUNK, LRU_W), lambda s, t: (blk(s, t), 0)),
            pl.BlockSpec((R_CHUNK, S5_W), lambda s, t: (blk(s, t), 0)),
            state_spec(LRU_W), state_spec(S5_N), state_spec(S5_N),
        ],
        scratch_shapes=[
            pltpu.VMEM((R_CHUNK + 3 * SUB, LRU_W), F32),
            pltpu.VMEM((R_CHUNK, LRU_W), F32), pltpu.VMEM((R_CHUNK, LRU_W), F32),
            pltpu.VMEM((R_CHUNK, S5_N), F32), pltpu.VMEM((R_CHUNK, S5_N), F32),
            pltpu.VMEM((SUB, LRU_W), F32), pltpu.VMEM((SUB, S5_N), F32), pltpu.VMEM((SUB, S5_N), F32),
        ],
    )
    return pl.pallas_call(
        functools.partial(_scan_kernel, reverse=reverse),
        grid_spec=grid_spec,
        out_shape=[jax.ShapeDtypeStruct((N, LRU_W), F32), jax.ShapeDtypeStruct((N, S5_W), F32),
                   jax.ShapeDtypeStruct((3, SUB, LRU_W), F32),
                   jax.ShapeDtypeStruct((3, SUB, S5_N), F32), jax.ShapeDtypeStruct((3, SUB, S5_N), F32)],
        compiler_params=_cparams(("arbitrary",)),
        name="ab_scan_bwd" if reverse else "ab_scan_fwd",
    )(tbl, xz, xz, xz, xz, h0l, h0r, h0i, cw, cb, wg, bg, lam, bm, ar, ai, cm)


def _about_kernel(haf, hab, yf, yr, ga, xb, x_ref, gate, g1, d_ref, wglu, bglu, wout, o_ref):
    ya = (haf[...] + hab[...]) * _gelu(ga[...])
    yb0 = _gelu(yf[...] + yr[...] + d_ref[...] * xb[...])
    yb = yb0 * _sigmoid(_dot(yb0.astype(BF16), wglu[...]) + bglu[...])
    out = _dot(ya.astype(BF16), wout[0:LRU_W]) + _dot(yb.astype(BF16), wout[LRU_W:LRU_W + S5_W])
    o_ref[...] = _gated(x_ref[...], out, g1[...], gate[0])


def _ab_out(haf, hab, yf, yr, xz, x, mods, g1, s5_d, wglu, bglu, wout):
    tm = 512
    grp = _grp_tm(tm)
    half = lambda c: pl.BlockSpec((tm, LRU_W), lambda i: (i, c))
    return pl.pallas_call(
        _about_kernel,
        grid=(N // tm,),
        in_specs=[half(0), half(0), half(0), half(0), half(1), half(2),
                  pl.BlockSpec((tm, D), lambda i: (i, 0)),
                  _mod_spec(2, grp), _full((1, D)), _full((1, S5_W)),
                  _full((S5_W, S5_W)), _full((1, S5_W)), _full((LRU_W + S5_W, D))],
        out_specs=pl.BlockSpec((tm, D), lambda i: (i, 0)),
        out_shape=jax.ShapeDtypeStruct((N, D), F32),
        compiler_params=_cparams(("arbitrary",)),
        name="ab_out",
    )(haf, hab, yf, yr, xz, xz, x, mods, g1, s5_d, wglu, bglu, wout)


def _swiglu_resident(hbf, wgu, wd, acc, between=None):
    for j in range(N_FF):
        h = hbf[...]
        g = _dot(h, wgu[:, j * FF_BLK:(j + 1) * FF_BLK])
        u = _dot(h, wgu[:, D_FF + j * FF_BLK:D_FF + (j + 1) * FF_BLK])
        act = (g * _sigmoid(g)) * u
        part = _dot(act.astype(BF16), wd[j * FF_BLK:(j + 1) * FF_BLK, :])
        if j == 0:
            acc[...] = part
        else:
            acc[...] += part
        if between is not None:
            between(j)


def _ffn_kernel(x_ref, sh, sc, gt, g2, g3, wgu, wd, o_ref, hbf, acc):
    hbf[...] = _adaln(x_ref[...], g2[...], sc[0], sh[0]).astype(BF16)
    _swiglu_resident(hbf, wgu, wd, acc)
    o_ref[...] = _gated(x_ref[...], acc[...], g3[...], gt[0])


def _ffn(x, mods, g2, g3, wgu, wd):
    tm = 1024
    grp = _grp_tm(tm)
    once = lambda shape: pl.BlockSpec(shape, lambda i: (0,) * len(shape), pipeline_mode=pl.Buffered(1))
    return pl.pallas_call(
        _ffn_kernel,
        grid=(N // tm,),
        in_specs=[pl.BlockSpec((tm, D), lambda i: (i, 0)),
                  _mod_spec(3, grp), _mod_spec(4, grp), _mod_spec(5, grp),
                  _full((1, D)), _full((1, D)),
                  once(wgu.shape), once(wd.shape)],
        out_specs=pl.BlockSpec((tm, D), lambda i: (i, 0)),
        out_shape=jax.ShapeDtypeStruct((N, D), F32),
        scratch_shapes=[pltpu.VMEM((tm, D), BF16), pltpu.VMEM((tm, D), F32)],
        compiler_params=_cparams(("arbitrary",)),
        name="ffn",
    )(x, mods, mods, mods, g2, g3, wgu, wd)


def _pair_swap(x):
    outs = []
    for c in range(x.shape[1] // LANE):
        xc = x[:, c * LANE:(c + 1) * LANE]
        even = lax.broadcasted_iota(I32, xc.shape, 1) % 2 == 0
        outs.append(jnp.where(even, pltpu.roll(xc, LANE - 1, 1), pltpu.roll(xc, 1, 1)))
    return outs[0] if len(outs) == 1 else jnp.concatenate(outs, axis=1)


def _mlaproj_kernel(x_ref, g0, sh, sc, w1, gq, gkv, wuq, wukv, cq_ref, sq_ref, ck_ref, sk_ref,
                    qn_ref, qr_ref, kn_ref, v_ref, kr2_ref, ckv_ref, krr_ref, *, prompt_steps):
    h = _adaln(x_ref[...], g0[...], sc[0], sh[0])
    dn = _dot(h.astype(BF16), w1[...])
    cq = _rms(dn[:, :Q_LORA], gq[...])
    ckv = _rms(dn[:, Q_LORA:Q_LORA + KV_LORA], gkv[...])
    krp = dn[:, Q_LORA + KV_LORA:]

    @pl.when(pl.program_id(0) < prompt_steps)
    def _():
        ckv_ref[...] = ckv
        krr_ref[...] = krp
    q = _dot(cq.astype(BF16), wuq[...])
    qn_ref[...] = q[:, :HEADS * QK_NOPE].astype(BF16)
    qr = q[:, HEADS * QK_NOPE:]
    qr_ref[...] = (qr * cq_ref[...] + _pair_swap(qr) * sq_ref[...]).astype(BF16)
    kv = _dot(ckv.astype(BF16), wukv[...])
    kn_ref[...] = kv[:, :HEADS * QK_NOPE].astype(BF16)
    v_ref[...] = kv[:, HEADS * QK_NOPE:].astype(BF16)
    kr = krp * ck_ref[...] + _pair_swap(krp) * sk_ref[...]
    kr2_ref[...] = jnp.concatenate([kr, pltpu.roll(kr, QK_ROPE, 1)], axis=1).astype(BF16)


def _mla_proj(x, g0, mods, w1, gq, gkv, wuq, wukv, cos_q, sin_q, cos_k, sin_k):
    tm = 256
    grp = _grp_bm(tm)
    n_pos = DEC_SEQ // tm
    tab = lambda w: pl.BlockSpec((tm, w), lambda i: (jnp.where(i * tm < NP, n_pos, (i - NP // tm) % n_pos), 0))
    row = lambda w: pl.BlockSpec((tm, w), lambda i: (i, 0))
    shp = lambda w, dt: jax.ShapeDtypeStruct((N, w), dt)
    n_p = NP // tm
    prow = lambda w: pl.BlockSpec((tm, w), lambda i: (jnp.minimum(i, n_p - 1), 0))
    return pl.pallas_call(
        functools.partial(_mlaproj_kernel, prompt_steps=n_p),
        grid=(N // tm,),
        in_specs=[row(D), _full((1, D)), _mod_spec(0, grp), _mod_spec(1, grp),
                  _full(w1.shape), _full((1, Q_LORA)), _full((1, KV_LORA)),
                  _full(wuq.shape), _full(wukv.shape),
                  tab(HEADS * QK_ROPE), tab(HEADS * QK_ROPE), tab(LANE), tab(LANE)],
        out_specs=[row(HEADS * QK_NOPE), row(HEADS * QK_ROPE), row(HEADS * QK_NOPE), row(HEADS * V_DIM),
                   row(2 * LANE), prow(KV_LORA), prow(LANE)],
        out_shape=[shp(HEADS * QK_NOPE, BF16), shp(HEADS * QK_ROPE, BF16), shp(HEADS * QK_NOPE, BF16),
                   shp(HEADS * V_DIM, BF16), shp(2 * LANE, BF16),
                   jax.ShapeDtypeStruct((NP, KV_LORA), F32), jax.ShapeDtypeStruct((NP, LANE), F32)],
        compiler_params=_cparams(("arbitrary",)),
        name="mla_proj",
    )(x, g0, mods, mods, w1, gq, gkv, wuq, wukv, cos_q, sin_q, cos_k, sin_k)


def _cachekv_kernel(c_ref, w_ref, kn_ref, v_ref):
    kv = _dot(c_ref[...].astype(BF16), w_ref[...])
    kn_ref[...] = kv[:, :HEADS * QK_NOPE].astype(BF16)
    v_ref[...] = kv[:, HEADS * QK_NOPE:].astype(BF16)


def _cache_kv(ckv_cache, wukv):
    rows = ckv_cache.shape[0]
    tm = 512
    return pl.pallas_call(
        _cachekv_kernel,
        grid=(rows // tm,),
        in_specs=[pl.BlockSpec((tm, KV_LORA), lambda i: (i, 0)), _full(wukv.shape)],
        out_specs=[pl.BlockSpec((tm, HEADS * QK_NOPE), lambda i: (i, 0)),
                   pl.BlockSpec((tm, HEADS * V_DIM), lambda i: (i, 0))],
        out_shape=[jax.ShapeDtypeStruct((rows, HEADS * QK_NOPE), BF16),
                   jax.ShapeDtypeStruct((rows, HEADS * V_DIM), BF16)],
        compiler_params=_cparams(("arbitrary",)),
        name="cache_kv",
    )(ckv_cache, wukv)


def _attn_kernel(*refs, has_cache):
    if has_cache:
        qn, qr, kn, kr, v, knc, krc, vc, o_ref, s_scr = refs
        streams = [(knc, krc, vc), (kn, kr, v)]
    else:
        qn, qr, kn, kr, v, o_ref, s_scr = refs
        streams = [(kn, kr, v)]
    chunks = [(k1, k2, vv, c * KEY_BLK) for k1, k2, vv in streams for c in range(k1.shape[0] // KEY_BLK)]
    tq = qn.shape[0]
    a = (QK_NOPE + QK_ROPE) ** -0.5 * math.log2(math.e)
    for hh in range(HEAD_GRP):
        cols = slice(hh * LANE, (hh + 1) * LANE)
        pair_cols = slice((hh // 2) * LANE, (hh // 2 + 1) * LANE)
        kr_cols = slice((hh % 2) * LANE, (hh % 2 + 1) * LANE)
        q = jnp.concatenate([qn[:, cols], qr[:, pair_cols]], axis=1)
        mx = jnp.full((tq, LANE), -jnp.inf, F32)
        for n, (k1, k2, _, r0) in enumerate(chunks):
            k = jnp.concatenate([k1[r0:r0 + KEY_BLK, cols], k2[r0:r0 + KEY_BLK, kr_cols]], axis=1)
            s = lax.dot_general(q, k, NT_DIMS, preferred_element_type=F32)
            s_scr[hh, :, n * KEY_BLK:(n + 1) * KEY_BLK] = s
            for c in range(KEY_BLK // LANE):
                mx = jnp.maximum(mx, s[:, c * LANE:(c + 1) * LANE])
        mb = jnp.max(mx, axis=-1, keepdims=True) * a
        den = jnp.zeros((tq, LANE), F32)
        o = jnp.zeros((tq, V_DIM), F32)
        for n, (_, _, vv, r0) in enumerate(chunks):
            p = jnp.exp2(s_scr[hh, :, n * KEY_BLK:(n + 1) * KEY_BLK] * a - mb)
            for c in range(KEY_BLK // LANE):
                den = den + p[:, c * LANE:(c + 1) * LANE]
            o = o + _dot(p.astype(BF16), vv[r0:r0 + KEY_BLK, cols])
        o_ref[:, cols] = (o / jnp.sum(den, axis=-1, keepdims=True)).astype(BF16)


def _attention(qn, qr, kn, kr2, v, *, row0, n_seq, seq, tq, cache=None):
    nq = seq // tq
    grp = HEAD_GRP * LANE
    qblk = lambda b, h, i: row0 // tq + b * nq + i
    kblk = lambda b: row0 // seq + b
    in_specs = [pl.BlockSpec((tq, grp), lambda b, h, i: (qblk(b, h, i), h)),
                pl.BlockSpec((tq, grp // 2), lambda b, h, i: (qblk(b, h, i), h)),
                pl.BlockSpec((seq, grp), lambda b, h, i: (kblk(b), h)),
                pl.BlockSpec((seq, 2 * LANE), lambda b, h, i: (kblk(b), 0)),
                pl.BlockSpec((seq, grp), lambda b, h, i: (kblk(b), h))]
    args = [qn, qr, kn, kr2, v]
    if cache is not None:
        knc, kr2c, vc = cache
        in_specs += [pl.BlockSpec((PAST_LEN, grp), lambda b, h, i: (b, h)),
                     pl.BlockSpec((PAST_LEN, 2 * LANE), lambda b, h, i: (b, 0)),
                     pl.BlockSpec((PAST_LEN, grp), lambda b, h, i: (b, h))]
        args += [knc, kr2c, vc]
    return pl.pallas_call(
        functools.partial(_attn_kernel, has_cache=cache is not None),
        grid=(n_seq, HEADS // HEAD_GRP, nq),
        in_specs=in_specs,
        out_specs=pl.BlockSpec((tq, grp), lambda b, h, i: (b * nq + i, h)),
        out_shape=jax.ShapeDtypeStruct((n_seq * seq, HEADS * V_DIM), BF16),
        scratch_shapes=[pltpu.VMEM((HEAD_GRP, tq, seq + (PAST_LEN if cache is not None else 0)), F32)],
        compiler_params=_cparams(("arbitrary", "arbitrary", "arbitrary")),
        name="attn_latent" if cache is not None else "attn_context",
    )(*args)


def _router_kernel(op_ref, os_ref, x_ref, gate1, g1, sh2, sc2, g2, wout, wr_hi, wr_lo, tri,
                   x3_ref, h_ref, ri_ref, rf_ref, cnt_ref, carry, *, prompt_steps):
    step = pl.program_id(0)

    @pl.when(step == 0)
    def _():
        carry[...] = jnp.zeros_like(carry)

    o = jnp.where(step < prompt_steps, op_ref[...], os_ref[...])
    x3 = _gated(x_ref[...], _dot(o, wout[...]), g1[...], gate1[0])
    x3_ref[...] = x3
    h = _adaln(x3, g2[...], sc2[0], sh2[0])
    _store_row_tiles(h_ref, lambda cols: h[:, cols])
    h_hi = h.astype(BF16)
    h_lo = (h - h_hi.astype(F32)).astype(BF16)
    dg = lambda a, b: lax.dot_general(a, b, NT_DIMS, preferred_element_type=F32)
    lg = dg(wr_hi[...], h_hi) + dg(wr_hi[...], h_lo) + dg(wr_lo[...], h_hi)
    eidx = lax.broadcasted_iota(I32, lg.shape, 0).astype(F32)
    m1 = jnp.max(lg, axis=0, keepdims=True)
    i1 = jnp.min(jnp.where(lg == m1, eidx, float(N_EXP)), axis=0, keepdims=True)
    sel1 = eidx == i1
    lg2 = jnp.where(sel1, -jnp.inf, lg)
    m2 = jnp.max(lg2, axis=0, keepdims=True)
    i2 = jnp.min(jnp.where(lg2 == m2, eidx, float(N_EXP)), axis=0, keepdims=True)
    sel2 = eidx == i2
    e = jnp.exp(m2 - m1)
    w1 = 1.0 / (1.0 + e)
    w2 = e / (1.0 + e)
    picked = jnp.where(sel1 | sel2, 1.0, 0.0)
    rank = _dot(picked.astype(BF16), tri[...]) + carry[:, 0:1]
    r1 = jnp.sum(jnp.where(sel1, rank, 0.0), axis=0, keepdims=True)
    r2 = jnp.sum(jnp.where(sel2, rank, 0.0), axis=0, keepdims=True)
    carry[...] = carry[...] + jnp.sum(picked, axis=1, keepdims=True)
    cnt_ref[...] = carry[...]
    ri_ref[...] = jnp.where(eidx == 0.0, i1, jnp.where(eidx == 1.0, i2, jnp.where(eidx == 2.0, r1, r2))).astype(I32)
    rf_ref[...] = jnp.where(eidx == 0.0, w1, w2)


def _attn_out_router(o_p, o_s, x, mods, g1, g2, wout, wr_hi, wr_lo):
    tm = 512
    grp = _grp_bm(tm)
    n_p = NP // tm
    o_specs = [pl.BlockSpec((tm, HEADS * V_DIM), lambda i: (jnp.minimum(i, n_p - 1), 0)),
               pl.BlockSpec((tm, HEADS * V_DIM), lambda i: (jnp.maximum(i - n_p, 0), 0))]
    tri = jnp.asarray(np.triu(np.ones((tm, tm), np.float32), 1), BF16)
    row = lambda w: pl.BlockSpec((tm, w), lambda i: (i, 0))
    col = pl.BlockSpec((N_EXP, tm), lambda i: (0, i))
    return pl.pallas_call(
        functools.partial(_router_kernel, prompt_steps=n_p),
        grid=(N // tm,),
        in_specs=o_specs + [row(D), _mod_spec(2, grp), _full((1, D)),
                  _mod_spec(3, grp), _mod_spec(4, grp), _full((1, D)),
                  _full((HEADS * V_DIM, D)), _full((N_EXP, D)), _full((N_EXP, D)), _full((tm, tm))],
        out_specs=[row(D), pl.BlockSpec((tm * ROW_TILE, LANE), lambda i: (i, 0)), col, col, _full((N_EXP, LANE))],
        out_shape=[jax.ShapeDtypeStruct((N, D), F32), jax.ShapeDtypeStruct((N * ROW_TILE, LANE), F32),
                   jax.ShapeDtypeStruct((N_EXP, N), I32), jax.ShapeDtypeStruct((N_EXP, N), F32),
                   jax.ShapeDtypeStruct((N_EXP, LANE), F32)],
        scratch_shapes=[pltpu.VMEM((N_EXP, LANE), F32)],
        compiler_params=_cparams(("arbitrary",)),
        name="attn_out_router",
    )(o_p, o_s, x, mods, g1, mods, mods, g2, wout, wr_hi, wr_lo, tri)


def _tile_rows(r):
    return r * ROW_TILE if isinstance(r, int) else pl.multiple_of(r * ROW_TILE, ROW_TILE)


def _moe_kernel(te, nu, src0_ref, src1_ref, dstp_ref, dstc_ref, h_hbm, wgu, wd, y_hbm,
                hsbuf, ybuf, hbf, acc, sem_g, sem_s):
    i = pl.program_id(0)
    last = pl.num_programs(0) - 1
    cur = i % 2
    nxt = 1 - cur
    buf_rows = TM_E * ROW_TILE

    def gather(idx_ref, r, slot):
        i_src = pl.multiple_of(idx_ref[0, 0, r] * ROW_TILE, ROW_TILE)
        return pltpu.make_async_copy(h_hbm.at[pl.ds(i_src, ROW_TILE), :],
                                     hsbuf.at[slot, pl.ds(_tile_rows(r), ROW_TILE), :], sem_g.at[slot])

    def scatter(idx_ref, r, slot):
        i_dst = pl.multiple_of(idx_ref[0, 0, r] * ROW_TILE, ROW_TILE)
        return pltpu.make_async_copy(ybuf.at[slot, pl.ds(_tile_rows(r), ROW_TILE), :],
                                     y_hbm.at[pl.ds(i_dst, ROW_TILE), :], sem_s.at[slot])

    def wait_gather(slot):
        pltpu.make_async_copy(h_hbm.at[pl.ds(0, buf_rows), :], hsbuf.at[slot], sem_g.at[slot]).wait()

    def wait_scatter(slot):
        pltpu.make_async_copy(ybuf.at[slot], y_hbm.at[pl.ds(0, buf_rows), :], sem_s.at[slot]).wait()

    def for_rows(fn):
        def body(g, carry):
            for k in range(ROW_TILE):
                fn(g * ROW_TILE + k, k % 2)
            return carry
        lax.fori_loop(0, TM_E // ROW_TILE, body, 0)

    def side_traffic(r, queue):
        gather(src1_ref, r, nxt).start(priority=queue)
        scatter(dstp_ref, r, nxt).start(priority=queue)

    @pl.when(i == 0)
    def _():
        ybuf[1] = jnp.zeros((buf_rows, LANE), F32)
        for_rows(lambda r, queue: gather(src0_ref, r, 0).start(priority=queue))

    wait_gather(cur)

    @pl.when(i < nu[0])
    def _():
        for c in range(ROW_TILE):
            hbf[:, c * LANE:(c + 1) * LANE] = hsbuf[cur, pl.ds(c, TM_E, stride=ROW_TILE), :].astype(BF16)
        per_block = TM_E // SIDE_BLOCKS

        def between(j):
            if j < SIDE_BLOCKS:
                for r in range(j * per_block, (j + 1) * per_block):
                    side_traffic(r, r % 2)

        _swiglu_resident(hbf, wgu.at[0], wd.at[0], acc, between)
        _store_row_tiles(ybuf.at[cur], lambda cols: acc[:, cols])

    @pl.when(i >= nu[0])
    def _():
        for_rows(side_traffic)
        ybuf[cur] = jnp.zeros((buf_rows, LANE), F32)

    wait_scatter(nxt)

    @pl.when(i == last)
    def _():
        for_rows(lambda r, queue: scatter(dstc_ref, r, cur).start(priority=queue))
        wait_scatter(cur)
        wait_gather(nxt)


def _moe_experts(tile_expert, n_used, src_tbl, dst_tbl, h, wgu, wd):
    smem = lambda off: pl.BlockSpec((1, 1, TM_E), lambda i, te, nu: (i + off, 0, 0), memory_space=pltpu.SMEM)
    grid_spec = pltpu.PrefetchScalarGridSpec(
        num_scalar_prefetch=2,
        grid=(N_TILES,),
        in_specs=[smem(0), smem(1), smem(0), smem(1),
                  pl.BlockSpec(memory_space=pl.ANY),
                  pl.BlockSpec((1,) + wgu.shape[1:], lambda i, te, nu: (te[i], 0, 0)),
                  pl.BlockSpec((1,) + wd.shape[1:], lambda i, te, nu: (te[i], 0, 0))],
        out_specs=pl.BlockSpec(memory_space=pl.ANY),
        scratch_shapes=[pltpu.VMEM((2, TM_E * ROW_TILE, LANE), F32), pltpu.VMEM((2, TM_E * ROW_TILE, LANE), F32),
                        pltpu.VMEM((TM_E, D), BF16), pltpu.VMEM((TM_E, D), F32),
                        pltpu.SemaphoreType.DMA((2,)), pltpu.SemaphoreType.DMA((2,))],
    )
    return pl.pallas_call(
        _moe_kernel,
        grid_spec=grid_spec,
        out_shape=jax.ShapeDtypeStruct(((2 * N + DUMP_ROWS) * ROW_TILE, LANE), F32),
        compiler_params=_cparams(("arbitrary",)),
        name="moe_experts",
    )(tile_expert, n_used, src_tbl, src_tbl, dst_tbl, dst_tbl, h, wgu, wd)


def _combine_kernel(y1_ref, y2_ref, x_ref, w1_ref, w2_ref, gate2, g3, op_ref, os_ref, *, prompt_steps):
    rows = x_ref.shape[0]
    w1 = w1_ref[...]
    w2 = w2_ref[...]
    f = jnp.concatenate([w1 * _load_row_tiles(y1_ref, c, rows) + w2 * _load_row_tiles(y2_ref, c, rows)
                         for c in range(ROW_TILE)], axis=1)
    out = _gated(x_ref[...], f, g3[...], gate2[0])
    step = pl.program_id(0)

    @pl.when(step < prompt_steps)
    def _():
        op_ref[...] = out

    @pl.when(step >= prompt_steps)
    def _():
        os_ref[...] = out


def _moe_combine(y, x, w1, w2, mods, g3):
    tm = 512
    grp = _grp_bm(tm)
    nb = N // tm
    n_p = NP // tm
    return pl.pallas_call(
        functools.partial(_combine_kernel, prompt_steps=n_p),
        grid=(nb,),
        in_specs=[pl.BlockSpec((tm * ROW_TILE, LANE), lambda i: (i, 0)),
                  pl.BlockSpec((tm * ROW_TILE, LANE), lambda i: (nb + i, 0)),
                  pl.BlockSpec((tm, D), lambda i: (i, 0)),
                  pl.BlockSpec((tm, 1), lambda i: (i, 0)), pl.BlockSpec((tm, 1), lambda i: (i, 0)),
                  _mod_spec(5, grp), _full((1, D))],
        out_specs=[pl.BlockSpec((tm, D), lambda i: (jnp.minimum(i, n_p - 1), 0)),
                   pl.BlockSpec((tm, D), lambda i: (jnp.maximum(i - n_p, 0), 0))],
        out_shape=[jax.ShapeDtypeStruct((NP, D), F32), jax.ShapeDtypeStruct((NS, D), F32)],
        compiler_params=_cparams(("arbitrary",)),
        name="moe_combine",
    )(y, y, x, w1, w2, mods, g3)


def _block_diag(w):
    hh, a, b = w.shape
    eye = jnp.eye(hh, dtype=w.dtype)
    return jnp.einsum('hab,hk->hakb', w, eye).reshape(hh * a, hh * b)


def _s5_matrices(a_re, a_im, log_dt, b_re, b_im, c_re, c_im):
    dt = jnp.exp(log_dt)[:, None]
    mag = jnp.exp(a_re * dt)
    abr = mag * jnp.cos(a_im * dt)
    abi = mag * jnp.sin(a_im * dt)
    den = a_re * a_re + a_im * a_im
    cr = ((abr - 1.0) * a_re + abi * a_im) / den
    ci = (abi * a_re - (abr - 1.0) * a_im) / den
    bbr = cr[..., None] * b_re - ci[..., None] * b_im
    bbi = cr[..., None] * b_im + ci[..., None] * b_re
    hg = S5_G // 2
    eye = jnp.eye(hg, dtype=F32)
    bms, cms = [], []
    for j in range(2):
        sl = slice(j * hg, (j + 1) * hg)
        bd = lambda m: jnp.einsum('gpc,gh->gchp', m[sl], eye).reshape(hg * S5_CH, hg * S5_P)
        bms.append(jnp.concatenate([bd(bbr), bd(bbi)], axis=1))
        cd = lambda m: jnp.einsum('gcp,gh->gphc', m[sl], eye).reshape(hg * S5_P, hg * S5_CH)
        cms.append(jnp.concatenate([cd(c_re), cd(-c_im)], axis=0))
    return (jnp.stack(bms).astype(BF16), abr.reshape(1, S5_N), abi.reshape(1, S5_N),
            jnp.stack(cms).astype(BF16))


def _rope_tables(tm):
    rows = DEC_SEQ // GRID_W
    row = jnp.repeat(jnp.arange(rows, dtype=F32), GRID_W)
    col = jnp.tile(jnp.arange(GRID_W, dtype=F32), rows)
    nf = QK_ROPE // 4
    inv = ROPE_THETA ** (-jnp.arange(nf, dtype=F32) / nf)
    ang = jnp.concatenate([row[:, None] * inv, col[:, None] * inv], axis=-1)
    cos = jnp.repeat(jnp.cos(ang), 2, axis=-1)
    sin = jnp.stack([-jnp.sin(ang), jnp.sin(ang)], axis=-1).reshape(DEC_SEQ, QK_ROPE)
    ident = lambda t, one: jnp.concatenate([t, jnp.full((tm, t.shape[1]), one, F32)], axis=0)
    cos_q = ident(jnp.tile(cos, (1, HEADS)), 1.0)
    sin_q = ident(jnp.tile(sin, (1, HEADS)), 0.0)
    pad = lambda t, one: jnp.concatenate([t, jnp.full((DEC_SEQ, LANE - QK_ROPE), one, F32)], axis=1)
    return cos_q, sin_q, ident(pad(cos, 1.0), 1.0), ident(pad(sin, 0.0), 0.0)


def _to_time_major(xp, xs):
    p = xp.reshape(BATCH // SUB, SUB, SEQ, -1).transpose(0, 2, 1, 3).reshape(NP, -1)
    s = xs.transpose(1, 0, 2).reshape(NS, -1)
    return jnp.concatenate([p, s], axis=0)


def _to_batch_major(x):
    p = x[:NP].reshape(BATCH // SUB, SEQ, SUB, -1).transpose(0, 2, 1, 3).reshape(NP, -1)
    s = x[NP:].reshape(DEC_SEQ, DEC_BATCH, -1).transpose(1, 0, 2).reshape(NS, -1)
    return jnp.concatenate([p, s], axis=0)


def _group_states(prompt_state, sample_state):
    w = sample_state.shape[-1]
    return jnp.concatenate([prompt_state.reshape(2, SUB, w), sample_state.reshape(1, SUB, w)], axis=0)


def _layer_ab(x, m, ng, j, state_lru, state_s5_re, state_s5_im, p):
    xz = _ab_inproj(x, ng[0:1], m, p['ab_w_in'][j].astype(BF16))
    zeros = lambda w: jnp.zeros((BATCH, w), F32)
    outs = []
    for d in range(2):
        wg = jnp.concatenate([_block_diag(p['lru_wa'][j, d]), _block_diag(p['lru_wx'][j, d])], axis=1).astype(BF16)
        bg = jnp.concatenate([p['lru_ba'][j, d], p['lru_bx'][j, d]])[None]
        bm, ar, ai, cm = _s5_matrices(p['s5_a_re'][j, d], p['s5_a_im'][j, d], p['s5_log_dt'][j, d],
                                      p['s5_b_re'][j, d], p['s5_b_im'][j, d], p['s5_c_re'][j, d], p['s5_c_im'][j, d])
        h0l = _group_states(zeros(LRU_W), state_lru[:, j, d])
        h0r = _group_states(zeros(S5_N), state_s5_re[:, j, d].reshape(DEC_BATCH, S5_N))
        h0i = _group_states(zeros(S5_N), state_s5_im[:, j, d].reshape(DEC_BATCH, S5_N))
        outs.append(_ab_scan(xz, h0l, h0r, h0i, p['ab_conv_w'][j], p['ab_conv_b'][j][None], wg, bg,
                             p['lru_lambda'][j, d][None], bm, ar, ai, cm, reverse=(d == 1)))
    (haf, yf, llf, lrf, lif), (hab, yr, llb, lrb, lib) = outs
    x = _ab_out(haf, hab, yf, yr, xz, x, m, ng[1:2], p['s5_d'][j][None], p['s5_w_glu'][j].astype(BF16),
                p['s5_b_glu'][j][None], p['ab_w_out'][j].astype(BF16))
    x = _ffn(x, m, ng[2:3], ng[3:4], p['ffn_w_gate_up'][j].astype(BF16), p['ffn_w_down'][j].astype(BF16))
    prompt = lambda f, b, w: jnp.stack([f[:2].reshape(BATCH, w), b[:2].reshape(BATCH, w)], axis=1)
    lru = prompt(llf, llb, LRU_W)
    s5r = prompt(lrf, lrb, S5_N).reshape(BATCH, 2, S5_G, S5_P)
    s5i = prompt(lif, lib, S5_N).reshape(BATCH, 2, S5_G, S5_P)
    return x, lru, s5r, s5i


def _head_major(w, parts):
    k = w.shape[0]
    per_head = w.reshape(k, HEADS, -1)
    out, start = [], 0
    for width in parts:
        out.append(per_head[:, :, start:start + width].reshape(k, HEADS * width))
        start += width
    return jnp.concatenate(out, axis=1)


def _layer_mla_moe(x, m, ng, j, cache_kv_latent, cache_k_rope, p):
    w1 = jnp.concatenate([p['mla_w_in'][j], jnp.zeros((D, LANE - QK_ROPE), F32)], axis=1).astype(BF16)
    wuq = _head_major(p['mla_w_uq'][j], (QK_NOPE, QK_ROPE)).astype(BF16)
    wukv = _head_major(p['mla_w_ukv'][j], (QK_NOPE, V_DIM)).astype(BF16)
    tables = _rope_tables(256)
    qn, qr, kn, v, kr2, ckv, krr = _mla_proj(x, ng[0:1], m, w1, p['mla_g_q'][j][None], p['mla_g_kv'][j][None],
                                              wuq, wukv, *tables)
    knc, vc = _cache_kv(cache_kv_latent[:, j].reshape(DEC_BATCH * PAST_LEN, KV_LORA), wukv)
    krc = cache_k_rope[:, j].reshape(DEC_BATCH * PAST_LEN, QK_ROPE)
    z = jnp.zeros_like(krc)
    kr2c = jnp.concatenate([krc, z, z, krc], axis=1).astype(BF16)
    o_p = _attention(qn, qr, kn, kr2, v, row0=0, n_seq=BATCH, seq=SEQ, tq=SEQ)
    o_s = _attention(qn, qr, kn, kr2, v, row0=NP, n_seq=DEC_BATCH, seq=DEC_SEQ, tq=512, cache=(knc, kr2c, vc))
    wr_t = p['moe_w_router'][j].T
    wr_hi = wr_t.astype(BF16)
    wr_lo = (wr_t - wr_hi.astype(F32)).astype(BF16)
    x3, h, ri, rf, cnt = _attn_out_router(o_p, o_s, x, m, ng[1:2], ng[2:3], p['mla_w_out'][j].astype(BF16), wr_hi, wr_lo)
    counts = cnt[:, 0].astype(I32)
    padded = ((counts + TM_E - 1) // TM_E) * TM_E
    ends = jnp.cumsum(padded)
    offs = ends - padded
    pos1 = offs[ri[0]] + ri[2]
    pos2 = offs[ri[1]] + ri[3]
    pick_tok = jnp.arange(2 * N, dtype=I32)
    dest = jnp.full((P_ROWS,), -1, I32).at[jnp.concatenate([pos1, pos2])].set(pick_tok, unique_indices=True)
    is_pad = dest < 0
    pad_row = 2 * N + TM_E + jnp.cumsum(is_pad.astype(I32)) - 1
    src_tbl = jnp.concatenate([jnp.where(is_pad, 0, dest % N), jnp.zeros((TM_E,), I32)])
    dst_tbl = jnp.concatenate([2 * N + jnp.arange(TM_E, dtype=I32), jnp.where(is_pad, pad_row, dest)])
    n_used = (ends[-1] // TM_E).astype(I32)[None]
    tile_row = jnp.minimum(jnp.arange(N_TILES, dtype=I32), n_used - 1) * TM_E
    tile_expert = jnp.sum((tile_row[:, None] >= ends[None, :]).astype(I32), axis=1)
    y = _moe_experts(tile_expert, n_used, src_tbl.reshape(N_TILES + 1, 1, TM_E), dst_tbl.reshape(N_TILES + 1, 1, TM_E),
                     h, p['moe_w_gate_up'][j].astype(BF16), p['moe_w_down'][j].astype(BF16))
    xp, xs = _moe_combine(y, x3, rf[0][:, None], rf[1][:, None], m, ng[3:4])
    kv_new = ckv.reshape(BATCH, SEQ, KV_LORA)
    kr_new = krr[:, :QK_ROPE].reshape(BATCH, SEQ, QK_ROPE)
    return (xp.reshape(BATCH, SEQ, D), xs.reshape(DEC_BATCH, DEC_SEQ, D)), kv_new, kr_new


def kernel(x_prompt, x_sample, c, state_lru, state_s5_re, state_s5_im, cache_kv_latent, cache_k_rope, c_ctx, w_mod, b_mod, norm_gains, ab_w_in, ab_conv_w, ab_conv_b, lru_wa, lru_ba, lru_wx, lru_bx, lru_lambda, s5_a_re, s5_a_im, s5_log_dt, s5_b_re, s5_b_im, s5_c_re, s5_c_im, s5_d, s5_w_glu, s5_b_glu, ab_w_out, ffn_w_gate_up, ffn_w_down, mla_w_in, mla_g_q, mla_g_kv, mla_w_uq, mla_w_ukv, mla_w_out, moe_w_router, moe_w_gate_up, moe_w_down):
    p = dict(ab_w_in=ab_w_in, ab_conv_w=ab_conv_w, ab_conv_b=ab_conv_b, lru_wa=lru_wa, lru_ba=lru_ba,
             lru_wx=lru_wx, lru_bx=lru_bx, lru_lambda=lru_lambda, s5_a_re=s5_a_re, s5_a_im=s5_a_im,
             s5_log_dt=s5_log_dt, s5_b_re=s5_b_re, s5_b_im=s5_b_im, s5_c_re=s5_c_re, s5_c_im=s5_c_im,
             s5_d=s5_d, s5_w_glu=s5_w_glu, s5_b_glu=s5_b_glu, ab_w_out=ab_w_out, ffn_w_gate_up=ffn_w_gate_up,
             ffn_w_down=ffn_w_down, mla_w_in=mla_w_in, mla_g_q=mla_g_q, mla_g_kv=mla_g_kv, mla_w_uq=mla_w_uq,
             mla_w_ukv=mla_w_ukv, mla_w_out=mla_w_out, moe_w_router=moe_w_router, moe_w_gate_up=moe_w_gate_up,
             moe_w_down=moe_w_down)
    depth = w_mod.shape[0]
    cond = jnp.concatenate([c_ctx[None], c, jnp.zeros((2 * SUB - 1 - DEC_BATCH, D), F32)], axis=0)
    mod = _modulation(cond, w_mod, b_mod)
    ctx_tile = lambda l: jnp.broadcast_to(mod[l, 0:1], (SUB, 6 * D))
    streams = (x_prompt, x_sample)
    lru_l, s5r_l, s5i_l, kv_l, kr_l = [], [], [], [], []
    for layer in range(depth):
        j = layer // 2
        ng = norm_gains[layer]
        if layer % 2 == 0:
            m = jnp.stack([ctx_tile(layer), mod[layer, 1:1 + DEC_BATCH]])
            x, lru, s5r, s5i = _layer_ab(_to_time_major(*streams), m, ng, j, state_lru, state_s5_re, state_s5_im, p)
            x = _to_batch_major(x)
            streams = (x[:NP].reshape(BATCH, SEQ, D), x[NP:].reshape(DEC_BATCH, DEC_SEQ, D))
            lru_l.append(lru)
            s5r_l.append(s5r)
            s5i_l.append(s5i)
        else:
            lat = jnp.broadcast_to(mod[layer, 1:1 + DEC_BATCH, None, :], (DEC_BATCH, SUB, 6 * D))
            m = jnp.concatenate([ctx_tile(layer)[None], lat], axis=0)
            x = jnp.concatenate([streams[0].reshape(NP, D), streams[1].reshape(NS, D)], axis=0)
            streams, kv_new, kr_new = _layer_mla_moe(x, m, ng, j, cache_kv_latent, cache_k_rope, p)
            kv_l.append(kv_new)
            kr_l.append(kr_new)
    return (streams[0], streams[1],
            jnp.stack(lru_l, axis=1), jnp.stack(s5r_l, axis=1), jnp.stack(s5i_l, axis=1),
            jnp.stack(kv_l, axis=1), jnp.stack(kr_l, axis=1))
```

```python
import functools
import math

import numpy as np
import jax
import jax.numpy as jnp
from jax import lax
from jax.experimental import pallas as pl
from jax.experimental.pallas import tpu as pltpu

F32 = jnp.float32
BF16 = jnp.bfloat16
I32 = jnp.int32

D = 1024
BATCH, SEQ = 16, 256
DEC_BATCH, DEC_SEQ = 8, 2048
PAST_LEN = 256
GRID_W = 64
LRU_W = 512
LRU_HEADS = 8
LRU_C = 8.0
CONV_W = 4
S5_W = 512
S5_CH = 16
S5_G = 32
S5_P = 64
S5_N = S5_G * S5_P
HEADS = 8
QK_NOPE, QK_ROPE, V_DIM = 128, 64, 128
Q_LORA, KV_LORA = 384, 256
ROPE_THETA = 10000.0
D_FF = 2816
N_EXP = 8
EPS = 1e-6

NP = BATCH * SEQ
NS = DEC_BATCH * DEC_SEQ
N = NP + NS
SUB = 8
LANE = 128
ROW_TILE = D // LANE
T_CHUNK = 32
R_CHUNK = T_CHUNK * SUB
FF_BLK = 256
KEY_BLK = 256
HEAD_GRP = 4
SIDE_ITERS = 4
N_FF = D_FF // FF_BLK
TM_E = 512
P_ROWS = 2 * N + N_EXP * TM_E
N_TILES = P_ROWS // TM_E
DUMP_ROWS = P_ROWS - 2 * N + TM_E
VMEM_LIMIT = 56 * 1024 * 1024

NT_DIMS = (((1,), (1,)), ((), ()))


def _cparams(sem):
    return pltpu.CompilerParams(dimension_semantics=sem, vmem_limit_bytes=VMEM_LIMIT)


def _dot(a, b):
    return jnp.dot(a, b, preferred_element_type=F32)


def _sigmoid(x):
    return 1.0 / (1.0 + jnp.exp(-x))


def _neg_expm1(z):
    series = -z * (1.0 + z / 2.0 * (1.0 + z / 3.0 * (1.0 + z / 4.0 * (1.0 + z / 5.0 * (1.0 + z / 6.0 * (1.0 + z / 7.0))))))
    return jnp.where(z > -0.1, series, 1.0 - jnp.exp(z))


def _gelu(x):
    return x * (0.5 * (1.0 + jnp.tanh(math.sqrt(2.0 / math.pi) * (x + 0.044715 * (x * x * x)))))


def _rms(x, g):
    ms = jnp.mean(x * x, axis=-1, keepdims=True)
    return x * lax.rsqrt(ms + EPS) * g


def _rows8(y, fn):
    r, c = y.shape
    return fn(y.reshape(r // SUB, SUB, c)).reshape(r, c)


def _adaln(x, g, scale, shift):
    return _rows8(_rms(x, g), lambda y: y * (1.0 + scale)[None] + shift[None])


def _gated(x, y, g, gate):
    return x + _rows8(_rms(y, g), lambda z: z * gate[None])


def _store_row_tiles(ref, piece):
    rows = ref.shape[0] // ROW_TILE
    for c in range(ROW_TILE):
        ref[pl.ds(c, rows, stride=ROW_TILE), :] = piece(slice(c * LANE, (c + 1) * LANE))


def _load_row_tiles(ref, c, rows):
    return ref[pl.ds(c, rows, stride=ROW_TILE), :]


def _full(shape):
    nd = len(shape)
    return pl.BlockSpec(shape, lambda *_: (0,) * nd)


def _mod_spec(k, grp):
    return pl.BlockSpec((1, SUB, D), lambda i, *_: (grp(i), 0, k))


def _grp_tm(tm):
    return lambda i: (i * tm >= NP).astype(I32)


def _grp_bm(tm):
    return lambda i: jnp.where(i * tm < NP, 0, 1 + (i * tm - NP) // DEC_SEQ)


def _mod_kernel(c_ref, w_ref, b_ref, o_ref):
    c = c_ref[...]
    s = c * _sigmoid(c)
    o_ref[0] = _dot(s.astype(BF16), w_ref[0].astype(BF16)) + b_ref[0]


def _modulation(cond, w_mod, b_mod):
    depth = w_mod.shape[0]
    rows = cond.shape[0]
    return pl.pallas_call(
        _mod_kernel,
        grid=(depth, 6),
        in_specs=[_full((rows, D)),
                  pl.BlockSpec((1, D, D), lambda l, j: (l, 0, j)),
                  pl.BlockSpec((1, 1, D), lambda l, j: (l, 0, j))],
        out_specs=pl.BlockSpec((1, rows, D), lambda l, j: (l, 0, j)),
        out_shape=jax.ShapeDtypeStruct((depth, rows, 6 * D), F32),
        compiler_params=_cparams(("arbitrary", "arbitrary")),
        name="modulation",
    )(cond, w_mod, b_mod.reshape(depth, 1, 6 * D))


def _inproj_kernel(x_ref, g_ref, sh_ref, sc_ref, w_ref, o_ref):
    h = _adaln(x_ref[...], g_ref[...], sc_ref[0], sh_ref[0])
    o_ref[...] = _dot(h.astype(BF16), w_ref[...])


def _ab_inproj(x, gain, mods, w_in):
    tm = 512
    nout = w_in.shape[1]
    grp = _grp_tm(tm)
    return pl.pallas_call(
        _inproj_kernel,
        grid=(N // tm,),
        in_specs=[pl.BlockSpec((tm, D), lambda i: (i, 0)),
                  _full((1, D)),
                  _mod_spec(0, grp), _mod_spec(1, grp),
                  _full((D, nout))],
        out_specs=pl.BlockSpec((tm, nout), lambda i: (i, 0)),
        out_shape=jax.ShapeDtypeStruct((N, nout), F32),
        compiler_params=_cparams(("arbitrary",)),
        name="ab_inproj",
    )(x, gain, mods, mods, w_in)


def _scan_table(reverse):
    cols = []
    groups = [(0, SEQ // T_CHUNK, 0), (1, SEQ // T_CHUNK, SEQ // T_CHUNK),
              (2, DEC_SEQ // T_CHUNK, NP // R_CHUNK)]
    for g, nc, base in groups:
        order = range(nc - 1, -1, -1) if reverse else range(nc)
        for k, c in enumerate(order):
            cols.append((base + c, g, int(k == 0), int(c > 0), int(c < nc - 1)))
    return np.asarray(cols, np.int32).T.copy()


def _scan_kernel(tbl, xa_ref, xp_ref, xn_ref, xb_ref, h0l_ref, h0r_ref, h0i_ref,
                 cw_ref, cb_ref, wg_ref, bg_ref, lam_ref, bm_ref, ar_ref, ai_ref, cm_ref,
                 ha_ref, y_ref, ll_ref, lr_ref, li_ref,
                 ext, abuf, bbuf, hre, him, hl, sre, sim, *, reverse):
    s = pl.program_id(0)

    @pl.when(tbl[2, s] == 1)
    def _():
        hl[...] = h0l_ref[0]
        sre[...] = h0r_ref[0]
        sim[...] = h0i_ref[0]

    ext[0:2 * SUB] = jnp.where(tbl[3, s] == 1, xp_ref[...], 0.0)
    ext[2 * SUB:2 * SUB + R_CHUNK] = xa_ref[...]
    ext[2 * SUB + R_CHUNK:3 * SUB + R_CHUNK] = jnp.where(tbl[4, s] == 1, xn_ref[...], 0.0)
    xa = cb_ref[...] + cw_ref[0:1] * ext[0:R_CHUNK]
    for k in range(1, CONV_W):
        xa = xa + cw_ref[k:k + 1] * ext[k * SUB:k * SUB + R_CHUNK]

    gz = _dot(xa.astype(BF16), wg_ref[...]) + bg_ref[...]
    r = _sigmoid(gz[:, :LRU_W])
    i = _sigmoid(gz[:, LRU_W:])
    lam = lam_ref[...]
    log_sig = jnp.minimum(lam, 0.0) - jnp.log1p(jnp.exp(-jnp.abs(lam)))
    log_a = LRU_C * r * log_sig
    abuf[...] = jnp.exp(log_a)
    bbuf[...] = jnp.sqrt(_neg_expm1(2.0 * log_a)) * (i * xa)

    order = range(T_CHUNK - 1, -1, -1) if reverse else range(T_CHUNK)
    h = hl[...]
    for t in order:
        rows = slice(t * SUB, (t + 1) * SUB)
        h = abuf[rows] * h + bbuf[rows]
        ha_ref[rows, :] = h
    hl[...] = h
    ll_ref[0] = h

    ub = xb_ref[...].astype(BF16)
    half = S5_N // 2
    for j in range(2):
        bu = _dot(ub[:, j * 256:(j + 1) * 256], bm_ref[j])
        hre[:, j * half:(j + 1) * half] = bu[:, :half]
        him[:, j * half:(j + 1) * half] = bu[:, half:]
    cblk = 4 * LANE
    for cb in range(S5_N // cblk):
        cols = slice(cb * cblk, (cb + 1) * cblk)
        ar = jnp.broadcast_to(ar_ref[:, cols], (SUB, cblk))
        ai = jnp.broadcast_to(ai_ref[:, cols], (SUB, cblk))
        hr = sre[:, cols]
        hi = sim[:, cols]
        for t in order:
            rows = slice(t * SUB, (t + 1) * SUB)
            nr = ar * hr - ai * hi + hre[rows, cols]
            ni = ar * hi + ai * hr + him[rows, cols]
            hr, hi = nr, ni
            hre[rows, cols] = hr
            him[rows, cols] = hi
        sre[:, cols] = hr
        sim[:, cols] = hi
    lr_ref[0] = sre[...]
    li_ref[0] = sim[...]
    for j in range(2):
        hc = jnp.concatenate([hre[:, j * half:(j + 1) * half], him[:, j * half:(j + 1) * half]],
                             axis=1).astype(BF16)
        y_ref[:, j * 256:(j + 1) * 256] = _dot(hc, cm_ref[j])


def _ab_scan(xz, h0l, h0r, h0i, cw, cb, wg, bg, lam, bm, ar, ai, cm, reverse):
    tbl = jnp.asarray(_scan_table(reverse))
    n_steps = tbl.shape[1]
    blk = lambda s, t: t[0, s]
    grp = lambda s, t: t[1, s]
    state_spec = lambda w: pl.BlockSpec((1, SUB, w), lambda s, t: (grp(s, t), 0, 0))
    const = lambda shape: pl.BlockSpec(shape, lambda s, t: (0,) * len(shape))
    grid_spec = pltpu.PrefetchScalarGridSpec(
        num_scalar_prefetch=1,
        grid=(n_steps,),
        in_specs=[
            pl.BlockSpec((R_CHUNK, LRU_W), lambda s, t: (blk(s, t), 0)),
            pl.BlockSpec((2 * SUB, LRU_W), lambda s, t: (jnp.maximum(blk(s, t) * (T_CHUNK // 2) - 1, 0), 0)),
            pl.BlockSpec((SUB, LRU_W), lambda s, t: (jnp.minimum((blk(s, t) + 1) * T_CHUNK, N // SUB - 1), 0)),
            pl.BlockSpec((R_CHUNK, S5_W), lambda s, t: (blk(s, t), 2)),
            state_spec(LRU_W), state_spec(S5_N), state_spec(S5_N),
            const((CONV_W, LRU_W)), const((1, LRU_W)),
            const((LRU_W, 2 * LRU_W)), const((1, 2 * LRU_W)), const((1, LRU_W)),
            const((2, 256, S5_N)), const((1, S5_N)), const((1, S5_N)), const((2, S5_N, 256)),
        ],
        out_specs=[
            pl.BlockSpec((R_CHUNK, LRU_W), lambda s, t: (blk(s, t), 0)),
            pl.BlockSpec((R_CHUNK, S5_W), lambda s, t: (blk(s, t), 0)),
            state_spec(LRU_W), state_spec(S5_N), state_spec(S5_N),
        ],
        scratch_shapes=[
            pltpu.VMEM((R_CHUNK + 3 * SUB, LRU_W), F32),
            pltpu.VMEM((R_CHUNK, LRU_W), F32), pltpu.VMEM((R_CHUNK, LRU_W), F32),
            pltpu.VMEM((R_CHUNK, S5_N), F32), pltpu.VMEM((R_CHUNK, S5_N), F32),
            pltpu.VMEM((SUB, LRU_W), F32), pltpu.VMEM((SUB, S5_N), F32), pltpu.VMEM((SUB, S5_N), F32),
        ],
    )
    return pl.pallas_call(
        functools.partial(_scan_kernel, reverse=reverse),
        grid_spec=grid_spec,
        out_shape=[jax.ShapeDtypeStruct((N, LRU_W), F32), jax.ShapeDtypeStruct((N, S5_W), F32),
                   jax.ShapeDtypeStruct((3, SUB, LRU_W), F32),
                   jax.ShapeDtypeStruct((3, SUB, S5_N), F32), jax.ShapeDtypeStruct((3, SUB, S5_N), F32)],
        compiler_params=_cparams(("arbitrary",)),
        name="ab_scan_bwd" if reverse else "ab_scan_fwd",
    )(tbl, xz, xz, xz, xz, h0l, h0r, h0i, cw, cb, wg, bg, lam, bm, ar, ai, cm)


def _about_kernel(haf, hab, yf, yr, ga, xb, x_ref, gate, g1, d_ref, wglu, bglu, wout, o_ref):
    ya = (haf[...] + hab[...]) * _gelu(ga[...])
    yb0 = _gelu(yf[...] + yr[...] + d_ref[...] * xb[...])
    yb = yb0 * _sigmoid(_dot(yb0.astype(BF16), wglu[...]) + bglu[...])
    out = _dot(ya.astype(BF16), wout[0:LRU_W]) + _dot(yb.astype(BF16), wout[LRU_W:LRU_W + S5_W])
    o_ref[...] = _gated(x_ref[...], out, g1[...], gate[0])


def _ab_out(haf, hab, yf, yr, xz, x, mods, g1, s5_d, wglu, bglu, wout):
    tm = 512
    grp = _grp_tm(tm)
    half = lambda c: pl.BlockSpec((tm, LRU_W), lambda i: (i, c))
    return pl.pallas_call(
        _about_kernel,
        grid=(N // tm,),
        in_specs=[half(0), half(0), half(0), half(0), half(1), half(2),
                  pl.BlockSpec((tm, D), lambda i: (i, 0)),
                  _mod_spec(2, grp), _full((1, D)), _full((1, S5_W)),
                  _full((S5_W, S5_W)), _full((1, S5_W)), _full((LRU_W + S5_W, D))],
        out_specs=pl.BlockSpec((tm, D), lambda i: (i, 0)),
        out_shape=jax.ShapeDtypeStruct((N, D), F32),
        compiler_params=_cparams(("arbitrary",)),
        name="ab_out",
    )(haf, hab, yf, yr, xz, xz, x, mods, g1, s5_d, wglu, bglu, wout)


def _swiglu_block(hbf, wgu, wd, acc, j):
    static = isinstance(j, int)
    blk = lambda start: pl.ds(start if static else pl.multiple_of(start, FF_BLK), FF_BLK)
    h = hbf[...]
    g = _dot(h, wgu[:, blk(j * FF_BLK)])
    u = _dot(h, wgu[:, blk(D_FF + j * FF_BLK)])
    act = (g * _sigmoid(g)) * u
    part = _dot(act.astype(BF16), wd[blk(j * FF_BLK), :])
    if static and j == 0:
        acc[...] = part
    else:
        acc[...] += part


def _ffn_kernel(x_ref, sh, sc, gt, g2, g3, wgu, wd, o_ref, hbf, acc):
    hbf[...] = _adaln(x_ref[...], g2[...], sc[0], sh[0]).astype(BF16)
    for j in range(N_FF):
        _swiglu_block(hbf, wgu, wd, acc, j)
    o_ref[...] = _gated(x_ref[...], acc[...], g3[...], gt[0])


def _ffn(x, mods, g2, g3, wgu, wd):
    tm = 1024
    grp = _grp_tm(tm)
    once = lambda shape: pl.BlockSpec(shape, lambda i: (0,) * len(shape), pipeline_mode=pl.Buffered(1))
    return pl.pallas_call(
        _ffn_kernel,
        grid=(N // tm,),
        in_specs=[pl.BlockSpec((tm, D), lambda i: (i, 0)),
                  _mod_spec(3, grp), _mod_spec(4, grp), _mod_spec(5, grp),
                  _full((1, D)), _full((1, D)),
                  once(wgu.shape), once(wd.shape)],
        out_specs=pl.BlockSpec((tm, D), lambda i: (i, 0)),
        out_shape=jax.ShapeDtypeStruct((N, D), F32),
        scratch_shapes=[pltpu.VMEM((tm, D), BF16), pltpu.VMEM((tm, D), F32)],
        compiler_params=_cparams(("arbitrary",)),
        name="ffn",
    )(x, mods, mods, mods, g2, g3, wgu, wd)


def _pair_swap(x):
    outs = []
    for c in range(x.shape[1] // LANE):
        xc = x[:, c * LANE:(c + 1) * LANE]
        even = lax.broadcasted_iota(I32, xc.shape, 1) % 2 == 0
        outs.append(jnp.where(even, pltpu.roll(xc, LANE - 1, 1), pltpu.roll(xc, 1, 1)))
    return outs[0] if len(outs) == 1 else jnp.concatenate(outs, axis=1)


def _mlaproj_kernel(x_ref, g0, sh, sc, w1, gq, gkv, wuq, wukv, cq_ref, sq_ref, ck_ref, sk_ref,
                    qn_ref, qr_ref, kn_ref, v_ref, kr2_ref, ckv_ref, krr_ref, *, prompt_steps):
    h = _adaln(x_ref[...], g0[...], sc[0], sh[0])
    dn = _dot(h.astype(BF16), w1[...])
    cq = _rms(dn[:, :Q_LORA], gq[...])
    ckv = _rms(dn[:, Q_LORA:Q_LORA + KV_LORA], gkv[...])
    krp = dn[:, Q_LORA + KV_LORA:]

    @pl.when(pl.program_id(0) < prompt_steps)
    def _():
        ckv_ref[...] = ckv
        krr_ref[...] = krp
    q = _dot(cq.astype(BF16), wuq[...])
    qn_ref[...] = q[:, :HEADS * QK_NOPE].astype(BF16)
    qr = q[:, HEADS * QK_NOPE:]
    qr_ref[...] = (qr * cq_ref[...] + _pair_swap(qr) * sq_ref[...]).astype(BF16)
    kv = _dot(ckv.astype(BF16), wukv[...])
    kn_ref[...] = kv[:, :HEADS * QK_NOPE].astype(BF16)
    v_ref[...] = kv[:, HEADS * QK_NOPE:].astype(BF16)
    kr = krp * ck_ref[...] + _pair_swap(krp) * sk_ref[...]
    kr2_ref[...] = jnp.concatenate([kr, pltpu.roll(kr, QK_ROPE, 1)], axis=1).astype(BF16)


def _mla_proj(x, g0, mods, w1, gq, gkv, wuq, wukv, cos_q, sin_q, cos_k, sin_k):
    tm = 256
    grp = _grp_bm(tm)
    n_pos = DEC_SEQ // tm
    tab = lambda w: pl.BlockSpec((tm, w), lambda i: (jnp.where(i * tm < NP, n_pos, (i - NP // tm) % n_pos), 0))
    row = lambda w: pl.BlockSpec((tm, w), lambda i: (i, 0))
    shp = lambda w, dt: jax.ShapeDtypeStruct((N, w), dt)
    n_p = NP // tm
    prow = lambda w: pl.BlockSpec((tm, w), lambda i: (jnp.minimum(i, n_p - 1), 0))
    return pl.pallas_call(
        functools.partial(_mlaproj_kernel, prompt_steps=n_p),
        grid=(N // tm,),
        in_specs=[row(D), _full((1, D)), _mod_spec(0, grp), _mod_spec(1, grp),
                  _full(w1.shape), _full((1, Q_LORA)), _full((1, KV_LORA)),
                  _full(wuq.shape), _full(wukv.shape),
                  tab(HEADS * QK_ROPE), tab(HEADS * QK_ROPE), tab(LANE), tab(LANE)],
        out_specs=[row(HEADS * QK_NOPE), row(HEADS * QK_ROPE), row(HEADS * QK_NOPE), row(HEADS * V_DIM),
                   row(2 * LANE), prow(KV_LORA), prow(LANE)],
        out_shape=[shp(HEADS * QK_NOPE, BF16), shp(HEADS * QK_ROPE, BF16), shp(HEADS * QK_NOPE, BF16),
                   shp(HEADS * V_DIM, BF16), shp(2 * LANE, BF16),
                   jax.ShapeDtypeStruct((NP, KV_LORA), F32), jax.ShapeDtypeStruct((NP, LANE), F32)],
        compiler_params=_cparams(("arbitrary",)),
        name="mla_proj",
    )(x, g0, mods, mods, w1, gq, gkv, wuq, wukv, cos_q, sin_q, cos_k, sin_k)


def _cachekv_kernel(c_ref, w_ref, kn_ref, v_ref):
    kv = _dot(c_ref[...].astype(BF16), w_ref[...])
    kn_ref[...] = kv[:, :HEADS * QK_NOPE].astype(BF16)
    v_ref[...] = kv[:, HEADS * QK_NOPE:].astype(BF16)


def _cache_kv(ckv_cache, wukv):
    rows = ckv_cache.shape[0]
    tm = 512
    return pl.pallas_call(
        _cachekv_kernel,
        grid=(rows // tm,),
        in_specs=[pl.BlockSpec((tm, KV_LORA), lambda i: (i, 0)), _full(wukv.shape)],
        out_specs=[pl.BlockSpec((tm, HEADS * QK_NOPE), lambda i: (i, 0)),
                   pl.BlockSpec((tm, HEADS * V_DIM), lambda i: (i, 0))],
        out_shape=[jax.ShapeDtypeStruct((rows, HEADS * QK_NOPE), BF16),
                   jax.ShapeDtypeStruct((rows, HEADS * V_DIM), BF16)],
        compiler_params=_cparams(("arbitrary",)),
        name="cache_kv",
    )(ckv_cache, wukv)


def _attn_kernel(*refs, has_cache):
    if has_cache:
        qn, qr, kn, kr, v, knc, krc, vc, o_ref, s_scr = refs
        streams = [(knc, krc, vc), (kn, kr, v)]
    else:
        qn, qr, kn, kr, v, o_ref, s_scr = refs
        streams = [(kn, kr, v)]
    chunks = [(k1, k2, vv, c * KEY_BLK) for k1, k2, vv in streams for c in range(k1.shape[0] // KEY_BLK)]
    tq = qn.shape[0]
    a = (QK_NOPE + QK_ROPE) ** -0.5 * math.log2(math.e)
    for hh in range(HEAD_GRP):
        cols = slice(hh * LANE, (hh + 1) * LANE)
        pair_cols = slice((hh // 2) * LANE, (hh // 2 + 1) * LANE)
        kr_cols = slice((hh % 2) * LANE, (hh % 2 + 1) * LANE)
        q = jnp.concatenate([qn[:, cols], qr[:, pair_cols]], axis=1)
        mx = jnp.full((tq, LANE), -jnp.inf, F32)
        for n, (k1, k2, _, r0) in enumerate(chunks):
            k = jnp.concatenate([k1[r0:r0 + KEY_BLK, cols], k2[r0:r0 + KEY_BLK, kr_cols]], axis=1)
            s = lax.dot_general(q, k, NT_DIMS, preferred_element_type=F32)
            s_scr[hh, :, n * KEY_BLK:(n + 1) * KEY_BLK] = s
            for c in range(KEY_BLK // LANE):
                mx = jnp.maximum(mx, s[:, c * LANE:(c + 1) * LANE])
        mb = jnp.max(mx, axis=-1, keepdims=True) * a
        den = jnp.zeros((tq, LANE), F32)
        o = jnp.zeros((tq, V_DIM), F32)
        for n, (_, _, vv, r0) in enumerate(chunks):
            p = jnp.exp2(s_scr[hh, :, n * KEY_BLK:(n + 1) * KEY_BLK] * a - mb)
            for c in range(KEY_BLK // LANE):
                den = den + p[:, c * LANE:(c + 1) * LANE]
            o = o + _dot(p.astype(BF16), vv[r0:r0 + KEY_BLK, cols])
        o_ref[:, cols] = (o / jnp.sum(den, axis=-1, keepdims=True)).astype(BF16)


def _attention(qn, qr, kn, kr2, v, *, row0, n_seq, seq, tq, cache=None):
    nq = seq // tq
    grp = HEAD_GRP * LANE
    qblk = lambda b, h, i: row0 // tq + b * nq + i
    kblk = lambda b: row0 // seq + b
    in_specs = [pl.BlockSpec((tq, grp), lambda b, h, i: (qblk(b, h, i), h)),
                pl.BlockSpec((tq, grp // 2), lambda b, h, i: (qblk(b, h, i), h)),
                pl.BlockSpec((seq, grp), lambda b, h, i: (kblk(b), h)),
                pl.BlockSpec((seq, 2 * LANE), lambda b, h, i: (kblk(b), 0)),
                pl.BlockSpec((seq, grp), lambda b, h, i: (kblk(b), h))]
    args = [qn, qr, kn, kr2, v]
    if cache is not None:
        knc, kr2c, vc = cache
        in_specs += [pl.BlockSpec((PAST_LEN, grp), lambda b, h, i: (b, h)),
                     pl.BlockSpec((PAST_LEN, 2 * LANE), lambda b, h, i: (b, 0)),
                     pl.BlockSpec((PAST_LEN, grp), lambda b, h, i: (b, h))]
        args += [knc, kr2c, vc]
    return pl.pallas_call(
        functools.partial(_attn_kernel, has_cache=cache is not None),
        grid=(n_seq, HEADS // HEAD_GRP, nq),
        in_specs=in_specs,
        out_specs=pl.BlockSpec((tq, grp), lambda b, h, i: (b * nq + i, h)),
        out_shape=jax.ShapeDtypeStruct((n_seq * seq, HEADS * V_DIM), BF16),
        scratch_shapes=[pltpu.VMEM((HEAD_GRP, tq, seq + (PAST_LEN if cache is not None else 0)), F32)],
        compiler_params=_cparams(("arbitrary", "arbitrary", "arbitrary")),
        name="attn_latent" if cache is not None else "attn_context",
    )(*args)


def _router_kernel(op_ref, os_ref, x_ref, gate1, g1, sh2, sc2, g2, wout, wr_hi, wr_lo, tri,
                   x3_ref, h_ref, ri_ref, rf_ref, cnt_ref, carry, *, prompt_steps):
    step = pl.program_id(0)

    @pl.when(step == 0)
    def _():
        carry[...] = jnp.zeros_like(carry)

    o = jnp.where(step < prompt_steps, op_ref[...], os_ref[...])
    x3 = _gated(x_ref[...], _dot(o, wout[...]), g1[...], gate1[0])
    x3_ref[...] = x3
    h = _adaln(x3, g2[...], sc2[0], sh2[0])
    _store_row_tiles(h_ref, lambda cols: h[:, cols])
    h_hi = h.astype(BF16)
    h_lo = (h - h_hi.astype(F32)).astype(BF16)
    dg = lambda a, b: lax.dot_general(a, b, NT_DIMS, preferred_element_type=F32)
    lg = dg(wr_hi[...], h_hi) + dg(wr_hi[...], h_lo) + dg(wr_lo[...], h_hi)
    eidx = lax.broadcasted_iota(I32, lg.shape, 0).astype(F32)
    m1 = jnp.max(lg, axis=0, keepdims=True)
    i1 = jnp.min(jnp.where(lg == m1, eidx, float(N_EXP)), axis=0, keepdims=True)
    sel1 = eidx == i1
    lg2 = jnp.where(sel1, -jnp.inf, lg)
    m2 = jnp.max(lg2, axis=0, keepdims=True)
    i2 = jnp.min(jnp.where(lg2 == m2, eidx, float(N_EXP)), axis=0, keepdims=True)
    sel2 = eidx == i2
    e = jnp.exp(m2 - m1)
    w1 = 1.0 / (1.0 + e)
    w2 = e / (1.0 + e)
    picked = jnp.where(sel1 | sel2, 1.0, 0.0)
    rank = _dot(picked.astype(BF16), tri[...]) + carry[:, 0:1]
    r1 = jnp.sum(jnp.where(sel1, rank, 0.0), axis=0, keepdims=True)
    r2 = jnp.sum(jnp.where(sel2, rank, 0.0), axis=0, keepdims=True)
    carry[...] = carry[...] + jnp.sum(picked, axis=1, keepdims=True)
    cnt_ref[...] = carry[...]
    ri_ref[...] = jnp.where(eidx == 0.0, i1, jnp.where(eidx == 1.0, i2, jnp.where(eidx == 2.0, r1, r2))).astype(I32)
    rf_ref[...] = jnp.where(eidx == 0.0, w1, w2)


def _attn_out_router(o_p, o_s, x, mods, g1, g2, wout, wr_hi, wr_lo):
    tm = 512
    grp = _grp_bm(tm)
    n_p = NP // tm
    o_specs = [pl.BlockSpec((tm, HEADS * V_DIM), lambda i: (jnp.minimum(i, n_p - 1), 0)),
               pl.BlockSpec((tm, HEADS * V_DIM), lambda i: (jnp.maximum(i - n_p, 0), 0))]
    tri = jnp.asarray(np.triu(np.ones((tm, tm), np.float32), 1), BF16)
    row = lambda w: pl.BlockSpec((tm, w), lambda i: (i, 0))
    col = pl.BlockSpec((N_EXP, tm), lambda i: (0, i))
    return pl.pallas_call(
        functools.partial(_router_kernel, prompt_steps=n_p),
        grid=(N // tm,),
        in_specs=o_specs + [row(D), _mod_spec(2, grp), _full((1, D)),
                  _mod_spec(3, grp), _mod_spec(4, grp), _full((1, D)),
                  _full((HEADS * V_DIM, D)), _full((N_EXP, D)), _full((N_EXP, D)), _full((tm, tm))],
        out_specs=[row(D), pl.BlockSpec((tm * ROW_TILE, LANE), lambda i: (i, 0)), col, col, _full((N_EXP, LANE))],
        out_shape=[jax.ShapeDtypeStruct((N, D), F32), jax.ShapeDtypeStruct((N * ROW_TILE, LANE), F32),
                   jax.ShapeDtypeStruct((N_EXP, N), I32), jax.ShapeDtypeStruct((N_EXP, N), F32),
                   jax.ShapeDtypeStruct((N_EXP, LANE), F32)],
        scratch_shapes=[pltpu.VMEM((N_EXP, LANE), F32)],
        compiler_params=_cparams(("arbitrary",)),
        name="attn_out_router",
    )(o_p, o_s, x, mods, g1, mods, mods, g2, wout, wr_hi, wr_lo, tri)


def _tile_rows(r):
    return r * ROW_TILE if isinstance(r, int) else pl.multiple_of(r * ROW_TILE, ROW_TILE)


def _moe_kernel(te, nu, src0_ref, src1_ref, dstp_ref, dstc_ref, h_hbm, wgu, wd, y_hbm,
                hsbuf, ybuf, hbf, acc, sem_g, sem_s):
    i = pl.program_id(0)
    last = pl.num_programs(0) - 1
    cur = i % 2
    nxt = 1 - cur
    buf_rows = TM_E * ROW_TILE

    def gather(idx_ref, r, slot):
        i_src = pl.multiple_of(idx_ref[0, 0, r] * ROW_TILE, ROW_TILE)
        return pltpu.make_async_copy(h_hbm.at[pl.ds(i_src, ROW_TILE), :],
                                     hsbuf.at[slot, pl.ds(_tile_rows(r), ROW_TILE), :], sem_g.at[slot])

    def scatter(idx_ref, r, slot):
        i_dst = pl.multiple_of(idx_ref[0, 0, r] * ROW_TILE, ROW_TILE)
        return pltpu.make_async_copy(ybuf.at[slot, pl.ds(_tile_rows(r), ROW_TILE), :],
                                     y_hbm.at[pl.ds(i_dst, ROW_TILE), :], sem_s.at[slot])

    def wait_gather(slot):
        pltpu.make_async_copy(h_hbm.at[pl.ds(0, buf_rows), :], hsbuf.at[slot], sem_g.at[slot]).wait()

    def wait_scatter(slot):
        pltpu.make_async_copy(ybuf.at[slot], y_hbm.at[pl.ds(0, buf_rows), :], sem_s.at[slot]).wait()

    def for_rows(fn):
        def body(g, carry):
            for k in range(ROW_TILE):
                fn(g * ROW_TILE + k, k % 2)
            return carry
        lax.fori_loop(0, TM_E // ROW_TILE, body, 0)

    def side_traffic(r, queue):
        gather(src1_ref, r, nxt).start(priority=queue)
        scatter(dstp_ref, r, nxt).start(priority=queue)

    @pl.when(i == 0)
    def _():
        ybuf[1] = jnp.zeros((buf_rows, LANE), F32)
        for_rows(lambda r, queue: gather(src0_ref, r, 0).start(priority=queue))

    wait_gather(cur)

    @pl.when(i < nu[0])
    def _():
        for c in range(ROW_TILE):
            hbf[:, c * LANE:(c + 1) * LANE] = hsbuf[cur, pl.ds(c, TM_E, stride=ROW_TILE), :].astype(BF16)
        w_gu = wgu.at[0]
        w_d = wd.at[0]
        _swiglu_block(hbf, w_gu, w_d, acc, 0)
        per_iter = TM_E // SIDE_ITERS

        def body(t, carry):
            _swiglu_block(hbf, w_gu, w_d, acc, 1 + 2 * t)
            _swiglu_block(hbf, w_gu, w_d, acc, 2 + 2 * t)
            for k in range(per_iter):
                side_traffic(t * per_iter + k, k % 2)
            return carry

        lax.fori_loop(0, SIDE_ITERS, body, 0)
        for j in range(1 + 2 * SIDE_ITERS, N_FF):
            _swiglu_block(hbf, w_gu, w_d, acc, j)
        _store_row_tiles(ybuf.at[cur], lambda cols: acc[:, cols])

    @pl.when(i >= nu[0])
    def _():
        for_rows(side_traffic)
        ybuf[cur] = jnp.zeros((buf_rows, LANE), F32)

    wait_scatter(nxt)

    @pl.when(i == last)
    def _():
        for_rows(lambda r, queue: scatter(dstc_ref, r, cur).start(priority=queue))
        wait_scatter(cur)
        wait_gather(nxt)


def _moe_experts(tile_expert, n_used, src_tbl, dst_tbl, h, wgu, wd):
    smem = lambda off: pl.BlockSpec((1, 1, TM_E), lambda i, te, nu: (i + off, 0, 0), memory_space=pltpu.SMEM)
    grid_spec = pltpu.PrefetchScalarGridSpec(
        num_scalar_prefetch=2,
        grid=(N_TILES,),
        in_specs=[smem(0), smem(1), smem(0), smem(1),
                  pl.BlockSpec(memory_space=pl.ANY),
                  pl.BlockSpec((1,) + wgu.shape[1:], lambda i, te, nu: (te[i], 0, 0)),
                  pl.BlockSpec((1,) + wd.shape[1:], lambda i, te, nu: (te[i], 0, 0))],
        out_specs=pl.BlockSpec(memory_space=pl.ANY),
        scratch_shapes=[pltpu.VMEM((2, TM_E * ROW_TILE, LANE), F32), pltpu.VMEM((2, TM_E * ROW_TILE, LANE), F32),
                        pltpu.VMEM((TM_E, D), BF16), pltpu.VMEM((TM_E, D), F32),
                        pltpu.SemaphoreType.DMA((2,)), pltpu.SemaphoreType.DMA((2,))],
    )
    return pl.pallas_call(
        _moe_kernel,
        grid_spec=grid_spec,
        out_shape=jax.ShapeDtypeStruct(((2 * N + DUMP_ROWS) * ROW_TILE, LANE), F32),
        compiler_params=_cparams(("arbitrary",)),
        name="moe_experts",
    )(tile_expert, n_used, src_tbl, src_tbl, dst_tbl, dst_tbl, h, wgu, wd)


def _combine_kernel(y1_ref, y2_ref, x_ref, w1_ref, w2_ref, gate2, g3, op_ref, os_ref, *, prompt_steps):
    rows = x_ref.shape[0]
    w1 = w1_ref[...]
    w2 = w2_ref[...]
    f = jnp.concatenate([w1 * _load_row_tiles(y1_ref, c, rows) + w2 * _load_row_tiles(y2_ref, c, rows)
                         for c in range(ROW_TILE)], axis=1)
    out = _gated(x_ref[...], f, g3[...], gate2[0])
    step = pl.program_id(0)

    @pl.when(step < prompt_steps)
    def _():
        op_ref[...] = out

    @pl.when(step >= prompt_steps)
    def _():
        os_ref[...] = out


def _moe_combine(y, x, w1, w2, mods, g3):
    tm = 512
    grp = _grp_bm(tm)
    nb = N // tm
    n_p = NP // tm
    return pl.pallas_call(
        functools.partial(_combine_kernel, prompt_steps=n_p),
        grid=(nb,),
        in_specs=[pl.BlockSpec((tm * ROW_TILE, LANE), lambda i: (i, 0)),
                  pl.BlockSpec((tm * ROW_TILE, LANE), lambda i: (nb + i, 0)),
                  pl.BlockSpec((tm, D), lambda i: (i, 0)),
                  pl.BlockSpec((tm, 1), lambda i: (i, 0)), pl.BlockSpec((tm, 1), lambda i: (i, 0)),
                  _mod_spec(5, grp), _full((1, D))],
        out_specs=[pl.BlockSpec((tm, D), lambda i: (jnp.minimum(i, n_p - 1), 0)),
                   pl.BlockSpec((tm, D), lambda i: (jnp.maximum(i - n_p, 0), 0))],
        out_shape=[jax.ShapeDtypeStruct((NP, D), F32), jax.ShapeDtypeStruct((NS, D), F32)],
        compiler_params=_cparams(("arbitrary",)),
        name="moe_combine",
    )(y, y, x, w1, w2, mods, g3)


def _block_diag(w):
    hh, a, b = w.shape
    eye = jnp.eye(hh, dtype=w.dtype)
    return jnp.einsum('hab,hk->hakb', w, eye).reshape(hh * a, hh * b)


def _s5_matrices(a_re, a_im, log_dt, b_re, b_im, c_re, c_im):
    dt = jnp.exp(log_dt)[:, None]
    mag = jnp.exp(a_re * dt)
    abr = mag * jnp.cos(a_im * dt)
    abi = mag * jnp.sin(a_im * dt)
    den = a_re * a_re + a_im * a_im
    cr = ((abr - 1.0) * a_re + abi * a_im) / den
    ci = (abi * a_re - (abr - 1.0) * a_im) / den
    bbr = cr[..., None] * b_re - ci[..., None] * b_im
    bbi = cr[..., None] * b_im + ci[..., None] * b_re
    hg = S5_G // 2
    eye = jnp.eye(hg, dtype=F32)
    bms, cms = [], []
    for j in range(2):
        sl = slice(j * hg, (j + 1) * hg)
        bd = lambda m: jnp.einsum('gpc,gh->gchp', m[sl], eye).reshape(hg * S5_CH, hg * S5_P)
        bms.append(jnp.concatenate([bd(bbr), bd(bbi)], axis=1))
        cd = lambda m: jnp.einsum('gcp,gh->gphc', m[sl], eye).reshape(hg * S5_P, hg * S5_CH)
        cms.append(jnp.concatenate([cd(c_re), cd(-c_im)], axis=0))
    return (jnp.stack(bms).astype(BF16), abr.reshape(1, S5_N), abi.reshape(1, S5_N),
            jnp.stack(cms).astype(BF16))


def _rope_tables(tm):
    rows = DEC_SEQ // GRID_W
    row = jnp.repeat(jnp.arange(rows, dtype=F32), GRID_W)
    col = jnp.tile(jnp.arange(GRID_W, dtype=F32), rows)
    nf = QK_ROPE // 4
    inv = ROPE_THETA ** (-jnp.arange(nf, dtype=F32) / nf)
    ang = jnp.concatenate([row[:, None] * inv, col[:, None] * inv], axis=-1)
    cos = jnp.repeat(jnp.cos(ang), 2, axis=-1)
    sin = jnp.stack([-jnp.sin(ang), jnp.sin(ang)], axis=-1).reshape(DEC_SEQ, QK_ROPE)
    ident = lambda t, one: jnp.concatenate([t, jnp.full((tm, t.shape[1]), one, F32)], axis=0)
    cos_q = ident(jnp.tile(cos, (1, HEADS)), 1.0)
    sin_q = ident(jnp.tile(sin, (1, HEADS)), 0.0)
    pad = lambda t, one: jnp.concatenate([t, jnp.full((DEC_SEQ, LANE - QK_ROPE), one, F32)], axis=1)
    return cos_q, sin_q, ident(pad(cos, 1.0), 1.0), ident(pad(sin, 0.0), 0.0)


def _to_time_major(xp, xs):
    p = xp.reshape(BATCH // SUB, SUB, SEQ, -1).transpose(0, 2, 1, 3).reshape(NP, -1)
    s = xs.transpose(1, 0, 2).reshape(NS, -1)
    return jnp.concatenate([p, s], axis=0)


def _to_batch_major(x):
    p = x[:NP].reshape(BATCH // SUB, SEQ, SUB, -1).transpose(0, 2, 1, 3).reshape(NP, -1)
    s = x[NP:].reshape(DEC_SEQ, DEC_BATCH, -1).transpose(1, 0, 2).reshape(NS, -1)
    return jnp.concatenate([p, s], axis=0)


def _group_states(prompt_state, sample_state):
    w = sample_state.shape[-1]
    return jnp.concatenate([prompt_state.reshape(2, SUB, w), sample_state.reshape(1, SUB, w)], axis=0)


def _layer_ab(x, m, ng, j, state_lru, state_s5_re, state_s5_im, p):
    xz = _ab_inproj(x, ng[0:1], m, p['ab_w_in'][j].astype(BF16))
    zeros = lambda w: jnp.zeros((BATCH, w), F32)
    outs = []
    for d in range(2):
        wg = jnp.concatenate([_block_diag(p['lru_wa'][j, d]), _block_diag(p['lru_wx'][j, d])], axis=1).astype(BF16)
        bg = jnp.concatenate([p['lru_ba'][j, d], p['lru_bx'][j, d]])[None]
        bm, ar, ai, cm = _s5_matrices(p['s5_a_re'][j, d], p['s5_a_im'][j, d], p['s5_log_dt'][j, d],
                                      p['s5_b_re'][j, d], p['s5_b_im'][j, d], p['s5_c_re'][j, d], p['s5_c_im'][j, d])
        h0l = _group_states(zeros(LRU_W), state_lru[:, j, d])
        h0r = _group_states(zeros(S5_N), state_s5_re[:, j, d].reshape(DEC_BATCH, S5_N))
        h0i = _group_states(zeros(S5_N), state_s5_im[:, j, d].reshape(DEC_BATCH, S5_N))
        outs.append(_ab_scan(xz, h0l, h0r, h0i, p['ab_conv_w'][j], p['ab_conv_b'][j][None], wg, bg,
                             p['lru_lambda'][j, d][None], bm, ar, ai, cm, reverse=(d == 1)))
    (haf, yf, llf, lrf, lif), (hab, yr, llb, lrb, lib) = outs
    x = _ab_out(haf, hab, yf, yr, xz, x, m, ng[1:2], p['s5_d'][j][None], p['s5_w_glu'][j].astype(BF16),
                p['s5_b_glu'][j][None], p['ab_w_out'][j].astype(BF16))
    x = _ffn(x, m, ng[2:3], ng[3:4], p['ffn_w_gate_up'][j].astype(BF16), p['ffn_w_down'][j].astype(BF16))
    prompt = lambda f, b, w: jnp.stack([f[:2].reshape(BATCH, w), b[:2].reshape(BATCH, w)], axis=1)
    lru = prompt(llf, llb, LRU_W)
    s5r = prompt(lrf, lrb, S5_N).reshape(BATCH, 2, S5_G, S5_P)
    s5i = prompt(lif, lib, S5_N).reshape(BATCH, 2, S5_G, S5_P)
    return x, lru, s5r, s5i


def _head_major(w, parts):
    k = w.shape[0]
    per_head = w.reshape(k, HEADS, -1)
    out, start = [], 0
    for width in parts:
        out.append(per_head[:, :, start:start + width].reshape(k, HEADS * width))
        start += width
    return jnp.concatenate(out, axis=1)


def _layer_mla_moe(x, m, ng, j, cache_kv_latent, cache_k_rope, p):
    w1 = jnp.concatenate([p['mla_w_in'][j], jnp.zeros((D, LANE - QK_ROPE), F32)], axis=1).astype(BF16)
    wuq = _head_major(p['mla_w_uq'][j], (QK_NOPE, QK_ROPE)).astype(BF16)
    wukv = _head_major(p['mla_w_ukv'][j], (QK_NOPE, V_DIM)).astype(BF16)
    tables = _rope_tables(256)
    qn, qr, kn, v, kr2, ckv, krr = _mla_proj(x, ng[0:1], m, w1, p['mla_g_q'][j][None], p['mla_g_kv'][j][None],
                                              wuq, wukv, *tables)
    knc, vc = _cache_kv(cache_kv_latent[:, j].reshape(DEC_BATCH * PAST_LEN, KV_LORA), wukv)
    krc = cache_k_rope[:, j].reshape(DEC_BATCH * PAST_LEN, QK_ROPE)
    z = jnp.zeros_like(krc)
    kr2c = jnp.concatenate([krc, z, z, krc], axis=1).astype(BF16)
    o_p = _attention(qn, qr, kn, kr2, v, row0=0, n_seq=BATCH, seq=SEQ, tq=SEQ)
    o_s = _attention(qn, qr, kn, kr2, v, row0=NP, n_seq=DEC_BATCH, seq=DEC_SEQ, tq=512, cache=(knc, kr2c, vc))
    wr_t = p['moe_w_router'][j].T
    wr_hi = wr_t.astype(BF16)
    wr_lo = (wr_t - wr_hi.astype(F32)).astype(BF16)
    x3, h, ri, rf, cnt = _attn_out_router(o_p, o_s, x, m, ng[1:2], ng[2:3], p['mla_w_out'][j].astype(BF16), wr_hi, wr_lo)
    counts = cnt[:, 0].astype(I32)
    padded = ((counts + TM_E - 1) // TM_E) * TM_E
    ends = jnp.cumsum(padded)
    offs = ends - padded
    pos1 = offs[ri[0]] + ri[2]
    pos2 = offs[ri[1]] + ri[3]
    pick_tok = jnp.arange(2 * N, dtype=I32)
    dest = jnp.full((P_ROWS,), -1, I32).at[jnp.concatenate([pos1, pos2])].set(pick_tok, unique_indices=True)
    is_pad = dest < 0
    pad_row = 2 * N + TM_E + jnp.cumsum(is_pad.astype(I32)) - 1
    src_tbl = jnp.concatenate([jnp.where(is_pad, 0, dest % N), jnp.zeros((TM_E,), I32)])
    dst_tbl = jnp.concatenate([2 * N + jnp.arange(TM_E, dtype=I32), jnp.where(is_pad, pad_row, dest)])
    n_used = (ends[-1] // TM_E).astype(I32)[None]
    tile_row = jnp.minimum(jnp.arange(N_TILES, dtype=I32), n_used - 1) * TM_E
    tile_expert = jnp.sum((tile_row[:, None] >= ends[None, :]).astype(I32), axis=1)
    y = _moe_experts(tile_expert, n_used, src_tbl.reshape(N_TILES + 1, 1, TM_E), dst_tbl.reshape(N_TILES + 1, 1, TM_E),
                     h, p['moe_w_gate_up'][j].astype(BF16), p['moe_w_down'][j].astype(BF16))
    xp, xs = _moe_combine(y, x3, rf[0][:, None], rf[1][:, None], m, ng[3:4])
    kv_new = ckv.reshape(BATCH, SEQ, KV_LORA)
    kr_new = krr[:, :QK_ROPE].reshape(BATCH, SEQ, QK_ROPE)
    return (xp.reshape(BATCH, SEQ, D), xs.reshape(DEC_BATCH, DEC_SEQ, D)), kv_new, kr_new


def kernel(x_prompt, x_sample, c, state_lru, state_s5_re, state_s5_im, cache_kv_latent, cache_k_rope, c_ctx, w_mod, b_mod, norm_gains, ab_w_in, ab_conv_w, ab_conv_b, lru_wa, lru_ba, lru_wx, lru_bx, lru_lambda, s5_a_re, s5_a_im, s5_log_dt, s5_b_re, s5_b_im, s5_c_re, s5_c_im, s5_d, s5_w_glu, s5_b_glu, ab_w_out, ffn_w_gate_up, ffn_w_down, mla_w_in, mla_g_q, mla_g_kv, mla_w_uq, mla_w_ukv, mla_w_out, moe_w_router, moe_w_gate_up, moe_w_down):
    p = dict(ab_w_in=ab_w_in, ab_conv_w=ab_conv_w, ab_conv_b=ab_conv_b, lru_wa=lru_wa, lru_ba=lru_ba,
             lru_wx=lru_wx, lru_bx=lru_bx, lru_lambda=lru_lambda, s5_a_re=s5_a_re, s5_a_im=s5_a_im,
             s5_log_dt=s5_log_dt, s5_b_re=s5_b_re, s5_b_im=s5_b_im, s5_c_re=s5_c_re, s5_c_im=s5_c_im,
             s5_d=s5_d, s5_w_glu=s5_w_glu, s5_b_glu=s5_b_glu, ab_w_out=ab_w_out, ffn_w_gate_up=ffn_w_gate_up,
             ffn_w_down=ffn_w_down, mla_w_in=mla_w_in, mla_g_q=mla_g_q, mla_g_kv=mla_g_kv, mla_w_uq=mla_w_uq,
             mla_w_ukv=mla_w_ukv, mla_w_out=mla_w_out, moe_w_router=moe_w_router, moe_w_gate_up=moe_w_gate_up,
             moe_w_down=moe_w_down)
    depth = w_mod.shape[0]
    cond = jnp.concatenate([c_ctx[None], c, jnp.zeros((2 * SUB - 1 - DEC_BATCH, D), F32)], axis=0)
    mod = _modulation(cond, w_mod, b_mod)
    ctx_tile = lambda l: jnp.broadcast_to(mod[l, 0:1], (SUB, 6 * D))
    streams = (x_prompt, x_sample)
    lru_l, s5r_l, s5i_l, kv_l, kr_l = [], [], [], [], []
    for layer in range(depth):
        j = layer // 2
        ng = norm_gains[layer]
        if layer % 2 == 0:
            m = jnp.stack([ctx_tile(layer), mod[layer, 1:1 + DEC_BATCH]])
            x, lru, s5r, s5i = _layer_ab(_to_time_major(*streams), m, ng, j, state_lru, state_s5_re, state_s5_im, p)
            x = _to_batch_major(x)
            streams = (x[:NP].reshape(BATCH, SEQ, D), x[NP:].reshape(DEC_BATCH, DEC_SEQ, D))
            lru_l.append(lru)
            s5r_l.append(s5r)
            s5i_l.append(s5i)
        else:
            lat = jnp.broadcast_to(mod[layer, 1:1 + DEC_BATCH, None, :], (DEC_BATCH, SUB, 6 * D))
            m = jnp.concatenate([ctx_tile(layer)[None], lat], axis=0)
            x = jnp.concatenate([streams[0].reshape(NP, D), streams[1].reshape(NS, D)], axis=0)
            streams, kv_new, kr_new = _layer_mla_moe(x, m, ng, j, cache_kv_latent, cache_k_rope, p)
            kv_l.append(kv_new)
            kr_l.append(kr_new)
    return (streams[0], streams[1],
            jnp.stack(lru_l, axis=1), jnp.stack(s5r_l, axis=1), jnp.stack(s5i_l, axis=1),
            jnp.stack(kv_l, axis=1), jnp.stack(kr_l, axis=1))
```

```python
import functools
import math

import numpy as np
import jax
import jax.numpy as jnp
from jax import lax
from jax.experimental import pallas as pl
from jax.experimental.pallas import tpu as pltpu

F32 = jnp.float32
BF16 = jnp.bfloat16
I32 = jnp.int32

D = 1024
BATCH, SEQ = 16, 256
DEC_BATCH, DEC_SEQ = 8, 2048
PAST_LEN = 256
GRID_W = 64
LRU_W = 512
LRU_HEADS = 8
LRU_C = 8.0
CONV_W = 4
S5_W = 512
S5_CH = 16
S5_G = 32
S5_P = 64
S5_N = S5_G * S5_P
HEADS = 8
QK_NOPE, QK_ROPE, V_DIM = 128, 64, 128
Q_LORA, KV_LORA = 384, 256
ROPE_THETA = 10000.0
D_FF = 2816
N_EXP = 8
EPS = 1e-6

NP = BATCH * SEQ
NS = DEC_BATCH * DEC_SEQ
N = NP + NS
SUB = 8
LANE = 128
ROW_TILE = D // LANE
T_CHUNK = 32
R_CHUNK = T_CHUNK * SUB
FF_BLK = 256
KEY_BLK = 256
HEAD_GRP = 4
INPROJ_T = 64
FFN_T = 128
SIDE_ITERS = 4
N_FF = D_FF // FF_BLK
TM_E = 512
P_ROWS = 2 * N + N_EXP * TM_E
N_TILES = P_ROWS // TM_E
DUMP_ROWS = P_ROWS - 2 * N + TM_E
VMEM_LIMIT = 56 * 1024 * 1024

NT_DIMS = (((1,), (1,)), ((), ()))


def _cparams(sem):
    return pltpu.CompilerParams(dimension_semantics=sem, vmem_limit_bytes=VMEM_LIMIT)


def _dot(a, b):
    return jnp.dot(a, b, preferred_element_type=F32)


def _sigmoid(x):
    return 1.0 / (1.0 + jnp.exp(-x))


def _neg_expm1_2x(log_a, a):
    z = 2.0 * log_a
    series = -z * (1.0 + z * (1.0 / 2.0) * (1.0 + z * (1.0 / 3.0) * (1.0 + z * (1.0 / 4.0))))
    return jnp.where(z > -0.02, series, (1.0 - a) * (1.0 + a))


def _gelu(x):
    return x * (0.5 * (1.0 + jnp.tanh(math.sqrt(2.0 / math.pi) * (x + 0.044715 * (x * x * x)))))


def _rms(x, g):
    ms = jnp.mean(x * x, axis=-1, keepdims=True)
    return x * lax.rsqrt(ms + EPS) * g


def _rows8(y, fn):
    r, c = y.shape
    return fn(y.reshape(r // SUB, SUB, c)).reshape(r, c)


def _adaln(x, g, scale, shift):
    return _rows8(_rms(x, g), lambda y: y * (1.0 + scale)[None] + shift[None])


def _gated(x, y, g, gate):
    return x + _rows8(_rms(y, g), lambda z: z * gate[None])


def _store_row_tiles(ref, piece):
    rows = ref.shape[0] // ROW_TILE
    for c in range(ROW_TILE):
        ref[pl.ds(c, rows, stride=ROW_TILE), :] = piece(slice(c * LANE, (c + 1) * LANE))


def _load_row_tiles(ref, c, rows):
    return ref[pl.ds(c, rows, stride=ROW_TILE), :]


def _full(shape):
    nd = len(shape)
    return pl.BlockSpec(shape, lambda *_: (0,) * nd)


def _mod_spec(k, grp):
    return pl.BlockSpec((1, SUB, D), lambda i, *_: (grp(i), 0, k))


def _grp_tm(tm):
    return lambda i: (i * tm >= NP).astype(I32)


def _grp_bm(tm):
    return lambda i: jnp.where(i * tm < NP, 0, 1 + (i * tm - NP) // DEC_SEQ)


def _mod_kernel(c_ref, w_ref, b_ref, o_ref):
    c = c_ref[...]
    s = c * _sigmoid(c)
    o_ref[0] = _dot(s.astype(BF16), w_ref[0].astype(BF16)) + b_ref[0]


def _modulation(cond, w_mod, b_mod):
    depth = w_mod.shape[0]
    rows = cond.shape[0]
    return pl.pallas_call(
        _mod_kernel,
        grid=(depth, 6),
        in_specs=[_full((rows, D)),
                  pl.BlockSpec((1, D, D), lambda l, j: (l, 0, j)),
                  pl.BlockSpec((1, 1, D), lambda l, j: (l, 0, j))],
        out_specs=pl.BlockSpec((1, rows, D), lambda l, j: (l, 0, j)),
        out_shape=jax.ShapeDtypeStruct((depth, rows, 6 * D), F32),
        compiler_params=_cparams(("arbitrary", "arbitrary")),
        name="modulation",
    )(cond, w_mod, b_mod.reshape(depth, 1, 6 * D))


def _time_major_copies(step, t_steps, xp_hbm, xs_hbm, buf, sem, to_hbm):
    p_steps = (BATCH // SUB) * (SEQ // t_steps)
    per_group = SEQ // t_steps

    def issue(hbm, seq0, t0):
        t0 = pl.multiple_of(t0, t_steps)
        for b in range(SUB):
            rows = hbm.at[seq0 + b, pl.ds(t0, t_steps), :]
            tile = buf.at[:, b, :]
            (pltpu.make_async_copy(tile, rows, sem) if to_hbm else pltpu.make_async_copy(rows, tile, sem)).start()

    @pl.when(step < p_steps)
    def _():
        issue(xp_hbm, (step // per_group) * SUB, (step % per_group) * t_steps)

    @pl.when(step >= p_steps)
    def _():
        issue(xs_hbm, 0, (step - p_steps) * t_steps)


def _time_major_wait(t_steps, xs_hbm, buf, sem, to_hbm):
    for b in range(SUB):
        rows = xs_hbm.at[0, pl.ds(0, t_steps), :]
        tile = buf.at[:, b, :]
        (pltpu.make_async_copy(tile, rows, sem) if to_hbm else pltpu.make_async_copy(rows, tile, sem)).wait()


def _inproj_kernel(xp_hbm, xs_hbm, g_ref, sh_ref, sc_ref, w_ref, o_ref, xtm_ref, buf, sem):
    step = pl.program_id(0)
    slot = step % 2

    @pl.when(step == 0)
    def _():
        _time_major_copies(step, INPROJ_T, xp_hbm, xs_hbm, buf.at[0], sem.at[0], False)

    @pl.when(step + 1 < pl.num_programs(0))
    def _():
        _time_major_copies(step + 1, INPROJ_T, xp_hbm, xs_hbm, buf.at[1 - slot], sem.at[1 - slot], False)

    _time_major_wait(INPROJ_T, xs_hbm, buf.at[slot], sem.at[slot], False)
    x = buf[slot].reshape(INPROJ_T * SUB, D)
    xtm_ref[...] = x
    h = _adaln(x, g_ref[...], sc_ref[0], sh_ref[0])
    o_ref[...] = _dot(h.astype(BF16), w_ref[...])


def _ab_inproj(xp, xs, gain, mods, w_in):
    tm = INPROJ_T * SUB
    nout = w_in.shape[1]
    grp = _grp_tm(tm)
    return pl.pallas_call(
        _inproj_kernel,
        grid=(N // tm,),
        in_specs=[pl.BlockSpec(memory_space=pl.ANY), pl.BlockSpec(memory_space=pl.ANY),
                  _full((1, D)),
                  _mod_spec(0, grp), _mod_spec(1, grp),
                  _full((D, nout))],
        out_specs=[pl.BlockSpec((tm, nout), lambda i: (i, 0)), pl.BlockSpec((tm, D), lambda i: (i, 0))],
        out_shape=[jax.ShapeDtypeStruct((N, nout), F32), jax.ShapeDtypeStruct((N, D), F32)],
        scratch_shapes=[pltpu.VMEM((2, INPROJ_T, SUB, D), F32), pltpu.SemaphoreType.DMA((2,))],
        compiler_params=_cparams(("arbitrary",)),
        name="ab_inproj",
    )(xp, xs, gain, mods, mods, w_in)


def _scan_table(reverse):
    cols = []
    groups = [(0, SEQ // T_CHUNK, 0), (1, SEQ // T_CHUNK, SEQ // T_CHUNK),
              (2, DEC_SEQ // T_CHUNK, NP // R_CHUNK)]
    for g, nc, base in groups:
        order = range(nc - 1, -1, -1) if reverse else range(nc)
        for k, c in enumerate(order):
            cols.append((base + c, g, int(k == 0), int(c > 0), int(c < nc - 1)))
    return np.asarray(cols, np.int32).T.copy()


def _scan_kernel(tbl, xa_ref, xp_ref, xn_ref, xb_ref, h0l_ref, h0r_ref, h0i_ref,
                 cw_ref, cb_ref, wg_ref, bg_ref, lam_ref, bm_ref, ar_ref, ai_ref, cm_ref,
                 ha_ref, y_ref, ll_ref, lr_ref, li_ref,
                 ext, abuf, bbuf, hre, him, hl, sre, sim, *, reverse):
    s = pl.program_id(0)

    @pl.when(tbl[2, s] == 1)
    def _():
        hl[...] = h0l_ref[0]
        sre[...] = h0r_ref[0]
        sim[...] = h0i_ref[0]

    ext[0:2 * SUB] = jnp.where(tbl[3, s] == 1, xp_ref[...], 0.0)
    ext[2 * SUB:2 * SUB + R_CHUNK] = xa_ref[...]
    ext[2 * SUB + R_CHUNK:3 * SUB + R_CHUNK] = jnp.where(tbl[4, s] == 1, xn_ref[...], 0.0)
    xa = cb_ref[...] + cw_ref[0:1] * ext[0:R_CHUNK]
    for k in range(1, CONV_W):
        xa = xa + cw_ref[k:k + 1] * ext[k * SUB:k * SUB + R_CHUNK]

    gz = _dot(xa.astype(BF16), wg_ref[...]) + bg_ref[...]
    r = _sigmoid(gz[:, :LRU_W])
    i = _sigmoid(gz[:, LRU_W:])
    lam = lam_ref[...]
    log_sig = jnp.minimum(lam, 0.0) - jnp.log1p(jnp.exp(-jnp.abs(lam)))
    log_a = LRU_C * r * log_sig
    a = jnp.exp(log_a)
    abuf[...] = a
    bbuf[...] = jnp.sqrt(_neg_expm1_2x(log_a, a)) * (i * xa)

    order = range(T_CHUNK - 1, -1, -1) if reverse else range(T_CHUNK)
    h = hl[...]
    for t in order:
        rows = slice(t * SUB, (t + 1) * SUB)
        h = abuf[rows] * h + bbuf[rows]
        ha_ref[rows, :] = h
    hl[...] = h
    ll_ref[0] = h

    ub = xb_ref[...].astype(BF16)
    half = S5_N // 2
    for j in range(2):
        bu = _dot(ub[:, j * 256:(j + 1) * 256], bm_ref[j])
        hre[:, j * half:(j + 1) * half] = bu[:, :half]
        him[:, j * half:(j + 1) * half] = bu[:, half:]
    cblk = 4 * LANE
    for cb in range(S5_N // cblk):
        cols = slice(cb * cblk, (cb + 1) * cblk)
        ar = jnp.broadcast_to(ar_ref[:, cols], (SUB, cblk))
        ai = jnp.broadcast_to(ai_ref[:, cols], (SUB, cblk))
        hr = sre[:, cols]
        hi = sim[:, cols]
        for t in order:
            rows = slice(t * SUB, (t + 1) * SUB)
            nr = ar * hr - ai * hi + hre[rows, cols]
            ni = ar * hi + ai * hr + him[rows, cols]
            hr, hi = nr, ni
            hre[rows, cols] = hr
            him[rows, cols] = hi
        sre[:, cols] = hr
        sim[:, cols] = hi
    lr_ref[0] = sre[...]
    li_ref[0] = sim[...]
    for j in range(2):
        hc = jnp.concatenate([hre[:, j * half:(j + 1) * half], him[:, j * half:(j + 1) * half]],
                             axis=1).astype(BF16)
        y_ref[:, j * 256:(j + 1) * 256] = _dot(hc, cm_ref[j])


def _ab_scan(xz, h0l, h0r, h0i, cw, cb, wg, bg, lam, bm, ar, ai, cm, reverse):
    tbl = jnp.asarray(_scan_table(reverse))
    n_steps = tbl.shape[1]
    blk = lambda s, t: t[0, s]
    grp = lambda s, t: t[1, s]
    state_spec = lambda w: pl.BlockSpec((1, SUB, w), lambda s, t: (grp(s, t), 0, 0))
    const = lambda shape: pl.BlockSpec(shape, lambda s, t: (0,) * len(shape))
    grid_spec = pltpu.PrefetchScalarGridSpec(
        num_scalar_prefetch=1,
        grid=(n_steps,),
        in_specs=[
            pl.BlockSpec((R_CHUNK, LRU_W), lambda s, t: (blk(s, t), 0)),
            pl.BlockSpec((2 * SUB, LRU_W), lambda s, t: (jnp.maximum(blk(s, t) * (T_CHUNK // 2) - 1, 0), 0)),
            pl.BlockSpec((SUB, LRU_W), lambda s, t: (jnp.minimum((blk(s, t) + 1) * T_CHUNK, N // SUB - 1), 0)),
            pl.BlockSpec((R_CHUNK, S5_W), lambda s, t: (blk(s, t), 2)),
            state_spec(LRU_W), state_spec(S5_N), state_spec(S5_N),
            const((CONV_W, LRU_W)), const((1, LRU_W)),
            const((LRU_W, 2 * LRU_W)), const((1, 2 * LRU_W)), const((1, LRU_W)),
            const((2, 256, S5_N)), const((1, S5_N)), const((1, S5_N)), const((2, S5_N, 256)),
        ],
        out_specs=[
            pl.BlockSpec((R_CHUNK, LRU_W), lambda s, t: (blk(s, t), 0)),
            pl.BlockSpec((R_CHUNK, S5_W), lambda s, t: (blk(s, t), 0)),
            state_spec(LRU_W), state_spec(S5_N), state_spec(S5_N),
        ],
        scratch_shapes=[
            pltpu.VMEM((R_CHUNK + 3 * SUB, LRU_W), F32),
            pltpu.VMEM((R_CHUNK, LRU_W), F32), pltpu.VMEM((R_CHUNK, LRU_W), F32),
            pltpu.VMEM((R_CHUNK, S5_N), F32), pltpu.VMEM((R_CHUNK, S5_N), F32),
            pltpu.VMEM((SUB, LRU_W), F32), pltpu.VMEM((SUB, S5_N), F32), pltpu.VMEM((SUB, S5_N), F32),
        ],
    )
    return pl.pallas_call(
        functools.partial(_scan_kernel, reverse=reverse),
        grid_spec=grid_spec,
        out_shape=[jax.ShapeDtypeStruct((N, LRU_W), F32), jax.ShapeDtypeStruct((N, S5_W), F32),
                   jax.ShapeDtypeStruct((3, SUB, LRU_W), F32),
                   jax.ShapeDtypeStruct((3, SUB, S5_N), F32), jax.ShapeDtypeStruct((3, SUB, S5_N), F32)],
        compiler_params=_cparams(("arbitrary",)),
        name="ab_scan_bwd" if reverse else "ab_scan_fwd",
    )(tbl, xz, xz, xz, xz, h0l, h0r, h0i, cw, cb, wg, bg, lam, bm, ar, ai, cm)


def _about_kernel(haf, hab, yf, yr, ga, xb, x_ref, gate, g1, d_ref, wglu, bglu, wout, o_ref):
    ya = (haf[...] + hab[...]) * _gelu(ga[...])
    yb0 = _gelu(yf[...] + yr[...] + d_ref[...] * xb[...])
    yb = yb0 * _sigmoid(_dot(yb0.astype(BF16), wglu[...]) + bglu[...])
    out = _dot(ya.astype(BF16), wout[0:LRU_W]) + _dot(yb.astype(BF16), wout[LRU_W:LRU_W + S5_W])
    o_ref[...] = _gated(x_ref[...], out, g1[...], gate[0])


def _ab_out(haf, hab, yf, yr, xz, x, mods, g1, s5_d, wglu, bglu, wout):
    tm = 512
    grp = _grp_tm(tm)
    half = lambda c: pl.BlockSpec((tm, LRU_W), lambda i: (i, c))
    return pl.pallas_call(
        _about_kernel,
        grid=(N // tm,),
        in_specs=[half(0), half(0), half(0), half(0), half(1), half(2),
                  pl.BlockSpec((tm, D), lambda i: (i, 0)),
                  _mod_spec(2, grp), _full((1, D)), _full((1, S5_W)),
                  _full((S5_W, S5_W)), _full((1, S5_W)), _full((LRU_W + S5_W, D))],
        out_specs=pl.BlockSpec((tm, D), lambda i: (i, 0)),
        out_shape=jax.ShapeDtypeStruct((N, D), F32),
        compiler_params=_cparams(("arbitrary",)),
        name="ab_out",
    )(haf, hab, yf, yr, xz, xz, x, mods, g1, s5_d, wglu, bglu, wout)


def _swiglu_block(hbf, wgu, wd, acc, j):
    static = isinstance(j, int)
    blk = lambda start: pl.ds(start if static else pl.multiple_of(start, FF_BLK), FF_BLK)
    h = hbf[...]
    g = _dot(h, wgu[:, blk(j * FF_BLK)])
    u = _dot(h, wgu[:, blk(D_FF + j * FF_BLK)])
    act = (g * _sigmoid(g)) * u
    part = _dot(act.astype(BF16), wd[blk(j * FF_BLK), :])
    if static and j == 0:
        acc[...] = part
    else:
        acc[...] += part


def _ffn_kernel(x_ref, sh, sc, gt, g2, g3, wgu, wd, op_hbm, os_hbm, hbf, acc, obuf, sem):
    step = pl.program_id(0)
    slot = step % 2
    hbf[...] = _adaln(x_ref[...], g2[...], sc[0], sh[0]).astype(BF16)
    for j in range(N_FF):
        _swiglu_block(hbf, wgu, wd, acc, j)
    obuf[slot] = _gated(x_ref[...], acc[...], g3[...], gt[0]).reshape(FFN_T, SUB, D)
    _time_major_copies(step, FFN_T, op_hbm, os_hbm, obuf.at[slot], sem.at[slot], True)

    @pl.when(step > 0)
    def _():
        _time_major_wait(FFN_T, os_hbm, obuf.at[1 - slot], sem.at[1 - slot], True)

    @pl.when(step == pl.num_programs(0) - 1)
    def _():
        _time_major_wait(FFN_T, os_hbm, obuf.at[slot], sem.at[slot], True)


def _ffn(x, mods, g2, g3, wgu, wd):
    tm = FFN_T * SUB
    grp = _grp_tm(tm)
    once = lambda shape: pl.BlockSpec(shape, lambda i: (0,) * len(shape), pipeline_mode=pl.Buffered(1))
    return pl.pallas_call(
        _ffn_kernel,
        grid=(N // tm,),
        in_specs=[pl.BlockSpec((tm, D), lambda i: (i, 0)),
                  _mod_spec(3, grp), _mod_spec(4, grp), _mod_spec(5, grp),
                  _full((1, D)), _full((1, D)),
                  once(wgu.shape), once(wd.shape)],
        out_specs=[pl.BlockSpec(memory_space=pl.ANY), pl.BlockSpec(memory_space=pl.ANY)],
        out_shape=[jax.ShapeDtypeStruct((BATCH, SEQ, D), F32), jax.ShapeDtypeStruct((DEC_BATCH, DEC_SEQ, D), F32)],
        scratch_shapes=[pltpu.VMEM((tm, D), BF16), pltpu.VMEM((tm, D), F32),
                        pltpu.VMEM((2, FFN_T, SUB, D), F32), pltpu.SemaphoreType.DMA((2,))],
        compiler_params=_cparams(("arbitrary",)),
        name="ffn",
    )(x, mods, mods, mods, g2, g3, wgu, wd)


def _pair_swap(x):
    outs = []
    for c in range(x.shape[1] // LANE):
        xc = x[:, c * LANE:(c + 1) * LANE]
        even = lax.broadcasted_iota(I32, xc.shape, 1) % 2 == 0
        outs.append(jnp.where(even, pltpu.roll(xc, LANE - 1, 1), pltpu.roll(xc, 1, 1)))
    return outs[0] if len(outs) == 1 else jnp.concatenate(outs, axis=1)


def _stream_specs(tm, width):
    n_p = NP // tm
    return [pl.BlockSpec((tm, width), lambda i: (jnp.minimum(i, n_p - 1), 0)),
            pl.BlockSpec((tm, width), lambda i: (jnp.maximum(i - n_p, 0), 0))]


def _stream_rows(p_ref, s_ref, prompt_steps):
    return jnp.where(pl.program_id(0) < prompt_steps, p_ref[...], s_ref[...])


def _mlaproj_kernel(xp_ref, xs_ref, g0, sh, sc, w1, gq, gkv, wuq, wukv, cq_ref, sq_ref, ck_ref, sk_ref,
                    qn_ref, qr_ref, kn_ref, v_ref, kr2_ref, ckv_ref, krr_ref, *, prompt_steps):
    h = _adaln(_stream_rows(xp_ref, xs_ref, prompt_steps), g0[...], sc[0], sh[0])
    dn = _dot(h.astype(BF16), w1[...])
    cq = _rms(dn[:, :Q_LORA], gq[...])
    ckv = _rms(dn[:, Q_LORA:Q_LORA + KV_LORA], gkv[...])
    krp = dn[:, Q_LORA + KV_LORA:]

    @pl.when(pl.program_id(0) < prompt_steps)
    def _():
        ckv_ref[...] = ckv
        krr_ref[...] = krp
    q = _dot(cq.astype(BF16), wuq[...])
    qn_ref[...] = q[:, :HEADS * QK_NOPE].astype(BF16)
    qr = q[:, HEADS * QK_NOPE:]
    qr_ref[...] = (qr * cq_ref[...] + _pair_swap(qr) * sq_ref[...]).astype(BF16)
    kv = _dot(ckv.astype(BF16), wukv[...])
    kn_ref[...] = kv[:, :HEADS * QK_NOPE].astype(BF16)
    v_ref[...] = kv[:, HEADS * QK_NOPE:].astype(BF16)
    kr = krp * ck_ref[...] + _pair_swap(krp) * sk_ref[...]
    kr2_ref[...] = jnp.concatenate([kr, pltpu.roll(kr, QK_ROPE, 1)], axis=1).astype(BF16)


def _mla_proj(xp, xs, g0, mods, w1, gq, gkv, wuq, wukv, cos_q, sin_q, cos_k, sin_k):
    tm = 256
    grp = _grp_bm(tm)
    n_pos = DEC_SEQ // tm
    tab = lambda w: pl.BlockSpec((tm, w), lambda i: (jnp.where(i * tm < NP, n_pos, (i - NP // tm) % n_pos), 0))
    row = lambda w: pl.BlockSpec((tm, w), lambda i: (i, 0))
    shp = lambda w, dt: jax.ShapeDtypeStruct((N, w), dt)
    n_p = NP // tm
    prow = lambda w: pl.BlockSpec((tm, w), lambda i: (jnp.minimum(i, n_p - 1), 0))
    return pl.pallas_call(
        functools.partial(_mlaproj_kernel, prompt_steps=n_p),
        grid=(N // tm,),
        in_specs=_stream_specs(tm, D) + [_full((1, D)), _mod_spec(0, grp), _mod_spec(1, grp),
                  _full(w1.shape), _full((1, Q_LORA)), _full((1, KV_LORA)),
                  _full(wuq.shape), _full(wukv.shape),
                  tab(HEADS * QK_ROPE), tab(HEADS * QK_ROPE), tab(LANE), tab(LANE)],
        out_specs=[row(HEADS * QK_NOPE), row(HEADS * QK_ROPE), row(HEADS * QK_NOPE), row(HEADS * V_DIM),
                   row(2 * LANE), prow(KV_LORA), prow(LANE)],
        out_shape=[shp(HEADS * QK_NOPE, BF16), shp(HEADS * QK_ROPE, BF16), shp(HEADS * QK_NOPE, BF16),
                   shp(HEADS * V_DIM, BF16), shp(2 * LANE, BF16),
                   jax.ShapeDtypeStruct((NP, KV_LORA), F32), jax.ShapeDtypeStruct((NP, LANE), F32)],
        compiler_params=_cparams(("arbitrary",)),
        name="mla_proj",
    )(xp, xs, g0, mods, mods, w1, gq, gkv, wuq, wukv, cos_q, sin_q, cos_k, sin_k)


def _cachekv_kernel(c_ref, w_ref, kn_ref, v_ref):
    kv = _dot(c_ref[...].astype(BF16), w_ref[...])
    kn_ref[...] = kv[:, :HEADS * QK_NOPE].astype(BF16)
    v_ref[...] = kv[:, HEADS * QK_NOPE:].astype(BF16)


def _cache_kv(ckv_cache, wukv):
    rows = ckv_cache.shape[0]
    tm = 512
    return pl.pallas_call(
        _cachekv_kernel,
        grid=(rows // tm,),
        in_specs=[pl.BlockSpec((tm, KV_LORA), lambda i: (i, 0)), _full(wukv.shape)],
        out_specs=[pl.BlockSpec((tm, HEADS * QK_NOPE), lambda i: (i, 0)),
                   pl.BlockSpec((tm, HEADS * V_DIM), lambda i: (i, 0))],
        out_shape=[jax.ShapeDtypeStruct((rows, HEADS * QK_NOPE), BF16),
                   jax.ShapeDtypeStruct((rows, HEADS * V_DIM), BF16)],
        compiler_params=_cparams(("arbitrary",)),
        name="cache_kv",
    )(ckv_cache, wukv)


def _attn_kernel(*refs, has_cache):
    if has_cache:
        qn, qr, kn, kr, v, knc, krc, vc, o_ref, s_scr = refs
        streams = [(knc, krc, vc), (kn, kr, v)]
    else:
        qn, qr, kn, kr, v, o_ref, s_scr = refs
        streams = [(kn, kr, v)]
    chunks = [(k1, k2, vv, c * KEY_BLK) for k1, k2, vv in streams for c in range(k1.shape[0] // KEY_BLK)]
    tq = qn.shape[0]
    a = (QK_NOPE + QK_ROPE) ** -0.5 * math.log2(math.e)
    for hh in range(HEAD_GRP):
        cols = slice(hh * LANE, (hh + 1) * LANE)
        pair_cols = slice((hh // 2) * LANE, (hh // 2 + 1) * LANE)
        kr_cols = slice((hh % 2) * LANE, (hh % 2 + 1) * LANE)
        q = jnp.concatenate([qn[:, cols], qr[:, pair_cols]], axis=1)
        mx = jnp.full((tq, LANE), -jnp.inf, F32)
        for n, (k1, k2, _, r0) in enumerate(chunks):
            k = jnp.concatenate([k1[r0:r0 + KEY_BLK, cols], k2[r0:r0 + KEY_BLK, kr_cols]], axis=1)
            s = lax.dot_general(q, k, NT_DIMS, preferred_element_type=F32)
            s_scr[hh, :, n * KEY_BLK:(n + 1) * KEY_BLK] = s
            for c in range(KEY_BLK // LANE):
                mx = jnp.maximum(mx, s[:, c * LANE:(c + 1) * LANE])
        mb = jnp.max(mx, axis=-1, keepdims=True) * a
        den = jnp.zeros((tq, LANE), F32)
        o = jnp.zeros((tq, V_DIM), F32)
        for n, (_, _, vv, r0) in enumerate(chunks):
            p = jnp.exp2(s_scr[hh, :, n * KEY_BLK:(n + 1) * KEY_BLK] * a - mb)
            for c in range(KEY_BLK // LANE):
                den = den + p[:, c * LANE:(c + 1) * LANE]
            o = o + _dot(p.astype(BF16), vv[r0:r0 + KEY_BLK, cols])
        o_ref[:, cols] = (o / jnp.sum(den, axis=-1, keepdims=True)).astype(BF16)


def _attention(qn, qr, kn, kr2, v, *, row0, n_seq, seq, tq, cache=None):
    nq = seq // tq
    grp = HEAD_GRP * LANE
    qblk = lambda b, h, i: row0 // tq + b * nq + i
    kblk = lambda b: row0 // seq + b
    in_specs = [pl.BlockSpec((tq, grp), lambda b, h, i: (qblk(b, h, i), h)),
                pl.BlockSpec((tq, grp // 2), lambda b, h, i: (qblk(b, h, i), h)),
                pl.BlockSpec((seq, grp), lambda b, h, i: (kblk(b), h)),
                pl.BlockSpec((seq, 2 * LANE), lambda b, h, i: (kblk(b), 0)),
                pl.BlockSpec((seq, grp), lambda b, h, i: (kblk(b), h))]
    args = [qn, qr, kn, kr2, v]
    if cache is not None:
        knc, kr2c, vc = cache
        in_specs += [pl.BlockSpec((PAST_LEN, grp), lambda b, h, i: (b, h)),
                     pl.BlockSpec((PAST_LEN, 2 * LANE), lambda b, h, i: (b, 0)),
                     pl.BlockSpec((PAST_LEN, grp), lambda b, h, i: (b, h))]
        args += [knc, kr2c, vc]
    return pl.pallas_call(
        functools.partial(_attn_kernel, has_cache=cache is not None),
        grid=(n_seq, HEADS // HEAD_GRP, nq),
        in_specs=in_specs,
        out_specs=pl.BlockSpec((tq, grp), lambda b, h, i: (b * nq + i, h)),
        out_shape=jax.ShapeDtypeStruct((n_seq * seq, HEADS * V_DIM), BF16),
        scratch_shapes=[pltpu.VMEM((HEAD_GRP, tq, seq + (PAST_LEN if cache is not None else 0)), F32)],
        compiler_params=_cparams(("arbitrary", "arbitrary", "arbitrary")),
        name="attn_latent" if cache is not None else "attn_context",
    )(*args)


def _router_kernel(op_ref, os_ref, xp_ref, xs_ref, gate1, g1, sh2, sc2, g2, wout, wr_hi, wr_lo, tri,
                   x3_ref, h_ref, ri_ref, rf_ref, cnt_ref, carry, *, prompt_steps):
    step = pl.program_id(0)

    @pl.when(step == 0)
    def _():
        carry[...] = jnp.zeros_like(carry)

    o = _stream_rows(op_ref, os_ref, prompt_steps)
    x3 = _gated(_stream_rows(xp_ref, xs_ref, prompt_steps), _dot(o, wout[...]), g1[...], gate1[0])
    x3_ref[...] = x3
    h = _adaln(x3, g2[...], sc2[0], sh2[0])
    _store_row_tiles(h_ref, lambda cols: h[:, cols])
    h_hi = h.astype(BF16)
    h_lo = (h - h_hi.astype(F32)).astype(BF16)
    dg = lambda a, b: lax.dot_general(a, b, NT_DIMS, preferred_element_type=F32)
    lg = dg(wr_hi[...], h_hi) + dg(wr_hi[...], h_lo) + dg(wr_lo[...], h_hi)
    eidx = lax.broadcasted_iota(I32, lg.shape, 0).astype(F32)
    m1 = jnp.max(lg, axis=0, keepdims=True)
    i1 = jnp.min(jnp.where(lg == m1, eidx, float(N_EXP)), axis=0, keepdims=True)
    sel1 = eidx == i1
    lg2 = jnp.where(sel1, -jnp.inf, lg)
    m2 = jnp.max(lg2, axis=0, keepdims=True)
    i2 = jnp.min(jnp.where(lg2 == m2, eidx, float(N_EXP)), axis=0, keepdims=True)
    sel2 = eidx == i2
    e = jnp.exp(m2 - m1)
    w1 = 1.0 / (1.0 + e)
    w2 = e / (1.0 + e)
    picked = jnp.where(sel1 | sel2, 1.0, 0.0)
    rank = _dot(picked.astype(BF16), tri[...]) + carry[:, 0:1]
    r1 = jnp.sum(jnp.where(sel1, rank, 0.0), axis=0, keepdims=True)
    r2 = jnp.sum(jnp.where(sel2, rank, 0.0), axis=0, keepdims=True)
    carry[...] = carry[...] + jnp.sum(picked, axis=1, keepdims=True)
    cnt_ref[...] = carry[...]
    ri_ref[...] = jnp.where(eidx == 0.0, i1, jnp.where(eidx == 1.0, i2, jnp.where(eidx == 2.0, r1, r2))).astype(I32)
    rf_ref[...] = jnp.where(eidx == 0.0, w1, w2)


def _attn_out_router(o_p, o_s, xp, xs, mods, g1, g2, wout, wr_hi, wr_lo):
    tm = 512
    grp = _grp_bm(tm)
    n_p = NP // tm
    tri = jnp.asarray(np.triu(np.ones((tm, tm), np.float32), 1), BF16)
    row = lambda w: pl.BlockSpec((tm, w), lambda i: (i, 0))
    col = pl.BlockSpec((N_EXP, tm), lambda i: (0, i))
    return pl.pallas_call(
        functools.partial(_router_kernel, prompt_steps=n_p),
        grid=(N // tm,),
        in_specs=_stream_specs(tm, HEADS * V_DIM) + _stream_specs(tm, D) + [_mod_spec(2, grp), _full((1, D)),
                  _mod_spec(3, grp), _mod_spec(4, grp), _full((1, D)),
                  _full((HEADS * V_DIM, D)), _full((N_EXP, D)), _full((N_EXP, D)), _full((tm, tm))],
        out_specs=[row(D), pl.BlockSpec((tm * ROW_TILE, LANE), lambda i: (i, 0)), col, col, _full((N_EXP, LANE))],
        out_shape=[jax.ShapeDtypeStruct((N, D), F32), jax.ShapeDtypeStruct((N * ROW_TILE, LANE), F32),
                   jax.ShapeDtypeStruct((N_EXP, N), I32), jax.ShapeDtypeStruct((N_EXP, N), F32),
                   jax.ShapeDtypeStruct((N_EXP, LANE), F32)],
        scratch_shapes=[pltpu.VMEM((N_EXP, LANE), F32)],
        compiler_params=_cparams(("arbitrary",)),
        name="attn_out_router",
    )(o_p, o_s, xp, xs, mods, g1, mods, mods, g2, wout, wr_hi, wr_lo, tri)


def _tile_rows(r):
    return r * ROW_TILE if isinstance(r, int) else pl.multiple_of(r * ROW_TILE, ROW_TILE)


def _moe_kernel(te, nu, src0_ref, src1_ref, dstp_ref, dstc_ref, h_hbm, wgu, wd, y_hbm,
                hsbuf, ybuf, hbf, acc, sem_g, sem_s):
    i = pl.program_id(0)
    last = pl.num_programs(0) - 1
    cur = i % 2
    nxt = 1 - cur
    buf_rows = TM_E * ROW_TILE

    def gather(idx_ref, r, slot):
        i_src = pl.multiple_of(idx_ref[0, 0, r] * ROW_TILE, ROW_TILE)
        return pltpu.make_async_copy(h_hbm.at[pl.ds(i_src, ROW_TILE), :],
                                     hsbuf.at[slot, pl.ds(_tile_rows(r), ROW_TILE), :], sem_g.at[slot])

    def scatter(idx_ref, r, slot):
        i_dst = pl.multiple_of(idx_ref[0, 0, r] * ROW_TILE, ROW_TILE)
        return pltpu.make_async_copy(ybuf.at[slot, pl.ds(_tile_rows(r), ROW_TILE), :],
                                     y_hbm.at[pl.ds(i_dst, ROW_TILE), :], sem_s.at[slot])

    def wait_gather(slot):
        pltpu.make_async_copy(h_hbm.at[pl.ds(0, buf_rows), :], hsbuf.at[slot], sem_g.at[slot]).wait()

    def wait_scatter(slot):
        pltpu.make_async_copy(ybuf.at[slot], y_hbm.at[pl.ds(0, buf_rows), :], sem_s.at[slot]).wait()

    def for_rows(fn):
        def body(g, carry):
            for k in range(ROW_TILE):
                fn(g * ROW_TILE + k, k % 2)
            return carry
        lax.fori_loop(0, TM_E // ROW_TILE, body, 0)

    def side_traffic(r, queue):
        gather(src1_ref, r, nxt).start(priority=queue)
        scatter(dstp_ref, r, nxt).start(priority=queue)

    @pl.when(i == 0)
    def _():
        ybuf[1] = jnp.zeros((buf_rows, LANE), F32)
        for_rows(lambda r, queue: gather(src0_ref, r, 0).start(priority=queue))

    wait_gather(cur)

    @pl.when(i < nu[0])
    def _():
        for c in range(ROW_TILE):
            hbf[:, c * LANE:(c + 1) * LANE] = hsbuf[cur, pl.ds(c, TM_E, stride=ROW_TILE), :].astype(BF16)
        w_gu = wgu.at[0]
        w_d = wd.at[0]
        _swiglu_block(hbf, w_gu, w_d, acc, 0)
        per_iter = TM_E // SIDE_ITERS

        def body(t, carry):
            _swiglu_block(hbf, w_gu, w_d, acc, 1 + 2 * t)
            _swiglu_block(hbf, w_gu, w_d, acc, 2 + 2 * t)
            for k in range(per_iter):
                side_traffic(t * per_iter + k, k % 2)
            return carry

        lax.fori_loop(0, SIDE_ITERS, body, 0)
        for j in range(1 + 2 * SIDE_ITERS, N_FF):
            _swiglu_block(hbf, w_gu, w_d, acc, j)
        _store_row_tiles(ybuf.at[cur], lambda cols: acc[:, cols])

    @pl.when(i >= nu[0])
    def _():
        for_rows(side_traffic)
        ybuf[cur] = jnp.zeros((buf_rows, LANE), F32)

    wait_scatter(nxt)

    @pl.when(i == last)
    def _():
        for_rows(lambda r, queue: scatter(dstc_ref, r, cur).start(priority=queue))
        wait_scatter(cur)
        wait_gather(nxt)


def _moe_experts(tile_expert, n_used, src_tbl, dst_tbl, h, wgu, wd):
    smem = lambda off: pl.BlockSpec((1, 1, TM_E), lambda i, te, nu: (i + off, 0, 0), memory_space=pltpu.SMEM)
    grid_spec = pltpu.PrefetchScalarGridSpec(
        num_scalar_prefetch=2,
        grid=(N_TILES,),
        in_specs=[smem(0), smem(1), smem(0), smem(1),
                  pl.BlockSpec(memory_space=pl.ANY),
                  pl.BlockSpec((1,) + wgu.shape[1:], lambda i, te, nu: (te[i], 0, 0)),
                  pl.BlockSpec((1,) + wd.shape[1:], lambda i, te, nu: (te[i], 0, 0))],
        out_specs=pl.BlockSpec(memory_space=pl.ANY),
        scratch_shapes=[pltpu.VMEM((2, TM_E * ROW_TILE, LANE), F32), pltpu.VMEM((2, TM_E * ROW_TILE, LANE), F32),
                        pltpu.VMEM((TM_E, D), BF16), pltpu.VMEM((TM_E, D), F32),
                        pltpu.SemaphoreType.DMA((2,)), pltpu.SemaphoreType.DMA((2,))],
    )
    return pl.pallas_call(
        _moe_kernel,
        grid_spec=grid_spec,
        out_shape=jax.ShapeDtypeStruct(((2 * N + DUMP_ROWS) * ROW_TILE, LANE), F32),
        compiler_params=_cparams(("arbitrary",)),
        name="moe_experts",
    )(tile_expert, n_used, src_tbl, src_tbl, dst_tbl, dst_tbl, h, wgu, wd)


def _combine_kernel(y1_ref, y2_ref, x_ref, w1_ref, w2_ref, gate2, g3, op_ref, os_ref, *, prompt_steps):
    rows = x_ref.shape[0]
    w1 = w1_ref[...]
    w2 = w2_ref[...]
    f = jnp.concatenate([w1 * _load_row_tiles(y1_ref, c, rows) + w2 * _load_row_tiles(y2_ref, c, rows)
                         for c in range(ROW_TILE)], axis=1)
    out = _gated(x_ref[...], f, g3[...], gate2[0])
    step = pl.program_id(0)

    @pl.when(step < prompt_steps)
    def _():
        op_ref[...] = out

    @pl.when(step >= prompt_steps)
    def _():
        os_ref[...] = out


def _moe_combine(y, x, w1, w2, mods, g3):
    tm = 512
    grp = _grp_bm(tm)
    nb = N // tm
    n_p = NP // tm
    return pl.pallas_call(
        functools.partial(_combine_kernel, prompt_steps=n_p),
        grid=(nb,),
        in_specs=[pl.BlockSpec((tm * ROW_TILE, LANE), lambda i: (i, 0)),
                  pl.BlockSpec((tm * ROW_TILE, LANE), lambda i: (nb + i, 0)),
                  pl.BlockSpec((tm, D), lambda i: (i, 0)),
                  pl.BlockSpec((tm, 1), lambda i: (i, 0)), pl.BlockSpec((tm, 1), lambda i: (i, 0)),
                  _mod_spec(5, grp), _full((1, D))],
        out_specs=[pl.BlockSpec((tm, D), lambda i: (jnp.minimum(i, n_p - 1), 0)),
                   pl.BlockSpec((tm, D), lambda i: (jnp.maximum(i - n_p, 0), 0))],
        out_shape=[jax.ShapeDtypeStruct((NP, D), F32), jax.ShapeDtypeStruct((NS, D), F32)],
        compiler_params=_cparams(("arbitrary",)),
        name="moe_combine",
    )(y, y, x, w1, w2, mods, g3)


def _block_diag(w):
    hh, a, b = w.shape
    eye = jnp.eye(hh, dtype=w.dtype)
    return jnp.einsum('hab,hk->hakb', w, eye).reshape(hh * a, hh * b)


def _s5_matrices(a_re, a_im, log_dt, b_re, b_im, c_re, c_im):
    dt = jnp.exp(log_dt)[:, None]
    mag = jnp.exp(a_re * dt)
    abr = mag * jnp.cos(a_im * dt)
    abi = mag * jnp.sin(a_im * dt)
    den = a_re * a_re + a_im * a_im
    cr = ((abr - 1.0) * a_re + abi * a_im) / den
    ci = (abi * a_re - (abr - 1.0) * a_im) / den
    bbr = cr[..., None] * b_re - ci[..., None] * b_im
    bbi = cr[..., None] * b_im + ci[..., None] * b_re
    hg = S5_G // 2
    eye = jnp.eye(hg, dtype=F32)
    bms, cms = [], []
    for j in range(2):
        sl = slice(j * hg, (j + 1) * hg)
        bd = lambda m: jnp.einsum('gpc,gh->gchp', m[sl], eye).reshape(hg * S5_CH, hg * S5_P)
        bms.append(jnp.concatenate([bd(bbr), bd(bbi)], axis=1))
        cd = lambda m: jnp.einsum('gcp,gh->gphc', m[sl], eye).reshape(hg * S5_P, hg * S5_CH)
        cms.append(jnp.concatenate([cd(c_re), cd(-c_im)], axis=0))
    return (jnp.stack(bms).astype(BF16), abr.reshape(1, S5_N), abi.reshape(1, S5_N),
            jnp.stack(cms).astype(BF16))


def _rope_tables(tm):
    rows = DEC_SEQ // GRID_W
    row = jnp.repeat(jnp.arange(rows, dtype=F32), GRID_W)
    col = jnp.tile(jnp.arange(GRID_W, dtype=F32), rows)
    nf = QK_ROPE // 4
    inv = ROPE_THETA ** (-jnp.arange(nf, dtype=F32) / nf)
    ang = jnp.concatenate([row[:, None] * inv, col[:, None] * inv], axis=-1)
    cos = jnp.repeat(jnp.cos(ang), 2, axis=-1)
    sin = jnp.stack([-jnp.sin(ang), jnp.sin(ang)], axis=-1).reshape(DEC_SEQ, QK_ROPE)
    ident = lambda t, one: jnp.concatenate([t, jnp.full((tm, t.shape[1]), one, F32)], axis=0)
    cos_q = ident(jnp.tile(cos, (1, HEADS)), 1.0)
    sin_q = ident(jnp.tile(sin, (1, HEADS)), 0.0)
    pad = lambda t, one: jnp.concatenate([t, jnp.full((DEC_SEQ, LANE - QK_ROPE), one, F32)], axis=1)
    return cos_q, sin_q, ident(pad(cos, 1.0), 1.0), ident(pad(sin, 0.0), 0.0)


def _group_states(prompt_state, sample_state):
    w = sample_state.shape[-1]
    return jnp.concatenate([prompt_state.reshape(2, SUB, w), sample_state.reshape(1, SUB, w)], axis=0)


def _layer_ab(xp, xs, m, ng, j, state_lru, state_s5_re, state_s5_im, p):
    xz, x = _ab_inproj(xp, xs, ng[0:1], m, p['ab_w_in'][j].astype(BF16))
    zeros = lambda w: jnp.zeros((BATCH, w), F32)
    outs = []
    for d in range(2):
        wg = jnp.concatenate([_block_diag(p['lru_wa'][j, d]), _block_diag(p['lru_wx'][j, d])], axis=1).astype(BF16)
        bg = jnp.concatenate([p['lru_ba'][j, d], p['lru_bx'][j, d]])[None]
        bm, ar, ai, cm = _s5_matrices(p['s5_a_re'][j, d], p['s5_a_im'][j, d], p['s5_log_dt'][j, d],
                                      p['s5_b_re'][j, d], p['s5_b_im'][j, d], p['s5_c_re'][j, d], p['s5_c_im'][j, d])
        h0l = _group_states(zeros(LRU_W), state_lru[:, j, d])
        h0r = _group_states(zeros(S5_N), state_s5_re[:, j, d].reshape(DEC_BATCH, S5_N))
        h0i = _group_states(zeros(S5_N), state_s5_im[:, j, d].reshape(DEC_BATCH, S5_N))
        outs.append(_ab_scan(xz, h0l, h0r, h0i, p['ab_conv_w'][j], p['ab_conv_b'][j][None], wg, bg,
                             p['lru_lambda'][j, d][None], bm, ar, ai, cm, reverse=(d == 1)))
    (haf, yf, llf, lrf, lif), (hab, yr, llb, lrb, lib) = outs
    x = _ab_out(haf, hab, yf, yr, xz, x, m, ng[1:2], p['s5_d'][j][None], p['s5_w_glu'][j].astype(BF16),
                p['s5_b_glu'][j][None], p['ab_w_out'][j].astype(BF16))
    streams = _ffn(x, m, ng[2:3], ng[3:4], p['ffn_w_gate_up'][j].astype(BF16), p['ffn_w_down'][j].astype(BF16))
    prompt = lambda f, b, w: jnp.stack([f[:2].reshape(BATCH, w), b[:2].reshape(BATCH, w)], axis=1)
    lru = prompt(llf, llb, LRU_W)
    s5r = prompt(lrf, lrb, S5_N).reshape(BATCH, 2, S5_G, S5_P)
    s5i = prompt(lif, lib, S5_N).reshape(BATCH, 2, S5_G, S5_P)
    return tuple(streams), lru, s5r, s5i


def _head_major(w, parts):
    k = w.shape[0]
    per_head = w.reshape(k, HEADS, -1)
    out, start = [], 0
    for width in parts:
        out.append(per_head[:, :, start:start + width].reshape(k, HEADS * width))
        start += width
    return jnp.concatenate(out, axis=1)


def _layer_mla_moe(xp, xs, m, ng, j, cache_kv_latent, cache_k_rope, p):
    w1 = jnp.concatenate([p['mla_w_in'][j], jnp.zeros((D, LANE - QK_ROPE), F32)], axis=1).astype(BF16)
    wuq = _head_major(p['mla_w_uq'][j], (QK_NOPE, QK_ROPE)).astype(BF16)
    wukv = _head_major(p['mla_w_ukv'][j], (QK_NOPE, V_DIM)).astype(BF16)
    tables = _rope_tables(256)
    qn, qr, kn, v, kr2, ckv, krr = _mla_proj(xp, xs, ng[0:1], m, w1, p['mla_g_q'][j][None], p['mla_g_kv'][j][None],
                                              wuq, wukv, *tables)
    knc, vc = _cache_kv(cache_kv_latent[:, j].reshape(DEC_BATCH * PAST_LEN, KV_LORA), wukv)
    krc = cache_k_rope[:, j].reshape(DEC_BATCH * PAST_LEN, QK_ROPE)
    z = jnp.zeros_like(krc)
    kr2c = jnp.concatenate([krc, z, z, krc], axis=1).astype(BF16)
    o_p = _attention(qn, qr, kn, kr2, v, row0=0, n_seq=BATCH, seq=SEQ, tq=SEQ)
    o_s = _attention(qn, qr, kn, kr2, v, row0=NP, n_seq=DEC_BATCH, seq=DEC_SEQ, tq=512, cache=(knc, kr2c, vc))
    wr_t = p['moe_w_router'][j].T
    wr_hi = wr_t.astype(BF16)
    wr_lo = (wr_t - wr_hi.astype(F32)).astype(BF16)
    x3, h, ri, rf, cnt = _attn_out_router(o_p, o_s, xp, xs, m, ng[1:2], ng[2:3], p['mla_w_out'][j].astype(BF16), wr_hi, wr_lo)
    counts = cnt[:, 0].astype(I32)
    padded = ((counts + TM_E - 1) // TM_E) * TM_E
    ends = jnp.cumsum(padded)
    offs = ends - padded
    pos1 = offs[ri[0]] + ri[2]
    pos2 = offs[ri[1]] + ri[3]
    pick_tok = jnp.arange(2 * N, dtype=I32)
    dest = jnp.full((P_ROWS,), -1, I32).at[jnp.concatenate([pos1, pos2])].set(pick_tok, unique_indices=True)
    is_pad = dest < 0
    pad_row = 2 * N + TM_E + jnp.cumsum(is_pad.astype(I32)) - 1
    src_tbl = jnp.concatenate([jnp.where(is_pad, 0, dest % N), jnp.zeros((TM_E,), I32)])
    dst_tbl = jnp.concatenate([2 * N + jnp.arange(TM_E, dtype=I32), jnp.where(is_pad, pad_row, dest)])
    n_used = (ends[-1] // TM_E).astype(I32)[None]
    tile_row = jnp.minimum(jnp.arange(N_TILES, dtype=I32), n_used - 1) * TM_E
    tile_expert = jnp.sum((tile_row[:, None] >= ends[None, :]).astype(I32), axis=1)
    y = _moe_experts(tile_expert, n_used, src_tbl.reshape(N_TILES + 1, 1, TM_E), dst_tbl.reshape(N_TILES + 1, 1, TM_E),
                     h, p['moe_w_gate_up'][j].astype(BF16), p['moe_w_down'][j].astype(BF16))
    xp, xs = _moe_combine(y, x3, rf[0][:, None], rf[1][:, None], m, ng[3:4])
    kv_new = ckv.reshape(BATCH, SEQ, KV_LORA)
    kr_new = krr[:, :QK_ROPE].reshape(BATCH, SEQ, QK_ROPE)
    return (xp.reshape(BATCH, SEQ, D), xs.reshape(DEC_BATCH, DEC_SEQ, D)), kv_new, kr_new


def kernel(x_prompt, x_sample, c, state_lru, state_s5_re, state_s5_im, cache_kv_latent, cache_k_rope, c_ctx, w_mod, b_mod, norm_gains, ab_w_in, ab_conv_w, ab_conv_b, lru_wa, lru_ba, lru_wx, lru_bx, lru_lambda, s5_a_re, s5_a_im, s5_log_dt, s5_b_re, s5_b_im, s5_c_re, s5_c_im, s5_d, s5_w_glu, s5_b_glu, ab_w_out, ffn_w_gate_up, ffn_w_down, mla_w_in, mla_g_q, mla_g_kv, mla_w_uq, mla_w_ukv, mla_w_out, moe_w_router, moe_w_gate_up, moe_w_down):
    p = dict(ab_w_in=ab_w_in, ab_conv_w=ab_conv_w, ab_conv_b=ab_conv_b, lru_wa=lru_wa, lru_ba=lru_ba,
             lru_wx=lru_wx, lru_bx=lru_bx, lru_lambda=lru_lambda, s5_a_re=s5_a_re, s5_a_im=s5_a_im,
             s5_log_dt=s5_log_dt, s5_b_re=s5_b_re, s5_b_im=s5_b_im, s5_c_re=s5_c_re, s5_c_im=s5_c_im,
             s5_d=s5_d, s5_w_glu=s5_w_glu, s5_b_glu=s5_b_glu, ab_w_out=ab_w_out, ffn_w_gate_up=ffn_w_gate_up,
             ffn_w_down=ffn_w_down, mla_w_in=mla_w_in, mla_g_q=mla_g_q, mla_g_kv=mla_g_kv, mla_w_uq=mla_w_uq,
             mla_w_ukv=mla_w_ukv, mla_w_out=mla_w_out, moe_w_router=moe_w_router, moe_w_gate_up=moe_w_gate_up,
             moe_w_down=moe_w_down)
    depth = w_mod.shape[0]
    cond = jnp.concatenate([c_ctx[None], c, jnp.zeros((2 * SUB - 1 - DEC_BATCH, D), F32)], axis=0)
    mod = _modulation(cond, w_mod, b_mod)
    ctx_tile = lambda l: jnp.broadcast_to(mod[l, 0:1], (SUB, 6 * D))
    streams = (x_prompt, x_sample)
    lru_l, s5r_l, s5i_l, kv_l, kr_l = [], [], [], [], []
    for layer in range(depth):
        j = layer // 2
        ng = norm_gains[layer]
        if layer % 2 == 0:
            m = jnp.stack([ctx_tile(layer), mod[layer, 1:1 + DEC_BATCH]])
            streams, lru, s5r, s5i = _layer_ab(*streams, m, ng, j, state_lru, state_s5_re, state_s5_im, p)
            lru_l.append(lru)
            s5r_l.append(s5r)
            s5i_l.append(s5i)
        else:
            lat = jnp.broadcast_to(mod[layer, 1:1 + DEC_BATCH, None, :], (DEC_BATCH, SUB, 6 * D))
            m = jnp.concatenate([ctx_tile(layer)[None], lat], axis=0)
            streams, kv_new, kr_new = _layer_mla_moe(streams[0].reshape(NP, D), streams[1].reshape(NS, D), m, ng, j,
                                                     cache_kv_latent, cache_k_rope, p)
            kv_l.append(kv_new)
            kr_l.append(kr_new)
    return (streams[0], streams[1],
            jnp.stack(lru_l, axis=1), jnp.stack(s5r_l, axis=1), jnp.stack(s5i_l, axis=1),
            jnp.stack(kv_l, axis=1), jnp.stack(kr_l, axis=1))
```

```python
import functools
import math

import numpy as np
import jax
import jax.numpy as jnp
from jax import lax
from jax.experimental import pallas as pl
from jax.experimental.pallas import tpu as pltpu

F32 = jnp.float32
BF16 = jnp.bfloat16
I32 = jnp.int32

D = 1024
BATCH, SEQ = 16, 256
DEC_BATCH, DEC_SEQ = 8, 2048
PAST_LEN = 256
GRID_W = 64
LRU_W = 512
LRU_HEADS = 8
LRU_C = 8.0
CONV_W = 4
S5_W = 512
S5_CH = 16
S5_G = 32
S5_P = 64
S5_N = S5_G * S5_P
HEADS = 8
QK_NOPE, QK_ROPE, V_DIM = 128, 64, 128
Q_LORA, KV_LORA = 384, 256
ROPE_THETA = 10000.0
D_FF = 2816
N_EXP = 8
EPS = 1e-6

NP = BATCH * SEQ
NS = DEC_BATCH * DEC_SEQ
N = NP + NS
SUB = 8
LANE = 128
ROW_TILE = D // LANE
T_CHUNK = 32
R_CHUNK = T_CHUNK * SUB
FF_BLK = 256
KEY_BLK = 256
HEAD_GRP = 4
MLA_TM = 512
MLA_SUB = 2
ROUTER_SUB = 1
ABOUT_SUB = 1
INPROJ_T = 64
FFN_T = 128
SIDE_ITERS = 8
N_FF = D_FF // FF_BLK
TM_E = 512
P_ROWS = 2 * N + N_EXP * TM_E
N_TILES = P_ROWS // TM_E
DUMP_ROWS = P_ROWS - 2 * N + TM_E
VMEM_LIMIT = 56 * 1024 * 1024

NT_DIMS = (((1,), (1,)), ((), ()))


def _cparams(sem):
    return pltpu.CompilerParams(dimension_semantics=sem, vmem_limit_bytes=VMEM_LIMIT)


def _dot(a, b):
    return jnp.dot(a, b, preferred_element_type=F32)


def _sigmoid(x):
    return 1.0 / (1.0 + jnp.exp(-x))


def _neg_expm1_2x(log_a, a):
    z = 2.0 * log_a
    series = -z * (1.0 + z * (1.0 / 2.0) * (1.0 + z * (1.0 / 3.0) * (1.0 + z * (1.0 / 4.0))))
    return jnp.where(z > -0.02, series, (1.0 - a) * (1.0 + a))


def _gelu(x):
    return x * (0.5 * (1.0 + jnp.tanh(math.sqrt(2.0 / math.pi) * (x + 0.044715 * (x * x * x)))))


def _rms(x, g):
    ms = jnp.mean(x * x, axis=-1, keepdims=True)
    return x * lax.rsqrt(ms + EPS) * g


def _rows8(y, fn):
    r, c = y.shape
    return fn(y.reshape(r // SUB, SUB, c)).reshape(r, c)


def _adaln(x, g, scale, shift):
    return _rows8(_rms(x, g), lambda y: y * (1.0 + scale)[None] + shift[None])


def _gated(x, y, g, gate):
    return x + _rows8(_rms(y, g), lambda z: z * gate[None])


def _store_row_tiles(ref, piece, row0=0, rows=None):
    rows = ref.shape[0] // ROW_TILE if rows is None else rows
    for c in range(ROW_TILE):
        ref[pl.ds(row0 * ROW_TILE + c, rows, stride=ROW_TILE), :] = piece(slice(c * LANE, (c + 1) * LANE))


def _load_row_tiles(ref, c, rows):
    return ref[pl.ds(c, rows, stride=ROW_TILE), :]


def _full(shape):
    nd = len(shape)
    return pl.BlockSpec(shape, lambda *_: (0,) * nd)


def _mod_spec(k, grp):
    return pl.BlockSpec((1, SUB, D), lambda i, *_: (grp(i), 0, k))


def _grp_tm(tm):
    return lambda i: (i * tm >= NP).astype(I32)


def _grp_bm(tm):
    return lambda i: jnp.where(i * tm < NP, 0, 1 + (i * tm - NP) // DEC_SEQ)


def _mod_kernel(c_ref, w_ref, b_ref, o_ref):
    c = c_ref[...]
    s = c * _sigmoid(c)
    o_ref[0] = _dot(s.astype(BF16), w_ref[0].astype(BF16)) + b_ref[0]


def _modulation(cond, w_mod, b_mod):
    depth = w_mod.shape[0]
    rows = cond.shape[0]
    return pl.pallas_call(
        _mod_kernel,
        grid=(depth, 6),
        in_specs=[_full((rows, D)),
                  pl.BlockSpec((1, D, D), lambda l, j: (l, 0, j)),
                  pl.BlockSpec((1, 1, D), lambda l, j: (l, 0, j))],
        out_specs=pl.BlockSpec((1, rows, D), lambda l, j: (l, 0, j)),
        out_shape=jax.ShapeDtypeStruct((depth, rows, 6 * D), F32),
        compiler_params=_cparams(("arbitrary", "arbitrary")),
        name="modulation",
    )(cond, w_mod, b_mod.reshape(depth, 1, 6 * D))


def _time_major_copies(step, t_steps, xp_hbm, xs_hbm, buf, sem, to_hbm):
    p_steps = (BATCH // SUB) * (SEQ // t_steps)
    per_group = SEQ // t_steps

    def issue(hbm, seq0, t0):
        t0 = pl.multiple_of(t0, t_steps)
        for b in range(SUB):
            rows = hbm.at[seq0 + b, pl.ds(t0, t_steps), :]
            tile = buf.at[:, b, :]
            (pltpu.make_async_copy(tile, rows, sem) if to_hbm else pltpu.make_async_copy(rows, tile, sem)).start()

    @pl.when(step < p_steps)
    def _():
        issue(xp_hbm, (step // per_group) * SUB, (step % per_group) * t_steps)

    @pl.when(step >= p_steps)
    def _():
        issue(xs_hbm, 0, (step - p_steps) * t_steps)


def _time_major_wait(t_steps, xs_hbm, buf, sem, to_hbm):
    for b in range(SUB):
        rows = xs_hbm.at[0, pl.ds(0, t_steps), :]
        tile = buf.at[:, b, :]
        (pltpu.make_async_copy(tile, rows, sem) if to_hbm else pltpu.make_async_copy(rows, tile, sem)).wait()


def _inproj_kernel(xp_hbm, xs_hbm, g_ref, sh_ref, sc_ref, w_ref, o_ref, xtm_ref, buf, sem):
    step = pl.program_id(0)
    slot = step % 2

    @pl.when(step == 0)
    def _():
        _time_major_copies(step, INPROJ_T, xp_hbm, xs_hbm, buf.at[0], sem.at[0], False)

    @pl.when(step + 1 < pl.num_programs(0))
    def _():
        _time_major_copies(step + 1, INPROJ_T, xp_hbm, xs_hbm, buf.at[1 - slot], sem.at[1 - slot], False)

    _time_major_wait(INPROJ_T, xs_hbm, buf.at[slot], sem.at[slot], False)
    x = buf[slot].reshape(INPROJ_T * SUB, D)
    xtm_ref[...] = x
    h = _adaln(x, g_ref[...], sc_ref[0], sh_ref[0])
    o_ref[...] = _dot(h.astype(BF16), w_ref[...])


def _ab_inproj(xp, xs, gain, mods, w_in):
    tm = INPROJ_T * SUB
    nout = w_in.shape[1]
    grp = _grp_tm(tm)
    return pl.pallas_call(
        _inproj_kernel,
        grid=(N // tm,),
        in_specs=[pl.BlockSpec(memory_space=pl.ANY), pl.BlockSpec(memory_space=pl.ANY),
                  _full((1, D)),
                  _mod_spec(0, grp), _mod_spec(1, grp),
                  _full((D, nout))],
        out_specs=[pl.BlockSpec((tm, nout), lambda i: (i, 0)), pl.BlockSpec((tm, D), lambda i: (i, 0))],
        out_shape=[jax.ShapeDtypeStruct((N, nout), F32), jax.ShapeDtypeStruct((N, D), F32)],
        scratch_shapes=[pltpu.VMEM((2, INPROJ_T, SUB, D), F32), pltpu.SemaphoreType.DMA((2,))],
        compiler_params=_cparams(("arbitrary",)),
        name="ab_inproj",
    )(xp, xs, gain, mods, mods, w_in)


def _scan_table(reverse):
    cols = []
    groups = [(0, SEQ // T_CHUNK, 0), (1, SEQ // T_CHUNK, SEQ // T_CHUNK),
              (2, DEC_SEQ // T_CHUNK, NP // R_CHUNK)]
    for g, nc, base in groups:
        order = range(nc - 1, -1, -1) if reverse else range(nc)
        for k, c in enumerate(order):
            cols.append((base + c, g, int(k == 0), int(c > 0), int(c < nc - 1)))
    return np.asarray(cols, np.int32).T.copy()


def _scan_kernel(tbl, xa_ref, xp_ref, xn_ref, xb_ref, h0l_ref, h0r_ref, h0i_ref,
                 cw_ref, cb_ref, wg_ref, bg_ref, lam_ref, bm_ref, ar_ref, ai_ref, cm_ref,
                 ha_ref, y_ref, ll_ref, lr_ref, li_ref,
                 ext, abuf, bbuf, hre, him, hl, sre, sim, *, reverse):
    s = pl.program_id(0)

    @pl.when(tbl[2, s] == 1)
    def _():
        hl[...] = h0l_ref[0]
        sre[...] = h0r_ref[0]
        sim[...] = h0i_ref[0]

    ext[0:2 * SUB] = jnp.where(tbl[3, s] == 1, xp_ref[...], 0.0)
    ext[2 * SUB:2 * SUB + R_CHUNK] = xa_ref[...]
    ext[2 * SUB + R_CHUNK:3 * SUB + R_CHUNK] = jnp.where(tbl[4, s] == 1, xn_ref[...], 0.0)
    xa = cb_ref[...] + cw_ref[0:1] * ext[0:R_CHUNK]
    for k in range(1, CONV_W):
        xa = xa + cw_ref[k:k + 1] * ext[k * SUB:k * SUB + R_CHUNK]

    gz = _dot(xa.astype(BF16), wg_ref[...]) + bg_ref[...]
    r = _sigmoid(gz[:, :LRU_W])
    i = _sigmoid(gz[:, LRU_W:])
    lam = lam_ref[...]
    log_sig = jnp.minimum(lam, 0.0) - jnp.log1p(jnp.exp(-jnp.abs(lam)))
    log_a = LRU_C * r * log_sig
    a = jnp.exp(log_a)
    abuf[...] = a
    bbuf[...] = jnp.sqrt(_neg_expm1_2x(log_a, a)) * (i * xa)

    order = range(T_CHUNK - 1, -1, -1) if reverse else range(T_CHUNK)
    h = hl[...]
    for t in order:
        rows = slice(t * SUB, (t + 1) * SUB)
        h = abuf[rows] * h + bbuf[rows]
        ha_ref[rows, :] = h
    hl[...] = h
    ll_ref[0] = h

    ub = xb_ref[...].astype(BF16)
    half = S5_N // 2
    for j in range(2):
        bu = _dot(ub[:, j * 256:(j + 1) * 256], bm_ref[j])
        hre[:, j * half:(j + 1) * half] = bu[:, :half]
        him[:, j * half:(j + 1) * half] = bu[:, half:]
    cblk = 4 * LANE
    for cb in range(S5_N // cblk):
        cols = slice(cb * cblk, (cb + 1) * cblk)
        ar = jnp.broadcast_to(ar_ref[:, cols], (SUB, cblk))
        ai = jnp.broadcast_to(ai_ref[:, cols], (SUB, cblk))
        hr = sre[:, cols]
        hi = sim[:, cols]
        for t in order:
            rows = slice(t * SUB, (t + 1) * SUB)
            nr = ar * hr - ai * hi + hre[rows, cols]
            ni = ar * hi + ai * hr + him[rows, cols]
            hr, hi = nr, ni
            hre[rows, cols] = hr
            him[rows, cols] = hi
        sre[:, cols] = hr
        sim[:, cols] = hi
    lr_ref[0] = sre[...]
    li_ref[0] = sim[...]
    for j in range(2):
        hc = jnp.concatenate([hre[:, j * half:(j + 1) * half], him[:, j * half:(j + 1) * half]],
                             axis=1).astype(BF16)
        y_ref[:, j * 256:(j + 1) * 256] = _dot(hc, cm_ref[j])


def _ab_scan(xz, h0l, h0r, h0i, cw, cb, wg, bg, lam, bm, ar, ai, cm, reverse):
    tbl = jnp.asarray(_scan_table(reverse))
    n_steps = tbl.shape[1]
    blk = lambda s, t: t[0, s]
    grp = lambda s, t: t[1, s]
    state_spec = lambda w: pl.BlockSpec((1, SUB, w), lambda s, t: (grp(s, t), 0, 0))
    const = lambda shape: pl.BlockSpec(shape, lambda s, t: (0,) * len(shape))
    grid_spec = pltpu.PrefetchScalarGridSpec(
        num_scalar_prefetch=1,
        grid=(n_steps,),
        in_specs=[
            pl.BlockSpec((R_CHUNK, LRU_W), lambda s, t: (blk(s, t), 0)),
            pl.BlockSpec((2 * SUB, LRU_W), lambda s, t: (jnp.maximum(blk(s, t) * (T_CHUNK // 2) - 1, 0), 0)),
            pl.BlockSpec((SUB, LRU_W), lambda s, t: (jnp.minimum((blk(s, t) + 1) * T_CHUNK, N // SUB - 1), 0)),
            pl.BlockSpec((R_CHUNK, S5_W), lambda s, t: (blk(s, t), 2)),
            state_spec(LRU_W), state_spec(S5_N), state_spec(S5_N),
            const((CONV_W, LRU_W)), const((1, LRU_W)),
            const((LRU_W, 2 * LRU_W)), const((1, 2 * LRU_W)), const((1, LRU_W)),
            const((2, 256, S5_N)), const((1, S5_N)), const((1, S5_N)), const((2, S5_N, 256)),
        ],
        out_specs=[
            pl.BlockSpec((R_CHUNK, LRU_W), lambda s, t: (blk(s, t), 0)),
            pl.BlockSpec((R_CHUNK, S5_W), lambda s, t: (blk(s, t), 0)),
            state_spec(LRU_W), state_spec(S5_N), state_spec(S5_N),
        ],
        scratch_shapes=[
            pltpu.VMEM((R_CHUNK + 3 * SUB, LRU_W), F32),
            pltpu.VMEM((R_CHUNK, LRU_W), F32), pltpu.VMEM((R_CHUNK, LRU_W), F32),
            pltpu.VMEM((R_CHUNK, S5_N), F32), pltpu.VMEM((R_CHUNK, S5_N), F32),
            pltpu.VMEM((SUB, LRU_W), F32), pltpu.VMEM((SUB, S5_N), F32), pltpu.VMEM((SUB, S5_N), F32),
        ],
    )
    return pl.pallas_call(
        functools.partial(_scan_kernel, reverse=reverse),
        grid_spec=grid_spec,
        out_shape=[jax.ShapeDtypeStruct((N, LRU_W), F32), jax.ShapeDtypeStruct((N, S5_W), F32),
                   jax.ShapeDtypeStruct((3, SUB, LRU_W), F32),
                   jax.ShapeDtypeStruct((3, SUB, S5_N), F32), jax.ShapeDtypeStruct((3, SUB, S5_N), F32)],
        compiler_params=_cparams(("arbitrary",)),
        name="ab_scan_bwd" if reverse else "ab_scan_fwd",
    )(tbl, xz, xz, xz, xz, h0l, h0r, h0i, cw, cb, wg, bg, lam, bm, ar, ai, cm)


def _about_kernel(haf, hab, yf, yr, ga, xb, x_ref, gate, g1, d_ref, wglu, bglu, wout, o_ref):
    sub = x_ref.shape[0] // ABOUT_SUB
    for r in range(ABOUT_SUB):
        rows = slice(r * sub, (r + 1) * sub)
        ya = (haf[rows, :] + hab[rows, :]) * _gelu(ga[rows, :])
        yb0 = _gelu(yf[rows, :] + yr[rows, :] + d_ref[...] * xb[rows, :])
        yb = yb0 * _sigmoid(_dot(yb0.astype(BF16), wglu[...]) + bglu[...])
        out = _dot(ya.astype(BF16), wout[0:LRU_W]) + _dot(yb.astype(BF16), wout[LRU_W:LRU_W + S5_W])
        o_ref[rows, :] = _gated(x_ref[rows, :], out, g1[...], gate[0])


def _ab_out(haf, hab, yf, yr, xz, x, mods, g1, s5_d, wglu, bglu, wout):
    tm = 512
    grp = _grp_tm(tm)
    half = lambda c: pl.BlockSpec((tm, LRU_W), lambda i: (i, c))
    return pl.pallas_call(
        _about_kernel,
        grid=(N // tm,),
        in_specs=[half(0), half(0), half(0), half(0), half(1), half(2),
                  pl.BlockSpec((tm, D), lambda i: (i, 0)),
                  _mod_spec(2, grp), _full((1, D)), _full((1, S5_W)),
                  _full((S5_W, S5_W)), _full((1, S5_W)), _full((LRU_W + S5_W, D))],
        out_specs=pl.BlockSpec((tm, D), lambda i: (i, 0)),
        out_shape=jax.ShapeDtypeStruct((N, D), F32),
        compiler_params=_cparams(("arbitrary",)),
        name="ab_out",
    )(haf, hab, yf, yr, xz, xz, x, mods, g1, s5_d, wglu, bglu, wout)


def _swiglu_block(hbf, wgu, wd, acc, j):
    static = isinstance(j, int)
    blk = lambda start: pl.ds(start if static else pl.multiple_of(start, FF_BLK), FF_BLK)
    h = hbf[...]
    g = _dot(h, wgu[:, blk(j * FF_BLK)])
    u = _dot(h, wgu[:, blk(D_FF + j * FF_BLK)])
    act = (g * _sigmoid(g)) * u
    part = _dot(act.astype(BF16), wd[blk(j * FF_BLK), :])
    if static and j == 0:
        acc[...] = part
    else:
        acc[...] += part


def _ffn_kernel(x_ref, sh, sc, gt, g2, g3, wgu, wd, op_hbm, os_hbm, hbf, acc, obuf, sem):
    step = pl.program_id(0)
    slot = step % 2
    hbf[...] = _adaln(x_ref[...], g2[...], sc[0], sh[0]).astype(BF16)
    for j in range(N_FF):
        _swiglu_block(hbf, wgu, wd, acc, j)
    obuf[slot] = _gated(x_ref[...], acc[...], g3[...], gt[0]).reshape(FFN_T, SUB, D)
    _time_major_copies(step, FFN_T, op_hbm, os_hbm, obuf.at[slot], sem.at[slot], True)

    @pl.when(step > 0)
    def _():
        _time_major_wait(FFN_T, os_hbm, obuf.at[1 - slot], sem.at[1 - slot], True)

    @pl.when(step == pl.num_programs(0) - 1)
    def _():
        _time_major_wait(FFN_T, os_hbm, obuf.at[slot], sem.at[slot], True)


def _ffn(x, mods, g2, g3, wgu, wd):
    tm = FFN_T * SUB
    grp = _grp_tm(tm)
    once = lambda shape: pl.BlockSpec(shape, lambda i: (0,) * len(shape), pipeline_mode=pl.Buffered(1))
    return pl.pallas_call(
        _ffn_kernel,
        grid=(N // tm,),
        in_specs=[pl.BlockSpec((tm, D), lambda i: (i, 0)),
                  _mod_spec(3, grp), _mod_spec(4, grp), _mod_spec(5, grp),
                  _full((1, D)), _full((1, D)),
                  once(wgu.shape), once(wd.shape)],
        out_specs=[pl.BlockSpec(memory_space=pl.ANY), pl.BlockSpec(memory_space=pl.ANY)],
        out_shape=[jax.ShapeDtypeStruct((BATCH, SEQ, D), F32), jax.ShapeDtypeStruct((DEC_BATCH, DEC_SEQ, D), F32)],
        scratch_shapes=[pltpu.VMEM((tm, D), BF16), pltpu.VMEM((tm, D), F32),
                        pltpu.VMEM((2, FFN_T, SUB, D), F32), pltpu.SemaphoreType.DMA((2,))],
        compiler_params=_cparams(("arbitrary",)),
        name="ffn",
    )(x, mods, mods, mods, g2, g3, wgu, wd)


def _pair_swap(x):
    outs = []
    for c in range(x.shape[1] // LANE):
        xc = x[:, c * LANE:(c + 1) * LANE]
        even = lax.broadcasted_iota(I32, xc.shape, 1) % 2 == 0
        outs.append(jnp.where(even, pltpu.roll(xc, LANE - 1, 1), pltpu.roll(xc, 1, 1)))
    return outs[0] if len(outs) == 1 else jnp.concatenate(outs, axis=1)


def _stream_specs(tm, width):
    n_p = NP // tm
    return [pl.BlockSpec((tm, width), lambda i: (jnp.minimum(i, n_p - 1), 0)),
            pl.BlockSpec((tm, width), lambda i: (jnp.maximum(i - n_p, 0), 0))]


def _stream_rows(p_ref, s_ref, prompt_steps):
    return jnp.where(pl.program_id(0) < prompt_steps, p_ref[...], s_ref[...])


def _mlaproj_kernel(xp_ref, xs_ref, g0, sh, sc, w1, gq, gkv, wuq, wukv, cq_ref, sq_ref, ck_ref, sk_ref,
                    qn_ref, qr_ref, kn_ref, v_ref, kr2_ref, ckv_ref, krr_ref, *, prompt_steps):
    x = _stream_rows(xp_ref, xs_ref, prompt_steps)
    sub = x.shape[0] // MLA_SUB
    cache_rows = []
    for r in range(MLA_SUB):
        rows = slice(r * sub, (r + 1) * sub)
        h = _adaln(x[rows], g0[...], sc[0], sh[0])
        dn = _dot(h.astype(BF16), w1[...])
        cq = _rms(dn[:, :Q_LORA], gq[...])
        ckv = _rms(dn[:, Q_LORA:Q_LORA + KV_LORA], gkv[...])
        krp = dn[:, Q_LORA + KV_LORA:]
        cache_rows.append((rows, ckv, krp))
        q = _dot(cq.astype(BF16), wuq[...])
        qn_ref[rows, :] = q[:, :HEADS * QK_NOPE].astype(BF16)
        qr = q[:, HEADS * QK_NOPE:]
        qr_ref[rows, :] = (qr * cq_ref[rows, :] + _pair_swap(qr) * sq_ref[rows, :]).astype(BF16)
        kv = _dot(ckv.astype(BF16), wukv[...])
        kn_ref[rows, :] = kv[:, :HEADS * QK_NOPE].astype(BF16)
        v_ref[rows, :] = kv[:, HEADS * QK_NOPE:].astype(BF16)
        kr = krp * ck_ref[rows, :] + _pair_swap(krp) * sk_ref[rows, :]
        kr2_ref[rows, :] = jnp.concatenate([kr, pltpu.roll(kr, QK_ROPE, 1)], axis=1).astype(BF16)

    @pl.when(pl.program_id(0) < prompt_steps)
    def _():
        for rows, ckv, krp in cache_rows:
            ckv_ref[rows, :] = ckv
            krr_ref[rows, :] = krp


def _mla_proj(xp, xs, g0, mods, w1, gq, gkv, wuq, wukv, cos_q, sin_q, cos_k, sin_k):
    tm = MLA_TM
    grp = _grp_bm(tm)
    n_pos = DEC_SEQ // tm
    tab = lambda w: pl.BlockSpec((tm, w), lambda i: (jnp.where(i * tm < NP, n_pos, (i - NP // tm) % n_pos), 0))
    row = lambda w: pl.BlockSpec((tm, w), lambda i: (i, 0))
    shp = lambda w, dt: jax.ShapeDtypeStruct((N, w), dt)
    n_p = NP // tm
    prow = lambda w: pl.BlockSpec((tm, w), lambda i: (jnp.minimum(i, n_p - 1), 0))
    return pl.pallas_call(
        functools.partial(_mlaproj_kernel, prompt_steps=n_p),
        grid=(N // tm,),
        in_specs=_stream_specs(tm, D) + [_full((1, D)), _mod_spec(0, grp), _mod_spec(1, grp),
                  _full(w1.shape), _full((1, Q_LORA)), _full((1, KV_LORA)),
                  _full(wuq.shape), _full(wukv.shape),
                  tab(HEADS * QK_ROPE), tab(HEADS * QK_ROPE), tab(LANE), tab(LANE)],
        out_specs=[row(HEADS * QK_NOPE), row(HEADS * QK_ROPE), row(HEADS * QK_NOPE), row(HEADS * V_DIM),
                   row(2 * LANE), prow(KV_LORA), prow(LANE)],
        out_shape=[shp(HEADS * QK_NOPE, BF16), shp(HEADS * QK_ROPE, BF16), shp(HEADS * QK_NOPE, BF16),
                   shp(HEADS * V_DIM, BF16), shp(2 * LANE, BF16),
                   jax.ShapeDtypeStruct((NP, KV_LORA), F32), jax.ShapeDtypeStruct((NP, LANE), F32)],
        compiler_params=_cparams(("arbitrary",)),
        name="mla_proj",
    )(xp, xs, g0, mods, mods, w1, gq, gkv, wuq, wukv, cos_q, sin_q, cos_k, sin_k)


def _cachekv_kernel(c_ref, w_ref, kn_ref, v_ref):
    kv = _dot(c_ref[...].astype(BF16), w_ref[...])
    kn_ref[...] = kv[:, :HEADS * QK_NOPE].astype(BF16)
    v_ref[...] = kv[:, HEADS * QK_NOPE:].astype(BF16)


def _cache_kv(ckv_cache, wukv):
    rows = ckv_cache.shape[0]
    tm = 512
    return pl.pallas_call(
        _cachekv_kernel,
        grid=(rows // tm,),
        in_specs=[pl.BlockSpec((tm, KV_LORA), lambda i: (i, 0)), _full(wukv.shape)],
        out_specs=[pl.BlockSpec((tm, HEADS * QK_NOPE), lambda i: (i, 0)),
                   pl.BlockSpec((tm, HEADS * V_DIM), lambda i: (i, 0))],
        out_shape=[jax.ShapeDtypeStruct((rows, HEADS * QK_NOPE), BF16),
                   jax.ShapeDtypeStruct((rows, HEADS * V_DIM), BF16)],
        compiler_params=_cparams(("arbitrary",)),
        name="cache_kv",
    )(ckv_cache, wukv)


def _attn_kernel(*refs, has_cache):
    if has_cache:
        qn, qr, kn, kr, v, knc, krc, vc, o_ref, s_scr = refs
        streams = [(knc, krc, vc), (kn, kr, v)]
    else:
        qn, qr, kn, kr, v, o_ref, s_scr = refs
        streams = [(kn, kr, v)]
    chunks = [(k1, k2, vv, c * KEY_BLK) for k1, k2, vv in streams for c in range(k1.shape[0] // KEY_BLK)]
    tq = qn.shape[0]
    a = (QK_NOPE + QK_ROPE) ** -0.5 * math.log2(math.e)
    for hh in range(HEAD_GRP):
        cols = slice(hh * LANE, (hh + 1) * LANE)
        pair_cols = slice((hh // 2) * LANE, (hh // 2 + 1) * LANE)
        kr_cols = slice((hh % 2) * LANE, (hh % 2 + 1) * LANE)
        q = jnp.concatenate([qn[:, cols], qr[:, pair_cols]], axis=1)
        mx = jnp.full((tq, LANE), -jnp.inf, F32)
        for n, (k1, k2, _, r0) in enumerate(chunks):
            k = jnp.concatenate([k1[r0:r0 + KEY_BLK, cols], k2[r0:r0 + KEY_BLK, kr_cols]], axis=1)
            s = lax.dot_general(q, k, NT_DIMS, preferred_element_type=F32)
            s_scr[hh, :, n * KEY_BLK:(n + 1) * KEY_BLK] = s
            for c in range(KEY_BLK // LANE):
                mx = jnp.maximum(mx, s[:, c * LANE:(c + 1) * LANE])
        mb = jnp.max(mx, axis=-1, keepdims=True) * a
        den = jnp.zeros((tq, LANE), F32)
        o = jnp.zeros((tq, V_DIM), F32)
        for n, (_, _, vv, r0) in enumerate(chunks):
            p = jnp.exp2(s_scr[hh, :, n * KEY_BLK:(n + 1) * KEY_BLK] * a - mb)
            for c in range(KEY_BLK // LANE):
                den = den + p[:, c * LANE:(c + 1) * LANE]
            o = o + _dot(p.astype(BF16), vv[r0:r0 + KEY_BLK, cols])
        o_ref[:, cols] = (o / jnp.sum(den, axis=-1, keepdims=True)).astype(BF16)


def _attention(qn, qr, kn, kr2, v, *, row0, n_seq, seq, tq, cache=None):
    nq = seq // tq
    grp = HEAD_GRP * LANE
    qblk = lambda b, h, i: row0 // tq + b * nq + i
    kblk = lambda b: row0 // seq + b
    in_specs = [pl.BlockSpec((tq, grp), lambda b, h, i: (qblk(b, h, i), h)),
                pl.BlockSpec((tq, grp // 2), lambda b, h, i: (qblk(b, h, i), h)),
                pl.BlockSpec((seq, grp), lambda b, h, i: (kblk(b), h)),
                pl.BlockSpec((seq, 2 * LANE), lambda b, h, i: (kblk(b), 0)),
                pl.BlockSpec((seq, grp), lambda b, h, i: (kblk(b), h))]
    args = [qn, qr, kn, kr2, v]
    if cache is not None:
        knc, kr2c, vc = cache
        in_specs += [pl.BlockSpec((PAST_LEN, grp), lambda b, h, i: (b, h)),
                     pl.BlockSpec((PAST_LEN, 2 * LANE), lambda b, h, i: (b, 0)),
                     pl.BlockSpec((PAST_LEN, grp), lambda b, h, i: (b, h))]
        args += [knc, kr2c, vc]
    return pl.pallas_call(
        functools.partial(_attn_kernel, has_cache=cache is not None),
        grid=(n_seq, HEADS // HEAD_GRP, nq),
        in_specs=in_specs,
        out_specs=pl.BlockSpec((tq, grp), lambda b, h, i: (b * nq + i, h)),
        out_shape=jax.ShapeDtypeStruct((n_seq * seq, HEADS * V_DIM), BF16),
        scratch_shapes=[pltpu.VMEM((HEAD_GRP, tq, seq + (PAST_LEN if cache is not None else 0)), F32)],
        compiler_params=_cparams(("arbitrary", "arbitrary", "arbitrary")),
        name="attn_latent" if cache is not None else "attn_context",
    )(*args)


def _router_kernel(op_ref, os_ref, xp_ref, xs_ref, gate1, g1, sh2, sc2, g2, wout, wr_hi, wr_lo, tri,
                   x3_ref, h_ref, ri_ref, rf_ref, cnt_ref, carry, *, prompt_steps):
    step = pl.program_id(0)

    @pl.when(step == 0)
    def _():
        carry[...] = jnp.zeros_like(carry)

    o_all = _stream_rows(op_ref, os_ref, prompt_steps)
    x_all = _stream_rows(xp_ref, xs_ref, prompt_steps)
    sub = x_all.shape[0] // ROUTER_SUB
    dg = lambda a, b: lax.dot_general(a, b, NT_DIMS, preferred_element_type=F32)
    logits = []
    for r in range(ROUTER_SUB):
        rows = slice(r * sub, (r + 1) * sub)
        x3 = _gated(x_all[rows], _dot(o_all[rows], wout[...]), g1[...], gate1[0])
        x3_ref[rows, :] = x3
        h = _adaln(x3, g2[...], sc2[0], sh2[0])
        _store_row_tiles(h_ref, lambda cols, h=h: h[:, cols], r * sub, sub)
        h_hi = h.astype(BF16)
        h_lo = (h - h_hi.astype(F32)).astype(BF16)
        logits.append(dg(wr_hi[...], h_hi) + dg(wr_hi[...], h_lo) + dg(wr_lo[...], h_hi))
    lg = jnp.concatenate(logits, axis=1)
    eidx = lax.broadcasted_iota(I32, lg.shape, 0).astype(F32)
    m1 = jnp.max(lg, axis=0, keepdims=True)
    i1 = jnp.min(jnp.where(lg == m1, eidx, float(N_EXP)), axis=0, keepdims=True)
    sel1 = eidx == i1
    lg2 = jnp.where(sel1, -jnp.inf, lg)
    m2 = jnp.max(lg2, axis=0, keepdims=True)
    i2 = jnp.min(jnp.where(lg2 == m2, eidx, float(N_EXP)), axis=0, keepdims=True)
    sel2 = eidx == i2
    e = jnp.exp(m2 - m1)
    w1 = 1.0 / (1.0 + e)
    w2 = e / (1.0 + e)
    picked = jnp.where(sel1 | sel2, 1.0, 0.0)
    rank = _dot(picked.astype(BF16), tri[...]) + carry[:, 0:1]
    r1 = jnp.sum(jnp.where(sel1, rank, 0.0), axis=0, keepdims=True)
    r2 = jnp.sum(jnp.where(sel2, rank, 0.0), axis=0, keepdims=True)
    carry[...] = carry[...] + jnp.sum(picked, axis=1, keepdims=True)
    cnt_ref[...] = carry[...]
    ri_ref[...] = jnp.where(eidx == 0.0, i1, jnp.where(eidx == 1.0, i2, jnp.where(eidx == 2.0, r1, r2))).astype(I32)
    rf_ref[...] = jnp.where(eidx == 0.0, w1, w2)


def _attn_out_router(o_p, o_s, xp, xs, mods, g1, g2, wout, wr_hi, wr_lo):
    tm = 512
    grp = _grp_bm(tm)
    n_p = NP // tm
    tri = jnp.asarray(np.triu(np.ones((tm, tm), np.float32), 1), BF16)
    row = lambda w: pl.BlockSpec((tm, w), lambda i: (i, 0))
    col = pl.BlockSpec((N_EXP, tm), lambda i: (0, i))
    return pl.pallas_call(
        functools.partial(_router_kernel, prompt_steps=n_p),
        grid=(N // tm,),
        in_specs=_stream_specs(tm, HEADS * V_DIM) + _stream_specs(tm, D) + [_mod_spec(2, grp), _full((1, D)),
                  _mod_spec(3, grp), _mod_spec(4, grp), _full((1, D)),
                  _full((HEADS * V_DIM, D)), _full((N_EXP, D)), _full((N_EXP, D)), _full((tm, tm))],
        out_specs=[row(D), pl.BlockSpec((tm * ROW_TILE, LANE), lambda i: (i, 0)), col, col, _full((N_EXP, LANE))],
        out_shape=[jax.ShapeDtypeStruct((N, D), F32), jax.ShapeDtypeStruct((N * ROW_TILE, LANE), F32),
                   jax.ShapeDtypeStruct((N_EXP, N), I32), jax.ShapeDtypeStruct((N_EXP, N), F32),
                   jax.ShapeDtypeStruct((N_EXP, LANE), F32)],
        scratch_shapes=[pltpu.VMEM((N_EXP, LANE), F32)],
        compiler_params=_cparams(("arbitrary",)),
        name="attn_out_router",
    )(o_p, o_s, xp, xs, mods, g1, mods, mods, g2, wout, wr_hi, wr_lo, tri)


def _tile_rows(r):
    return r * ROW_TILE if isinstance(r, int) else pl.multiple_of(r * ROW_TILE, ROW_TILE)


def _moe_kernel(te, nu, src0_ref, src1_ref, dstp_ref, dstc_ref, h_hbm, wgu, wd, y_hbm,
                hsbuf, ybuf, hbf, acc, sem_g, sem_s):
    i = pl.program_id(0)
    last = pl.num_programs(0) - 1
    cur = i % 2
    nxt = 1 - cur
    buf_rows = TM_E * ROW_TILE

    def gather(idx_ref, r, slot):
        i_src = pl.multiple_of(idx_ref[0, 0, r] * ROW_TILE, ROW_TILE)
        return pltpu.make_async_copy(h_hbm.at[pl.ds(i_src, ROW_TILE), :],
                                     hsbuf.at[slot, pl.ds(_tile_rows(r), ROW_TILE), :], sem_g.at[slot])

    def scatter(idx_ref, r, slot):
        i_dst = pl.multiple_of(idx_ref[0, 0, r] * ROW_TILE, ROW_TILE)
        return pltpu.make_async_copy(ybuf.at[slot, pl.ds(_tile_rows(r), ROW_TILE), :],
                                     y_hbm.at[pl.ds(i_dst, ROW_TILE), :], sem_s.at[slot])

    def wait_gather(slot):
        pltpu.make_async_copy(h_hbm.at[pl.ds(0, buf_rows), :], hsbuf.at[slot], sem_g.at[slot]).wait()

    def wait_scatter(slot):
        pltpu.make_async_copy(ybuf.at[slot], y_hbm.at[pl.ds(0, buf_rows), :], sem_s.at[slot]).wait()

    def for_rows(fn):
        def body(g, carry):
            for k in range(ROW_TILE):
                fn(g * ROW_TILE + k, k % 2)
            return carry
        lax.fori_loop(0, TM_E // ROW_TILE, body, 0)

    def side_traffic(r, queue):
        gather(src1_ref, r, nxt).start(priority=queue)
        scatter(dstp_ref, r, nxt).start(priority=queue)

    @pl.when(i == 0)
    def _():
        ybuf[1] = jnp.zeros((buf_rows, LANE), F32)
        for_rows(lambda r, queue: gather(src0_ref, r, 0).start(priority=queue))

    wait_gather(cur)

    @pl.when(i < nu[0])
    def _():
        for c in range(ROW_TILE):
            hbf[:, c * LANE:(c + 1) * LANE] = hsbuf[cur, pl.ds(c, TM_E, stride=ROW_TILE), :].astype(BF16)
        w_gu = wgu.at[0]
        w_d = wd.at[0]
        _swiglu_block(hbf, w_gu, w_d, acc, 0)
        per_iter = TM_E // SIDE_ITERS

        def body(t, carry):
            _swiglu_block(hbf, w_gu, w_d, acc, 1 + t)
            for k in range(per_iter):
                side_traffic(t * per_iter + k, k % 2)
            return carry

        lax.fori_loop(0, SIDE_ITERS, body, 0)
        for j in range(1 + SIDE_ITERS, N_FF):
            _swiglu_block(hbf, w_gu, w_d, acc, j)
        _store_row_tiles(ybuf.at[cur], lambda cols: acc[:, cols])

    @pl.when(i >= nu[0])
    def _():
        for_rows(side_traffic)
        ybuf[cur] = jnp.zeros((buf_rows, LANE), F32)

    wait_scatter(nxt)

    @pl.when(i == last)
    def _():
        for_rows(lambda r, queue: scatter(dstc_ref, r, cur).start(priority=queue))
        wait_scatter(cur)
        wait_gather(nxt)


def _moe_experts(tile_expert, n_used, src_tbl, dst_tbl, h, wgu, wd):
    smem = lambda off: pl.BlockSpec((1, 1, TM_E), lambda i, te, nu: (i + off, 0, 0), memory_space=pltpu.SMEM)
    grid_spec = pltpu.PrefetchScalarGridSpec(
        num_scalar_prefetch=2,
        grid=(N_TILES,),
        in_specs=[smem(0), smem(1), smem(0), smem(1),
                  pl.BlockSpec(memory_space=pl.ANY),
                  pl.BlockSpec((1,) + wgu.shape[1:], lambda i, te, nu: (te[i], 0, 0)),
                  pl.BlockSpec((1,) + wd.shape[1:], lambda i, te, nu: (te[i], 0, 0))],
        out_specs=pl.BlockSpec(memory_space=pl.ANY),
        scratch_shapes=[pltpu.VMEM((2, TM_E * ROW_TILE, LANE), F32), pltpu.VMEM((2, TM_E * ROW_TILE, LANE), F32),
                        pltpu.VMEM((TM_E, D), BF16), pltpu.VMEM((TM_E, D), F32),
                        pltpu.SemaphoreType.DMA((2,)), pltpu.SemaphoreType.DMA((2,))],
    )
    return pl.pallas_call(
        _moe_kernel,
        grid_spec=grid_spec,
        out_shape=jax.ShapeDtypeStruct(((2 * N + DUMP_ROWS) * ROW_TILE, LANE), F32),
        compiler_params=_cparams(("arbitrary",)),
        name="moe_experts",
    )(tile_expert, n_used, src_tbl, src_tbl, dst_tbl, dst_tbl, h, wgu, wd)


def _combine_kernel(y1_ref, y2_ref, x_ref, w1_ref, w2_ref, gate2, g3, op_ref, os_ref, *, prompt_steps):
    rows = x_ref.shape[0]
    w1 = w1_ref[...]
    w2 = w2_ref[...]
    f = jnp.concatenate([w1 * _load_row_tiles(y1_ref, c, rows) + w2 * _load_row_tiles(y2_ref, c, rows)
                         for c in range(ROW_TILE)], axis=1)
    out = _gated(x_ref[...], f, g3[...], gate2[0])
    step = pl.program_id(0)

    @pl.when(step < prompt_steps)
    def _():
        op_ref[...] = out

    @pl.when(step >= prompt_steps)
    def _():
        os_ref[...] = out


def _moe_combine(y, x, w1, w2, mods, g3):
    tm = 512
    grp = _grp_bm(tm)
    nb = N // tm
    n_p = NP // tm
    return pl.pallas_call(
        functools.partial(_combine_kernel, prompt_steps=n_p),
        grid=(nb,),
        in_specs=[pl.BlockSpec((tm * ROW_TILE, LANE), lambda i: (i, 0)),
                  pl.BlockSpec((tm * ROW_TILE, LANE), lambda i: (nb + i, 0)),
                  pl.BlockSpec((tm, D), lambda i: (i, 0)),
                  pl.BlockSpec((tm, 1), lambda i: (i, 0)), pl.BlockSpec((tm, 1), lambda i: (i, 0)),
                  _mod_spec(5, grp), _full((1, D))],
        out_specs=[pl.BlockSpec((tm, D), lambda i: (jnp.minimum(i, n_p - 1), 0)),
                   pl.BlockSpec((tm, D), lambda i: (jnp.maximum(i - n_p, 0), 0))],
        out_shape=[jax.ShapeDtypeStruct((NP, D), F32), jax.ShapeDtypeStruct((NS, D), F32)],
        compiler_params=_cparams(("arbitrary",)),
        name="moe_combine",
    )(y, y, x, w1, w2, mods, g3)


def _block_diag(w):
    hh, a, b = w.shape
    eye = jnp.eye(hh, dtype=w.dtype)
    return jnp.einsum('hab,hk->hakb', w, eye).reshape(hh * a, hh * b)


def _s5_matrices(a_re, a_im, log_dt, b_re, b_im, c_re, c_im):
    dt = jnp.exp(log_dt)[:, None]
    mag = jnp.exp(a_re * dt)
    abr = mag * jnp.cos(a_im * dt)
    abi = mag * jnp.sin(a_im * dt)
    den = a_re * a_re + a_im * a_im
    cr = ((abr - 1.0) * a_re + abi * a_im) / den
    ci = (abi * a_re - (abr - 1.0) * a_im) / den
    bbr = cr[..., None] * b_re - ci[..., None] * b_im
    bbi = cr[..., None] * b_im + ci[..., None] * b_re
    hg = S5_G // 2
    eye = jnp.eye(hg, dtype=F32)
    bms, cms = [], []
    for j in range(2):
        sl = slice(j * hg, (j + 1) * hg)
        bd = lambda m: jnp.einsum('gpc,gh->gchp', m[sl], eye).reshape(hg * S5_CH, hg * S5_P)
        bms.append(jnp.concatenate([bd(bbr), bd(bbi)], axis=1))
        cd = lambda m: jnp.einsum('gcp,gh->gphc', m[sl], eye).reshape(hg * S5_P, hg * S5_CH)
        cms.append(jnp.concatenate([cd(c_re), cd(-c_im)], axis=0))
    return (jnp.stack(bms).astype(BF16), abr.reshape(1, S5_N), abi.reshape(1, S5_N),
            jnp.stack(cms).astype(BF16))


def _rope_tables(tm):
    rows = DEC_SEQ // GRID_W
    row = jnp.repeat(jnp.arange(rows, dtype=F32), GRID_W)
    col = jnp.tile(jnp.arange(GRID_W, dtype=F32), rows)
    nf = QK_ROPE // 4
    inv = ROPE_THETA ** (-jnp.arange(nf, dtype=F32) / nf)
    ang = jnp.concatenate([row[:, None] * inv, col[:, None] * inv], axis=-1)
    cos = jnp.repeat(jnp.cos(ang), 2, axis=-1)
    sin = jnp.stack([-jnp.sin(ang), jnp.sin(ang)], axis=-1).reshape(DEC_SEQ, QK_ROPE)
    ident = lambda t, one: jnp.concatenate([t, jnp.full((tm, t.shape[1]), one, F32)], axis=0)
    cos_q = ident(jnp.tile(cos, (1, HEADS)), 1.0)
    sin_q = ident(jnp.tile(sin, (1, HEADS)), 0.0)
    pad = lambda t, one: jnp.concatenate([t, jnp.full((DEC_SEQ, LANE - QK_ROPE), one, F32)], axis=1)
    return cos_q, sin_q, ident(pad(cos, 1.0), 1.0), ident(pad(sin, 0.0), 0.0)


def _group_states(prompt_state, sample_state):
    w = sample_state.shape[-1]
    return jnp.concatenate([prompt_state.reshape(2, SUB, w), sample_state.reshape(1, SUB, w)], axis=0)


def _layer_ab(xp, xs, m, ng, j, state_lru, state_s5_re, state_s5_im, p):
    xz, x = _ab_inproj(xp, xs, ng[0:1], m, p['ab_w_in'][j].astype(BF16))
    zeros = lambda w: jnp.zeros((BATCH, w), F32)
    outs = []
    for d in range(2):
        wg = jnp.concatenate([_block_diag(p['lru_wa'][j, d]), _block_diag(p['lru_wx'][j, d])], axis=1).astype(BF16)
        bg = jnp.concatenate([p['lru_ba'][j, d], p['lru_bx'][j, d]])[None]
        bm, ar, ai, cm = _s5_matrices(p['s5_a_re'][j, d], p['s5_a_im'][j, d], p['s5_log_dt'][j, d],
                                      p['s5_b_re'][j, d], p['s5_b_im'][j, d], p['s5_c_re'][j, d], p['s5_c_im'][j, d])
        h0l = _group_states(zeros(LRU_W), state_lru[:, j, d])
        h0r = _group_states(zeros(S5_N), state_s5_re[:, j, d].reshape(DEC_BATCH, S5_N))
        h0i = _group_states(zeros(S5_N), state_s5_im[:, j, d].reshape(DEC_BATCH, S5_N))
        outs.append(_ab_scan(xz, h0l, h0r, h0i, p['ab_conv_w'][j], p['ab_conv_b'][j][None], wg, bg,
                             p['lru_lambda'][j, d][None], bm, ar, ai, cm, reverse=(d == 1)))
    (haf, yf, llf, lrf, lif), (hab, yr, llb, lrb, lib) = outs
    x = _ab_out(haf, hab, yf, yr, xz, x, m, ng[1:2], p['s5_d'][j][None], p['s5_w_glu'][j].astype(BF16),
                p['s5_b_glu'][j][None], p['ab_w_out'][j].astype(BF16))
    streams = _ffn(x, m, ng[2:3], ng[3:4], p['ffn_w_gate_up'][j].astype(BF16), p['ffn_w_down'][j].astype(BF16))
    prompt = lambda f, b, w: jnp.stack([f[:2].reshape(BATCH, w), b[:2].reshape(BATCH, w)], axis=1)
    lru = prompt(llf, llb, LRU_W)
    s5r = prompt(lrf, lrb, S5_N).reshape(BATCH, 2, S5_G, S5_P)
    s5i = prompt(lif, lib, S5_N).reshape(BATCH, 2, S5_G, S5_P)
    return tuple(streams), lru, s5r, s5i


def _head_major(w, parts):
    k = w.shape[0]
    per_head = w.reshape(k, HEADS, -1)
    out, start = [], 0
    for width in parts:
        out.append(per_head[:, :, start:start + width].reshape(k, HEADS * width))
        start += width
    return jnp.concatenate(out, axis=1)


def _layer_mla_moe(xp, xs, m, ng, j, cache_kv_latent, cache_k_rope, p):
    w1 = jnp.concatenate([p['mla_w_in'][j], jnp.zeros((D, LANE - QK_ROPE), F32)], axis=1).astype(BF16)
    wuq = _head_major(p['mla_w_uq'][j], (QK_NOPE, QK_ROPE)).astype(BF16)
    wukv = _head_major(p['mla_w_ukv'][j], (QK_NOPE, V_DIM)).astype(BF16)
    tables = _rope_tables(MLA_TM)
    qn, qr, kn, v, kr2, ckv, krr = _mla_proj(xp, xs, ng[0:1], m, w1, p['mla_g_q'][j][None], p['mla_g_kv'][j][None],
                                              wuq, wukv, *tables)
    knc, vc = _cache_kv(cache_kv_latent[:, j].reshape(DEC_BATCH * PAST_LEN, KV_LORA), wukv)
    krc = cache_k_rope[:, j].reshape(DEC_BATCH * PAST_LEN, QK_ROPE)
    z = jnp.zeros_like(krc)
    kr2c = jnp.concatenate([krc, z, z, krc], axis=1).astype(BF16)
    o_p = _attention(qn, qr, kn, kr2, v, row0=0, n_seq=BATCH, seq=SEQ, tq=SEQ)
    o_s = _attention(qn, qr, kn, kr2, v, row0=NP, n_seq=DEC_BATCH, seq=DEC_SEQ, tq=512, cache=(knc, kr2c, vc))
    wr_t = p['moe_w_router'][j].T
    wr_hi = wr_t.astype(BF16)
    wr_lo = (wr_t - wr_hi.astype(F32)).astype(BF16)
    x3, h, ri, rf, cnt = _attn_out_router(o_p, o_s, xp, xs, m, ng[1:2], ng[2:3], p['mla_w_out'][j].astype(BF16), wr_hi, wr_lo)
    counts = cnt[:, 0].astype(I32)
    padded = ((counts + TM_E - 1) // TM_E) * TM_E
    ends = jnp.cumsum(padded)
    offs = ends - padded
    pos1 = offs[ri[0]] + ri[2]
    pos2 = offs[ri[1]] + ri[3]
    pick_tok = jnp.arange(2 * N, dtype=I32)
    dest = jnp.full((P_ROWS,), -1, I32).at[jnp.concatenate([pos1, pos2])].set(pick_tok, unique_indices=True)
    is_pad = dest < 0
    pad_row = 2 * N + TM_E + jnp.cumsum(is_pad.astype(I32)) - 1
    src_tbl = jnp.concatenate([jnp.where(is_pad, 0, dest % N), jnp.zeros((TM_E,), I32)])
    dst_tbl = jnp.concatenate([2 * N + jnp.arange(TM_E, dtype=I32), jnp.where(is_pad, pad_row, dest)])
    n_used = (ends[-1] // TM_E).astype(I32)[None]
    tile_row = jnp.minimum(jnp.arange(N_TILES, dtype=I32), n_used - 1) * TM_E
    tile_expert = jnp.sum((tile_row[:, None] >= ends[None, :]).astype(I32), axis=1)
    y = _moe_experts(tile_expert, n_used, src_tbl.reshape(N_TILES + 1, 1, TM_E), dst_tbl.reshape(N_TILES + 1, 1, TM_E),
                     h, p['moe_w_gate_up'][j].astype(BF16), p['moe_w_down'][j].astype(BF16))
    xp, xs = _moe_combine(y, x3, rf[0][:, None], rf[1][:, None], m, ng[3:4])
    kv_new = ckv.reshape(BATCH, SEQ, KV_LORA)
    kr_new = krr[:, :QK_ROPE].reshape(BATCH, SEQ, QK_ROPE)
    return (xp.reshape(BATCH, SEQ, D), xs.reshape(DEC_BATCH, DEC_SEQ, D)), kv_new, kr_new


def kernel(x_prompt, x_sample, c, state_lru, state_s5_re, state_s5_im, cache_kv_latent, cache_k_rope, c_ctx, w_mod, b_mod, norm_gains, ab_w_in, ab_conv_w, ab_conv_b, lru_wa, lru_ba, lru_wx, lru_bx, lru_lambda, s5_a_re, s5_a_im, s5_log_dt, s5_b_re, s5_b_im, s5_c_re, s5_c_im, s5_d, s5_w_glu, s5_b_glu, ab_w_out, ffn_w_gate_up, ffn_w_down, mla_w_in, mla_g_q, mla_g_kv, mla_w_uq, mla_w_ukv, mla_w_out, moe_w_router, moe_w_gate_up, moe_w_down):
    p = dict(ab_w_in=ab_w_in, ab_conv_w=ab_conv_w, ab_conv_b=ab_conv_b, lru_wa=lru_wa, lru_ba=lru_ba,
             lru_wx=lru_wx, lru_bx=lru_bx, lru_lambda=lru_lambda, s5_a_re=s5_a_re, s5_a_im=s5_a_im,
             s5_log_dt=s5_log_dt, s5_b_re=s5_b_re, s5_b_im=s5_b_im, s5_c_re=s5_c_re, s5_c_im=s5_c_im,
             s5_d=s5_d, s5_w_glu=s5_w_glu, s5_b_glu=s5_b_glu, ab_w_out=ab_w_out, ffn_w_gate_up=ffn_w_gate_up,
             ffn_w_down=ffn_w_down, mla_w_in=mla_w_in, mla_g_q=mla_g_q, mla_g_kv=mla_g_kv, mla_w_uq=mla_w_uq,
             mla_w_ukv=mla_w_ukv, mla_w_out=mla_w_out, moe_w_router=moe_w_router, moe_w_gate_up=moe_w_gate_up,
             moe_w_down=moe_w_down)
    depth = w_mod.shape[0]
    cond = jnp.concatenate([c_ctx[None], c, jnp.zeros((2 * SUB - 1 - DEC_BATCH, D), F32)], axis=0)
    mod = _modulation(cond, w_mod, b_mod)
    ctx_tile = lambda l: jnp.broadcast_to(mod[l, 0:1], (SUB, 6 * D))
    streams = (x_prompt, x_sample)
    lru_l, s5r_l, s5i_l, kv_l, kr_l = [], [], [], [], []
    for layer in range(depth):
        j = layer // 2
        ng = norm_gains[layer]
        if layer % 2 == 0:
            m = jnp.stack([ctx_tile(layer), mod[layer, 1:1 + DEC_BATCH]])
            streams, lru, s5r, s5i = _layer_ab(*streams, m, ng, j, state_lru, state_s5_re, state_s5_im, p)
            lru_l.append(lru)
            s5r_l.append(s5r)
            s5i_l.append(s5i)
        else:
            lat = jnp.broadcast_to(mod[layer, 1:1 + DEC_BATCH, None, :], (DEC_BATCH, SUB, 6 * D))
            m = jnp.concatenate([ctx_tile(layer)[None], lat], axis=0)
            streams, kv_new, kr_new = _layer_mla_moe(streams[0].reshape(NP, D), streams[1].reshape(NS, D), m, ng, j,
                                                     cache_kv_latent, cache_k_rope, p)
            kv_l.append(kv_new)
            kr_l.append(kr_new)
    return (streams[0], streams[1],
            jnp.stack(lru_l, axis=1), jnp.stack(s5r_l, axis=1), jnp.stack(s5i_l, axis=1),
            jnp.stack(kv_l, axis=1), jnp.stack(kr_l, axis=1))
```

```python
import functools
import math

import numpy as np
import jax
import jax.numpy as jnp
from jax import lax
from jax.experimental import pallas as pl
from jax.experimental.pallas import tpu as pltpu

F32 = jnp.float32
BF16 = jnp.bfloat16
I32 = jnp.int32

D = 1024
BATCH, SEQ = 16, 256
DEC_BATCH, DEC_SEQ = 8, 2048
PAST_LEN = 256
GRID_W = 64
LRU_W = 512
LRU_HEADS = 8
LRU_C = 8.0
CONV_W = 4
S5_W = 512
S5_CH = 16
S5_G = 32
S5_P = 64
S5_N = S5_G * S5_P
HEADS = 8
QK_NOPE, QK_ROPE, V_DIM = 128, 64, 128
Q_LORA, KV_LORA = 384, 256
ROPE_THETA = 10000.0
D_FF = 2816
N_EXP = 8
EPS = 1e-6

NP = BATCH * SEQ
NS = DEC_BATCH * DEC_SEQ
N = NP + NS
SUB = 8
LANE = 128
ROW_TILE = D // LANE
T_CHUNK = 32
R_CHUNK = T_CHUNK * SUB
FF_BLK = 256
KEY_BLK = 256
HEAD_GRP = 4
MLA_TM = 512
MLA_SUB = 2
ROUTER_SUB = 1
ABOUT_SUB = 1
INPROJ_T = 64
FFN_T = 128
SIDE_ITERS = 8
N_FF = D_FF // FF_BLK
TM_E = 512
P_ROWS = 2 * N + N_EXP * TM_E
N_TILES = P_ROWS // TM_E
DUMP_ROWS = P_ROWS - 2 * N + TM_E
VMEM_LIMIT = 56 * 1024 * 1024

NT_DIMS = (((1,), (1,)), ((), ()))


def _cparams(sem):
    return pltpu.CompilerParams(dimension_semantics=sem, vmem_limit_bytes=VMEM_LIMIT)


def _dot(a, b):
    return jnp.dot(a, b, preferred_element_type=F32)


def _sigmoid(x):
    return 1.0 / (1.0 + jnp.exp(-x))


def _neg_expm1_2x(log_a, a):
    z = 2.0 * log_a
    series = -z * (1.0 + z * (1.0 / 2.0) * (1.0 + z * (1.0 / 3.0) * (1.0 + z * (1.0 / 4.0))))
    return jnp.where(z > -0.02, series, (1.0 - a) * (1.0 + a))


def _gelu(x):
    return x * (0.5 * (1.0 + jnp.tanh(math.sqrt(2.0 / math.pi) * (x + 0.044715 * (x * x * x)))))


def _rms(x, g):
    ms = jnp.mean(x * x, axis=-1, keepdims=True)
    return x * lax.rsqrt(ms + EPS) * g


def _rows8(y, fn):
    r, c = y.shape
    return fn(y.reshape(r // SUB, SUB, c)).reshape(r, c)


def _adaln(x, g, scale, shift):
    return _rows8(_rms(x, g), lambda y: y * (1.0 + scale)[None] + shift[None])


def _gated(x, y, g, gate):
    return x + _rows8(_rms(y, g), lambda z: z * gate[None])


def _store_row_tiles(ref, piece, row0=0, rows=None):
    rows = ref.shape[0] // ROW_TILE if rows is None else rows
    for c in range(ROW_TILE):
        ref[pl.ds(row0 * ROW_TILE + c, rows, stride=ROW_TILE), :] = piece(slice(c * LANE, (c + 1) * LANE))


def _load_row_tiles(ref, c, rows):
    return ref[pl.ds(c, rows, stride=ROW_TILE), :]


def _full(shape):
    nd = len(shape)
    return pl.BlockSpec(shape, lambda *_: (0,) * nd)


def _mod_spec(k, grp):
    return pl.BlockSpec((1, SUB, D), lambda i, *_: (grp(i), 0, k))


def _grp_tm(tm):
    return lambda i: (i * tm >= NP).astype(I32)


def _grp_bm(tm):
    return lambda i: jnp.where(i * tm < NP, 0, 1 + (i * tm - NP) // DEC_SEQ)


def _mod_kernel(c_ref, w_ref, b_ref, o_ref):
    c = c_ref[...]
    s = c * _sigmoid(c)
    o_ref[0] = _dot(s.astype(BF16), w_ref[0].astype(BF16)) + b_ref[0]


def _modulation(cond, w_mod, b_mod):
    depth = w_mod.shape[0]
    rows = cond.shape[0]
    return pl.pallas_call(
        _mod_kernel,
        grid=(depth, 6),
        in_specs=[_full((rows, D)),
                  pl.BlockSpec((1, D, D), lambda l, j: (l, 0, j)),
                  pl.BlockSpec((1, 1, D), lambda l, j: (l, 0, j))],
        out_specs=pl.BlockSpec((1, rows, D), lambda l, j: (l, 0, j)),
        out_shape=jax.ShapeDtypeStruct((depth, rows, 6 * D), F32),
        compiler_params=_cparams(("arbitrary", "arbitrary")),
        name="modulation",
    )(cond, w_mod, b_mod.reshape(depth, 1, 6 * D))


def _time_major_copies(step, t_steps, xp_hbm, xs_hbm, buf, sem, to_hbm):
    p_steps = (BATCH // SUB) * (SEQ // t_steps)
    per_group = SEQ // t_steps

    def issue(hbm, seq0, t0):
        t0 = pl.multiple_of(t0, t_steps)
        for b in range(SUB):
            rows = hbm.at[seq0 + b, pl.ds(t0, t_steps), :]
            tile = buf.at[:, b, :]
            (pltpu.make_async_copy(tile, rows, sem) if to_hbm else pltpu.make_async_copy(rows, tile, sem)).start()

    @pl.when(step < p_steps)
    def _():
        issue(xp_hbm, (step // per_group) * SUB, (step % per_group) * t_steps)

    @pl.when(step >= p_steps)
    def _():
        issue(xs_hbm, 0, (step - p_steps) * t_steps)


def _time_major_wait(t_steps, xs_hbm, buf, sem, to_hbm):
    for b in range(SUB):
        rows = xs_hbm.at[0, pl.ds(0, t_steps), :]
        tile = buf.at[:, b, :]
        (pltpu.make_async_copy(tile, rows, sem) if to_hbm else pltpu.make_async_copy(rows, tile, sem)).wait()


def _inproj_kernel(xp_hbm, xs_hbm, g_ref, sh_ref, sc_ref, w_ref, o_ref, xtm_ref, buf, sem):
    step = pl.program_id(0)
    slot = step % 2

    @pl.when(step == 0)
    def _():
        _time_major_copies(step, INPROJ_T, xp_hbm, xs_hbm, buf.at[0], sem.at[0], False)

    @pl.when(step + 1 < pl.num_programs(0))
    def _():
        _time_major_copies(step + 1, INPROJ_T, xp_hbm, xs_hbm, buf.at[1 - slot], sem.at[1 - slot], False)

    _time_major_wait(INPROJ_T, xs_hbm, buf.at[slot], sem.at[slot], False)
    x = buf[slot].reshape(INPROJ_T * SUB, D)
    xtm_ref[...] = x
    h = _adaln(x, g_ref[...], sc_ref[0], sh_ref[0])
    o_ref[...] = _dot(h.astype(BF16), w_ref[...])


def _ab_inproj(xp, xs, gain, mods, w_in):
    tm = INPROJ_T * SUB
    nout = w_in.shape[1]
    grp = _grp_tm(tm)
    return pl.pallas_call(
        _inproj_kernel,
        grid=(N // tm,),
        in_specs=[pl.BlockSpec(memory_space=pl.ANY), pl.BlockSpec(memory_space=pl.ANY),
                  _full((1, D)),
                  _mod_spec(0, grp), _mod_spec(1, grp),
                  _full((D, nout))],
        out_specs=[pl.BlockSpec((tm, nout), lambda i: (i, 0)), pl.BlockSpec((tm, D), lambda i: (i, 0))],
        out_shape=[jax.ShapeDtypeStruct((N, nout), F32), jax.ShapeDtypeStruct((N, D), F32)],
        scratch_shapes=[pltpu.VMEM((2, INPROJ_T, SUB, D), F32), pltpu.SemaphoreType.DMA((2,))],
        compiler_params=_cparams(("arbitrary",)),
        name="ab_inproj",
    )(xp, xs, gain, mods, mods, w_in)


def _scan_table(reverse):
    cols = []
    groups = [(0, SEQ // T_CHUNK, 0), (1, SEQ // T_CHUNK, SEQ // T_CHUNK),
              (2, DEC_SEQ // T_CHUNK, NP // R_CHUNK)]
    for g, nc, base in groups:
        order = range(nc - 1, -1, -1) if reverse else range(nc)
        for k, c in enumerate(order):
            cols.append((base + c, g, int(k == 0), int(c > 0), int(c < nc - 1)))
    return np.asarray(cols, np.int32).T.copy()


N_SCAN_IN, N_SCAN_OUT, N_SCAN_SCRATCH = 16, 5, 8
SCAN_TBL_ROWS = 5


def _scan_kernel(tbl, *refs):
    ins = [refs[d * N_SCAN_IN:(d + 1) * N_SCAN_IN] for d in range(2)]
    o0 = 2 * N_SCAN_IN
    outs = [refs[o0 + d * N_SCAN_OUT:o0 + (d + 1) * N_SCAN_OUT] for d in range(2)]
    s0 = o0 + 2 * N_SCAN_OUT
    scr = [refs[s0 + d * N_SCAN_SCRATCH:s0 + (d + 1) * N_SCAN_SCRATCH] for d in range(2)]
    s = pl.program_id(0)
    for d in range(2):
        h0l_ref, h0r_ref, h0i_ref = ins[d][4:7]
        hl, sre, sim = scr[d][5:8]

        @pl.when(tbl[d * SCAN_TBL_ROWS + 2, s] == 1)
        def _(h0l_ref=h0l_ref, h0r_ref=h0r_ref, h0i_ref=h0i_ref, hl=hl, sre=sre, sim=sim):
            hl[...] = h0l_ref[0]
            sre[...] = h0r_ref[0]
            sim[...] = h0i_ref[0]

    fwd, bwd = [_ScanChunk(tbl, d * SCAN_TBL_ROWS, ins[d], outs[d], scr[d], reverse=(d == 1)) for d in range(2)]
    fwd.s5_project()
    fwd.conv()
    fwd.gates()
    bwd.s5_project()
    fwd.s5_recurrence()
    bwd.conv()
    fwd.s5_readout()
    bwd.gates()
    fwd.lru()
    bwd.s5_recurrence()
    bwd.s5_readout()
    bwd.lru()


class _ScanChunk:
    def __init__(self, tbl, row0, ins, outs, scratch, reverse):
        (self.xa_ref, self.xp_ref, self.xn_ref, self.xb_ref, _, _, _, self.cw_ref, self.cb_ref, self.wg_ref,
         self.bg_ref, self.lam_ref, self.bm_ref, self.ar_ref, self.ai_ref, self.cm_ref) = ins
        self.ha_ref, self.y_ref, self.ll_ref, self.lr_ref, self.li_ref = outs
        self.ext, self.abuf, self.bbuf, self.hre, self.him, self.hl, self.sre, self.sim = scratch
        step = pl.program_id(0)
        self.has_prev = tbl[row0 + 3, step] == 1
        self.has_next = tbl[row0 + 4, step] == 1
        self.order = range(T_CHUNK - 1, -1, -1) if reverse else range(T_CHUNK)
        self.half = S5_N // 2

    def s5_project(self):
        ub = self.xb_ref[...].astype(BF16)
        half = self.half
        for j in range(2):
            bu = _dot(ub[:, j * 256:(j + 1) * 256], self.bm_ref[j])
            self.hre[:, j * half:(j + 1) * half] = bu[:, :half]
            self.him[:, j * half:(j + 1) * half] = bu[:, half:]

    def conv(self):
        ext = self.ext
        ext[0:2 * SUB] = jnp.where(self.has_prev, self.xp_ref[...], 0.0)
        ext[2 * SUB:2 * SUB + R_CHUNK] = self.xa_ref[...]
        ext[2 * SUB + R_CHUNK:3 * SUB + R_CHUNK] = jnp.where(self.has_next, self.xn_ref[...], 0.0)
        xa = self.cb_ref[...] + self.cw_ref[0:1] * ext[0:R_CHUNK]
        for k in range(1, CONV_W):
            xa = xa + self.cw_ref[k:k + 1] * ext[k * SUB:k * SUB + R_CHUNK]
        self.xa = xa

    def gates(self):
        self.gz = _dot(self.xa.astype(BF16), self.wg_ref[...]) + self.bg_ref[...]

    def s5_recurrence(self):
        cblk = 4 * LANE
        for cb in range(S5_N // cblk):
            cols = slice(cb * cblk, (cb + 1) * cblk)
            ar = jnp.broadcast_to(self.ar_ref[:, cols], (SUB, cblk))
            ai = jnp.broadcast_to(self.ai_ref[:, cols], (SUB, cblk))
            hr = self.sre[:, cols]
            hi = self.sim[:, cols]
            for t in self.order:
                rows = slice(t * SUB, (t + 1) * SUB)
                nr = ar * hr - ai * hi + self.hre[rows, cols]
                ni = ar * hi + ai * hr + self.him[rows, cols]
                hr, hi = nr, ni
                self.hre[rows, cols] = hr
                self.him[rows, cols] = hi
            self.sre[:, cols] = hr
            self.sim[:, cols] = hi
        self.lr_ref[0] = self.sre[...]
        self.li_ref[0] = self.sim[...]

    def s5_readout(self):
        half = self.half
        for j in range(2):
            hc = jnp.concatenate([self.hre[:, j * half:(j + 1) * half], self.him[:, j * half:(j + 1) * half]],
                                 axis=1).astype(BF16)
            self.y_ref[:, j * 256:(j + 1) * 256] = _dot(hc, self.cm_ref[j])

    def lru(self):
        r = _sigmoid(self.gz[:, :LRU_W])
        i = _sigmoid(self.gz[:, LRU_W:])
        lam = self.lam_ref[...]
        log_sig = jnp.minimum(lam, 0.0) - jnp.log1p(jnp.exp(-jnp.abs(lam)))
        log_a = LRU_C * r * log_sig
        a = jnp.exp(log_a)
        self.abuf[...] = a
        self.bbuf[...] = jnp.sqrt(_neg_expm1_2x(log_a, a)) * (i * self.xa)
        h = self.hl[...]
        for t in self.order:
            rows = slice(t * SUB, (t + 1) * SUB)
            h = self.abuf[rows] * h + self.bbuf[rows]
            self.ha_ref[rows, :] = h
        self.hl[...] = h
        self.ll_ref[0] = h


def _ab_scan(xz, per_dir):
    tbl = jnp.asarray(np.concatenate([_scan_table(False), _scan_table(True)], axis=0))
    n_steps = tbl.shape[1]
    in_specs, out_specs, scratch, out_shape, args = [], [], [], [], []
    for d in range(2):
        blk = lambda s, t, d=d: t[d * SCAN_TBL_ROWS, s]
        grp = lambda s, t, d=d: t[d * SCAN_TBL_ROWS + 1, s]
        state_spec = lambda w, grp=grp: pl.BlockSpec((1, SUB, w), lambda s, t: (grp(s, t), 0, 0))
        const = lambda shape: pl.BlockSpec(shape, lambda s, t: (0,) * len(shape))
        in_specs += [
            pl.BlockSpec((R_CHUNK, LRU_W), lambda s, t, blk=blk: (blk(s, t), 0)),
            pl.BlockSpec((2 * SUB, LRU_W),
                         lambda s, t, blk=blk: (jnp.maximum(blk(s, t) * (T_CHUNK // 2) - 1, 0), 0)),
            pl.BlockSpec((SUB, LRU_W),
                         lambda s, t, blk=blk: (jnp.minimum((blk(s, t) + 1) * T_CHUNK, N // SUB - 1), 0)),
            pl.BlockSpec((R_CHUNK, S5_W), lambda s, t, blk=blk: (blk(s, t), 2)),
            state_spec(LRU_W), state_spec(S5_N), state_spec(S5_N),
            const((CONV_W, LRU_W)), const((1, LRU_W)),
            const((LRU_W, 2 * LRU_W)), const((1, 2 * LRU_W)), const((1, LRU_W)),
            const((2, 256, S5_N)), const((1, S5_N)), const((1, S5_N)), const((2, S5_N, 256)),
        ]
        out_specs += [
            pl.BlockSpec((R_CHUNK, LRU_W), lambda s, t, blk=blk: (blk(s, t), 0)),
            pl.BlockSpec((R_CHUNK, S5_W), lambda s, t, blk=blk: (blk(s, t), 0)),
            state_spec(LRU_W), state_spec(S5_N), state_spec(S5_N),
        ]
        scratch += [
            pltpu.VMEM((R_CHUNK + 3 * SUB, LRU_W), F32),
            pltpu.VMEM((R_CHUNK, LRU_W), F32), pltpu.VMEM((R_CHUNK, LRU_W), F32),
            pltpu.VMEM((R_CHUNK, S5_N), F32), pltpu.VMEM((R_CHUNK, S5_N), F32),
            pltpu.VMEM((SUB, LRU_W), F32), pltpu.VMEM((SUB, S5_N), F32), pltpu.VMEM((SUB, S5_N), F32),
        ]
        out_shape += [jax.ShapeDtypeStruct((N, LRU_W), F32), jax.ShapeDtypeStruct((N, S5_W), F32),
                      jax.ShapeDtypeStruct((3, SUB, LRU_W), F32),
                      jax.ShapeDtypeStruct((3, SUB, S5_N), F32), jax.ShapeDtypeStruct((3, SUB, S5_N), F32)]
        args += [xz, xz, xz, xz, *per_dir[d]]
    grid_spec = pltpu.PrefetchScalarGridSpec(num_scalar_prefetch=1, grid=(n_steps,), in_specs=in_specs,
                                             out_specs=out_specs, scratch_shapes=scratch)
    outs = pl.pallas_call(
        _scan_kernel,
        grid_spec=grid_spec,
        out_shape=out_shape,
        compiler_params=_cparams(("arbitrary",)),
        name="ab_scan",
    )(tbl, *args)
    return outs[:N_SCAN_OUT], outs[N_SCAN_OUT:]


def _about_kernel(haf, hab, yf, yr, ga, xb, x_ref, gate, g1, d_ref, wglu, bglu, wout, o_ref):
    sub = x_ref.shape[0] // ABOUT_SUB
    for r in range(ABOUT_SUB):
        rows = slice(r * sub, (r + 1) * sub)
        ya = (haf[rows, :] + hab[rows, :]) * _gelu(ga[rows, :])
        yb0 = _gelu(yf[rows, :] + yr[rows, :] + d_ref[...] * xb[rows, :])
        yb = yb0 * _sigmoid(_dot(yb0.astype(BF16), wglu[...]) + bglu[...])
        out = _dot(ya.astype(BF16), wout[0:LRU_W]) + _dot(yb.astype(BF16), wout[LRU_W:LRU_W + S5_W])
        o_ref[rows, :] = _gated(x_ref[rows, :], out, g1[...], gate[0])


def _ab_out(haf, hab, yf, yr, xz, x, mods, g1, s5_d, wglu, bglu, wout):
    tm = 512
    grp = _grp_tm(tm)
    half = lambda c: pl.BlockSpec((tm, LRU_W), lambda i: (i, c))
    return pl.pallas_call(
        _about_kernel,
        grid=(N // tm,),
        in_specs=[half(0), half(0), half(0), half(0), half(1), half(2),
                  pl.BlockSpec((tm, D), lambda i: (i, 0)),
                  _mod_spec(2, grp), _full((1, D)), _full((1, S5_W)),
                  _full((S5_W, S5_W)), _full((1, S5_W)), _full((LRU_W + S5_W, D))],
        out_specs=pl.BlockSpec((tm, D), lambda i: (i, 0)),
        out_shape=jax.ShapeDtypeStruct((N, D), F32),
        compiler_params=_cparams(("arbitrary",)),
        name="ab_out",
    )(haf, hab, yf, yr, xz, xz, x, mods, g1, s5_d, wglu, bglu, wout)


def _swiglu_block(hbf, wgu, wd, acc, j):
    static = isinstance(j, int)
    blk = lambda start: pl.ds(start if static else pl.multiple_of(start, FF_BLK), FF_BLK)
    h = hbf[...]
    g = _dot(h, wgu[:, blk(j * FF_BLK)])
    u = _dot(h, wgu[:, blk(D_FF + j * FF_BLK)])
    act = (g * _sigmoid(g)) * u
    part = _dot(act.astype(BF16), wd[blk(j * FF_BLK), :])
    if static and j == 0:
        acc[...] = part
    else:
        acc[...] += part


def _ffn_kernel(x_ref, sh, sc, gt, g2, g3, wgu, wd, op_hbm, os_hbm, hbf, acc, obuf, sem):
    step = pl.program_id(0)
    slot = step % 2
    hbf[...] = _adaln(x_ref[...], g2[...], sc[0], sh[0]).astype(BF16)
    for j in range(N_FF):
        _swiglu_block(hbf, wgu, wd, acc, j)
    obuf[slot] = _gated(x_ref[...], acc[...], g3[...], gt[0]).reshape(FFN_T, SUB, D)
    _time_major_copies(step, FFN_T, op_hbm, os_hbm, obuf.at[slot], sem.at[slot], True)

    @pl.when(step > 0)
    def _():
        _time_major_wait(FFN_T, os_hbm, obuf.at[1 - slot], sem.at[1 - slot], True)

    @pl.when(step == pl.num_programs(0) - 1)
    def _():
        _time_major_wait(FFN_T, os_hbm, obuf.at[slot], sem.at[slot], True)


def _ffn(x, mods, g2, g3, wgu, wd):
    tm = FFN_T * SUB
    grp = _grp_tm(tm)
    once = lambda shape: pl.BlockSpec(shape, lambda i: (0,) * len(shape), pipeline_mode=pl.Buffered(1))
    return pl.pallas_call(
        _ffn_kernel,
        grid=(N // tm,),
        in_specs=[pl.BlockSpec((tm, D), lambda i: (i, 0)),
                  _mod_spec(3, grp), _mod_spec(4, grp), _mod_spec(5, grp),
                  _full((1, D)), _full((1, D)),
                  once(wgu.shape), once(wd.shape)],
        out_specs=[pl.BlockSpec(memory_space=pl.ANY), pl.BlockSpec(memory_space=pl.ANY)],
        out_shape=[jax.ShapeDtypeStruct((BATCH, SEQ, D), F32), jax.ShapeDtypeStruct((DEC_BATCH, DEC_SEQ, D), F32)],
        scratch_shapes=[pltpu.VMEM((tm, D), BF16), pltpu.VMEM((tm, D), F32),
                        pltpu.VMEM((2, FFN_T, SUB, D), F32), pltpu.SemaphoreType.DMA((2,))],
        compiler_params=_cparams(("arbitrary",)),
        name="ffn",
    )(x, mods, mods, mods, g2, g3, wgu, wd)


def _pair_swap(x):
    outs = []
    for c in range(x.shape[1] // LANE):
        xc = x[:, c * LANE:(c + 1) * LANE]
        even = lax.broadcasted_iota(I32, xc.shape, 1) % 2 == 0
        outs.append(jnp.where(even, pltpu.roll(xc, LANE - 1, 1), pltpu.roll(xc, 1, 1)))
    return outs[0] if len(outs) == 1 else jnp.concatenate(outs, axis=1)


def _stream_specs(tm, width):
    n_p = NP // tm
    return [pl.BlockSpec((tm, width), lambda i: (jnp.minimum(i, n_p - 1), 0)),
            pl.BlockSpec((tm, width), lambda i: (jnp.maximum(i - n_p, 0), 0))]


def _stream_rows(p_ref, s_ref, prompt_steps):
    return jnp.where(pl.program_id(0) < prompt_steps, p_ref[...], s_ref[...])


def _mlaproj_kernel(xp_ref, xs_ref, g0, sh, sc, w1, gq, gkv, wuq, wukv, cq_ref, sq_ref, ck_ref, sk_ref,
                    qn_ref, qr_ref, kn_ref, v_ref, kr2_ref, ckv_ref, krr_ref, *, prompt_steps):
    x = _stream_rows(xp_ref, xs_ref, prompt_steps)
    sub = x.shape[0] // MLA_SUB
    cache_rows = []
    for r in range(MLA_SUB):
        rows = slice(r * sub, (r + 1) * sub)
        h = _adaln(x[rows], g0[...], sc[0], sh[0])
        dn = _dot(h.astype(BF16), w1[...])
        cq = _rms(dn[:, :Q_LORA], gq[...])
        ckv = _rms(dn[:, Q_LORA:Q_LORA + KV_LORA], gkv[...])
        krp = dn[:, Q_LORA + KV_LORA:]
        cache_rows.append((rows, ckv, krp))
        q = _dot(cq.astype(BF16), wuq[...])
        qn_ref[rows, :] = q[:, :HEADS * QK_NOPE].astype(BF16)
        qr = q[:, HEADS * QK_NOPE:]
        qr_ref[rows, :] = (qr * cq_ref[rows, :] + _pair_swap(qr) * sq_ref[rows, :]).astype(BF16)
        kv = _dot(ckv.astype(BF16), wukv[...])
        kn_ref[rows, :] = kv[:, :HEADS * QK_NOPE].astype(BF16)
        v_ref[rows, :] = kv[:, HEADS * QK_NOPE:].astype(BF16)
        kr = krp * ck_ref[rows, :] + _pair_swap(krp) * sk_ref[rows, :]
        kr2_ref[rows, :] = jnp.concatenate([kr, pltpu.roll(kr, QK_ROPE, 1)], axis=1).astype(BF16)

    @pl.when(pl.program_id(0) < prompt_steps)
    def _():
        for rows, ckv, krp in cache_rows:
            ckv_ref[rows, :] = ckv
            krr_ref[rows, :] = krp


def _mla_proj(xp, xs, g0, mods, w1, gq, gkv, wuq, wukv, cos_q, sin_q, cos_k, sin_k):
    tm = MLA_TM
    grp = _grp_bm(tm)
    n_pos = DEC_SEQ // tm
    tab = lambda w: pl.BlockSpec((tm, w), lambda i: (jnp.where(i * tm < NP, n_pos, (i - NP // tm) % n_pos), 0))
    row = lambda w: pl.BlockSpec((tm, w), lambda i: (i, 0))
    shp = lambda w, dt: jax.ShapeDtypeStruct((N, w), dt)
    n_p = NP // tm
    prow = lambda w: pl.BlockSpec((tm, w), lambda i: (jnp.minimum(i, n_p - 1), 0))
    return pl.pallas_call(
        functools.partial(_mlaproj_kernel, prompt_steps=n_p),
        grid=(N // tm,),
        in_specs=_stream_specs(tm, D) + [_full((1, D)), _mod_spec(0, grp), _mod_spec(1, grp),
                  _full(w1.shape), _full((1, Q_LORA)), _full((1, KV_LORA)),
                  _full(wuq.shape), _full(wukv.shape),
                  tab(HEADS * QK_ROPE), tab(HEADS * QK_ROPE), tab(LANE), tab(LANE)],
        out_specs=[row(HEADS * QK_NOPE), row(HEADS * QK_ROPE), row(HEADS * QK_NOPE), row(HEADS * V_DIM),
                   row(2 * LANE), prow(KV_LORA), prow(LANE)],
        out_shape=[shp(HEADS * QK_NOPE, BF16), shp(HEADS * QK_ROPE, BF16), shp(HEADS * QK_NOPE, BF16),
                   shp(HEADS * V_DIM, BF16), shp(2 * LANE, BF16),
                   jax.ShapeDtypeStruct((NP, KV_LORA), F32), jax.ShapeDtypeStruct((NP, LANE), F32)],
        compiler_params=_cparams(("arbitrary",)),
        name="mla_proj",
    )(xp, xs, g0, mods, mods, w1, gq, gkv, wuq, wukv, cos_q, sin_q, cos_k, sin_k)


def _cachekv_kernel(c_ref, w_ref, kn_ref, v_ref):
    kv = _dot(c_ref[...].astype(BF16), w_ref[...])
    kn_ref[...] = kv[:, :HEADS * QK_NOPE].astype(BF16)
    v_ref[...] = kv[:, HEADS * QK_NOPE:].astype(BF16)


def _cache_kv(ckv_cache, wukv):
    rows = ckv_cache.shape[0]
    tm = 512
    return pl.pallas_call(
        _cachekv_kernel,
        grid=(rows // tm,),
        in_specs=[pl.BlockSpec((tm, KV_LORA), lambda i: (i, 0)), _full(wukv.shape)],
        out_specs=[pl.BlockSpec((tm, HEADS * QK_NOPE), lambda i: (i, 0)),
                   pl.BlockSpec((tm, HEADS * V_DIM), lambda i: (i, 0))],
        out_shape=[jax.ShapeDtypeStruct((rows, HEADS * QK_NOPE), BF16),
                   jax.ShapeDtypeStruct((rows, HEADS * V_DIM), BF16)],
        compiler_params=_cparams(("arbitrary",)),
        name="cache_kv",
    )(ckv_cache, wukv)


def _attn_kernel(*refs, has_cache):
    if has_cache:
        qn, qr, kn, kr, v, knc, krc, vc, o_ref, s_scr = refs
        streams = [(knc, krc, vc), (kn, kr, v)]
    else:
        qn, qr, kn, kr, v, o_ref, s_scr = refs
        streams = [(kn, kr, v)]
    chunks = [(k1, k2, vv, c * KEY_BLK) for k1, k2, vv in streams for c in range(k1.shape[0] // KEY_BLK)]
    tq = qn.shape[0]
    a = (QK_NOPE + QK_ROPE) ** -0.5 * math.log2(math.e)
    for hh in range(HEAD_GRP):
        cols = slice(hh * LANE, (hh + 1) * LANE)
        pair_cols = slice((hh // 2) * LANE, (hh // 2 + 1) * LANE)
        kr_cols = slice((hh % 2) * LANE, (hh % 2 + 1) * LANE)
        q = jnp.concatenate([qn[:, cols], qr[:, pair_cols]], axis=1)
        mx = jnp.full((tq, LANE), -jnp.inf, F32)
        for n, (k1, k2, _, r0) in enumerate(chunks):
            k = jnp.concatenate([k1[r0:r0 + KEY_BLK, cols], k2[r0:r0 + KEY_BLK, kr_cols]], axis=1)
            s = lax.dot_general(q, k, NT_DIMS, preferred_element_type=F32)
            s_scr[hh, :, n * KEY_BLK:(n + 1) * KEY_BLK] = s
            for c in range(KEY_BLK // LANE):
                mx = jnp.maximum(mx, s[:, c * LANE:(c + 1) * LANE])
        mb = jnp.max(mx, axis=-1, keepdims=True) * a
        den = jnp.zeros((tq, LANE), F32)
        o = jnp.zeros((tq, V_DIM), F32)
        for n, (_, _, vv, r0) in enumerate(chunks):
            p = jnp.exp2(s_scr[hh, :, n * KEY_BLK:(n + 1) * KEY_BLK] * a - mb)
            for c in range(KEY_BLK // LANE):
                den = den + p[:, c * LANE:(c + 1) * LANE]
            o = o + _dot(p.astype(BF16), vv[r0:r0 + KEY_BLK, cols])
        o_ref[:, cols] = (o / jnp.sum(den, axis=-1, keepdims=True)).astype(BF16)


def _attention(qn, qr, kn, kr2, v, *, row0, n_seq, seq, tq, cache=None):
    nq = seq // tq
    grp = HEAD_GRP * LANE
    qblk = lambda b, h, i: row0 // tq + b * nq + i
    kblk = lambda b: row0 // seq + b
    in_specs = [pl.BlockSpec((tq, grp), lambda b, h, i: (qblk(b, h, i), h)),
                pl.BlockSpec((tq, grp // 2), lambda b, h, i: (qblk(b, h, i), h)),
                pl.BlockSpec((seq, grp), lambda b, h, i: (kblk(b), h)),
                pl.BlockSpec((seq, 2 * LANE), lambda b, h, i: (kblk(b), 0)),
                pl.BlockSpec((seq, grp), lambda b, h, i: (kblk(b), h))]
    args = [qn, qr, kn, kr2, v]
    if cache is not None:
        knc, kr2c, vc = cache
        in_specs += [pl.BlockSpec((PAST_LEN, grp), lambda b, h, i: (b, h)),
                     pl.BlockSpec((PAST_LEN, 2 * LANE), lambda b, h, i: (b, 0)),
                     pl.BlockSpec((PAST_LEN, grp), lambda b, h, i: (b, h))]
        args += [knc, kr2c, vc]
    return pl.pallas_call(
        functools.partial(_attn_kernel, has_cache=cache is not None),
        grid=(n_seq, HEADS // HEAD_GRP, nq),
        in_specs=in_specs,
        out_specs=pl.BlockSpec((tq, grp), lambda b, h, i: (b * nq + i, h)),
        out_shape=jax.ShapeDtypeStruct((n_seq * seq, HEADS * V_DIM), BF16),
        scratch_shapes=[pltpu.VMEM((HEAD_GRP, tq, seq + (PAST_LEN if cache is not None else 0)), F32)],
        compiler_params=_cparams(("arbitrary", "arbitrary", "arbitrary")),
        name="attn_latent" if cache is not None else "attn_context",
    )(*args)


def _router_kernel(op_ref, os_ref, xp_ref, xs_ref, gate1, g1, sh2, sc2, g2, wout, wr_hi, wr_lo, tri,
                   x3_ref, h_ref, ri_ref, rf_ref, cnt_ref, carry, *, prompt_steps):
    step = pl.program_id(0)

    @pl.when(step == 0)
    def _():
        carry[...] = jnp.zeros_like(carry)

    o_all = _stream_rows(op_ref, os_ref, prompt_steps)
    x_all = _stream_rows(xp_ref, xs_ref, prompt_steps)
    sub = x_all.shape[0] // ROUTER_SUB
    dg = lambda a, b: lax.dot_general(a, b, NT_DIMS, preferred_element_type=F32)
    logits = []
    for r in range(ROUTER_SUB):
        rows = slice(r * sub, (r + 1) * sub)
        x3 = _gated(x_all[rows], _dot(o_all[rows], wout[...]), g1[...], gate1[0])
        x3_ref[rows, :] = x3
        h = _adaln(x3, g2[...], sc2[0], sh2[0])
        _store_row_tiles(h_ref, lambda cols, h=h: h[:, cols], r * sub, sub)
        h_hi = h.astype(BF16)
        h_lo = (h - h_hi.astype(F32)).astype(BF16)
        logits.append(dg(wr_hi[...], h_hi) + dg(wr_hi[...], h_lo) + dg(wr_lo[...], h_hi))
    lg = jnp.concatenate(logits, axis=1)
    eidx = lax.broadcasted_iota(I32, lg.shape, 0).astype(F32)
    m1 = jnp.max(lg, axis=0, keepdims=True)
    i1 = jnp.min(jnp.where(lg == m1, eidx, float(N_EXP)), axis=0, keepdims=True)
    sel1 = eidx == i1
    lg2 = jnp.where(sel1, -jnp.inf, lg)
    m2 = jnp.max(lg2, axis=0, keepdims=True)
    i2 = jnp.min(jnp.where(lg2 == m2, eidx, float(N_EXP)), axis=0, keepdims=True)
    sel2 = eidx == i2
    e = jnp.exp(m2 - m1)
    w1 = 1.0 / (1.0 + e)
    w2 = e / (1.0 + e)
    picked = jnp.where(sel1 | sel2, 1.0, 0.0)
    rank = _dot(picked.astype(BF16), tri[...]) + carry[:, 0:1]
    r1 = jnp.sum(jnp.where(sel1, rank, 0.0), axis=0, keepdims=True)
    r2 = jnp.sum(jnp.where(sel2, rank, 0.0), axis=0, keepdims=True)
    carry[...] = carry[...] + jnp.sum(picked, axis=1, keepdims=True)
    cnt_ref[...] = carry[...]
    ri_ref[...] = jnp.where(eidx == 0.0, i1, jnp.where(eidx == 1.0, i2, jnp.where(eidx == 2.0, r1, r2))).astype(I32)
    rf_ref[...] = jnp.where(eidx == 0.0, w1, w2)


def _attn_out_router(o_p, o_s, xp, xs, mods, g1, g2, wout, wr_hi, wr_lo):
    tm = 512
    grp = _grp_bm(tm)
    n_p = NP // tm
    tri = jnp.asarray(np.triu(np.ones((tm, tm), np.float32), 1), BF16)
    row = lambda w: pl.BlockSpec((tm, w), lambda i: (i, 0))
    col = pl.BlockSpec((N_EXP, tm), lambda i: (0, i))
    return pl.pallas_call(
        functools.partial(_router_kernel, prompt_steps=n_p),
        grid=(N // tm,),
        in_specs=_stream_specs(tm, HEADS * V_DIM) + _stream_specs(tm, D) + [_mod_spec(2, grp), _full((1, D)),
                  _mod_spec(3, grp), _mod_spec(4, grp), _full((1, D)),
                  _full((HEADS * V_DIM, D)), _full((N_EXP, D)), _full((N_EXP, D)), _full((tm, tm))],
        out_specs=[row(D), pl.BlockSpec((tm * ROW_TILE, LANE), lambda i: (i, 0)), col, col, _full((N_EXP, LANE))],
        out_shape=[jax.ShapeDtypeStruct((N, D), F32), jax.ShapeDtypeStruct((N * ROW_TILE, LANE), F32),
                   jax.ShapeDtypeStruct((N_EXP, N), I32), jax.ShapeDtypeStruct((N_EXP, N), F32),
                   jax.ShapeDtypeStruct((N_EXP, LANE), F32)],
        scratch_shapes=[pltpu.VMEM((N_EXP, LANE), F32)],
        compiler_params=_cparams(("arbitrary",)),
        name="attn_out_router",
    )(o_p, o_s, xp, xs, mods, g1, mods, mods, g2, wout, wr_hi, wr_lo, tri)


def _tile_rows(r):
    return r * ROW_TILE if isinstance(r, int) else pl.multiple_of(r * ROW_TILE, ROW_TILE)


def _moe_kernel(te, nu, src0_ref, src1_ref, dstp_ref, dstc_ref, h_hbm, wgu, wd, y_hbm,
                hsbuf, ybuf, hbf, acc, sem_g, sem_s):
    i = pl.program_id(0)
    last = pl.num_programs(0) - 1
    cur = i % 2
    nxt = 1 - cur
    buf_rows = TM_E * ROW_TILE

    def gather(idx_ref, r, slot):
        i_src = pl.multiple_of(idx_ref[0, 0, r] * ROW_TILE, ROW_TILE)
        return pltpu.make_async_copy(h_hbm.at[pl.ds(i_src, ROW_TILE), :],
                                     hsbuf.at[slot, pl.ds(_tile_rows(r), ROW_TILE), :], sem_g.at[slot])

    def scatter(idx_ref, r, slot):
        i_dst = pl.multiple_of(idx_ref[0, 0, r] * ROW_TILE, ROW_TILE)
        return pltpu.make_async_copy(ybuf.at[slot, pl.ds(_tile_rows(r), ROW_TILE), :],
                                     y_hbm.at[pl.ds(i_dst, ROW_TILE), :], sem_s.at[slot])

    def wait_gather(slot):
        pltpu.make_async_copy(h_hbm.at[pl.ds(0, buf_rows), :], hsbuf.at[slot], sem_g.at[slot]).wait()

    def wait_scatter(slot):
        pltpu.make_async_copy(ybuf.at[slot], y_hbm.at[pl.ds(0, buf_rows), :], sem_s.at[slot]).wait()

    def for_rows(fn):
        def body(g, carry):
            for k in range(ROW_TILE):
                fn(g * ROW_TILE + k, k % 2)
            return carry
        lax.fori_loop(0, TM_E // ROW_TILE, body, 0)

    def side_traffic(r, queue):
        gather(src1_ref, r, nxt).start(priority=queue)
        scatter(dstp_ref, r, nxt).start(priority=queue)

    @pl.when(i == 0)
    def _():
        ybuf[1] = jnp.zeros((buf_rows, LANE), F32)
        for_rows(lambda r, queue: gather(src0_ref, r, 0).start(priority=queue))

    wait_gather(cur)

    @pl.when(i < nu[0])
    def _():
        for c in range(ROW_TILE):
            hbf[:, c * LANE:(c + 1) * LANE] = hsbuf[cur, pl.ds(c, TM_E, stride=ROW_TILE), :].astype(BF16)
        w_gu = wgu.at[0]
        w_d = wd.at[0]
        _swiglu_block(hbf, w_gu, w_d, acc, 0)
        per_iter = TM_E // SIDE_ITERS

        def body(t, carry):
            _swiglu_block(hbf, w_gu, w_d, acc, 1 + t)
            for k in range(per_iter):
                side_traffic(t * per_iter + k, k % 2)
            return carry

        lax.fori_loop(0, SIDE_ITERS, body, 0)
        for j in range(1 + SIDE_ITERS, N_FF):
            _swiglu_block(hbf, w_gu, w_d, acc, j)
        _store_row_tiles(ybuf.at[cur], lambda cols: acc[:, cols])

    @pl.when(i >= nu[0])
    def _():
        for_rows(side_traffic)
        ybuf[cur] = jnp.zeros((buf_rows, LANE), F32)

    wait_scatter(nxt)

    @pl.when(i == last)
    def _():
        for_rows(lambda r, queue: scatter(dstc_ref, r, cur).start(priority=queue))
        wait_scatter(cur)
        wait_gather(nxt)


def _moe_experts(tile_expert, n_used, src_tbl, dst_tbl, h, wgu, wd):
    smem = lambda off: pl.BlockSpec((1, 1, TM_E), lambda i, te, nu: (i + off, 0, 0), memory_space=pltpu.SMEM)
    grid_spec = pltpu.PrefetchScalarGridSpec(
        num_scalar_prefetch=2,
        grid=(N_TILES,),
        in_specs=[smem(0), smem(1), smem(0), smem(1),
                  pl.BlockSpec(memory_space=pl.ANY),
                  pl.BlockSpec((1,) + wgu.shape[1:], lambda i, te, nu: (te[i], 0, 0)),
                  pl.BlockSpec((1,) + wd.shape[1:], lambda i, te, nu: (te[i], 0, 0))],
        out_specs=pl.BlockSpec(memory_space=pl.ANY),
        scratch_shapes=[pltpu.VMEM((2, TM_E * ROW_TILE, LANE), F32), pltpu.VMEM((2, TM_E * ROW_TILE, LANE), F32),
                        pltpu.VMEM((TM_E, D), BF16), pltpu.VMEM((TM_E, D), F32),
                        pltpu.SemaphoreType.DMA((2,)), pltpu.SemaphoreType.DMA((2,))],
    )
    return pl.pallas_call(
        _moe_kernel,
        grid_spec=grid_spec,
        out_shape=jax.ShapeDtypeStruct(((2 * N + DUMP_ROWS) * ROW_TILE, LANE), F32),
        compiler_params=_cparams(("arbitrary",)),
        name="moe_experts",
    )(tile_expert, n_used, src_tbl, src_tbl, dst_tbl, dst_tbl, h, wgu, wd)


def _combine_kernel(y1_ref, y2_ref, x_ref, w1_ref, w2_ref, gate2, g3, op_ref, os_ref, *, prompt_steps):
    rows = x_ref.shape[0]
    w1 = w1_ref[...]
    w2 = w2_ref[...]
    f = jnp.concatenate([w1 * _load_row_tiles(y1_ref, c, rows) + w2 * _load_row_tiles(y2_ref, c, rows)
                         for c in range(ROW_TILE)], axis=1)
    out = _gated(x_ref[...], f, g3[...], gate2[0])
    step = pl.program_id(0)

    @pl.when(step < prompt_steps)
    def _():
        op_ref[...] = out

    @pl.when(step >= prompt_steps)
    def _():
        os_ref[...] = out


def _moe_combine(y, x, w1, w2, mods, g3):
    tm = 512
    grp = _grp_bm(tm)
    nb = N // tm
    n_p = NP // tm
    return pl.pallas_call(
        functools.partial(_combine_kernel, prompt_steps=n_p),
        grid=(nb,),
        in_specs=[pl.BlockSpec((tm * ROW_TILE, LANE), lambda i: (i, 0)),
                  pl.BlockSpec((tm * ROW_TILE, LANE), lambda i: (nb + i, 0)),
                  pl.BlockSpec((tm, D), lambda i: (i, 0)),
                  pl.BlockSpec((tm, 1), lambda i: (i, 0)), pl.BlockSpec((tm, 1), lambda i: (i, 0)),
                  _mod_spec(5, grp), _full((1, D))],
        out_specs=[pl.BlockSpec((tm, D), lambda i: (jnp.minimum(i, n_p - 1), 0)),
                   pl.BlockSpec((tm, D), lambda i: (jnp.maximum(i - n_p, 0), 0))],
        out_shape=[jax.ShapeDtypeStruct((NP, D), F32), jax.ShapeDtypeStruct((NS, D), F32)],
        compiler_params=_cparams(("arbitrary",)),
        name="moe_combine",
    )(y, y, x, w1, w2, mods, g3)


def _block_diag(w):
    hh, a, b = w.shape
    eye = jnp.eye(hh, dtype=w.dtype)
    return jnp.einsum('hab,hk->hakb', w, eye).reshape(hh * a, hh * b)


def _s5_matrices(a_re, a_im, log_dt, b_re, b_im, c_re, c_im):
    dt = jnp.exp(log_dt)[:, None]
    mag = jnp.exp(a_re * dt)
    abr = mag * jnp.cos(a_im * dt)
    abi = mag * jnp.sin(a_im * dt)
    den = a_re * a_re + a_im * a_im
    cr = ((abr - 1.0) * a_re + abi * a_im) / den
    ci = (abi * a_re - (abr - 1.0) * a_im) / den
    bbr = cr[..., None] * b_re - ci[..., None] * b_im
    bbi = cr[..., None] * b_im + ci[..., None] * b_re
    hg = S5_G // 2
    eye = jnp.eye(hg, dtype=F32)
    bms, cms = [], []
    for j in range(2):
        sl = slice(j * hg, (j + 1) * hg)
        bd = lambda m: jnp.einsum('gpc,gh->gchp', m[sl], eye).reshape(hg * S5_CH, hg * S5_P)
        bms.append(jnp.concatenate([bd(bbr), bd(bbi)], axis=1))
        cd = lambda m: jnp.einsum('gcp,gh->gphc', m[sl], eye).reshape(hg * S5_P, hg * S5_CH)
        cms.append(jnp.concatenate([cd(c_re), cd(-c_im)], axis=0))
    return (jnp.stack(bms).astype(BF16), abr.reshape(1, S5_N), abi.reshape(1, S5_N),
            jnp.stack(cms).astype(BF16))


def _rope_tables(tm):
    rows = DEC_SEQ // GRID_W
    row = jnp.repeat(jnp.arange(rows, dtype=F32), GRID_W)
    col = jnp.tile(jnp.arange(GRID_W, dtype=F32), rows)
    nf = QK_ROPE // 4
    inv = ROPE_THETA ** (-jnp.arange(nf, dtype=F32) / nf)
    ang = jnp.concatenate([row[:, None] * inv, col[:, None] * inv], axis=-1)
    cos = jnp.repeat(jnp.cos(ang), 2, axis=-1)
    sin = jnp.stack([-jnp.sin(ang), jnp.sin(ang)], axis=-1).reshape(DEC_SEQ, QK_ROPE)
    ident = lambda t, one: jnp.concatenate([t, jnp.full((tm, t.shape[1]), one, F32)], axis=0)
    cos_q = ident(jnp.tile(cos, (1, HEADS)), 1.0)
    sin_q = ident(jnp.tile(sin, (1, HEADS)), 0.0)
    pad = lambda t, one: jnp.concatenate([t, jnp.full((DEC_SEQ, LANE - QK_ROPE), one, F32)], axis=1)
    return cos_q, sin_q, ident(pad(cos, 1.0), 1.0), ident(pad(sin, 0.0), 0.0)


def _group_states(prompt_state, sample_state):
    w = sample_state.shape[-1]
    return jnp.concatenate([prompt_state.reshape(2, SUB, w), sample_state.reshape(1, SUB, w)], axis=0)


def _layer_ab(xp, xs, m, ng, j, state_lru, state_s5_re, state_s5_im, p):
    xz, x = _ab_inproj(xp, xs, ng[0:1], m, p['ab_w_in'][j].astype(BF16))
    zeros = lambda w: jnp.zeros((BATCH, w), F32)
    per_dir = []
    for d in range(2):
        wg = jnp.concatenate([_block_diag(p['lru_wa'][j, d]), _block_diag(p['lru_wx'][j, d])], axis=1).astype(BF16)
        bg = jnp.concatenate([p['lru_ba'][j, d], p['lru_bx'][j, d]])[None]
        bm, ar, ai, cm = _s5_matrices(p['s5_a_re'][j, d], p['s5_a_im'][j, d], p['s5_log_dt'][j, d],
                                      p['s5_b_re'][j, d], p['s5_b_im'][j, d], p['s5_c_re'][j, d], p['s5_c_im'][j, d])
        h0l = _group_states(zeros(LRU_W), state_lru[:, j, d])
        h0r = _group_states(zeros(S5_N), state_s5_re[:, j, d].reshape(DEC_BATCH, S5_N))
        h0i = _group_states(zeros(S5_N), state_s5_im[:, j, d].reshape(DEC_BATCH, S5_N))
        per_dir.append((h0l, h0r, h0i, p['ab_conv_w'][j], p['ab_conv_b'][j][None], wg, bg,
                        p['lru_lambda'][j, d][None], bm, ar, ai, cm))
    (haf, yf, llf, lrf, lif), (hab, yr, llb, lrb, lib) = _ab_scan(xz, per_dir)
    x = _ab_out(haf, hab, yf, yr, xz, x, m, ng[1:2], p['s5_d'][j][None], p['s5_w_glu'][j].astype(BF16),
                p['s5_b_glu'][j][None], p['ab_w_out'][j].astype(BF16))
    streams = _ffn(x, m, ng[2:3], ng[3:4], p['ffn_w_gate_up'][j].astype(BF16), p['ffn_w_down'][j].astype(BF16))
    prompt = lambda f, b, w: jnp.stack([f[:2].reshape(BATCH, w), b[:2].reshape(BATCH, w)], axis=1)
    lru = prompt(llf, llb, LRU_W)
    s5r = prompt(lrf, lrb, S5_N).reshape(BATCH, 2, S5_G, S5_P)
    s5i = prompt(lif, lib, S5_N).reshape(BATCH, 2, S5_G, S5_P)
    return tuple(streams), lru, s5r, s5i


def _head_major(w, parts):
    k = w.shape[0]
    per_head = w.reshape(k, HEADS, -1)
    out, start = [], 0
    for width in parts:
        out.append(per_head[:, :, start:start + width].reshape(k, HEADS * width))
        start += width
    return jnp.concatenate(out, axis=1)


def _layer_mla_moe(xp, xs, m, ng, j, cache_kv_latent, cache_k_rope, p):
    w1 = jnp.concatenate([p['mla_w_in'][j], jnp.zeros((D, LANE - QK_ROPE), F32)], axis=1).astype(BF16)
    wuq = _head_major(p['mla_w_uq'][j], (QK_NOPE, QK_ROPE)).astype(BF16)
    wukv = _head_major(p['mla_w_ukv'][j], (QK_NOPE, V_DIM)).astype(BF16)
    tables = _rope_tables(MLA_TM)
    qn, qr, kn, v, kr2, ckv, krr = _mla_proj(xp, xs, ng[0:1], m, w1, p['mla_g_q'][j][None], p['mla_g_kv'][j][None],
                                              wuq, wukv, *tables)
    knc, vc = _cache_kv(cache_kv_latent[:, j].reshape(DEC_BATCH * PAST_LEN, KV_LORA), wukv)
    krc = cache_k_rope[:, j].reshape(DEC_BATCH * PAST_LEN, QK_ROPE)
    z = jnp.zeros_like(krc)
    kr2c = jnp.concatenate([krc, z, z, krc], axis=1).astype(BF16)
    o_p = _attention(qn, qr, kn, kr2, v, row0=0, n_seq=BATCH, seq=SEQ, tq=SEQ)
    o_s = _attention(qn, qr, kn, kr2, v, row0=NP, n_seq=DEC_BATCH, seq=DEC_SEQ, tq=1024, cache=(knc, kr2c, vc))
    wr_t = p['moe_w_router'][j].T
    wr_hi = wr_t.astype(BF16)
    wr_lo = (wr_t - wr_hi.astype(F32)).astype(BF16)
    x3, h, ri, rf, cnt = _attn_out_router(o_p, o_s, xp, xs, m, ng[1:2], ng[2:3], p['mla_w_out'][j].astype(BF16), wr_hi, wr_lo)
    counts = cnt[:, 0].astype(I32)
    padded = ((counts + TM_E - 1) // TM_E) * TM_E
    ends = jnp.cumsum(padded)
    offs = ends - padded
    pos1 = offs[ri[0]] + ri[2]
    pos2 = offs[ri[1]] + ri[3]
    pick_tok = jnp.arange(2 * N, dtype=I32)
    dest = jnp.full((P_ROWS,), -1, I32).at[jnp.concatenate([pos1, pos2])].set(pick_tok, unique_indices=True)
    is_pad = dest < 0
    pad_row = 2 * N + TM_E + jnp.cumsum(is_pad.astype(I32)) - 1
    src_tbl = jnp.concatenate([jnp.where(is_pad, 0, dest % N), jnp.zeros((TM_E,), I32)])
    dst_tbl = jnp.concatenate([2 * N + jnp.arange(TM_E, dtype=I32), jnp.where(is_pad, pad_row, dest)])
    n_used = (ends[-1] // TM_E).astype(I32)[None]
    tile_row = jnp.minimum(jnp.arange(N_TILES, dtype=I32), n_used - 1) * TM_E
    tile_expert = jnp.sum((tile_row[:, None] >= ends[None, :]).astype(I32), axis=1)
    y = _moe_experts(tile_expert, n_used, src_tbl.reshape(N_TILES + 1, 1, TM_E), dst_tbl.reshape(N_TILES + 1, 1, TM_E),
                     h, p['moe_w_gate_up'][j].astype(BF16), p['moe_w_down'][j].astype(BF16))
    xp, xs = _moe_combine(y, x3, rf[0][:, None], rf[1][:, None], m, ng[3:4])
    kv_new = ckv.reshape(BATCH, SEQ, KV_LORA)
    kr_new = krr[:, :QK_ROPE].reshape(BATCH, SEQ, QK_ROPE)
    return (xp.reshape(BATCH, SEQ, D), xs.reshape(DEC_BATCH, DEC_SEQ, D)), kv_new, kr_new


def kernel(x_prompt, x_sample, c, state_lru, state_s5_re, state_s5_im, cache_kv_latent, cache_k_rope, c_ctx, w_mod, b_mod, norm_gains, ab_w_in, ab_conv_w, ab_conv_b, lru_wa, lru_ba, lru_wx, lru_bx, lru_lambda, s5_a_re, s5_a_im, s5_log_dt, s5_b_re, s5_b_im, s5_c_re, s5_c_im, s5_d, s5_w_glu, s5_b_glu, ab_w_out, ffn_w_gate_up, ffn_w_down, mla_w_in, mla_g_q, mla_g_kv, mla_w_uq, mla_w_ukv, mla_w_out, moe_w_router, moe_w_gate_up, moe_w_down):
    p = dict(ab_w_in=ab_w_in, ab_conv_w=ab_conv_w, ab_conv_b=ab_conv_b, lru_wa=lru_wa, lru_ba=lru_ba,
             lru_wx=lru_wx, lru_bx=lru_bx, lru_lambda=lru_lambda, s5_a_re=s5_a_re, s5_a_im=s5_a_im,
             s5_log_dt=s5_log_dt, s5_b_re=s5_b_re, s5_b_im=s5_b_im, s5_c_re=s5_c_re, s5_c_im=s5_c_im,
             s5_d=s5_d, s5_w_glu=s5_w_glu, s5_b_glu=s5_b_glu, ab_w_out=ab_w_out, ffn_w_gate_up=ffn_w_gate_up,
             ffn_w_down=ffn_w_down, mla_w_in=mla_w_in, mla_g_q=mla_g_q, mla_g_kv=mla_g_kv, mla_w_uq=mla_w_uq,
             mla_w_ukv=mla_w_ukv, mla_w_out=mla_w_out, moe_w_router=moe_w_router, moe_w_gate_up=moe_w_gate_up,
             moe_w_down=moe_w_down)
    depth = w_mod.shape[0]
    cond = jnp.concatenate([c_ctx[None], c, jnp.zeros((2 * SUB - 1 - DEC_BATCH, D), F32)], axis=0)
    mod = _modulation(cond, w_mod, b_mod)
    ctx_tile = lambda l: jnp.broadcast_to(mod[l, 0:1], (SUB, 6 * D))
    streams = (x_prompt, x_sample)
    lru_l, s5r_l, s5i_l, kv_l, kr_l = [], [], [], [], []
    for layer in range(depth):
        j = layer // 2
        ng = norm_gains[layer]
        if layer % 2 == 0:
            m = jnp.stack([ctx_tile(layer), mod[layer, 1:1 + DEC_BATCH]])
            streams, lru, s5r, s5i = _layer_ab(*streams, m, ng, j, state_lru, state_s5_re, state_s5_im, p)
            lru_l.append(lru)
            s5r_l.append(s5r)
            s5i_l.append(s5i)
        else:
            lat = jnp.broadcast_to(mod[layer, 1:1 + DEC_BATCH, None, :], (DEC_BATCH, SUB, 6 * D))
            m = jnp.concatenate([ctx_tile(layer)[None], lat], axis=0)
            streams, kv_new, kr_new = _layer_mla_moe(streams[0].reshape(NP, D), streams[1].reshape(NS, D), m, ng, j,
                                                     cache_kv_latent, cache_k_rope, p)
            kv_l.append(kv_new)
            kr_l.append(kr_new)
    return (streams[0], streams[1],
            jnp.stack(lru_l, axis=1), jnp.stack(s5r_l, axis=1), jnp.stack(s5i_l, axis=1),
            jnp.stack(kv_l, axis=1), jnp.stack(kr_l, axis=1))
```

```python
import functools
import math

import numpy as np
import jax
import jax.numpy as jnp
from jax import lax
from jax.experimental import pallas as pl
from jax.experimental.pallas import tpu as pltpu

F32 = jnp.float32
BF16 = jnp.bfloat16
I32 = jnp.int32

D = 1024
BATCH, SEQ = 16, 256
DEC_BATCH, DEC_SEQ = 8, 2048
PAST_LEN = 256
GRID_W = 64
LRU_W = 512
LRU_HEADS = 8
LRU_C = 8.0
CONV_W = 4
S5_W = 512
S5_CH = 16
S5_G = 32
S5_P = 64
S5_N = S5_G * S5_P
HEADS = 8
QK_NOPE, QK_ROPE, V_DIM = 128, 64, 128
Q_LORA, KV_LORA = 384, 256
ROPE_THETA = 10000.0
D_FF = 2816
N_EXP = 8
EPS = 1e-6

NP = BATCH * SEQ
NS = DEC_BATCH * DEC_SEQ
N = NP + NS
SUB = 8
LANE = 128
ROW_TILE = D // LANE
T_CHUNK = 64
R_CHUNK = T_CHUNK * SUB
FF_BLK = 256
KEY_BLK = 256
HEAD_GRP = 4
MLA_TM = 512
MLA_SUB = 2
ROUTER_SUB = 1
ABOUT_SUB = 1
INPROJ_T = 64
FFN_T = 128
SIDE_ITERS = 8
N_FF = D_FF // FF_BLK
TM_E = 512
P_ROWS = 2 * N + N_EXP * TM_E
N_TILES = P_ROWS // TM_E
DUMP_ROWS = P_ROWS - 2 * N + TM_E
VMEM_LIMIT = 56 * 1024 * 1024

NT_DIMS = (((1,), (1,)), ((), ()))


def _cparams(sem):
    return pltpu.CompilerParams(dimension_semantics=sem, vmem_limit_bytes=VMEM_LIMIT)


def _dot(a, b):
    return jnp.dot(a, b, preferred_element_type=F32)


def _sigmoid(x):
    return 1.0 / (1.0 + jnp.exp(-x))


def _neg_expm1_2x(log_a, a):
    series = -2.0 * log_a * (1.0 + log_a * (1.0 + log_a * (2.0 / 3.0) * (1.0 + log_a * 0.5)))
    return jnp.where(log_a > -0.01, series, (1.0 - a) * (1.0 + a))


def _sqrt_nonneg(v):
    return jnp.where(v > 0.0, v * lax.rsqrt(v), 0.0)


def _gelu(x):
    return x * (0.5 * (1.0 + jnp.tanh(math.sqrt(2.0 / math.pi) * (x + 0.044715 * (x * x * x)))))


def _rms(x, g):
    ms = jnp.mean(x * x, axis=-1, keepdims=True)
    return x * lax.rsqrt(ms + EPS) * g


def _rows8(y, fn):
    r, c = y.shape
    return fn(y.reshape(r // SUB, SUB, c)).reshape(r, c)


def _adaln(x, g, scale, shift):
    return _rows8(_rms(x, g), lambda y: y * (1.0 + scale)[None] + shift[None])


def _gated(x, y, g, gate):
    return x + _rows8(_rms(y, g), lambda z: z * gate[None])


def _store_row_tiles(ref, piece, row0=0, rows=None):
    rows = ref.shape[0] // ROW_TILE if rows is None else rows
    for c in range(ROW_TILE):
        ref[pl.ds(row0 * ROW_TILE + c, rows, stride=ROW_TILE), :] = piece(slice(c * LANE, (c + 1) * LANE))


def _load_row_tiles(ref, c, rows):
    return ref[pl.ds(c, rows, stride=ROW_TILE), :]


def _full(shape):
    nd = len(shape)
    return pl.BlockSpec(shape, lambda *_: (0,) * nd)


def _mod_spec(k, grp):
    return pl.BlockSpec((1, SUB, D), lambda i, *_: (grp(i), 0, k))


def _grp_tm(tm):
    return lambda i: (i * tm >= NP).astype(I32)


def _grp_bm(tm):
    return lambda i: jnp.where(i * tm < NP, 0, 1 + (i * tm - NP) // DEC_SEQ)


def _mod_kernel(c_ref, w_ref, b_ref, o_ref):
    c = c_ref[...]
    s = c * _sigmoid(c)
    o_ref[0] = _dot(s.astype(BF16), w_ref[0].astype(BF16)) + b_ref[0]


def _modulation(cond, w_mod, b_mod):
    depth = w_mod.shape[0]
    rows = cond.shape[0]
    return pl.pallas_call(
        _mod_kernel,
        grid=(depth, 6),
        in_specs=[_full((rows, D)),
                  pl.BlockSpec((1, D, D), lambda l, j: (l, 0, j)),
                  pl.BlockSpec((1, 1, D), lambda l, j: (l, 0, j))],
        out_specs=pl.BlockSpec((1, rows, D), lambda l, j: (l, 0, j)),
        out_shape=jax.ShapeDtypeStruct((depth, rows, 6 * D), F32),
        compiler_params=_cparams(("arbitrary", "arbitrary")),
        name="modulation",
    )(cond, w_mod, b_mod.reshape(depth, 1, 6 * D))


def _time_major_copies(step, t_steps, xp_hbm, xs_hbm, buf, sem, to_hbm):
    p_steps = (BATCH // SUB) * (SEQ // t_steps)
    per_group = SEQ // t_steps

    def issue(hbm, seq0, t0):
        t0 = pl.multiple_of(t0, t_steps)
        for b in range(SUB):
            rows = hbm.at[seq0 + b, pl.ds(t0, t_steps), :]
            tile = buf.at[:, b, :]
            (pltpu.make_async_copy(tile, rows, sem) if to_hbm else pltpu.make_async_copy(rows, tile, sem)).start()

    @pl.when(step < p_steps)
    def _():
        issue(xp_hbm, (step // per_group) * SUB, (step % per_group) * t_steps)

    @pl.when(step >= p_steps)
    def _():
        issue(xs_hbm, 0, (step - p_steps) * t_steps)


def _time_major_wait(t_steps, xs_hbm, buf, sem, to_hbm):
    for b in range(SUB):
        rows = xs_hbm.at[0, pl.ds(0, t_steps), :]
        tile = buf.at[:, b, :]
        (pltpu.make_async_copy(tile, rows, sem) if to_hbm else pltpu.make_async_copy(rows, tile, sem)).wait()


def _inproj_kernel(xp_hbm, xs_hbm, g_ref, sh_ref, sc_ref, w_ref, o_ref, xtm_ref, buf, sem):
    step = pl.program_id(0)
    slot = step % 2

    @pl.when(step == 0)
    def _():
        _time_major_copies(step, INPROJ_T, xp_hbm, xs_hbm, buf.at[0], sem.at[0], False)

    @pl.when(step + 1 < pl.num_programs(0))
    def _():
        _time_major_copies(step + 1, INPROJ_T, xp_hbm, xs_hbm, buf.at[1 - slot], sem.at[1 - slot], False)

    _time_major_wait(INPROJ_T, xs_hbm, buf.at[slot], sem.at[slot], False)
    x = buf[slot].reshape(INPROJ_T * SUB, D)
    xtm_ref[...] = x
    h = _adaln(x, g_ref[...], sc_ref[0], sh_ref[0])
    o_ref[...] = _dot(h.astype(BF16), w_ref[...])


def _ab_inproj(xp, xs, gain, mods, w_in):
    tm = INPROJ_T * SUB
    nout = w_in.shape[1]
    grp = _grp_tm(tm)
    return pl.pallas_call(
        _inproj_kernel,
        grid=(N // tm,),
        in_specs=[pl.BlockSpec(memory_space=pl.ANY), pl.BlockSpec(memory_space=pl.ANY),
                  _full((1, D)),
                  _mod_spec(0, grp), _mod_spec(1, grp),
                  _full((D, nout))],
        out_specs=[pl.BlockSpec((tm, nout), lambda i: (i, 0)), pl.BlockSpec((tm, D), lambda i: (i, 0))],
        out_shape=[jax.ShapeDtypeStruct((N, nout), F32), jax.ShapeDtypeStruct((N, D), F32)],
        scratch_shapes=[pltpu.VMEM((2, INPROJ_T, SUB, D), F32), pltpu.SemaphoreType.DMA((2,))],
        compiler_params=_cparams(("arbitrary",)),
        name="ab_inproj",
    )(xp, xs, gain, mods, mods, w_in)


def _scan_table(reverse):
    cols = []
    groups = [(0, SEQ // T_CHUNK, 0), (1, SEQ // T_CHUNK, SEQ // T_CHUNK),
              (2, DEC_SEQ // T_CHUNK, NP // R_CHUNK)]
    for g, nc, base in groups:
        order = range(nc - 1, -1, -1) if reverse else range(nc)
        for k, c in enumerate(order):
            cols.append((base + c, g, int(k == 0), int(c > 0), int(c < nc - 1)))
    return np.asarray(cols, np.int32).T.copy()


N_SCAN_IN, N_SCAN_OUT, N_SCAN_SCRATCH = 16, 5, 8
SCAN_TBL_ROWS = 5


def _scan_kernel(tbl, *refs):
    ins = [refs[d * N_SCAN_IN:(d + 1) * N_SCAN_IN] for d in range(2)]
    o0 = 2 * N_SCAN_IN
    outs = [refs[o0 + d * N_SCAN_OUT:o0 + (d + 1) * N_SCAN_OUT] for d in range(2)]
    s0 = o0 + 2 * N_SCAN_OUT
    scr = [refs[s0 + d * N_SCAN_SCRATCH:s0 + (d + 1) * N_SCAN_SCRATCH] for d in range(2)]
    s = pl.program_id(0)
    for d in range(2):
        h0l_ref, h0r_ref, h0i_ref = ins[d][4:7]
        hl, sre, sim = scr[d][5:8]

        @pl.when(tbl[d * SCAN_TBL_ROWS + 2, s] == 1)
        def _(h0l_ref=h0l_ref, h0r_ref=h0r_ref, h0i_ref=h0i_ref, hl=hl, sre=sre, sim=sim):
            hl[...] = h0l_ref[0]
            sre[...] = h0r_ref[0]
            sim[...] = h0i_ref[0]

    fwd, bwd = [_ScanChunk(tbl, d * SCAN_TBL_ROWS, ins[d], outs[d], scr[d], reverse=(d == 1)) for d in range(2)]
    fwd.s5_project()
    fwd.conv()
    fwd.gates()
    bwd.s5_project()
    fwd.s5_recurrence()
    bwd.conv()
    fwd.s5_readout()
    bwd.gates()
    fwd.lru_coefficients()
    fwd.lru_recurrence()
    bwd.s5_recurrence()
    bwd.s5_readout()
    bwd.lru_coefficients()
    bwd.lru_recurrence()


class _ScanChunk:
    def __init__(self, tbl, row0, ins, outs, scratch, reverse):
        (self.xa_ref, self.xp_ref, self.xn_ref, self.xb_ref, _, _, _, self.cw_ref, self.cb_ref, self.wg_ref,
         self.bg_ref, self.lam_ref, self.bm_ref, self.ar_ref, self.ai_ref, self.cm_ref) = ins
        self.ha_ref, self.y_ref, self.ll_ref, self.lr_ref, self.li_ref = outs
        self.ext, self.abuf, self.bbuf, self.hre, self.him, self.hl, self.sre, self.sim = scratch
        step = pl.program_id(0)
        self.has_prev = tbl[row0 + 3, step] == 1
        self.has_next = tbl[row0 + 4, step] == 1
        self.order = range(T_CHUNK - 1, -1, -1) if reverse else range(T_CHUNK)
        self.half = S5_N // 2

    def s5_project(self):
        ub = self.xb_ref[...].astype(BF16)
        half = self.half
        for j in range(2):
            bu = _dot(ub[:, j * 256:(j + 1) * 256], self.bm_ref[j])
            self.hre[:, j * half:(j + 1) * half] = bu[:, :half]
            self.him[:, j * half:(j + 1) * half] = bu[:, half:]

    def conv(self):
        ext = self.ext
        ext[0:2 * SUB] = jnp.where(self.has_prev, self.xp_ref[...], 0.0)
        ext[2 * SUB:2 * SUB + R_CHUNK] = self.xa_ref[...]
        ext[2 * SUB + R_CHUNK:3 * SUB + R_CHUNK] = jnp.where(self.has_next, self.xn_ref[...], 0.0)
        xa = self.cb_ref[...] + self.cw_ref[0:1] * ext[0:R_CHUNK]
        for k in range(1, CONV_W):
            xa = xa + self.cw_ref[k:k + 1] * ext[k * SUB:k * SUB + R_CHUNK]
        self.xa = xa

    def gates(self):
        self.gz = _dot(self.xa.astype(BF16), self.wg_ref[...]) + self.bg_ref[...]

    def s5_recurrence(self):
        cblk = 4 * LANE
        for cb in range(S5_N // cblk):
            cols = slice(cb * cblk, (cb + 1) * cblk)
            ar = jnp.broadcast_to(self.ar_ref[:, cols], (SUB, cblk))
            ai = jnp.broadcast_to(self.ai_ref[:, cols], (SUB, cblk))
            hr = self.sre[:, cols]
            hi = self.sim[:, cols]
            for t in self.order:
                rows = slice(t * SUB, (t + 1) * SUB)
                nr = ar * hr - ai * hi + self.hre[rows, cols]
                ni = ar * hi + ai * hr + self.him[rows, cols]
                hr, hi = nr, ni
                self.hre[rows, cols] = hr
                self.him[rows, cols] = hi
            self.sre[:, cols] = hr
            self.sim[:, cols] = hi
        self.lr_ref[0] = self.sre[...]
        self.li_ref[0] = self.sim[...]

    def s5_readout(self):
        half = self.half
        for j in range(2):
            hc = jnp.concatenate([self.hre[:, j * half:(j + 1) * half], self.him[:, j * half:(j + 1) * half]],
                                 axis=1).astype(BF16)
            self.y_ref[:, j * 256:(j + 1) * 256] = _dot(hc, self.cm_ref[j])

    def lru_coefficients(self):
        r = 0.5 * jnp.tanh(0.5 * self.gz[:, :LRU_W]) + 0.5
        i = 0.5 * jnp.tanh(0.5 * self.gz[:, LRU_W:]) + 0.5
        lam = self.lam_ref[...]
        log_sig = jnp.minimum(lam, 0.0) - jnp.log1p(jnp.exp(-jnp.abs(lam)))
        log_a = LRU_C * r * log_sig
        a = jnp.exp(log_a)
        self.abuf[...] = a
        self.bbuf[...] = _sqrt_nonneg(_neg_expm1_2x(log_a, a)) * (i * self.xa)

    def lru_recurrence(self):
        h = self.hl[...]
        for t in self.order:
            rows = slice(t * SUB, (t + 1) * SUB)
            h = self.abuf[rows] * h + self.bbuf[rows]
            self.ha_ref[rows, :] = h
        self.hl[...] = h
        self.ll_ref[0] = h


def _ab_scan(xz, per_dir):
    tbl = jnp.asarray(np.concatenate([_scan_table(False), _scan_table(True)], axis=0))
    n_steps = tbl.shape[1]
    in_specs, out_specs, scratch, out_shape, args = [], [], [], [], []
    for d in range(2):
        blk = lambda s, t, d=d: t[d * SCAN_TBL_ROWS, s]
        grp = lambda s, t, d=d: t[d * SCAN_TBL_ROWS + 1, s]
        state_spec = lambda w, grp=grp: pl.BlockSpec((1, SUB, w), lambda s, t: (grp(s, t), 0, 0))
        const = lambda shape: pl.BlockSpec(shape, lambda s, t: (0,) * len(shape))
        in_specs += [
            pl.BlockSpec((R_CHUNK, LRU_W), lambda s, t, blk=blk: (blk(s, t), 0)),
            pl.BlockSpec((2 * SUB, LRU_W),
                         lambda s, t, blk=blk: (jnp.maximum(blk(s, t) * (T_CHUNK // 2) - 1, 0), 0)),
            pl.BlockSpec((SUB, LRU_W),
                         lambda s, t, blk=blk: (jnp.minimum((blk(s, t) + 1) * T_CHUNK, N // SUB - 1), 0)),
            pl.BlockSpec((R_CHUNK, S5_W), lambda s, t, blk=blk: (blk(s, t), 2)),
            state_spec(LRU_W), state_spec(S5_N), state_spec(S5_N),
            const((CONV_W, LRU_W)), const((1, LRU_W)),
            const((LRU_W, 2 * LRU_W)), const((1, 2 * LRU_W)), const((1, LRU_W)),
            const((2, 256, S5_N)), const((1, S5_N)), const((1, S5_N)), const((2, S5_N, 256)),
        ]
        out_specs += [
            pl.BlockSpec((R_CHUNK, LRU_W), lambda s, t, blk=blk: (blk(s, t), 0)),
            pl.BlockSpec((R_CHUNK, S5_W), lambda s, t, blk=blk: (blk(s, t), 0)),
            state_spec(LRU_W), state_spec(S5_N), state_spec(S5_N),
        ]
        scratch += [
            pltpu.VMEM((R_CHUNK + 3 * SUB, LRU_W), F32),
            pltpu.VMEM((R_CHUNK, LRU_W), F32), pltpu.VMEM((R_CHUNK, LRU_W), F32),
            pltpu.VMEM((R_CHUNK, S5_N), F32), pltpu.VMEM((R_CHUNK, S5_N), F32),
            pltpu.VMEM((SUB, LRU_W), F32), pltpu.VMEM((SUB, S5_N), F32), pltpu.VMEM((SUB, S5_N), F32),
        ]
        out_shape += [jax.ShapeDtypeStruct((N, LRU_W), F32), jax.ShapeDtypeStruct((N, S5_W), F32),
                      jax.ShapeDtypeStruct((3, SUB, LRU_W), F32),
                      jax.ShapeDtypeStruct((3, SUB, S5_N), F32), jax.ShapeDtypeStruct((3, SUB, S5_N), F32)]
        args += [xz, xz, xz, xz, *per_dir[d]]
    grid_spec = pltpu.PrefetchScalarGridSpec(num_scalar_prefetch=1, grid=(n_steps,), in_specs=in_specs,
                                             out_specs=out_specs, scratch_shapes=scratch)
    outs = pl.pallas_call(
        _scan_kernel,
        grid_spec=grid_spec,
        out_shape=out_shape,
        compiler_params=_cparams(("arbitrary",)),
        name="ab_scan",
    )(tbl, *args)
    return outs[:N_SCAN_OUT], outs[N_SCAN_OUT:]


def _about_kernel(haf, hab, yf, yr, ga, xb, x_ref, gate, g1, d_ref, wglu, bglu, wout, o_ref):
    sub = x_ref.shape[0] // ABOUT_SUB
    for r in range(ABOUT_SUB):
        rows = slice(r * sub, (r + 1) * sub)
        ya = (haf[rows, :] + hab[rows, :]) * _gelu(ga[rows, :])
        yb0 = _gelu(yf[rows, :] + yr[rows, :] + d_ref[...] * xb[rows, :])
        yb = yb0 * _sigmoid(_dot(yb0.astype(BF16), wglu[...]) + bglu[...])
        out = _dot(ya.astype(BF16), wout[0:LRU_W]) + _dot(yb.astype(BF16), wout[LRU_W:LRU_W + S5_W])
        o_ref[rows, :] = _gated(x_ref[rows, :], out, g1[...], gate[0])


def _ab_out(haf, hab, yf, yr, xz, x, mods, g1, s5_d, wglu, bglu, wout):
    tm = 512
    grp = _grp_tm(tm)
    half = lambda c: pl.BlockSpec((tm, LRU_W), lambda i: (i, c))
    return pl.pallas_call(
        _about_kernel,
        grid=(N // tm,),
        in_specs=[half(0), half(0), half(0), half(0), half(1), half(2),
                  pl.BlockSpec((tm, D), lambda i: (i, 0)),
                  _mod_spec(2, grp), _full((1, D)), _full((1, S5_W)),
                  _full((S5_W, S5_W)), _full((1, S5_W)), _full((LRU_W + S5_W, D))],
        out_specs=pl.BlockSpec((tm, D), lambda i: (i, 0)),
        out_shape=jax.ShapeDtypeStruct((N, D), F32),
        compiler_params=_cparams(("arbitrary",)),
        name="ab_out",
    )(haf, hab, yf, yr, xz, xz, x, mods, g1, s5_d, wglu, bglu, wout)


def _swiglu_block(hbf, wgu, wd, acc, j):
    static = isinstance(j, int)
    blk = lambda start: pl.ds(start if static else pl.multiple_of(start, FF_BLK), FF_BLK)
    h = hbf[...]
    g = _dot(h, wgu[:, blk(j * FF_BLK)])
    u = _dot(h, wgu[:, blk(D_FF + j * FF_BLK)])
    act = (g * _sigmoid(g)) * u
    part = _dot(act.astype(BF16), wd[blk(j * FF_BLK), :])
    if static and j == 0:
        acc[...] = part
    else:
        acc[...] += part


def _ffn_kernel(x_ref, sh, sc, gt, g2, g3, wgu, wd, op_hbm, os_hbm, hbf, acc, obuf, sem):
    step = pl.program_id(0)
    slot = step % 2
    hbf[...] = _adaln(x_ref[...], g2[...], sc[0], sh[0]).astype(BF16)
    for j in range(N_FF):
        _swiglu_block(hbf, wgu, wd, acc, j)
    obuf[slot] = _gated(x_ref[...], acc[...], g3[...], gt[0]).reshape(FFN_T, SUB, D)
    _time_major_copies(step, FFN_T, op_hbm, os_hbm, obuf.at[slot], sem.at[slot], True)

    @pl.when(step > 0)
    def _():
        _time_major_wait(FFN_T, os_hbm, obuf.at[1 - slot], sem.at[1 - slot], True)

    @pl.when(step == pl.num_programs(0) - 1)
    def _():
        _time_major_wait(FFN_T, os_hbm, obuf.at[slot], sem.at[slot], True)


def _ffn(x, mods, g2, g3, wgu, wd):
    tm = FFN_T * SUB
    grp = _grp_tm(tm)
    once = lambda shape: pl.BlockSpec(shape, lambda i: (0,) * len(shape), pipeline_mode=pl.Buffered(1))
    return pl.pallas_call(
        _ffn_kernel,
        grid=(N // tm,),
        in_specs=[pl.BlockSpec((tm, D), lambda i: (i, 0)),
                  _mod_spec(3, grp), _mod_spec(4, grp), _mod_spec(5, grp),
                  _full((1, D)), _full((1, D)),
                  once(wgu.shape), once(wd.shape)],
        out_specs=[pl.BlockSpec(memory_space=pl.ANY), pl.BlockSpec(memory_space=pl.ANY)],
        out_shape=[jax.ShapeDtypeStruct((BATCH, SEQ, D), F32), jax.ShapeDtypeStruct((DEC_BATCH, DEC_SEQ, D), F32)],
        scratch_shapes=[pltpu.VMEM((tm, D), BF16), pltpu.VMEM((tm, D), F32),
                        pltpu.VMEM((2, FFN_T, SUB, D), F32), pltpu.SemaphoreType.DMA((2,))],
        compiler_params=_cparams(("arbitrary",)),
        name="ffn",
    )(x, mods, mods, mods, g2, g3, wgu, wd)


def _pair_swap(x):
    outs = []
    for c in range(x.shape[1] // LANE):
        xc = x[:, c * LANE:(c + 1) * LANE]
        even = lax.broadcasted_iota(I32, xc.shape, 1) % 2 == 0
        outs.append(jnp.where(even, pltpu.roll(xc, LANE - 1, 1), pltpu.roll(xc, 1, 1)))
    return outs[0] if len(outs) == 1 else jnp.concatenate(outs, axis=1)


def _stream_specs(tm, width):
    n_p = NP // tm
    return [pl.BlockSpec((tm, width), lambda i: (jnp.minimum(i, n_p - 1), 0)),
            pl.BlockSpec((tm, width), lambda i: (jnp.maximum(i - n_p, 0), 0))]


def _stream_rows(p_ref, s_ref, prompt_steps):
    return jnp.where(pl.program_id(0) < prompt_steps, p_ref[...], s_ref[...])


def _mlaproj_kernel(xp_ref, xs_ref, g0, sh, sc, w1, gq, gkv, wuq, wukv, cq_ref, sq_ref, ck_ref, sk_ref,
                    qn_ref, qr_ref, kn_ref, v_ref, kr2_ref, ckv_ref, krr_ref, *, prompt_steps):
    x = _stream_rows(xp_ref, xs_ref, prompt_steps)
    sub = x.shape[0] // MLA_SUB
    cache_rows = []
    for r in range(MLA_SUB):
        rows = slice(r * sub, (r + 1) * sub)
        h = _adaln(x[rows], g0[...], sc[0], sh[0])
        dn = _dot(h.astype(BF16), w1[...])
        cq = _rms(dn[:, :Q_LORA], gq[...])
        ckv = _rms(dn[:, Q_LORA:Q_LORA + KV_LORA], gkv[...])
        krp = dn[:, Q_LORA + KV_LORA:]
        cache_rows.append((rows, ckv, krp))
        q = _dot(cq.astype(BF16), wuq[...])
        qn_ref[rows, :] = q[:, :HEADS * QK_NOPE].astype(BF16)
        qr = q[:, HEADS * QK_NOPE:]
        qr_ref[rows, :] = (qr * cq_ref[rows, :] + _pair_swap(qr) * sq_ref[rows, :]).astype(BF16)
        kv = _dot(ckv.astype(BF16), wukv[...])
        kn_ref[rows, :] = kv[:, :HEADS * QK_NOPE].astype(BF16)
        v_ref[rows, :] = kv[:, HEADS * QK_NOPE:].astype(BF16)
        kr = krp * ck_ref[rows, :] + _pair_swap(krp) * sk_ref[rows, :]
        kr2_ref[rows, :] = jnp.concatenate([kr, pltpu.roll(kr, QK_ROPE, 1)], axis=1).astype(BF16)

    @pl.when(pl.program_id(0) < prompt_steps)
    def _():
        for rows, ckv, krp in cache_rows:
            ckv_ref[rows, :] = ckv
            krr_ref[rows, :] = krp


def _mla_proj(xp, xs, g0, mods, w1, gq, gkv, wuq, wukv, cos_q, sin_q, cos_k, sin_k):
    tm = MLA_TM
    grp = _grp_bm(tm)
    n_pos = DEC_SEQ // tm
    tab = lambda w: pl.BlockSpec((tm, w), lambda i: (jnp.where(i * tm < NP, n_pos, (i - NP // tm) % n_pos), 0))
    row = lambda w: pl.BlockSpec((tm, w), lambda i: (i, 0))
    shp = lambda w, dt: jax.ShapeDtypeStruct((N, w), dt)
    n_p = NP // tm
    prow = lambda w: pl.BlockSpec((tm, w), lambda i: (jnp.minimum(i, n_p - 1), 0))
    return pl.pallas_call(
        functools.partial(_mlaproj_kernel, prompt_steps=n_p),
        grid=(N // tm,),
        in_specs=_stream_specs(tm, D) + [_full((1, D)), _mod_spec(0, grp), _mod_spec(1, grp),
                  _full(w1.shape), _full((1, Q_LORA)), _full((1, KV_LORA)),
                  _full(wuq.shape), _full(wukv.shape),
                  tab(HEADS * QK_ROPE), tab(HEADS * QK_ROPE), tab(LANE), tab(LANE)],
        out_specs=[row(HEADS * QK_NOPE), row(HEADS * QK_ROPE), row(HEADS * QK_NOPE), row(HEADS * V_DIM),
                   row(2 * LANE), prow(KV_LORA), prow(LANE)],
        out_shape=[shp(HEADS * QK_NOPE, BF16), shp(HEADS * QK_ROPE, BF16), shp(HEADS * QK_NOPE, BF16),
                   shp(HEADS * V_DIM, BF16), shp(2 * LANE, BF16),
                   jax.ShapeDtypeStruct((NP, KV_LORA), F32), jax.ShapeDtypeStruct((NP, LANE), F32)],
        compiler_params=_cparams(("arbitrary",)),
        name="mla_proj",
    )(xp, xs, g0, mods, mods, w1, gq, gkv, wuq, wukv, cos_q, sin_q, cos_k, sin_k)


def _cachekv_kernel(c_ref, w_ref, kn_ref, v_ref):
    kv = _dot(c_ref[...].astype(BF16), w_ref[...])
    kn_ref[...] = kv[:, :HEADS * QK_NOPE].astype(BF16)
    v_ref[...] = kv[:, HEADS * QK_NOPE:].astype(BF16)


def _cache_kv(ckv_cache, wukv):
    rows = ckv_cache.shape[0]
    tm = 512
    return pl.pallas_call(
        _cachekv_kernel,
        grid=(rows // tm,),
        in_specs=[pl.BlockSpec((tm, KV_LORA), lambda i: (i, 0)), _full(wukv.shape)],
        out_specs=[pl.BlockSpec((tm, HEADS * QK_NOPE), lambda i: (i, 0)),
                   pl.BlockSpec((tm, HEADS * V_DIM), lambda i: (i, 0))],
        out_shape=[jax.ShapeDtypeStruct((rows, HEADS * QK_NOPE), BF16),
                   jax.ShapeDtypeStruct((rows, HEADS * V_DIM), BF16)],
        compiler_params=_cparams(("arbitrary",)),
        name="cache_kv",
    )(ckv_cache, wukv)


def _attn_kernel(*refs, has_cache):
    if has_cache:
        qn, qr, kn, kr, v, knc, krc, vc, o_ref, s_scr = refs
        streams = [(knc, krc, vc), (kn, kr, v)]
    else:
        qn, qr, kn, kr, v, o_ref, s_scr = refs
        streams = [(kn, kr, v)]
    chunks = [(k1, k2, vv, c * KEY_BLK) for k1, k2, vv in streams for c in range(k1.shape[0] // KEY_BLK)]
    tq = qn.shape[0]
    a = (QK_NOPE + QK_ROPE) ** -0.5 * math.log2(math.e)
    for hh in range(HEAD_GRP):
        cols = slice(hh * LANE, (hh + 1) * LANE)
        pair_cols = slice((hh // 2) * LANE, (hh // 2 + 1) * LANE)
        kr_cols = slice((hh % 2) * LANE, (hh % 2 + 1) * LANE)
        q = jnp.concatenate([qn[:, cols], qr[:, pair_cols]], axis=1)
        mx = jnp.full((tq, LANE), -jnp.inf, F32)
        for n, (k1, k2, _, r0) in enumerate(chunks):
            k = jnp.concatenate([k1[r0:r0 + KEY_BLK, cols], k2[r0:r0 + KEY_BLK, kr_cols]], axis=1)
            s = lax.dot_general(q, k, NT_DIMS, preferred_element_type=F32)
            s_scr[hh, :, n * KEY_BLK:(n + 1) * KEY_BLK] = s
            for c in range(KEY_BLK // LANE):
                mx = jnp.maximum(mx, s[:, c * LANE:(c + 1) * LANE])
        mb = jnp.max(mx, axis=-1, keepdims=True) * a
        den = jnp.zeros((tq, LANE), F32)
        o = jnp.zeros((tq, V_DIM), F32)
        for n, (_, _, vv, r0) in enumerate(chunks):
            p = jnp.exp2(s_scr[hh, :, n * KEY_BLK:(n + 1) * KEY_BLK] * a - mb)
            for c in range(KEY_BLK // LANE):
                den = den + p[:, c * LANE:(c + 1) * LANE]
            o = o + _dot(p.astype(BF16), vv[r0:r0 + KEY_BLK, cols])
        o_ref[:, cols] = (o / jnp.sum(den, axis=-1, keepdims=True)).astype(BF16)


def _attention(qn, qr, kn, kr2, v, *, row0, n_seq, seq, tq, cache=None):
    nq = seq // tq
    grp = HEAD_GRP * LANE
    qblk = lambda b, h, i: row0 // tq + b * nq + i
    kblk = lambda b: row0 // seq + b
    in_specs = [pl.BlockSpec((tq, grp), lambda b, h, i: (qblk(b, h, i), h)),
                pl.BlockSpec((tq, grp // 2), lambda b, h, i: (qblk(b, h, i), h)),
                pl.BlockSpec((seq, grp), lambda b, h, i: (kblk(b), h)),
                pl.BlockSpec((seq, 2 * LANE), lambda b, h, i: (kblk(b), 0)),
                pl.BlockSpec((seq, grp), lambda b, h, i: (kblk(b), h))]
    args = [qn, qr, kn, kr2, v]
    if cache is not None:
        knc, kr2c, vc = cache
        in_specs += [pl.BlockSpec((PAST_LEN, grp), lambda b, h, i: (b, h)),
                     pl.BlockSpec((PAST_LEN, 2 * LANE), lambda b, h, i: (b, 0)),
                     pl.BlockSpec((PAST_LEN, grp), lambda b, h, i: (b, h))]
        args += [knc, kr2c, vc]
    return pl.pallas_call(
        functools.partial(_attn_kernel, has_cache=cache is not None),
        grid=(n_seq, HEADS // HEAD_GRP, nq),
        in_specs=in_specs,
        out_specs=pl.BlockSpec((tq, grp), lambda b, h, i: (b * nq + i, h)),
        out_shape=jax.ShapeDtypeStruct((n_seq * seq, HEADS * V_DIM), BF16),
        scratch_shapes=[pltpu.VMEM((HEAD_GRP, tq, seq + (PAST_LEN if cache is not None else 0)), F32)],
        compiler_params=_cparams(("arbitrary", "arbitrary", "arbitrary")),
        name="attn_latent" if cache is not None else "attn_context",
    )(*args)


def _router_kernel(op_ref, os_ref, xp_ref, xs_ref, gate1, g1, sh2, sc2, g2, wout, wr_hi, wr_lo, tri,
                   x3_ref, h_ref, ri_ref, rf_ref, cnt_ref, carry, *, prompt_steps):
    step = pl.program_id(0)

    @pl.when(step == 0)
    def _():
        carry[...] = jnp.zeros_like(carry)

    o_all = _stream_rows(op_ref, os_ref, prompt_steps)
    x_all = _stream_rows(xp_ref, xs_ref, prompt_steps)
    sub = x_all.shape[0] // ROUTER_SUB
    dg = lambda a, b: lax.dot_general(a, b, NT_DIMS, preferred_element_type=F32)
    logits = []
    for r in range(ROUTER_SUB):
        rows = slice(r * sub, (r + 1) * sub)
        x3 = _gated(x_all[rows], _dot(o_all[rows], wout[...]), g1[...], gate1[0])
        x3_ref[rows, :] = x3
        h = _adaln(x3, g2[...], sc2[0], sh2[0])
        _store_row_tiles(h_ref, lambda cols, h=h: h[:, cols], r * sub, sub)
        h_hi = h.astype(BF16)
        h_lo = (h - h_hi.astype(F32)).astype(BF16)
        logits.append(dg(wr_hi[...], h_hi) + dg(wr_hi[...], h_lo) + dg(wr_lo[...], h_hi))
    lg = jnp.concatenate(logits, axis=1)
    eidx = lax.broadcasted_iota(I32, lg.shape, 0).astype(F32)
    m1 = jnp.max(lg, axis=0, keepdims=True)
    i1 = jnp.min(jnp.where(lg == m1, eidx, float(N_EXP)), axis=0, keepdims=True)
    sel1 = eidx == i1
    lg2 = jnp.where(sel1, -jnp.inf, lg)
    m2 = jnp.max(lg2, axis=0, keepdims=True)
    i2 = jnp.min(jnp.where(lg2 == m2, eidx, float(N_EXP)), axis=0, keepdims=True)
    sel2 = eidx == i2
    e = jnp.exp(m2 - m1)
    w1 = 1.0 / (1.0 + e)
    w2 = e / (1.0 + e)
    picked = jnp.where(sel1 | sel2, 1.0, 0.0)
    rank = _dot(picked.astype(BF16), tri[...]) + carry[:, 0:1]
    r1 = jnp.sum(jnp.where(sel1, rank, 0.0), axis=0, keepdims=True)
    r2 = jnp.sum(jnp.where(sel2, rank, 0.0), axis=0, keepdims=True)
    carry[...] = carry[...] + jnp.sum(picked, axis=1, keepdims=True)
    cnt_ref[...] = carry[...]
    ri_ref[...] = jnp.where(eidx == 0.0, i1, jnp.where(eidx == 1.0, i2, jnp.where(eidx == 2.0, r1, r2))).astype(I32)
    rf_ref[...] = jnp.where(eidx == 0.0, w1, w2)


def _attn_out_router(o_p, o_s, xp, xs, mods, g1, g2, wout, wr_hi, wr_lo):
    tm = 512
    grp = _grp_bm(tm)
    n_p = NP // tm
    tri = jnp.asarray(np.triu(np.ones((tm, tm), np.float32), 1), BF16)
    row = lambda w: pl.BlockSpec((tm, w), lambda i: (i, 0))
    col = pl.BlockSpec((N_EXP, tm), lambda i: (0, i))
    return pl.pallas_call(
        functools.partial(_router_kernel, prompt_steps=n_p),
        grid=(N // tm,),
        in_specs=_stream_specs(tm, HEADS * V_DIM) + _stream_specs(tm, D) + [_mod_spec(2, grp), _full((1, D)),
                  _mod_spec(3, grp), _mod_spec(4, grp), _full((1, D)),
                  _full((HEADS * V_DIM, D)), _full((N_EXP, D)), _full((N_EXP, D)), _full((tm, tm))],
        out_specs=[row(D), pl.BlockSpec((tm * ROW_TILE, LANE), lambda i: (i, 0)), col, col, _full((N_EXP, LANE))],
        out_shape=[jax.ShapeDtypeStruct((N, D), F32), jax.ShapeDtypeStruct((N * ROW_TILE, LANE), F32),
                   jax.ShapeDtypeStruct((N_EXP, N), I32), jax.ShapeDtypeStruct((N_EXP, N), F32),
                   jax.ShapeDtypeStruct((N_EXP, LANE), F32)],
        scratch_shapes=[pltpu.VMEM((N_EXP, LANE), F32)],
        compiler_params=_cparams(("arbitrary",)),
        name="attn_out_router",
    )(o_p, o_s, xp, xs, mods, g1, mods, mods, g2, wout, wr_hi, wr_lo, tri)


def _tile_rows(r):
    return r * ROW_TILE if isinstance(r, int) else pl.multiple_of(r * ROW_TILE, ROW_TILE)


def _moe_kernel(te, nu, src0_ref, src1_ref, dstp_ref, dstc_ref, h_hbm, wgu, wd, y_hbm,
                hsbuf, ybuf, hbf, acc, sem_g, sem_s):
    i = pl.program_id(0)
    last = pl.num_programs(0) - 1
    cur = i % 2
    nxt = 1 - cur
    buf_rows = TM_E * ROW_TILE

    def gather(idx_ref, r, slot):
        i_src = pl.multiple_of(idx_ref[0, 0, r] * ROW_TILE, ROW_TILE)
        return pltpu.make_async_copy(h_hbm.at[pl.ds(i_src, ROW_TILE), :],
                                     hsbuf.at[slot, pl.ds(_tile_rows(r), ROW_TILE), :], sem_g.at[slot])

    def scatter(idx_ref, r, slot):
        i_dst = pl.multiple_of(idx_ref[0, 0, r] * ROW_TILE, ROW_TILE)
        return pltpu.make_async_copy(ybuf.at[slot, pl.ds(_tile_rows(r), ROW_TILE), :],
                                     y_hbm.at[pl.ds(i_dst, ROW_TILE), :], sem_s.at[slot])

    def wait_gather(slot):
        pltpu.make_async_copy(h_hbm.at[pl.ds(0, buf_rows), :], hsbuf.at[slot], sem_g.at[slot]).wait()

    def wait_scatter(slot):
        pltpu.make_async_copy(ybuf.at[slot], y_hbm.at[pl.ds(0, buf_rows), :], sem_s.at[slot]).wait()

    def for_rows(fn):
        def body(g, carry):
            for k in range(ROW_TILE):
                fn(g * ROW_TILE + k, k % 2)
            return carry
        lax.fori_loop(0, TM_E // ROW_TILE, body, 0)

    def side_traffic(r, queue):
        gather(src1_ref, r, nxt).start(priority=queue)
        scatter(dstp_ref, r, nxt).start(priority=queue)

    @pl.when(i == 0)
    def _():
        ybuf[1] = jnp.zeros((buf_rows, LANE), F32)
        for_rows(lambda r, queue: gather(src0_ref, r, 0).start(priority=queue))

    wait_gather(cur)

    @pl.when(i < nu[0])
    def _():
        for c in range(ROW_TILE):
            hbf[:, c * LANE:(c + 1) * LANE] = hsbuf[cur, pl.ds(c, TM_E, stride=ROW_TILE), :].astype(BF16)
        w_gu = wgu.at[0]
        w_d = wd.at[0]
        _swiglu_block(hbf, w_gu, w_d, acc, 0)
        per_iter = TM_E // SIDE_ITERS

        def body(t, carry):
            _swiglu_block(hbf, w_gu, w_d, acc, 1 + t)
            for k in range(per_iter):
                side_traffic(t * per_iter + k, k % 2)
            return carry

        lax.fori_loop(0, SIDE_ITERS, body, 0)
        for j in range(1 + SIDE_ITERS, N_FF):
            _swiglu_block(hbf, w_gu, w_d, acc, j)
        _store_row_tiles(ybuf.at[cur], lambda cols: acc[:, cols])

    @pl.when(i >= nu[0])
    def _():
        for_rows(side_traffic)
        ybuf[cur] = jnp.zeros((buf_rows, LANE), F32)

    wait_scatter(nxt)

    @pl.when(i == last)
    def _():
        for_rows(lambda r, queue: scatter(dstc_ref, r, cur).start(priority=queue))
        wait_scatter(cur)
        wait_gather(nxt)


def _moe_experts(tile_expert, n_used, src_tbl, dst_tbl, h, wgu, wd):
    smem = lambda off: pl.BlockSpec((1, 1, TM_E), lambda i, te, nu: (i + off, 0, 0), memory_space=pltpu.SMEM)
    grid_spec = pltpu.PrefetchScalarGridSpec(
        num_scalar_prefetch=2,
        grid=(N_TILES,),
        in_specs=[smem(0), smem(1), smem(0), smem(1),
                  pl.BlockSpec(memory_space=pl.ANY),
                  pl.BlockSpec((1,) + wgu.shape[1:], lambda i, te, nu: (te[i], 0, 0)),
                  pl.BlockSpec((1,) + wd.shape[1:], lambda i, te, nu: (te[i], 0, 0))],
        out_specs=pl.BlockSpec(memory_space=pl.ANY),
        scratch_shapes=[pltpu.VMEM((2, TM_E * ROW_TILE, LANE), F32), pltpu.VMEM((2, TM_E * ROW_TILE, LANE), F32),
                        pltpu.VMEM((TM_E, D), BF16), pltpu.VMEM((TM_E, D), F32),
                        pltpu.SemaphoreType.DMA((2,)), pltpu.SemaphoreType.DMA((2,))],
    )
    return pl.pallas_call(
        _moe_kernel,
        grid_spec=grid_spec,
        out_shape=jax.ShapeDtypeStruct(((2 * N + DUMP_ROWS) * ROW_TILE, LANE), F32),
        compiler_params=_cparams(("arbitrary",)),
        name="moe_experts",
    )(tile_expert, n_used, src_tbl, src_tbl, dst_tbl, dst_tbl, h, wgu, wd)


def _combine_kernel(y1_ref, y2_ref, x_ref, w1_ref, w2_ref, gate2, g3, op_ref, os_ref, *, prompt_steps):
    rows = x_ref.shape[0]
    w1 = w1_ref[...]
    w2 = w2_ref[...]
    f = jnp.concatenate([w1 * _load_row_tiles(y1_ref, c, rows) + w2 * _load_row_tiles(y2_ref, c, rows)
                         for c in range(ROW_TILE)], axis=1)
    out = _gated(x_ref[...], f, g3[...], gate2[0])
    step = pl.program_id(0)

    @pl.when(step < prompt_steps)
    def _():
        op_ref[...] = out

    @pl.when(step >= prompt_steps)
    def _():
        os_ref[...] = out


def _moe_combine(y, x, w1, w2, mods, g3):
    tm = 512
    grp = _grp_bm(tm)
    nb = N // tm
    n_p = NP // tm
    return pl.pallas_call(
        functools.partial(_combine_kernel, prompt_steps=n_p),
        grid=(nb,),
        in_specs=[pl.BlockSpec((tm * ROW_TILE, LANE), lambda i: (i, 0)),
                  pl.BlockSpec((tm * ROW_TILE, LANE), lambda i: (nb + i, 0)),
                  pl.BlockSpec((tm, D), lambda i: (i, 0)),
                  pl.BlockSpec((tm, 1), lambda i: (i, 0)), pl.BlockSpec((tm, 1), lambda i: (i, 0)),
                  _mod_spec(5, grp), _full((1, D))],
        out_specs=[pl.BlockSpec((tm, D), lambda i: (jnp.minimum(i, n_p - 1), 0)),
                   pl.BlockSpec((tm, D), lambda i: (jnp.maximum(i - n_p, 0), 0))],
        out_shape=[jax.ShapeDtypeStruct((NP, D), F32), jax.ShapeDtypeStruct((NS, D), F32)],
        compiler_params=_cparams(("arbitrary",)),
        name="moe_combine",
    )(y, y, x, w1, w2, mods, g3)


def _block_diag(w):
    hh, a, b = w.shape
    eye = jnp.eye(hh, dtype=w.dtype)
    return jnp.einsum('hab,hk->hakb', w, eye).reshape(hh * a, hh * b)


def _s5_matrices(a_re, a_im, log_dt, b_re, b_im, c_re, c_im):
    dt = jnp.exp(log_dt)[:, None]
    mag = jnp.exp(a_re * dt)
    abr = mag * jnp.cos(a_im * dt)
    abi = mag * jnp.sin(a_im * dt)
    den = a_re * a_re + a_im * a_im
    cr = ((abr - 1.0) * a_re + abi * a_im) / den
    ci = (abi * a_re - (abr - 1.0) * a_im) / den
    bbr = cr[..., None] * b_re - ci[..., None] * b_im
    bbi = cr[..., None] * b_im + ci[..., None] * b_re
    hg = S5_G // 2
    eye = jnp.eye(hg, dtype=F32)
    bms, cms = [], []
    for j in range(2):
        sl = slice(j * hg, (j + 1) * hg)
        bd = lambda m: jnp.einsum('gpc,gh->gchp', m[sl], eye).reshape(hg * S5_CH, hg * S5_P)
        bms.append(jnp.concatenate([bd(bbr), bd(bbi)], axis=1))
        cd = lambda m: jnp.einsum('gcp,gh->gphc', m[sl], eye).reshape(hg * S5_P, hg * S5_CH)
        cms.append(jnp.concatenate([cd(c_re), cd(-c_im)], axis=0))
    return (jnp.stack(bms).astype(BF16), abr.reshape(1, S5_N), abi.reshape(1, S5_N),
            jnp.stack(cms).astype(BF16))


def _rope_tables(tm):
    rows = DEC_SEQ // GRID_W
    row = jnp.repeat(jnp.arange(rows, dtype=F32), GRID_W)
    col = jnp.tile(jnp.arange(GRID_W, dtype=F32), rows)
    nf = QK_ROPE // 4
    inv = ROPE_THETA ** (-jnp.arange(nf, dtype=F32) / nf)
    ang = jnp.concatenate([row[:, None] * inv, col[:, None] * inv], axis=-1)
    cos = jnp.repeat(jnp.cos(ang), 2, axis=-1)
    sin = jnp.stack([-jnp.sin(ang), jnp.sin(ang)], axis=-1).reshape(DEC_SEQ, QK_ROPE)
    ident = lambda t, one: jnp.concatenate([t, jnp.full((tm, t.shape[1]), one, F32)], axis=0)
    cos_q = ident(jnp.tile(cos, (1, HEADS)), 1.0)
    sin_q = ident(jnp.tile(sin, (1, HEADS)), 0.0)
    pad = lambda t, one: jnp.concatenate([t, jnp.full((DEC_SEQ, LANE - QK_ROPE), one, F32)], axis=1)
    return cos_q, sin_q, ident(pad(cos, 1.0), 1.0), ident(pad(sin, 0.0), 0.0)


def _group_states(prompt_state, sample_state):
    w = sample_state.shape[-1]
    return jnp.concatenate([prompt_state.reshape(2, SUB, w), sample_state.reshape(1, SUB, w)], axis=0)


def _layer_ab(xp, xs, m, ng, j, state_lru, state_s5_re, state_s5_im, p):
    xz, x = _ab_inproj(xp, xs, ng[0:1], m, p['ab_w_in'][j].astype(BF16))
    zeros = lambda w: jnp.zeros((BATCH, w), F32)
    per_dir = []
    for d in range(2):
        wg = jnp.concatenate([_block_diag(p['lru_wa'][j, d]), _block_diag(p['lru_wx'][j, d])], axis=1).astype(BF16)
        bg = jnp.concatenate([p['lru_ba'][j, d], p['lru_bx'][j, d]])[None]
        bm, ar, ai, cm = _s5_matrices(p['s5_a_re'][j, d], p['s5_a_im'][j, d], p['s5_log_dt'][j, d],
                                      p['s5_b_re'][j, d], p['s5_b_im'][j, d], p['s5_c_re'][j, d], p['s5_c_im'][j, d])
        h0l = _group_states(zeros(LRU_W), state_lru[:, j, d])
        h0r = _group_states(zeros(S5_N), state_s5_re[:, j, d].reshape(DEC_BATCH, S5_N))
        h0i = _group_states(zeros(S5_N), state_s5_im[:, j, d].reshape(DEC_BATCH, S5_N))
        per_dir.append((h0l, h0r, h0i, p['ab_conv_w'][j], p['ab_conv_b'][j][None], wg, bg,
                        p['lru_lambda'][j, d][None], bm, ar, ai, cm))
    (haf, yf, llf, lrf, lif), (hab, yr, llb, lrb, lib) = _ab_scan(xz, per_dir)
    x = _ab_out(haf, hab, yf, yr, xz, x, m, ng[1:2], p['s5_d'][j][None], p['s5_w_glu'][j].astype(BF16),
                p['s5_b_glu'][j][None], p['ab_w_out'][j].astype(BF16))
    streams = _ffn(x, m, ng[2:3], ng[3:4], p['ffn_w_gate_up'][j].astype(BF16), p['ffn_w_down'][j].astype(BF16))
    prompt = lambda f, b, w: jnp.stack([f[:2].reshape(BATCH, w), b[:2].reshape(BATCH, w)], axis=1)
    lru = prompt(llf, llb, LRU_W)
    s5r = prompt(lrf, lrb, S5_N).reshape(BATCH, 2, S5_G, S5_P)
    s5i = prompt(lif, lib, S5_N).reshape(BATCH, 2, S5_G, S5_P)
    return tuple(streams), lru, s5r, s5i


def _head_major(w, parts):
    k = w.shape[0]
    per_head = w.reshape(k, HEADS, -1)
    out, start = [], 0
    for width in parts:
        out.append(per_head[:, :, start:start + width].reshape(k, HEADS * width))
        start += width
    return jnp.concatenate(out, axis=1)


def _layer_mla_moe(xp, xs, m, ng, j, cache_kv_latent, cache_k_rope, p):
    w1 = jnp.concatenate([p['mla_w_in'][j], jnp.zeros((D, LANE - QK_ROPE), F32)], axis=1).astype(BF16)
    wuq = _head_major(p['mla_w_uq'][j], (QK_NOPE, QK_ROPE)).astype(BF16)
    wukv = _head_major(p['mla_w_ukv'][j], (QK_NOPE, V_DIM)).astype(BF16)
    tables = _rope_tables(MLA_TM)
    qn, qr, kn, v, kr2, ckv, krr = _mla_proj(xp, xs, ng[0:1], m, w1, p['mla_g_q'][j][None], p['mla_g_kv'][j][None],
                                              wuq, wukv, *tables)
    knc, vc = _cache_kv(cache_kv_latent[:, j].reshape(DEC_BATCH * PAST_LEN, KV_LORA), wukv)
    krc = cache_k_rope[:, j].reshape(DEC_BATCH * PAST_LEN, QK_ROPE)
    z = jnp.zeros_like(krc)
    kr2c = jnp.concatenate([krc, z, z, krc], axis=1).astype(BF16)
    o_p = _attention(qn, qr, kn, kr2, v, row0=0, n_seq=BATCH, seq=SEQ, tq=SEQ)
    o_s = _attention(qn, qr, kn, kr2, v, row0=NP, n_seq=DEC_BATCH, seq=DEC_SEQ, tq=1024, cache=(knc, kr2c, vc))
    wr_t = p['moe_w_router'][j].T
    wr_hi = wr_t.astype(BF16)
    wr_lo = (wr_t - wr_hi.astype(F32)).astype(BF16)
    x3, h, ri, rf, cnt = _attn_out_router(o_p, o_s, xp, xs, m, ng[1:2], ng[2:3], p['mla_w_out'][j].astype(BF16), wr_hi, wr_lo)
    counts = cnt[:, 0].astype(I32)
    padded = ((counts + TM_E - 1) // TM_E) * TM_E
    ends = jnp.cumsum(padded)
    offs = ends - padded
    pos1 = offs[ri[0]] + ri[2]
    pos2 = offs[ri[1]] + ri[3]
    pick_tok = jnp.arange(2 * N, dtype=I32)
    dest = jnp.full((P_ROWS,), -1, I32).at[jnp.concatenate([pos1, pos2])].set(pick_tok, unique_indices=True)
    is_pad = dest < 0
    pad_row = 2 * N + TM_E + jnp.cumsum(is_pad.astype(I32)) - 1
    src_tbl = jnp.concatenate([jnp.where(is_pad, 0, dest % N), jnp.zeros((TM_E,), I32)])
    dst_tbl = jnp.concatenate([2 * N + jnp.arange(TM_E, dtype=I32), jnp.where(is_pad, pad_row, dest)])
    n_used = (ends[-1] // TM_E).astype(I32)[None]
    tile_row = jnp.minimum(jnp.arange(N_TILES, dtype=I32), n_used - 1) * TM_E
    tile_expert = jnp.sum((tile_row[:, None] >= ends[None, :]).astype(I32), axis=1)
    y = _moe_experts(tile_expert, n_used, src_tbl.reshape(N_TILES + 1, 1, TM_E), dst_tbl.reshape(N_TILES + 1, 1, TM_E),
                     h, p['moe_w_gate_up'][j].astype(BF16), p['moe_w_down'][j].astype(BF16))
    xp, xs = _moe_combine(y, x3, rf[0][:, None], rf[1][:, None], m, ng[3:4])
    kv_new = ckv.reshape(BATCH, SEQ, KV_LORA)
    kr_new = krr[:, :QK_ROPE].reshape(BATCH, SEQ, QK_ROPE)
    return (xp.reshape(BATCH, SEQ, D), xs.reshape(DEC_BATCH, DEC_SEQ, D)), kv_new, kr_new


def kernel(x_prompt, x_sample, c, state_lru, state_s5_re, state_s5_im, cache_kv_latent, cache_k_rope, c_ctx, w_mod, b_mod, norm_gains, ab_w_in, ab_conv_w, ab_conv_b, lru_wa, lru_ba, lru_wx, lru_bx, lru_lambda, s5_a_re, s5_a_im, s5_log_dt, s5_b_re, s5_b_im, s5_c_re, s5_c_im, s5_d, s5_w_glu, s5_b_glu, ab_w_out, ffn_w_gate_up, ffn_w_down, mla_w_in, mla_g_q, mla_g_kv, mla_w_uq, mla_w_ukv, mla_w_out, moe_w_router, moe_w_gate_up, moe_w_down):
    p = dict(ab_w_in=ab_w_in, ab_conv_w=ab_conv_w, ab_conv_b=ab_conv_b, lru_wa=lru_wa, lru_ba=lru_ba,
             lru_wx=lru_wx, lru_bx=lru_bx, lru_lambda=lru_lambda, s5_a_re=s5_a_re, s5_a_im=s5_a_im,
             s5_log_dt=s5_log_dt, s5_b_re=s5_b_re, s5_b_im=s5_b_im, s5_c_re=s5_c_re, s5_c_im=s5_c_im,
             s5_d=s5_d, s5_w_glu=s5_w_glu, s5_b_glu=s5_b_glu, ab_w_out=ab_w_out, ffn_w_gate_up=ffn_w_gate_up,
             ffn_w_down=ffn_w_down, mla_w_in=mla_w_in, mla_g_q=mla_g_q, mla_g_kv=mla_g_kv, mla_w_uq=mla_w_uq,
             mla_w_ukv=mla_w_ukv, mla_w_out=mla_w_out, moe_w_router=moe_w_router, moe_w_gate_up=moe_w_gate_up,
             moe_w_down=moe_w_down)
    depth = w_mod.shape[0]
    cond = jnp.concatenate([c_ctx[None], c, jnp.zeros((2 * SUB - 1 - DEC_BATCH, D), F32)], axis=0)
    mod = _modulation(cond, w_mod, b_mod)
    ctx_tile = lambda l: jnp.broadcast_to(mod[l, 0:1], (SUB, 6 * D))
    streams = (x_prompt, x_sample)
    lru_l, s5r_l, s5i_l, kv_l, kr_l = [], [], [], [], []
    for layer in range(depth):
        j = layer // 2
        ng = norm_gains[layer]
        if layer % 2 == 0:
            m = jnp.stack([ctx_tile(layer), mod[layer, 1:1 + DEC_BATCH]])
            streams, lru, s5r, s5i = _layer_ab(*streams, m, ng, j, state_lru, state_s5_re, state_s5_im, p)
            lru_l.append(lru)
            s5r_l.append(s5r)
            s5i_l.append(s5i)
        else:
            lat = jnp.broadcast_to(mod[layer, 1:1 + DEC_BATCH, None, :], (DEC_BATCH, SUB, 6 * D))
            m = jnp.concatenate([ctx_tile(layer)[None], lat], axis=0)
            streams, kv_new, kr_new = _layer_mla_moe(streams[0].reshape(NP, D), streams[1].reshape(NS, D), m, ng, j,
                                                     cache_kv_latent, cache_k_rope, p)
            kv_l.append(kv_new)
            kr_l.append(kr_new)
    return (streams[0], streams[1],
            jnp.stack(lru_l, axis=1), jnp.stack(s5r_l, axis=1), jnp.stack(s5i_l, axis=1),
            jnp.stack(kv_l, axis=1), jnp.stack(kr_l, axis=1))
```

```python
import functools
import math

import numpy as np
import jax
import jax.numpy as jnp
from jax import lax
from jax.experimental import pallas as pl
from jax.experimental.pallas import tpu as pltpu

F32 = jnp.float32
BF16 = jnp.bfloat16
I32 = jnp.int32

D = 1024
BATCH, SEQ = 16, 256
DEC_BATCH, DEC_SEQ = 8, 2048
PAST_LEN = 256
GRID_W = 64
LRU_W = 512
LRU_HEADS = 8
LRU_C = 8.0
CONV_W = 4
S5_W = 512
S5_CH = 16
S5_G = 32
S5_P = 64
S5_N = S5_G * S5_P
HEADS = 8
QK_NOPE, QK_ROPE, V_DIM = 128, 64, 128
Q_LORA, KV_LORA = 384, 256
ROPE_THETA = 10000.0
D_FF = 2816
N_EXP = 8
EPS = 1e-6

NP = BATCH * SEQ
NS = DEC_BATCH * DEC_SEQ
N = NP + NS
SUB = 8
LANE = 128
ROW_TILE = D // LANE
T_CHUNK = 64
R_CHUNK = T_CHUNK * SUB
FF_BLK = 256
KEY_BLK = 256
HEAD_GRP = 4
MLA_TM = 512
MLA_SUB = 2
ROUTER_SUB = 1
ABOUT_SUB = 1
INPROJ_SUB = 2
INPROJ_T = 64
FFN_T = 128
SIDE_ITERS = 4
N_FF = D_FF // FF_BLK
TM_E = 512
P_ROWS = 2 * N + N_EXP * TM_E
N_TILES = P_ROWS // TM_E
DUMP_ROWS = P_ROWS - 2 * N + TM_E
VMEM_LIMIT = 56 * 1024 * 1024

NT_DIMS = (((1,), (1,)), ((), ()))


def _cparams(sem):
    return pltpu.CompilerParams(dimension_semantics=sem, vmem_limit_bytes=VMEM_LIMIT)


def _dot(a, b):
    return jnp.dot(a, b, preferred_element_type=F32)


def _sigmoid(x):
    return 1.0 / (1.0 + jnp.exp(-x))


def _neg_expm1_2x(log_a, a):
    series = -2.0 * log_a * (1.0 + log_a * (1.0 + log_a * (2.0 / 3.0) * (1.0 + log_a * 0.5)))
    return jnp.where(log_a > -0.01, series, (1.0 - a) * (1.0 + a))


def _sqrt_nonneg(v):
    return jnp.where(v > 0.0, v * lax.rsqrt(v), 0.0)


def _gelu(x):
    k = math.sqrt(2.0 / math.pi)
    half = 0.5 * x
    return half + half * jnp.tanh(x * (k + (k * 0.044715) * (x * x)))


def _rms(x, g):
    ms = jnp.mean(x * x, axis=-1, keepdims=True)
    return x * lax.rsqrt(ms + EPS) * g


def _rows8(y, fn):
    r, c = y.shape
    return fn(y.reshape(r // SUB, SUB, c)).reshape(r, c)


def _adaln(x, g, scale, shift):
    return _rows8(_rms(x, g), lambda y: y * (1.0 + scale)[None] + shift[None])


def _gated(x, y, g, gate):
    return x + _rows8(_rms(y, g), lambda z: z * gate[None])


def _store_row_tiles(ref, piece, row0=0, rows=None):
    rows = ref.shape[0] // ROW_TILE if rows is None else rows
    for c in range(ROW_TILE):
        ref[pl.ds(row0 * ROW_TILE + c, rows, stride=ROW_TILE), :] = piece(slice(c * LANE, (c + 1) * LANE))


def _load_row_tiles(ref, c, rows):
    return ref[pl.ds(c, rows, stride=ROW_TILE), :]


def _full(shape):
    nd = len(shape)
    return pl.BlockSpec(shape, lambda *_: (0,) * nd)


def _mod_spec(k, grp):
    return pl.BlockSpec((1, SUB, D), lambda i, *_: (grp(i), 0, k))


def _grp_tm(tm):
    return lambda i: (i * tm >= NP).astype(I32)


def _grp_bm(tm):
    return lambda i: jnp.where(i * tm < NP, 0, 1 + (i * tm - NP) // DEC_SEQ)


def _mod_kernel(c_ref, w_ref, b_ref, o_ref):
    c = c_ref[...]
    s = c * _sigmoid(c)
    o_ref[0] = _dot(s.astype(BF16), w_ref[0].astype(BF16)) + b_ref[0]


def _modulation(cond, w_mod, b_mod):
    depth = w_mod.shape[0]
    rows = cond.shape[0]
    return pl.pallas_call(
        _mod_kernel,
        grid=(depth, 6),
        in_specs=[_full((rows, D)),
                  pl.BlockSpec((1, D, D), lambda l, j: (l, 0, j)),
                  pl.BlockSpec((1, 1, D), lambda l, j: (l, 0, j))],
        out_specs=pl.BlockSpec((1, rows, D), lambda l, j: (l, 0, j)),
        out_shape=jax.ShapeDtypeStruct((depth, rows, 6 * D), F32),
        compiler_params=_cparams(("arbitrary", "arbitrary")),
        name="modulation",
    )(cond, w_mod, b_mod.reshape(depth, 1, 6 * D))


def _time_major_copies(step, t_steps, xp_hbm, xs_hbm, buf, sem, to_hbm):
    p_steps = (BATCH // SUB) * (SEQ // t_steps)
    per_group = SEQ // t_steps

    def issue(hbm, seq0, t0):
        t0 = pl.multiple_of(t0, t_steps)
        for b in range(SUB):
            rows = hbm.at[seq0 + b, pl.ds(t0, t_steps), :]
            tile = buf.at[:, b, :]
            (pltpu.make_async_copy(tile, rows, sem) if to_hbm else pltpu.make_async_copy(rows, tile, sem)).start()

    @pl.when(step < p_steps)
    def _():
        issue(xp_hbm, (step // per_group) * SUB, (step % per_group) * t_steps)

    @pl.when(step >= p_steps)
    def _():
        issue(xs_hbm, 0, (step - p_steps) * t_steps)


def _time_major_wait(t_steps, xs_hbm, buf, sem, to_hbm):
    for b in range(SUB):
        rows = xs_hbm.at[0, pl.ds(0, t_steps), :]
        tile = buf.at[:, b, :]
        (pltpu.make_async_copy(tile, rows, sem) if to_hbm else pltpu.make_async_copy(rows, tile, sem)).wait()


def _inproj_kernel(xp_hbm, xs_hbm, g_ref, sh_ref, sc_ref, w_ref, o_ref, xtm_ref, buf, sem):
    step = pl.program_id(0)
    slot = step % 2

    @pl.when(step == 0)
    def _():
        _time_major_copies(step, INPROJ_T, xp_hbm, xs_hbm, buf.at[0], sem.at[0], False)

    @pl.when(step + 1 < pl.num_programs(0))
    def _():
        _time_major_copies(step + 1, INPROJ_T, xp_hbm, xs_hbm, buf.at[1 - slot], sem.at[1 - slot], False)

    _time_major_wait(INPROJ_T, xs_hbm, buf.at[slot], sem.at[slot], False)
    sub_t = INPROJ_T // INPROJ_SUB
    for r in range(INPROJ_SUB):
        rows = slice(r * sub_t * SUB, (r + 1) * sub_t * SUB)
        x = buf[slot, r * sub_t:(r + 1) * sub_t].reshape(sub_t * SUB, D)
        xtm_ref[rows, :] = x
        h = _adaln(x, g_ref[...], sc_ref[0], sh_ref[0])
        o_ref[rows, :] = _dot(h.astype(BF16), w_ref[...])


def _ab_inproj(xp, xs, gain, mods, w_in):
    tm = INPROJ_T * SUB
    nout = w_in.shape[1]
    grp = _grp_tm(tm)
    return pl.pallas_call(
        _inproj_kernel,
        grid=(N // tm,),
        in_specs=[pl.BlockSpec(memory_space=pl.ANY), pl.BlockSpec(memory_space=pl.ANY),
                  _full((1, D)),
                  _mod_spec(0, grp), _mod_spec(1, grp),
                  _full((D, nout))],
        out_specs=[pl.BlockSpec((tm, nout), lambda i: (i, 0)), pl.BlockSpec((tm, D), lambda i: (i, 0))],
        out_shape=[jax.ShapeDtypeStruct((N, nout), F32), jax.ShapeDtypeStruct((N, D), F32)],
        scratch_shapes=[pltpu.VMEM((2, INPROJ_T, SUB, D), F32), pltpu.SemaphoreType.DMA((2,))],
        compiler_params=_cparams(("arbitrary",)),
        name="ab_inproj",
    )(xp, xs, gain, mods, mods, w_in)


def _scan_table(reverse):
    cols = []
    groups = [(0, SEQ // T_CHUNK, 0), (1, SEQ // T_CHUNK, SEQ // T_CHUNK),
              (2, DEC_SEQ // T_CHUNK, NP // R_CHUNK)]
    for g, nc, base in groups:
        order = range(nc - 1, -1, -1) if reverse else range(nc)
        for k, c in enumerate(order):
            cols.append((base + c, g, int(k == 0), int(c > 0), int(c < nc - 1)))
    return np.asarray(cols, np.int32).T.copy()


N_SCAN_IN, N_SCAN_OUT, N_SCAN_SCRATCH = 16, 5, 8
SCAN_TBL_ROWS = 5


def _scan_kernel(tbl, *refs):
    ins = [refs[d * N_SCAN_IN:(d + 1) * N_SCAN_IN] for d in range(2)]
    o0 = 2 * N_SCAN_IN
    outs = [refs[o0 + d * N_SCAN_OUT:o0 + (d + 1) * N_SCAN_OUT] for d in range(2)]
    s0 = o0 + 2 * N_SCAN_OUT
    scr = [refs[s0 + d * N_SCAN_SCRATCH:s0 + (d + 1) * N_SCAN_SCRATCH] for d in range(2)]
    s = pl.program_id(0)
    for d in range(2):
        h0l_ref, h0r_ref, h0i_ref = ins[d][4:7]
        hl, sre, sim = scr[d][5:8]

        @pl.when(tbl[d * SCAN_TBL_ROWS + 2, s] == 1)
        def _(h0l_ref=h0l_ref, h0r_ref=h0r_ref, h0i_ref=h0i_ref, hl=hl, sre=sre, sim=sim):
            hl[...] = h0l_ref[0]
            sre[...] = h0r_ref[0]
            sim[...] = h0i_ref[0]

    fwd, bwd = [_ScanChunk(tbl, d * SCAN_TBL_ROWS, ins[d], outs[d], scr[d], reverse=(d == 1)) for d in range(2)]
    fwd.s5_project()
    fwd.conv()
    fwd.gates()
    bwd.s5_project()
    fwd.s5_recurrence()
    bwd.conv()
    fwd.s5_readout()
    bwd.gates()
    fwd.lru_coefficients()
    fwd.lru_recurrence()
    bwd.s5_recurrence()
    bwd.s5_readout()
    bwd.lru_coefficients()
    bwd.lru_recurrence()


class _ScanChunk:
    def __init__(self, tbl, row0, ins, outs, scratch, reverse):
        (self.xa_ref, self.xp_ref, self.xn_ref, self.xb_ref, _, _, _, self.cw_ref, self.cb_ref, self.wg_ref,
         self.bg_ref, self.lam_ref, self.bm_ref, self.ar_ref, self.ai_ref, self.cm_ref) = ins
        self.ha_ref, self.y_ref, self.ll_ref, self.lr_ref, self.li_ref = outs
        self.ext, self.abuf, self.bbuf, self.hre, self.him, self.hl, self.sre, self.sim = scratch
        step = pl.program_id(0)
        self.has_prev = tbl[row0 + 3, step] == 1
        self.has_next = tbl[row0 + 4, step] == 1
        self.order = range(T_CHUNK - 1, -1, -1) if reverse else range(T_CHUNK)
        self.half = S5_N // 2

    def s5_project(self):
        ub = self.xb_ref[...].astype(BF16)
        half = self.half
        for j in range(2):
            bu = _dot(ub[:, j * 256:(j + 1) * 256], self.bm_ref[j])
            self.hre[:, j * half:(j + 1) * half] = bu[:, :half]
            self.him[:, j * half:(j + 1) * half] = bu[:, half:]

    def conv(self):
        ext = self.ext
        ext[0:2 * SUB] = jnp.where(self.has_prev, self.xp_ref[...], 0.0)
        ext[2 * SUB:2 * SUB + R_CHUNK] = self.xa_ref[...]
        ext[2 * SUB + R_CHUNK:3 * SUB + R_CHUNK] = jnp.where(self.has_next, self.xn_ref[...], 0.0)
        xa = self.cb_ref[...] + self.cw_ref[0:1] * ext[0:R_CHUNK]
        for k in range(1, CONV_W):
            xa = xa + self.cw_ref[k:k + 1] * ext[k * SUB:k * SUB + R_CHUNK]
        self.xa = xa

    def gates(self):
        self.gz = _dot(self.xa.astype(BF16), self.wg_ref[...]) + self.bg_ref[...]

    def s5_recurrence(self):
        cblk = 4 * LANE
        for cb in range(S5_N // cblk):
            cols = slice(cb * cblk, (cb + 1) * cblk)
            ar = jnp.broadcast_to(self.ar_ref[:, cols], (SUB, cblk))
            ai = jnp.broadcast_to(self.ai_ref[:, cols], (SUB, cblk))
            hr = self.sre[:, cols]
            hi = self.sim[:, cols]
            for t in self.order:
                rows = slice(t * SUB, (t + 1) * SUB)
                nr = ar * hr - ai * hi + self.hre[rows, cols]
                ni = ar * hi + ai * hr + self.him[rows, cols]
                hr, hi = nr, ni
                self.hre[rows, cols] = hr
                self.him[rows, cols] = hi
            self.sre[:, cols] = hr
            self.sim[:, cols] = hi
        self.lr_ref[0] = self.sre[...]
        self.li_ref[0] = self.sim[...]

    def s5_readout(self):
        half = self.half
        for j in range(2):
            hc = jnp.concatenate([self.hre[:, j * half:(j + 1) * half], self.him[:, j * half:(j + 1) * half]],
                                 axis=1).astype(BF16)
            self.y_ref[:, j * 256:(j + 1) * 256] = _dot(hc, self.cm_ref[j])

    def lru_coefficients(self):
        r = 0.5 * jnp.tanh(0.5 * self.gz[:, :LRU_W]) + 0.5
        i = 0.5 * jnp.tanh(0.5 * self.gz[:, LRU_W:]) + 0.5
        lam = self.lam_ref[...]
        log_sig = jnp.minimum(lam, 0.0) - jnp.log1p(jnp.exp(-jnp.abs(lam)))
        log_a = LRU_C * r * log_sig
        a = jnp.exp(log_a)
        self.abuf[...] = a
        self.bbuf[...] = _sqrt_nonneg(_neg_expm1_2x(log_a, a)) * (i * self.xa)

    def lru_recurrence(self):
        h = self.hl[...]
        for t in self.order:
            rows = slice(t * SUB, (t + 1) * SUB)
            h = self.abuf[rows] * h + self.bbuf[rows]
            self.ha_ref[rows, :] = h
        self.hl[...] = h
        self.ll_ref[0] = h


def _ab_scan(xz, per_dir):
    tbl = jnp.asarray(np.concatenate([_scan_table(False), _scan_table(True)], axis=0))
    n_steps = tbl.shape[1]
    in_specs, out_specs, scratch, out_shape, args = [], [], [], [], []
    for d in range(2):
        blk = lambda s, t, d=d: t[d * SCAN_TBL_ROWS, s]
        grp = lambda s, t, d=d: t[d * SCAN_TBL_ROWS + 1, s]
        state_spec = lambda w, grp=grp: pl.BlockSpec((1, SUB, w), lambda s, t: (grp(s, t), 0, 0))
        const = lambda shape: pl.BlockSpec(shape, lambda s, t: (0,) * len(shape))
        in_specs += [
            pl.BlockSpec((R_CHUNK, LRU_W), lambda s, t, blk=blk: (blk(s, t), 0)),
            pl.BlockSpec((2 * SUB, LRU_W),
                         lambda s, t, blk=blk: (jnp.maximum(blk(s, t) * (T_CHUNK // 2) - 1, 0), 0)),
            pl.BlockSpec((SUB, LRU_W),
                         lambda s, t, blk=blk: (jnp.minimum((blk(s, t) + 1) * T_CHUNK, N // SUB - 1), 0)),
            pl.BlockSpec((R_CHUNK, S5_W), lambda s, t, blk=blk: (blk(s, t), 2)),
            state_spec(LRU_W), state_spec(S5_N), state_spec(S5_N),
            const((CONV_W, LRU_W)), const((1, LRU_W)),
            const((LRU_W, 2 * LRU_W)), const((1, 2 * LRU_W)), const((1, LRU_W)),
            const((2, 256, S5_N)), const((1, S5_N)), const((1, S5_N)), const((2, S5_N, 256)),
        ]
        out_specs += [
            pl.BlockSpec((R_CHUNK, LRU_W), lambda s, t, blk=blk: (blk(s, t), 0)),
            pl.BlockSpec((R_CHUNK, S5_W), lambda s, t, blk=blk: (blk(s, t), 0)),
            state_spec(LRU_W), state_spec(S5_N), state_spec(S5_N),
        ]
        scratch += [
            pltpu.VMEM((R_CHUNK + 3 * SUB, LRU_W), F32),
            pltpu.VMEM((R_CHUNK, LRU_W), F32), pltpu.VMEM((R_CHUNK, LRU_W), F32),
            pltpu.VMEM((R_CHUNK, S5_N), F32), pltpu.VMEM((R_CHUNK, S5_N), F32),
            pltpu.VMEM((SUB, LRU_W), F32), pltpu.VMEM((SUB, S5_N), F32), pltpu.VMEM((SUB, S5_N), F32),
        ]
        out_shape += [jax.ShapeDtypeStruct((N, LRU_W), F32), jax.ShapeDtypeStruct((N, S5_W), F32),
                      jax.ShapeDtypeStruct((3, SUB, LRU_W), F32),
                      jax.ShapeDtypeStruct((3, SUB, S5_N), F32), jax.ShapeDtypeStruct((3, SUB, S5_N), F32)]
        args += [xz, xz, xz, xz, *per_dir[d]]
    grid_spec = pltpu.PrefetchScalarGridSpec(num_scalar_prefetch=1, grid=(n_steps,), in_specs=in_specs,
                                             out_specs=out_specs, scratch_shapes=scratch)
    outs = pl.pallas_call(
        _scan_kernel,
        grid_spec=grid_spec,
        out_shape=out_shape,
        compiler_params=_cparams(("arbitrary",)),
        name="ab_scan",
    )(tbl, *args)
    return outs[:N_SCAN_OUT], outs[N_SCAN_OUT:]


def _about_kernel(haf, hab, yf, yr, ga, xb, x_ref, gate, g1, d_ref, wglu, bglu, wout, o_ref):
    sub = x_ref.shape[0] // ABOUT_SUB
    for r in range(ABOUT_SUB):
        rows = slice(r * sub, (r + 1) * sub)
        ya = (haf[rows, :] + hab[rows, :]) * _gelu(ga[rows, :])
        yb0 = _gelu(yf[rows, :] + yr[rows, :] + d_ref[...] * xb[rows, :])
        half_yb0 = 0.5 * yb0
        yb = half_yb0 + half_yb0 * jnp.tanh(0.5 * (_dot(yb0.astype(BF16), wglu[...]) + bglu[...]))
        out = _dot(ya.astype(BF16), wout[0:LRU_W]) + _dot(yb.astype(BF16), wout[LRU_W:LRU_W + S5_W])
        o_ref[rows, :] = _gated(x_ref[rows, :], out, g1[...], gate[0])


def _ab_out(haf, hab, yf, yr, xz, x, mods, g1, s5_d, wglu, bglu, wout):
    tm = 512
    grp = _grp_tm(tm)
    half = lambda c: pl.BlockSpec((tm, LRU_W), lambda i: (i, c))
    return pl.pallas_call(
        _about_kernel,
        grid=(N // tm,),
        in_specs=[half(0), half(0), half(0), half(0), half(1), half(2),
                  pl.BlockSpec((tm, D), lambda i: (i, 0)),
                  _mod_spec(2, grp), _full((1, D)), _full((1, S5_W)),
                  _full((S5_W, S5_W)), _full((1, S5_W)), _full((LRU_W + S5_W, D))],
        out_specs=pl.BlockSpec((tm, D), lambda i: (i, 0)),
        out_shape=jax.ShapeDtypeStruct((N, D), F32),
        compiler_params=_cparams(("arbitrary",)),
        name="ab_out",
    )(haf, hab, yf, yr, xz, xz, x, mods, g1, s5_d, wglu, bglu, wout)


def _swiglu_block(hbf, wgu, wd, acc, j):
    static = isinstance(j, int)
    blk = lambda start: pl.ds(start if static else pl.multiple_of(start, FF_BLK), FF_BLK)
    h = hbf[...]
    g = _dot(h, wgu[:, blk(j * FF_BLK)])
    u = _dot(h, wgu[:, blk(D_FF + j * FF_BLK)])
    act = (g * _sigmoid(g)) * u
    part = _dot(act.astype(BF16), wd[blk(j * FF_BLK), :])
    if static and j == 0:
        acc[...] = part
    else:
        acc[...] += part


def _ffn_kernel(x_ref, sh, sc, gt, g2, g3, wgu, wd, op_hbm, os_hbm, hbf, acc, obuf, sem):
    step = pl.program_id(0)
    slot = step % 2
    hbf[...] = _adaln(x_ref[...], g2[...], sc[0], sh[0]).astype(BF16)
    for j in range(N_FF):
        _swiglu_block(hbf, wgu, wd, acc, j)
    obuf[slot] = _gated(x_ref[...], acc[...], g3[...], gt[0]).reshape(FFN_T, SUB, D)
    _time_major_copies(step, FFN_T, op_hbm, os_hbm, obuf.at[slot], sem.at[slot], True)

    @pl.when(step > 0)
    def _():
        _time_major_wait(FFN_T, os_hbm, obuf.at[1 - slot], sem.at[1 - slot], True)

    @pl.when(step == pl.num_programs(0) - 1)
    def _():
        _time_major_wait(FFN_T, os_hbm, obuf.at[slot], sem.at[slot], True)


def _ffn(x, mods, g2, g3, wgu, wd):
    tm = FFN_T * SUB
    grp = _grp_tm(tm)
    once = lambda shape: pl.BlockSpec(shape, lambda i: (0,) * len(shape), pipeline_mode=pl.Buffered(1))
    return pl.pallas_call(
        _ffn_kernel,
        grid=(N // tm,),
        in_specs=[pl.BlockSpec((tm, D), lambda i: (i, 0)),
                  _mod_spec(3, grp), _mod_spec(4, grp), _mod_spec(5, grp),
                  _full((1, D)), _full((1, D)),
                  once(wgu.shape), once(wd.shape)],
        out_specs=[pl.BlockSpec(memory_space=pl.ANY), pl.BlockSpec(memory_space=pl.ANY)],
        out_shape=[jax.ShapeDtypeStruct((BATCH, SEQ, D), F32), jax.ShapeDtypeStruct((DEC_BATCH, DEC_SEQ, D), F32)],
        scratch_shapes=[pltpu.VMEM((tm, D), BF16), pltpu.VMEM((tm, D), F32),
                        pltpu.VMEM((2, FFN_T, SUB, D), F32), pltpu.SemaphoreType.DMA((2,))],
        compiler_params=_cparams(("arbitrary",)),
        name="ffn",
    )(x, mods, mods, mods, g2, g3, wgu, wd)


def _pair_swap(x):
    outs = []
    for c in range(x.shape[1] // LANE):
        xc = x[:, c * LANE:(c + 1) * LANE]
        even = lax.broadcasted_iota(I32, xc.shape, 1) % 2 == 0
        outs.append(jnp.where(even, pltpu.roll(xc, LANE - 1, 1), pltpu.roll(xc, 1, 1)))
    return outs[0] if len(outs) == 1 else jnp.concatenate(outs, axis=1)


def _stream_specs(tm, width):
    n_p = NP // tm
    return [pl.BlockSpec((tm, width), lambda i: (jnp.minimum(i, n_p - 1), 0)),
            pl.BlockSpec((tm, width), lambda i: (jnp.maximum(i - n_p, 0), 0))]


def _stream_rows(p_ref, s_ref, prompt_steps):
    return jnp.where(pl.program_id(0) < prompt_steps, p_ref[...], s_ref[...])


def _mlaproj_kernel(xp_ref, xs_ref, g0, sh, sc, w1, gq, gkv, wuq, wukv, cq_ref, sq_ref, ck_ref, sk_ref,
                    qn_ref, qr_ref, kn_ref, v_ref, kr2_ref, ckv_ref, krr_ref, *, prompt_steps):
    x = _stream_rows(xp_ref, xs_ref, prompt_steps)
    sub = x.shape[0] // MLA_SUB
    cache_rows = []
    for r in range(MLA_SUB):
        rows = slice(r * sub, (r + 1) * sub)
        h = _adaln(x[rows], g0[...], sc[0], sh[0])
        dn = _dot(h.astype(BF16), w1[...])
        cq = _rms(dn[:, :Q_LORA], gq[...])
        ckv = _rms(dn[:, Q_LORA:Q_LORA + KV_LORA], gkv[...])
        krp = dn[:, Q_LORA + KV_LORA:]
        cache_rows.append((rows, ckv, krp))
        q = _dot(cq.astype(BF16), wuq[...])
        qn_ref[rows, :] = q[:, :HEADS * QK_NOPE].astype(BF16)
        qr = q[:, HEADS * QK_NOPE:]
        qr_ref[rows, :] = (qr * cq_ref[rows, :] + _pair_swap(qr) * sq_ref[rows, :]).astype(BF16)
        kv = _dot(ckv.astype(BF16), wukv[...])
        kn_ref[rows, :] = kv[:, :HEADS * QK_NOPE].astype(BF16)
        v_ref[rows, :] = kv[:, HEADS * QK_NOPE:].astype(BF16)
        kr = krp * ck_ref[rows, :] + _pair_swap(krp) * sk_ref[rows, :]
        kr2_ref[rows, :] = jnp.concatenate([kr, pltpu.roll(kr, QK_ROPE, 1)], axis=1).astype(BF16)

    @pl.when(pl.program_id(0) < prompt_steps)
    def _():
        for rows, ckv, krp in cache_rows:
            ckv_ref[rows, :] = ckv
            krr_ref[rows, :] = krp


def _mla_proj(xp, xs, g0, mods, w1, gq, gkv, wuq, wukv, cos_q, sin_q, cos_k, sin_k):
    tm = MLA_TM
    grp = _grp_bm(tm)
    n_pos = DEC_SEQ // tm
    tab = lambda w: pl.BlockSpec((tm, w), lambda i: (jnp.where(i * tm < NP, n_pos, (i - NP // tm) % n_pos), 0))
    row = lambda w: pl.BlockSpec((tm, w), lambda i: (i, 0))
    shp = lambda w, dt: jax.ShapeDtypeStruct((N, w), dt)
    n_p = NP // tm
    prow = lambda w: pl.BlockSpec((tm, w), lambda i: (jnp.minimum(i, n_p - 1), 0))
    return pl.pallas_call(
        functools.partial(_mlaproj_kernel, prompt_steps=n_p),
        grid=(N // tm,),
        in_specs=_stream_specs(tm, D) + [_full((1, D)), _mod_spec(0, grp), _mod_spec(1, grp),
                  _full(w1.shape), _full((1, Q_LORA)), _full((1, KV_LORA)),
                  _full(wuq.shape), _full(wukv.shape),
                  tab(HEADS * QK_ROPE), tab(HEADS * QK_ROPE), tab(LANE), tab(LANE)],
        out_specs=[row(HEADS * QK_NOPE), row(HEADS * QK_ROPE), row(HEADS * QK_NOPE), row(HEADS * V_DIM),
                   row(2 * LANE), prow(KV_LORA), prow(LANE)],
        out_shape=[shp(HEADS * QK_NOPE, BF16), shp(HEADS * QK_ROPE, BF16), shp(HEADS * QK_NOPE, BF16),
                   shp(HEADS * V_DIM, BF16), shp(2 * LANE, BF16),
                   jax.ShapeDtypeStruct((NP, KV_LORA), F32), jax.ShapeDtypeStruct((NP, LANE), F32)],
        compiler_params=_cparams(("arbitrary",)),
        name="mla_proj",
    )(xp, xs, g0, mods, mods, w1, gq, gkv, wuq, wukv, cos_q, sin_q, cos_k, sin_k)


def _cachekv_kernel(c_ref, w_ref, kn_ref, v_ref):
    kv = _dot(c_ref[...].astype(BF16), w_ref[...])
    kn_ref[...] = kv[:, :HEADS * QK_NOPE].astype(BF16)
    v_ref[...] = kv[:, HEADS * QK_NOPE:].astype(BF16)


def _cache_kv(ckv_cache, wukv):
    rows = ckv_cache.shape[0]
    tm = 512
    return pl.pallas_call(
        _cachekv_kernel,
        grid=(rows // tm,),
        in_specs=[pl.BlockSpec((tm, KV_LORA), lambda i: (i, 0)), _full(wukv.shape)],
        out_specs=[pl.BlockSpec((tm, HEADS * QK_NOPE), lambda i: (i, 0)),
                   pl.BlockSpec((tm, HEADS * V_DIM), lambda i: (i, 0))],
        out_shape=[jax.ShapeDtypeStruct((rows, HEADS * QK_NOPE), BF16),
                   jax.ShapeDtypeStruct((rows, HEADS * V_DIM), BF16)],
        compiler_params=_cparams(("arbitrary",)),
        name="cache_kv",
    )(ckv_cache, wukv)


def _attn_kernel(*refs, has_cache):
    if has_cache:
        qn, qr, kn, kr, v, knc, krc, vc, o_ref, s_scr = refs
        streams = [(knc, krc, vc), (kn, kr, v)]
    else:
        qn, qr, kn, kr, v, o_ref, s_scr = refs
        streams = [(kn, kr, v)]
    chunks = [(k1, k2, vv, c * KEY_BLK) for k1, k2, vv in streams for c in range(k1.shape[0] // KEY_BLK)]
    tq = qn.shape[0]
    a = (QK_NOPE + QK_ROPE) ** -0.5 * math.log2(math.e)
    for hh in range(HEAD_GRP):
        cols = slice(hh * LANE, (hh + 1) * LANE)
        pair_cols = slice((hh // 2) * LANE, (hh // 2 + 1) * LANE)
        kr_cols = slice((hh % 2) * LANE, (hh % 2 + 1) * LANE)
        q = jnp.concatenate([qn[:, cols], qr[:, pair_cols]], axis=1)
        mx = jnp.full((tq, LANE), -jnp.inf, F32)
        for n, (k1, k2, _, r0) in enumerate(chunks):
            k = jnp.concatenate([k1[r0:r0 + KEY_BLK, cols], k2[r0:r0 + KEY_BLK, kr_cols]], axis=1)
            s = lax.dot_general(q, k, NT_DIMS, preferred_element_type=F32)
            s_scr[hh, :, n * KEY_BLK:(n + 1) * KEY_BLK] = s
            for c in range(KEY_BLK // LANE):
                mx = jnp.maximum(mx, s[:, c * LANE:(c + 1) * LANE])
        mb = jnp.max(mx, axis=-1, keepdims=True) * a
        den = jnp.zeros((tq, LANE), F32)
        o = jnp.zeros((tq, V_DIM), F32)
        for n, (_, _, vv, r0) in enumerate(chunks):
            p = jnp.exp2(s_scr[hh, :, n * KEY_BLK:(n + 1) * KEY_BLK] * a - mb)
            for c in range(KEY_BLK // LANE):
                den = den + p[:, c * LANE:(c + 1) * LANE]
            o = o + _dot(p.astype(BF16), vv[r0:r0 + KEY_BLK, cols])
        o_ref[:, cols] = (o / jnp.sum(den, axis=-1, keepdims=True)).astype(BF16)


def _attention(qn, qr, kn, kr2, v, *, row0, n_seq, seq, tq, cache=None):
    nq = seq // tq
    grp = HEAD_GRP * LANE
    qblk = lambda b, h, i: row0 // tq + b * nq + i
    kblk = lambda b: row0 // seq + b
    in_specs = [pl.BlockSpec((tq, grp), lambda b, h, i: (qblk(b, h, i), h)),
                pl.BlockSpec((tq, grp // 2), lambda b, h, i: (qblk(b, h, i), h)),
                pl.BlockSpec((seq, grp), lambda b, h, i: (kblk(b), h)),
                pl.BlockSpec((seq, 2 * LANE), lambda b, h, i: (kblk(b), 0)),
                pl.BlockSpec((seq, grp), lambda b, h, i: (kblk(b), h))]
    args = [qn, qr, kn, kr2, v]
    if cache is not None:
        knc, kr2c, vc = cache
        in_specs += [pl.BlockSpec((PAST_LEN, grp), lambda b, h, i: (b, h)),
                     pl.BlockSpec((PAST_LEN, 2 * LANE), lambda b, h, i: (b, 0)),
                     pl.BlockSpec((PAST_LEN, grp), lambda b, h, i: (b, h))]
        args += [knc, kr2c, vc]
    return pl.pallas_call(
        functools.partial(_attn_kernel, has_cache=cache is not None),
        grid=(n_seq, HEADS // HEAD_GRP, nq),
        in_specs=in_specs,
        out_specs=pl.BlockSpec((tq, grp), lambda b, h, i: (b * nq + i, h)),
        out_shape=jax.ShapeDtypeStruct((n_seq * seq, HEADS * V_DIM), BF16),
        scratch_shapes=[pltpu.VMEM((HEAD_GRP, tq, seq + (PAST_LEN if cache is not None else 0)), F32)],
        compiler_params=_cparams(("arbitrary", "arbitrary", "arbitrary")),
        name="attn_latent" if cache is not None else "attn_context",
    )(*args)


def _router_kernel(op_ref, os_ref, xp_ref, xs_ref, gate1, g1, sh2, sc2, g2, wout, wr_hi, wr_lo, tri,
                   x3_ref, h_ref, ri_ref, rf_ref, cnt_ref, carry, *, prompt_steps):
    step = pl.program_id(0)

    @pl.when(step == 0)
    def _():
        carry[...] = jnp.zeros_like(carry)

    o_all = _stream_rows(op_ref, os_ref, prompt_steps)
    x_all = _stream_rows(xp_ref, xs_ref, prompt_steps)
    sub = x_all.shape[0] // ROUTER_SUB
    dg = lambda a, b: lax.dot_general(a, b, NT_DIMS, preferred_element_type=F32)
    logits = []
    for r in range(ROUTER_SUB):
        rows = slice(r * sub, (r + 1) * sub)
        x3 = _gated(x_all[rows], _dot(o_all[rows], wout[...]), g1[...], gate1[0])
        x3_ref[rows, :] = x3
        h = _adaln(x3, g2[...], sc2[0], sh2[0])
        _store_row_tiles(h_ref, lambda cols, h=h: h[:, cols], r * sub, sub)
        h_hi = h.astype(BF16)
        h_lo = (h - h_hi.astype(F32)).astype(BF16)
        logits.append(dg(wr_hi[...], h_hi) + dg(wr_hi[...], h_lo) + dg(wr_lo[...], h_hi))
    lg = jnp.concatenate(logits, axis=1)
    eidx = lax.broadcasted_iota(I32, lg.shape, 0).astype(F32)
    m1 = jnp.max(lg, axis=0, keepdims=True)
    i1 = jnp.min(jnp.where(lg == m1, eidx, float(N_EXP)), axis=0, keepdims=True)
    sel1 = eidx == i1
    lg2 = jnp.where(sel1, -jnp.inf, lg)
    m2 = jnp.max(lg2, axis=0, keepdims=True)
    i2 = jnp.min(jnp.where(lg2 == m2, eidx, float(N_EXP)), axis=0, keepdims=True)
    sel2 = eidx == i2
    e = jnp.exp(m2 - m1)
    w1 = 1.0 / (1.0 + e)
    w2 = e / (1.0 + e)
    picked = jnp.where(sel1 | sel2, 1.0, 0.0)
    rank = _dot(picked.astype(BF16), tri[...]) + carry[:, 0:1]
    r1 = jnp.sum(jnp.where(sel1, rank, 0.0), axis=0, keepdims=True)
    r2 = jnp.sum(jnp.where(sel2, rank, 0.0), axis=0, keepdims=True)
    carry[...] = carry[...] + jnp.sum(picked, axis=1, keepdims=True)
    cnt_ref[...] = carry[...]
    ri_ref[...] = jnp.where(eidx == 0.0, i1, jnp.where(eidx == 1.0, i2, jnp.where(eidx == 2.0, r1, r2))).astype(I32)
    rf_ref[...] = jnp.where(eidx == 0.0, w1, w2)


def _attn_out_router(o_p, o_s, xp, xs, mods, g1, g2, wout, wr_hi, wr_lo):
    tm = 512
    grp = _grp_bm(tm)
    n_p = NP // tm
    tri = jnp.asarray(np.triu(np.ones((tm, tm), np.float32), 1), BF16)
    row = lambda w: pl.BlockSpec((tm, w), lambda i: (i, 0))
    col = pl.BlockSpec((N_EXP, tm), lambda i: (0, i))
    return pl.pallas_call(
        functools.partial(_router_kernel, prompt_steps=n_p),
        grid=(N // tm,),
        in_specs=_stream_specs(tm, HEADS * V_DIM) + _stream_specs(tm, D) + [_mod_spec(2, grp), _full((1, D)),
                  _mod_spec(3, grp), _mod_spec(4, grp), _full((1, D)),
                  _full((HEADS * V_DIM, D)), _full((N_EXP, D)), _full((N_EXP, D)), _full((tm, tm))],
        out_specs=[row(D), pl.BlockSpec((tm * ROW_TILE, LANE), lambda i: (i, 0)), col, col, _full((N_EXP, LANE))],
        out_shape=[jax.ShapeDtypeStruct((N, D), F32), jax.ShapeDtypeStruct((N * ROW_TILE, LANE), F32),
                   jax.ShapeDtypeStruct((N_EXP, N), I32), jax.ShapeDtypeStruct((N_EXP, N), F32),
                   jax.ShapeDtypeStruct((N_EXP, LANE), F32)],
        scratch_shapes=[pltpu.VMEM((N_EXP, LANE), F32)],
        compiler_params=_cparams(("arbitrary",)),
        name="attn_out_router",
    )(o_p, o_s, xp, xs, mods, g1, mods, mods, g2, wout, wr_hi, wr_lo, tri)


def _tile_rows(r):
    return r * ROW_TILE if isinstance(r, int) else pl.multiple_of(r * ROW_TILE, ROW_TILE)


def _moe_kernel(te, nu, src0_ref, src1_ref, dstp_ref, dstc_ref, h_hbm, wgu, wd, y_hbm,
                hsbuf, ybuf, hbf, acc, sem_g, sem_s):
    i = pl.program_id(0)
    last = pl.num_programs(0) - 1
    cur = i % 2
    nxt = 1 - cur
    buf_rows = TM_E * ROW_TILE

    def gather(idx_ref, r, slot):
        i_src = pl.multiple_of(idx_ref[0, 0, r] * ROW_TILE, ROW_TILE)
        return pltpu.make_async_copy(h_hbm.at[pl.ds(i_src, ROW_TILE), :],
                                     hsbuf.at[slot, pl.ds(_tile_rows(r), ROW_TILE), :], sem_g.at[slot])

    def scatter(idx_ref, r, slot):
        i_dst = pl.multiple_of(idx_ref[0, 0, r] * ROW_TILE, ROW_TILE)
        return pltpu.make_async_copy(ybuf.at[slot, pl.ds(_tile_rows(r), ROW_TILE), :],
                                     y_hbm.at[pl.ds(i_dst, ROW_TILE), :], sem_s.at[slot])

    def wait_gather(slot):
        pltpu.make_async_copy(h_hbm.at[pl.ds(0, buf_rows), :], hsbuf.at[slot], sem_g.at[slot]).wait()

    def wait_scatter(slot):
        pltpu.make_async_copy(ybuf.at[slot], y_hbm.at[pl.ds(0, buf_rows), :], sem_s.at[slot]).wait()

    def for_rows(fn):
        def body(g, carry):
            for k in range(ROW_TILE):
                fn(g * ROW_TILE + k, k % 2)
            return carry
        lax.fori_loop(0, TM_E // ROW_TILE, body, 0)

    def side_traffic(r, queue):
        gather(src1_ref, r, nxt).start(priority=queue)
        scatter(dstp_ref, r, nxt).start(priority=queue)

    @pl.when(i == 0)
    def _():
        ybuf[1] = jnp.zeros((buf_rows, LANE), F32)
        for_rows(lambda r, queue: gather(src0_ref, r, 0).start(priority=queue))

    wait_gather(cur)

    @pl.when(i < nu[0])
    def _():
        for c in range(ROW_TILE):
            hbf[:, c * LANE:(c + 1) * LANE] = hsbuf[cur, pl.ds(c, TM_E, stride=ROW_TILE), :].astype(BF16)
        w_gu = wgu.at[0]
        w_d = wd.at[0]
        _swiglu_block(hbf, w_gu, w_d, acc, 0)
        per_iter = TM_E // SIDE_ITERS

        def body(t, carry):
            _swiglu_block(hbf, w_gu, w_d, acc, 1 + 2 * t)
            _swiglu_block(hbf, w_gu, w_d, acc, 2 + 2 * t)
            for k in range(per_iter):
                side_traffic(t * per_iter + k, k % 2)
            return carry

        lax.fori_loop(0, SIDE_ITERS, body, 0)
        for j in range(1 + 2 * SIDE_ITERS, N_FF):
            _swiglu_block(hbf, w_gu, w_d, acc, j)
        _store_row_tiles(ybuf.at[cur], lambda cols: acc[:, cols])

    @pl.when(i >= nu[0])
    def _():
        for_rows(side_traffic)
        ybuf[cur] = jnp.zeros((buf_rows, LANE), F32)

    wait_scatter(nxt)

    @pl.when(i == last)
    def _():
        for_rows(lambda r, queue: scatter(dstc_ref, r, cur).start(priority=queue))
        wait_scatter(cur)
        wait_gather(nxt)


def _moe_experts(tile_expert, n_used, src_tbl, dst_tbl, h, wgu, wd):
    smem = lambda off: pl.BlockSpec((1, 1, TM_E), lambda i, te, nu: (i + off, 0, 0), memory_space=pltpu.SMEM)
    grid_spec = pltpu.PrefetchScalarGridSpec(
        num_scalar_prefetch=2,
        grid=(N_TILES,),
        in_specs=[smem(0), smem(1), smem(0), smem(1),
                  pl.BlockSpec(memory_space=pl.ANY),
                  pl.BlockSpec((1,) + wgu.shape[1:], lambda i, te, nu: (te[i], 0, 0)),
                  pl.BlockSpec((1,) + wd.shape[1:], lambda i, te, nu: (te[i], 0, 0))],
        out_specs=pl.BlockSpec(memory_space=pl.ANY),
        scratch_shapes=[pltpu.VMEM((2, TM_E * ROW_TILE, LANE), F32), pltpu.VMEM((2, TM_E * ROW_TILE, LANE), F32),
                        pltpu.VMEM((TM_E, D), BF16), pltpu.VMEM((TM_E, D), F32),
                        pltpu.SemaphoreType.DMA((2,)), pltpu.SemaphoreType.DMA((2,))],
    )
    return pl.pallas_call(
        _moe_kernel,
        grid_spec=grid_spec,
        out_shape=jax.ShapeDtypeStruct(((2 * N + DUMP_ROWS) * ROW_TILE, LANE), F32),
        compiler_params=_cparams(("arbitrary",)),
        name="moe_experts",
    )(tile_expert, n_used, src_tbl, src_tbl, dst_tbl, dst_tbl, h, wgu, wd)


def _combine_kernel(y1_ref, y2_ref, x_ref, w1_ref, w2_ref, gate2, g3, op_ref, os_ref, *, prompt_steps):
    rows = x_ref.shape[0]
    w1 = w1_ref[...]
    w2 = w2_ref[...]
    f = jnp.concatenate([w1 * _load_row_tiles(y1_ref, c, rows) + w2 * _load_row_tiles(y2_ref, c, rows)
                         for c in range(ROW_TILE)], axis=1)
    out = _gated(x_ref[...], f, g3[...], gate2[0])
    step = pl.program_id(0)

    @pl.when(step < prompt_steps)
    def _():
        op_ref[...] = out

    @pl.when(step >= prompt_steps)
    def _():
        os_ref[...] = out


def _moe_combine(y, x, w1, w2, mods, g3):
    tm = 512
    grp = _grp_bm(tm)
    nb = N // tm
    n_p = NP // tm
    return pl.pallas_call(
        functools.partial(_combine_kernel, prompt_steps=n_p),
        grid=(nb,),
        in_specs=[pl.BlockSpec((tm * ROW_TILE, LANE), lambda i: (i, 0)),
                  pl.BlockSpec((tm * ROW_TILE, LANE), lambda i: (nb + i, 0)),
                  pl.BlockSpec((tm, D), lambda i: (i, 0)),
                  pl.BlockSpec((tm, 1), lambda i: (i, 0)), pl.BlockSpec((tm, 1), lambda i: (i, 0)),
                  _mod_spec(5, grp), _full((1, D))],
        out_specs=[pl.BlockSpec((tm, D), lambda i: (jnp.minimum(i, n_p - 1), 0)),
                   pl.BlockSpec((tm, D), lambda i: (jnp.maximum(i - n_p, 0), 0))],
        out_shape=[jax.ShapeDtypeStruct((NP, D), F32), jax.ShapeDtypeStruct((NS, D), F32)],
        compiler_params=_cparams(("arbitrary",)),
        name="moe_combine",
    )(y, y, x, w1, w2, mods, g3)


def _block_diag(w):
    hh, a, b = w.shape
    eye = jnp.eye(hh, dtype=w.dtype)
    return jnp.einsum('hab,hk->hakb', w, eye).reshape(hh * a, hh * b)


def _s5_matrices(a_re, a_im, log_dt, b_re, b_im, c_re, c_im):
    dt = jnp.exp(log_dt)[:, None]
    mag = jnp.exp(a_re * dt)
    abr = mag * jnp.cos(a_im * dt)
    abi = mag * jnp.sin(a_im * dt)
    den = a_re * a_re + a_im * a_im
    cr = ((abr - 1.0) * a_re + abi * a_im) / den
    ci = (abi * a_re - (abr - 1.0) * a_im) / den
    bbr = cr[..., None] * b_re - ci[..., None] * b_im
    bbi = cr[..., None] * b_im + ci[..., None] * b_re
    hg = S5_G // 2
    eye = jnp.eye(hg, dtype=F32)
    bms, cms = [], []
    for j in range(2):
        sl = slice(j * hg, (j + 1) * hg)
        bd = lambda m: jnp.einsum('gpc,gh->gchp', m[sl], eye).reshape(hg * S5_CH, hg * S5_P)
        bms.append(jnp.concatenate([bd(bbr), bd(bbi)], axis=1))
        cd = lambda m: jnp.einsum('gcp,gh->gphc', m[sl], eye).reshape(hg * S5_P, hg * S5_CH)
        cms.append(jnp.concatenate([cd(c_re), cd(-c_im)], axis=0))
    return (jnp.stack(bms).astype(BF16), abr.reshape(1, S5_N), abi.reshape(1, S5_N),
            jnp.stack(cms).astype(BF16))


def _rope_tables(tm):
    rows = DEC_SEQ // GRID_W
    row = jnp.repeat(jnp.arange(rows, dtype=F32), GRID_W)
    col = jnp.tile(jnp.arange(GRID_W, dtype=F32), rows)
    nf = QK_ROPE // 4
    inv = ROPE_THETA ** (-jnp.arange(nf, dtype=F32) / nf)
    ang = jnp.concatenate([row[:, None] * inv, col[:, None] * inv], axis=-1)
    cos = jnp.repeat(jnp.cos(ang), 2, axis=-1)
    sin = jnp.stack([-jnp.sin(ang), jnp.sin(ang)], axis=-1).reshape(DEC_SEQ, QK_ROPE)
    ident = lambda t, one: jnp.concatenate([t, jnp.full((tm, t.shape[1]), one, F32)], axis=0)
    cos_q = ident(jnp.tile(cos, (1, HEADS)), 1.0)
    sin_q = ident(jnp.tile(sin, (1, HEADS)), 0.0)
    pad = lambda t, one: jnp.concatenate([t, jnp.full((DEC_SEQ, LANE - QK_ROPE), one, F32)], axis=1)
    return cos_q, sin_q, ident(pad(cos, 1.0), 1.0), ident(pad(sin, 0.0), 0.0)


def _group_states(prompt_state, sample_state):
    w = sample_state.shape[-1]
    return jnp.concatenate([prompt_state.reshape(2, SUB, w), sample_state.reshape(1, SUB, w)], axis=0)


def _layer_ab(xp, xs, m, ng, j, state_lru, state_s5_re, state_s5_im, p):
    xz, x = _ab_inproj(xp, xs, ng[0:1], m, p['ab_w_in'][j].astype(BF16))
    zeros = lambda w: jnp.zeros((BATCH, w), F32)
    per_dir = []
    for d in range(2):
        wg = jnp.concatenate([_block_diag(p['lru_wa'][j, d]), _block_diag(p['lru_wx'][j, d])], axis=1).astype(BF16)
        bg = jnp.concatenate([p['lru_ba'][j, d], p['lru_bx'][j, d]])[None]
        bm, ar, ai, cm = _s5_matrices(p['s5_a_re'][j, d], p['s5_a_im'][j, d], p['s5_log_dt'][j, d],
                                      p['s5_b_re'][j, d], p['s5_b_im'][j, d], p['s5_c_re'][j, d], p['s5_c_im'][j, d])
        h0l = _group_states(zeros(LRU_W), state_lru[:, j, d])
        h0r = _group_states(zeros(S5_N), state_s5_re[:, j, d].reshape(DEC_BATCH, S5_N))
        h0i = _group_states(zeros(S5_N), state_s5_im[:, j, d].reshape(DEC_BATCH, S5_N))
        per_dir.append((h0l, h0r, h0i, p['ab_conv_w'][j], p['ab_conv_b'][j][None], wg, bg,
                        p['lru_lambda'][j, d][None], bm, ar, ai, cm))
    (haf, yf, llf, lrf, lif), (hab, yr, llb, lrb, lib) = _ab_scan(xz, per_dir)
    x = _ab_out(haf, hab, yf, yr, xz, x, m, ng[1:2], p['s5_d'][j][None], p['s5_w_glu'][j].astype(BF16),
                p['s5_b_glu'][j][None], p['ab_w_out'][j].astype(BF16))
    streams = _ffn(x, m, ng[2:3], ng[3:4], p['ffn_w_gate_up'][j].astype(BF16), p['ffn_w_down'][j].astype(BF16))
    prompt = lambda f, b, w: jnp.stack([f[:2].reshape(BATCH, w), b[:2].reshape(BATCH, w)], axis=1)
    lru = prompt(llf, llb, LRU_W)
    s5r = prompt(lrf, lrb, S5_N).reshape(BATCH, 2, S5_G, S5_P)
    s5i = prompt(lif, lib, S5_N).reshape(BATCH, 2, S5_G, S5_P)
    return tuple(streams), lru, s5r, s5i


def _head_major(w, parts):
    k = w.shape[0]
    per_head = w.reshape(k, HEADS, -1)
    out, start = [], 0
    for width in parts:
        out.append(per_head[:, :, start:start + width].reshape(k, HEADS * width))
        start += width
    return jnp.concatenate(out, axis=1)


def _layer_mla_moe(xp, xs, m, ng, j, cache_kv_latent, cache_k_rope, p):
    w1 = jnp.concatenate([p['mla_w_in'][j], jnp.zeros((D, LANE - QK_ROPE), F32)], axis=1).astype(BF16)
    wuq = _head_major(p['mla_w_uq'][j], (QK_NOPE, QK_ROPE)).astype(BF16)
    wukv = _head_major(p['mla_w_ukv'][j], (QK_NOPE, V_DIM)).astype(BF16)
    tables = _rope_tables(MLA_TM)
    qn, qr, kn, v, kr2, ckv, krr = _mla_proj(xp, xs, ng[0:1], m, w1, p['mla_g_q'][j][None], p['mla_g_kv'][j][None],
                                              wuq, wukv, *tables)
    knc, vc = _cache_kv(cache_kv_latent[:, j].reshape(DEC_BATCH * PAST_LEN, KV_LORA), wukv)
    krc = cache_k_rope[:, j].reshape(DEC_BATCH * PAST_LEN, QK_ROPE)
    z = jnp.zeros_like(krc)
    kr2c = jnp.concatenate([krc, z, z, krc], axis=1).astype(BF16)
    o_p = _attention(qn, qr, kn, kr2, v, row0=0, n_seq=BATCH, seq=SEQ, tq=SEQ)
    o_s = _attention(qn, qr, kn, kr2, v, row0=NP, n_seq=DEC_BATCH, seq=DEC_SEQ, tq=1024, cache=(knc, kr2c, vc))
    wr_t = p['moe_w_router'][j].T
    wr_hi = wr_t.astype(BF16)
    wr_lo = (wr_t - wr_hi.astype(F32)).astype(BF16)
    x3, h, ri, rf, cnt = _attn_out_router(o_p, o_s, xp, xs, m, ng[1:2], ng[2:3], p['mla_w_out'][j].astype(BF16), wr_hi, wr_lo)
    counts = cnt[:, 0].astype(I32)
    padded = ((counts + TM_E - 1) // TM_E) * TM_E
    ends = jnp.cumsum(padded)
    offs = ends - padded
    pos1 = offs[ri[0]] + ri[2]
    pos2 = offs[ri[1]] + ri[3]
    pick_tok = jnp.arange(2 * N, dtype=I32)
    dest = jnp.full((P_ROWS,), -1, I32).at[jnp.concatenate([pos1, pos2])].set(pick_tok, unique_indices=True)
    is_pad = dest < 0
    pad_row = 2 * N + TM_E + jnp.cumsum(is_pad.astype(I32)) - 1
    src_tbl = jnp.concatenate([jnp.where(is_pad, 0, dest % N), jnp.zeros((TM_E,), I32)])
    dst_tbl = jnp.concatenate([2 * N + jnp.arange(TM_E, dtype=I32), jnp.where(is_pad, pad_row, dest)])
    n_used = (ends[-1] // TM_E).astype(I32)[None]
    tile_row = jnp.minimum(jnp.arange(N_TILES, dtype=I32), n_used - 1) * TM_E
    tile_expert = jnp.sum((tile_row[:, None] >= ends[None, :]).astype(I32), axis=1)
    y = _moe_experts(tile_expert, n_used, src_tbl.reshape(N_TILES + 1, 1, TM_E), dst_tbl.reshape(N_TILES + 1, 1, TM_E),
                     h, p['moe_w_gate_up'][j].astype(BF16), p['moe_w_down'][j].astype(BF16))
    xp, xs = _moe_combine(y, x3, rf[0][:, None], rf[1][:, None], m, ng[3:4])
    kv_new = ckv.reshape(BATCH, SEQ, KV_LORA)
    kr_new = krr[:, :QK_ROPE].reshape(BATCH, SEQ, QK_ROPE)
    return (xp.reshape(BATCH, SEQ, D), xs.reshape(DEC_BATCH, DEC_SEQ, D)), kv_new, kr_new


def kernel(x_prompt, x_sample, c, state_lru, state_s5_re, state_s5_im, cache_kv_latent, cache_k_rope, c_ctx, w_mod, b_mod, norm_gains, ab_w_in, ab_conv_w, ab_conv_b, lru_wa, lru_ba, lru_wx, lru_bx, lru_lambda, s5_a_re, s5_a_im, s5_log_dt, s5_b_re, s5_b_im, s5_c_re, s5_c_im, s5_d, s5_w_glu, s5_b_glu, ab_w_out, ffn_w_gate_up, ffn_w_down, mla_w_in, mla_g_q, mla_g_kv, mla_w_uq, mla_w_ukv, mla_w_out, moe_w_router, moe_w_gate_up, moe_w_down):
    p = dict(ab_w_in=ab_w_in, ab_conv_w=ab_conv_w, ab_conv_b=ab_conv_b, lru_wa=lru_wa, lru_ba=lru_ba,
             lru_wx=lru_wx, lru_bx=lru_bx, lru_lambda=lru_lambda, s5_a_re=s5_a_re, s5_a_im=s5_a_im,
             s5_log_dt=s5_log_dt, s5_b_re=s5_b_re, s5_b_im=s5_b_im, s5_c_re=s5_c_re, s5_c_im=s5_c_im,
             s5_d=s5_d, s5_w_glu=s5_w_glu, s5_b_glu=s5_b_glu, ab_w_out=ab_w_out, ffn_w_gate_up=ffn_w_gate_up,
             ffn_w_down=ffn_w_down, mla_w_in=mla_w_in, mla_g_q=mla_g_q, mla_g_kv=mla_g_kv, mla_w_uq=mla_w_uq,
             mla_w_ukv=mla_w_ukv, mla_w_out=mla_w_out, moe_w_router=moe_w_router, moe_w_gate_up=moe_w_gate_up,
             moe_w_down=moe_w_down)
    depth = w_mod.shape[0]
    cond = jnp.concatenate([c_ctx[None], c, jnp.zeros((2 * SUB - 1 - DEC_BATCH, D), F32)], axis=0)
    mod = _modulation(cond, w_mod, b_mod)
    ctx_tile = lambda l: jnp.broadcast_to(mod[l, 0:1], (SUB, 6 * D))
    streams = (x_prompt, x_sample)
    lru_l, s5r_l, s5i_l, kv_l, kr_l = [], [], [], [], []
    for layer in range(depth):
        j = layer // 2
        ng = norm_gains[layer]
        if layer % 2 == 0:
            m = jnp.stack([ctx_tile(layer), mod[layer, 1:1 + DEC_BATCH]])
            streams, lru, s5r, s5i = _layer_ab(*streams, m, ng, j, state_lru, state_s5_re, state_s5_im, p)
            lru_l.append(lru)
            s5r_l.append(s5r)
            s5i_l.append(s5i)
        else:
            lat = jnp.broadcast_to(mod[layer, 1:1 + DEC_BATCH, None, :], (DEC_BATCH, SUB, 6 * D))
            m = jnp.concatenate([ctx_tile(layer)[None], lat], axis=0)
            streams, kv_new, kr_new = _layer_mla_moe(streams[0].reshape(NP, D), streams[1].reshape(NS, D), m, ng, j,
                                                     cache_kv_latent, cache_k_rope, p)
            kv_l.append(kv_new)
            kr_l.append(kr_new)
    return (streams[0], streams[1],
            jnp.stack(lru_l, axis=1), jnp.stack(s5r_l, axis=1), jnp.stack(s5i_l, axis=1),
            jnp.stack(kv_l, axis=1), jnp.stack(kr_l, axis=1))
```

```python
import functools
import math

import numpy as np
import jax
import jax.numpy as jnp
from jax import lax
from jax.experimental import pallas as pl
from jax.experimental.pallas import tpu as pltpu

F32 = jnp.float32
BF16 = jnp.bfloat16
I32 = jnp.int32

D = 1024
BATCH, SEQ = 16, 256
DEC_BATCH, DEC_SEQ = 8, 2048
PAST_LEN = 256
GRID_W = 64
LRU_W = 512
LRU_HEADS = 8
LRU_C = 8.0
CONV_W = 4
S5_W = 512
S5_CH = 16
S5_G = 32
S5_P = 64
S5_N = S5_G * S5_P
HEADS = 8
QK_NOPE, QK_ROPE, V_DIM = 128, 64, 128
Q_LORA, KV_LORA = 384, 256
ROPE_THETA = 10000.0
D_FF = 2816
N_EXP = 8
EPS = 1e-6

NP = BATCH * SEQ
NS = DEC_BATCH * DEC_SEQ
N = NP + NS
SUB = 8
LANE = 128
ROW_TILE = D // LANE
T_CHUNK = 64
R_CHUNK = T_CHUNK * SUB
FF_BLK = 256
KEY_BLK = 256
HEAD_GRP = 4
MLA_TM = 512
MLA_SUB = 2
ROUTER_SUB = 1
ABOUT_SUB = 1
ABOUT_T = 64
INPROJ_SUB = 2
INPROJ_T = 64
FFN_T = 128
SIDE_ITERS = 4
N_FF = D_FF // FF_BLK
TM_E = 512
P_ROWS = 2 * N + N_EXP * TM_E
N_TILES = P_ROWS // TM_E
DUMP_ROWS = P_ROWS - 2 * N + TM_E
VMEM_LIMIT = 56 * 1024 * 1024

NT_DIMS = (((1,), (1,)), ((), ()))


def _cparams(sem):
    return pltpu.CompilerParams(dimension_semantics=sem, vmem_limit_bytes=VMEM_LIMIT)


def _dot(a, b):
    return jnp.dot(a, b, preferred_element_type=F32)


def _sigmoid(x):
    return 1.0 / (1.0 + jnp.exp(-x))


def _neg_expm1_2x(log_a, a):
    series = -2.0 * log_a * (1.0 + log_a * (1.0 + log_a * (2.0 / 3.0) * (1.0 + log_a * 0.5)))
    return jnp.where(log_a > -0.01, series, (1.0 - a) * (1.0 + a))


def _sqrt_nonneg(v):
    return jnp.where(v > 0.0, v * lax.rsqrt(v), 0.0)


def _gelu(x):
    k = math.sqrt(2.0 / math.pi)
    half = 0.5 * x
    return half + half * jnp.tanh(x * (k + (k * 0.044715) * (x * x)))


def _rms(x, g):
    ms = jnp.mean(x * x, axis=-1, keepdims=True)
    return x * lax.rsqrt(ms + EPS) * g


def _rows8(y, fn):
    r, c = y.shape
    return fn(y.reshape(r // SUB, SUB, c)).reshape(r, c)


def _adaln(x, g, scale, shift):
    return _rows8(_rms(x, g), lambda y: y * (1.0 + scale)[None] + shift[None])


def _gated(x, y, g, gate):
    return x + _rows8(_rms(y, g), lambda z: z * gate[None])


def _store_row_tiles(ref, piece, row0=0, rows=None):
    rows = ref.shape[0] // ROW_TILE if rows is None else rows
    for c in range(ROW_TILE):
        ref[pl.ds(row0 * ROW_TILE + c, rows, stride=ROW_TILE), :] = piece(slice(c * LANE, (c + 1) * LANE))


def _load_row_tiles(ref, c, rows):
    return ref[pl.ds(c, rows, stride=ROW_TILE), :]


def _full(shape):
    nd = len(shape)
    return pl.BlockSpec(shape, lambda *_: (0,) * nd)


def _mod_spec(k, grp):
    return pl.BlockSpec((1, SUB, D), lambda i, *_: (grp(i), 0, k))


def _grp_tm(tm):
    return lambda i: (i * tm >= NP).astype(I32)


def _grp_bm(tm):
    return lambda i: jnp.where(i * tm < NP, 0, 1 + (i * tm - NP) // DEC_SEQ)


def _mod_kernel(c_ref, w_ref, b_ref, o_ref):
    c = c_ref[...]
    s = c * _sigmoid(c)
    o_ref[0] = _dot(s.astype(BF16), w_ref[0].astype(BF16)) + b_ref[0]


def _modulation(cond, w_mod, b_mod):
    depth = w_mod.shape[0]
    rows = cond.shape[0]
    return pl.pallas_call(
        _mod_kernel,
        grid=(depth, 6),
        in_specs=[_full((rows, D)),
                  pl.BlockSpec((1, D, D), lambda l, j: (l, 0, j)),
                  pl.BlockSpec((1, 1, D), lambda l, j: (l, 0, j))],
        out_specs=pl.BlockSpec((1, rows, D), lambda l, j: (l, 0, j)),
        out_shape=jax.ShapeDtypeStruct((depth, rows, 6 * D), F32),
        compiler_params=_cparams(("arbitrary", "arbitrary")),
        name="modulation",
    )(cond, w_mod, b_mod.reshape(depth, 1, 6 * D))


def _time_major_copies(step, t_steps, xp_hbm, xs_hbm, buf, sem, to_hbm):
    p_steps = (BATCH // SUB) * (SEQ // t_steps)
    per_group = SEQ // t_steps

    def issue(hbm, seq0, t0):
        t0 = pl.multiple_of(t0, t_steps)
        for b in range(SUB):
            rows = hbm.at[seq0 + b, pl.ds(t0, t_steps), :]
            tile = buf.at[:, b, :]
            (pltpu.make_async_copy(tile, rows, sem) if to_hbm else pltpu.make_async_copy(rows, tile, sem)).start()

    @pl.when(step < p_steps)
    def _():
        issue(xp_hbm, (step // per_group) * SUB, (step % per_group) * t_steps)

    @pl.when(step >= p_steps)
    def _():
        issue(xs_hbm, 0, (step - p_steps) * t_steps)


def _time_major_wait(t_steps, xs_hbm, buf, sem, to_hbm):
    for b in range(SUB):
        rows = xs_hbm.at[0, pl.ds(0, t_steps), :]
        tile = buf.at[:, b, :]
        (pltpu.make_async_copy(tile, rows, sem) if to_hbm else pltpu.make_async_copy(rows, tile, sem)).wait()


def _time_major_read(t_steps, xp_hbm, xs_hbm, buf, sem):
    step = pl.program_id(0)
    slot = step % 2

    @pl.when(step == 0)
    def _():
        _time_major_copies(step, t_steps, xp_hbm, xs_hbm, buf.at[0], sem.at[0], False)

    @pl.when(step + 1 < pl.num_programs(0))
    def _():
        _time_major_copies(step + 1, t_steps, xp_hbm, xs_hbm, buf.at[1 - slot], sem.at[1 - slot], False)

    _time_major_wait(t_steps, xs_hbm, buf.at[slot], sem.at[slot], False)
    return slot


def _inproj_kernel(xp_hbm, xs_hbm, g_ref, sh_ref, sc_ref, w_ref, o_ref, buf, sem):
    slot = _time_major_read(INPROJ_T, xp_hbm, xs_hbm, buf, sem)
    sub_t = INPROJ_T // INPROJ_SUB
    for r in range(INPROJ_SUB):
        rows = slice(r * sub_t * SUB, (r + 1) * sub_t * SUB)
        x = buf[slot, r * sub_t:(r + 1) * sub_t].reshape(sub_t * SUB, D)
        h = _adaln(x, g_ref[...], sc_ref[0], sh_ref[0])
        o_ref[rows, :] = _dot(h.astype(BF16), w_ref[...])


def _ab_inproj(xp, xs, gain, mods, w_in):
    tm = INPROJ_T * SUB
    nout = w_in.shape[1]
    grp = _grp_tm(tm)
    return pl.pallas_call(
        _inproj_kernel,
        grid=(N // tm,),
        in_specs=[pl.BlockSpec(memory_space=pl.ANY), pl.BlockSpec(memory_space=pl.ANY),
                  _full((1, D)),
                  _mod_spec(0, grp), _mod_spec(1, grp),
                  _full((D, nout))],
        out_specs=pl.BlockSpec((tm, nout), lambda i: (i, 0)),
        out_shape=jax.ShapeDtypeStruct((N, nout), F32),
        scratch_shapes=[pltpu.VMEM((2, INPROJ_T, SUB, D), F32), pltpu.SemaphoreType.DMA((2,))],
        compiler_params=_cparams(("arbitrary",)),
        name="ab_inproj",
    )(xp, xs, gain, mods, mods, w_in)


def _scan_table(reverse):
    cols = []
    groups = [(0, SEQ // T_CHUNK, 0), (1, SEQ // T_CHUNK, SEQ // T_CHUNK),
              (2, DEC_SEQ // T_CHUNK, NP // R_CHUNK)]
    for g, nc, base in groups:
        order = range(nc - 1, -1, -1) if reverse else range(nc)
        for k, c in enumerate(order):
            cols.append((base + c, g, int(k == 0), int(c > 0), int(c < nc - 1)))
    return np.asarray(cols, np.int32).T.copy()


N_SCAN_IN, N_SCAN_OUT, N_SCAN_SCRATCH = 16, 5, 8
SCAN_TBL_ROWS = 5


def _scan_kernel(tbl, *refs):
    ins = [refs[d * N_SCAN_IN:(d + 1) * N_SCAN_IN] for d in range(2)]
    o0 = 2 * N_SCAN_IN
    outs = [refs[o0 + d * N_SCAN_OUT:o0 + (d + 1) * N_SCAN_OUT] for d in range(2)]
    s0 = o0 + 2 * N_SCAN_OUT
    scr = [refs[s0 + d * N_SCAN_SCRATCH:s0 + (d + 1) * N_SCAN_SCRATCH] for d in range(2)]
    s = pl.program_id(0)
    for d in range(2):
        h0l_ref, h0r_ref, h0i_ref = ins[d][4:7]
        hl, sre, sim = scr[d][5:8]

        @pl.when(tbl[d * SCAN_TBL_ROWS + 2, s] == 1)
        def _(h0l_ref=h0l_ref, h0r_ref=h0r_ref, h0i_ref=h0i_ref, hl=hl, sre=sre, sim=sim):
            hl[...] = h0l_ref[0]
            sre[...] = h0r_ref[0]
            sim[...] = h0i_ref[0]

    fwd, bwd = [_ScanChunk(tbl, d * SCAN_TBL_ROWS, ins[d], outs[d], scr[d], reverse=(d == 1)) for d in range(2)]
    fwd.s5_project()
    fwd.conv()
    fwd.gates()
    bwd.s5_project()
    fwd.s5_recurrence()
    bwd.conv()
    fwd.s5_readout()
    bwd.gates()
    fwd.lru_coefficients()
    fwd.lru_recurrence()
    bwd.s5_recurrence()
    bwd.s5_readout()
    bwd.lru_coefficients()
    bwd.lru_recurrence()


class _ScanChunk:
    def __init__(self, tbl, row0, ins, outs, scratch, reverse):
        (self.xa_ref, self.xp_ref, self.xn_ref, self.xb_ref, _, _, _, self.cw_ref, self.cb_ref, self.wg_ref,
         self.bg_ref, self.lam_ref, self.bm_ref, self.ar_ref, self.ai_ref, self.cm_ref) = ins
        self.ha_ref, self.y_ref, self.ll_ref, self.lr_ref, self.li_ref = outs
        self.ext, self.abuf, self.bbuf, self.hre, self.him, self.hl, self.sre, self.sim = scratch
        step = pl.program_id(0)
        self.has_prev = tbl[row0 + 3, step] == 1
        self.has_next = tbl[row0 + 4, step] == 1
        self.order = range(T_CHUNK - 1, -1, -1) if reverse else range(T_CHUNK)
        self.half = S5_N // 2

    def s5_project(self):
        ub = self.xb_ref[...].astype(BF16)
        half = self.half
        for j in range(2):
            bu = _dot(ub[:, j * 256:(j + 1) * 256], self.bm_ref[j])
            self.hre[:, j * half:(j + 1) * half] = bu[:, :half]
            self.him[:, j * half:(j + 1) * half] = bu[:, half:]

    def conv(self):
        ext = self.ext
        ext[0:2 * SUB] = jnp.where(self.has_prev, self.xp_ref[...], 0.0)
        ext[2 * SUB:2 * SUB + R_CHUNK] = self.xa_ref[...]
        ext[2 * SUB + R_CHUNK:3 * SUB + R_CHUNK] = jnp.where(self.has_next, self.xn_ref[...], 0.0)
        xa = self.cb_ref[...] + self.cw_ref[0:1] * ext[0:R_CHUNK]
        for k in range(1, CONV_W):
            xa = xa + self.cw_ref[k:k + 1] * ext[k * SUB:k * SUB + R_CHUNK]
        self.xa = xa

    def gates(self):
        self.gz = _dot(self.xa.astype(BF16), self.wg_ref[...]) + self.bg_ref[...]

    def s5_recurrence(self):
        cblk = 4 * LANE
        for cb in range(S5_N // cblk):
            cols = slice(cb * cblk, (cb + 1) * cblk)
            ar = jnp.broadcast_to(self.ar_ref[:, cols], (SUB, cblk))
            ai = jnp.broadcast_to(self.ai_ref[:, cols], (SUB, cblk))
            hr = self.sre[:, cols]
            hi = self.sim[:, cols]
            for t in self.order:
                rows = slice(t * SUB, (t + 1) * SUB)
                nr = ar * hr - ai * hi + self.hre[rows, cols]
                ni = ar * hi + ai * hr + self.him[rows, cols]
                hr, hi = nr, ni
                self.hre[rows, cols] = hr
                self.him[rows, cols] = hi
            self.sre[:, cols] = hr
            self.sim[:, cols] = hi
        self.lr_ref[0] = self.sre[...]
        self.li_ref[0] = self.sim[...]

    def s5_readout(self):
        half = self.half
        for j in range(2):
            hc = jnp.concatenate([self.hre[:, j * half:(j + 1) * half], self.him[:, j * half:(j + 1) * half]],
                                 axis=1).astype(BF16)
            self.y_ref[:, j * 256:(j + 1) * 256] = _dot(hc, self.cm_ref[j])

    def lru_coefficients(self):
        r = 0.5 * jnp.tanh(0.5 * self.gz[:, :LRU_W]) + 0.5
        i = 0.5 * jnp.tanh(0.5 * self.gz[:, LRU_W:]) + 0.5
        lam = self.lam_ref[...]
        log_sig = jnp.minimum(lam, 0.0) - jnp.log1p(jnp.exp(-jnp.abs(lam)))
        log_a = LRU_C * r * log_sig
        a = jnp.exp(log_a)
        self.abuf[...] = a
        self.bbuf[...] = _sqrt_nonneg(_neg_expm1_2x(log_a, a)) * (i * self.xa)

    def lru_recurrence(self):
        h = self.hl[...]
        for t in self.order:
            rows = slice(t * SUB, (t + 1) * SUB)
            h = self.abuf[rows] * h + self.bbuf[rows]
            self.ha_ref[rows, :] = h
        self.hl[...] = h
        self.ll_ref[0] = h


def _ab_scan(xz, per_dir):
    tbl = jnp.asarray(np.concatenate([_scan_table(False), _scan_table(True)], axis=0))
    n_steps = tbl.shape[1]
    in_specs, out_specs, scratch, out_shape, args = [], [], [], [], []
    for d in range(2):
        blk = lambda s, t, d=d: t[d * SCAN_TBL_ROWS, s]
        grp = lambda s, t, d=d: t[d * SCAN_TBL_ROWS + 1, s]
        state_spec = lambda w, grp=grp: pl.BlockSpec((1, SUB, w), lambda s, t: (grp(s, t), 0, 0))
        const = lambda shape: pl.BlockSpec(shape, lambda s, t: (0,) * len(shape))
        in_specs += [
            pl.BlockSpec((R_CHUNK, LRU_W), lambda s, t, blk=blk: (blk(s, t), 0)),
            pl.BlockSpec((2 * SUB, LRU_W),
                         lambda s, t, blk=blk: (jnp.maximum(blk(s, t) * (T_CHUNK // 2) - 1, 0), 0)),
            pl.BlockSpec((SUB, LRU_W),
                         lambda s, t, blk=blk: (jnp.minimum((blk(s, t) + 1) * T_CHUNK, N // SUB - 1), 0)),
            pl.BlockSpec((R_CHUNK, S5_W), lambda s, t, blk=blk: (blk(s, t), 2)),
            state_spec(LRU_W), state_spec(S5_N), state_spec(S5_N),
            const((CONV_W, LRU_W)), const((1, LRU_W)),
            const((LRU_W, 2 * LRU_W)), const((1, 2 * LRU_W)), const((1, LRU_W)),
            const((2, 256, S5_N)), const((1, S5_N)), const((1, S5_N)), const((2, S5_N, 256)),
        ]
        out_specs += [
            pl.BlockSpec((R_CHUNK, LRU_W), lambda s, t, blk=blk: (blk(s, t), 0)),
            pl.BlockSpec((R_CHUNK, S5_W), lambda s, t, blk=blk: (blk(s, t), 0)),
            state_spec(LRU_W), state_spec(S5_N), state_spec(S5_N),
        ]
        scratch += [
            pltpu.VMEM((R_CHUNK + 3 * SUB, LRU_W), F32),
            pltpu.VMEM((R_CHUNK, LRU_W), F32), pltpu.VMEM((R_CHUNK, LRU_W), F32),
            pltpu.VMEM((R_CHUNK, S5_N), F32), pltpu.VMEM((R_CHUNK, S5_N), F32),
            pltpu.VMEM((SUB, LRU_W), F32), pltpu.VMEM((SUB, S5_N), F32), pltpu.VMEM((SUB, S5_N), F32),
        ]
        out_shape += [jax.ShapeDtypeStruct((N, LRU_W), F32), jax.ShapeDtypeStruct((N, S5_W), F32),
                      jax.ShapeDtypeStruct((3, SUB, LRU_W), F32),
                      jax.ShapeDtypeStruct((3, SUB, S5_N), F32), jax.ShapeDtypeStruct((3, SUB, S5_N), F32)]
        args += [xz, xz, xz, xz, *per_dir[d]]
    grid_spec = pltpu.PrefetchScalarGridSpec(num_scalar_prefetch=1, grid=(n_steps,), in_specs=in_specs,
                                             out_specs=out_specs, scratch_shapes=scratch)
    outs = pl.pallas_call(
        _scan_kernel,
        grid_spec=grid_spec,
        out_shape=out_shape,
        compiler_params=_cparams(("arbitrary",)),
        name="ab_scan",
    )(tbl, *args)
    return outs[:N_SCAN_OUT], outs[N_SCAN_OUT:]


def _about_kernel(haf, hab, yf, yr, ga, xb, xp_hbm, xs_hbm, gate, g1, d_ref, wglu, bglu, wout, o_ref, buf, sem):
    slot = _time_major_read(ABOUT_T, xp_hbm, xs_hbm, buf, sem)
    x_all = buf[slot].reshape(ABOUT_T * SUB, D)
    sub = ABOUT_T * SUB // ABOUT_SUB
    for r in range(ABOUT_SUB):
        rows = slice(r * sub, (r + 1) * sub)
        ya = (haf[rows, :] + hab[rows, :]) * _gelu(ga[rows, :])
        yb0 = _gelu(yf[rows, :] + yr[rows, :] + d_ref[...] * xb[rows, :])
        half_yb0 = 0.5 * yb0
        yb = half_yb0 + half_yb0 * jnp.tanh(0.5 * (_dot(yb0.astype(BF16), wglu[...]) + bglu[...]))
        out = _dot(ya.astype(BF16), wout[0:LRU_W]) + _dot(yb.astype(BF16), wout[LRU_W:LRU_W + S5_W])
        o_ref[rows, :] = _gated(x_all[rows], out, g1[...], gate[0])


def _ab_out(haf, hab, yf, yr, xz, xp, xs, mods, g1, s5_d, wglu, bglu, wout):
    tm = ABOUT_T * SUB
    grp = _grp_tm(tm)
    half = lambda c: pl.BlockSpec((tm, LRU_W), lambda i: (i, c))
    return pl.pallas_call(
        _about_kernel,
        grid=(N // tm,),
        in_specs=[half(0), half(0), half(0), half(0), half(1), half(2),
                  pl.BlockSpec(memory_space=pl.ANY), pl.BlockSpec(memory_space=pl.ANY),
                  _mod_spec(2, grp), _full((1, D)), _full((1, S5_W)),
                  _full((S5_W, S5_W)), _full((1, S5_W)), _full((LRU_W + S5_W, D))],
        out_specs=pl.BlockSpec((tm, D), lambda i: (i, 0)),
        out_shape=jax.ShapeDtypeStruct((N, D), F32),
        scratch_shapes=[pltpu.VMEM((2, ABOUT_T, SUB, D), F32), pltpu.SemaphoreType.DMA((2,))],
        compiler_params=_cparams(("arbitrary",)),
        name="ab_out",
    )(haf, hab, yf, yr, xz, xz, xp, xs, mods, g1, s5_d, wglu, bglu, wout)


def _swiglu_block(hbf, wgu, wd, acc, j):
    static = isinstance(j, int)
    blk = lambda start: pl.ds(start if static else pl.multiple_of(start, FF_BLK), FF_BLK)
    h = hbf[...]
    g = _dot(h, wgu[:, blk(j * FF_BLK)])
    u = _dot(h, wgu[:, blk(D_FF + j * FF_BLK)])
    act = (g * _sigmoid(g)) * u
    part = _dot(act.astype(BF16), wd[blk(j * FF_BLK), :])
    if static and j == 0:
        acc[...] = part
    else:
        acc[...] += part


def _ffn_kernel(x_ref, sh, sc, gt, g2, g3, wgu, wd, op_hbm, os_hbm, hbf, acc, obuf, sem):
    step = pl.program_id(0)
    slot = step % 2
    hbf[...] = _adaln(x_ref[...], g2[...], sc[0], sh[0]).astype(BF16)
    for j in range(N_FF):
        _swiglu_block(hbf, wgu, wd, acc, j)
    obuf[slot] = _gated(x_ref[...], acc[...], g3[...], gt[0]).reshape(FFN_T, SUB, D)
    _time_major_copies(step, FFN_T, op_hbm, os_hbm, obuf.at[slot], sem.at[slot], True)

    @pl.when(step > 0)
    def _():
        _time_major_wait(FFN_T, os_hbm, obuf.at[1 - slot], sem.at[1 - slot], True)

    @pl.when(step == pl.num_programs(0) - 1)
    def _():
        _time_major_wait(FFN_T, os_hbm, obuf.at[slot], sem.at[slot], True)


def _ffn(x, mods, g2, g3, wgu, wd):
    tm = FFN_T * SUB
    grp = _grp_tm(tm)
    once = lambda shape: pl.BlockSpec(shape, lambda i: (0,) * len(shape), pipeline_mode=pl.Buffered(1))
    return pl.pallas_call(
        _ffn_kernel,
        grid=(N // tm,),
        in_specs=[pl.BlockSpec((tm, D), lambda i: (i, 0)),
                  _mod_spec(3, grp), _mod_spec(4, grp), _mod_spec(5, grp),
                  _full((1, D)), _full((1, D)),
                  once(wgu.shape), once(wd.shape)],
        out_specs=[pl.BlockSpec(memory_space=pl.ANY), pl.BlockSpec(memory_space=pl.ANY)],
        out_shape=[jax.ShapeDtypeStruct((BATCH, SEQ, D), F32), jax.ShapeDtypeStruct((DEC_BATCH, DEC_SEQ, D), F32)],
        scratch_shapes=[pltpu.VMEM((tm, D), BF16), pltpu.VMEM((tm, D), F32),
                        pltpu.VMEM((2, FFN_T, SUB, D), F32), pltpu.SemaphoreType.DMA((2,))],
        compiler_params=_cparams(("arbitrary",)),
        name="ffn",
    )(x, mods, mods, mods, g2, g3, wgu, wd)


def _pair_swap(x):
    outs = []
    for c in range(x.shape[1] // LANE):
        xc = x[:, c * LANE:(c + 1) * LANE]
        even = lax.broadcasted_iota(I32, xc.shape, 1) % 2 == 0
        outs.append(jnp.where(even, pltpu.roll(xc, LANE - 1, 1), pltpu.roll(xc, 1, 1)))
    return outs[0] if len(outs) == 1 else jnp.concatenate(outs, axis=1)


def _stream_specs(tm, width):
    n_p = NP // tm
    return [pl.BlockSpec((tm, width), lambda i: (jnp.minimum(i, n_p - 1), 0)),
            pl.BlockSpec((tm, width), lambda i: (jnp.maximum(i - n_p, 0), 0))]


def _stream_rows(p_ref, s_ref, prompt_steps):
    return jnp.where(pl.program_id(0) < prompt_steps, p_ref[...], s_ref[...])


def _mlaproj_kernel(xp_ref, xs_ref, g0, sh, sc, w1, gq, gkv, wuq, wukv, cq_ref, sq_ref, ck_ref, sk_ref,
                    qn_ref, qr_ref, kn_ref, v_ref, kr2_ref, ckv_ref, krr_ref, *, prompt_steps):
    x = _stream_rows(xp_ref, xs_ref, prompt_steps)
    sub = x.shape[0] // MLA_SUB
    cache_rows = []
    for r in range(MLA_SUB):
        rows = slice(r * sub, (r + 1) * sub)
        h = _adaln(x[rows], g0[...], sc[0], sh[0])
        dn = _dot(h.astype(BF16), w1[...])
        cq = _rms(dn[:, :Q_LORA], gq[...])
        ckv = _rms(dn[:, Q_LORA:Q_LORA + KV_LORA], gkv[...])
        krp = dn[:, Q_LORA + KV_LORA:]
        cache_rows.append((rows, ckv, krp))
        q = _dot(cq.astype(BF16), wuq[...])
        qn_ref[rows, :] = q[:, :HEADS * QK_NOPE].astype(BF16)
        qr = q[:, HEADS * QK_NOPE:]
        qr_ref[rows, :] = (qr * cq_ref[rows, :] + _pair_swap(qr) * sq_ref[rows, :]).astype(BF16)
        kv = _dot(ckv.astype(BF16), wukv[...])
        kn_ref[rows, :] = kv[:, :HEADS * QK_NOPE].astype(BF16)
        v_ref[rows, :] = kv[:, HEADS * QK_NOPE:].astype(BF16)
        kr = krp * ck_ref[rows, :] + _pair_swap(krp) * sk_ref[rows, :]
        kr2_ref[rows, :] = jnp.concatenate([kr, pltpu.roll(kr, QK_ROPE, 1)], axis=1).astype(BF16)

    @pl.when(pl.program_id(0) < prompt_steps)
    def _():
        for rows, ckv, krp in cache_rows:
            ckv_ref[rows, :] = ckv
            krr_ref[rows, :] = krp


def _mla_proj(xp, xs, g0, mods, w1, gq, gkv, wuq, wukv, cos_q, sin_q, cos_k, sin_k):
    tm = MLA_TM
    grp = _grp_bm(tm)
    n_pos = DEC_SEQ // tm
    tab = lambda w: pl.BlockSpec((tm, w), lambda i: (jnp.where(i * tm < NP, n_pos, (i - NP // tm) % n_pos), 0))
    row = lambda w: pl.BlockSpec((tm, w), lambda i: (i, 0))
    shp = lambda w, dt: jax.ShapeDtypeStruct((N, w), dt)
    n_p = NP // tm
    prow = lambda w: pl.BlockSpec((tm, w), lambda i: (jnp.minimum(i, n_p - 1), 0))
    return pl.pallas_call(
        functools.partial(_mlaproj_kernel, prompt_steps=n_p),
        grid=(N // tm,),
        in_specs=_stream_specs(tm, D) + [_full((1, D)), _mod_spec(0, grp), _mod_spec(1, grp),
                  _full(w1.shape), _full((1, Q_LORA)), _full((1, KV_LORA)),
                  _full(wuq.shape), _full(wukv.shape),
                  tab(HEADS * QK_ROPE), tab(HEADS * QK_ROPE), tab(LANE), tab(LANE)],
        out_specs=[row(HEADS * QK_NOPE), row(HEADS * QK_ROPE), row(HEADS * QK_NOPE), row(HEADS * V_DIM),
                   row(2 * LANE), prow(KV_LORA), prow(LANE)],
        out_shape=[shp(HEADS * QK_NOPE, BF16), shp(HEADS * QK_ROPE, BF16), shp(HEADS * QK_NOPE, BF16),
                   shp(HEADS * V_DIM, BF16), shp(2 * LANE, BF16),
                   jax.ShapeDtypeStruct((NP, KV_LORA), F32), jax.ShapeDtypeStruct((NP, LANE), F32)],
        compiler_params=_cparams(("arbitrary",)),
        name="mla_proj",
    )(xp, xs, g0, mods, mods, w1, gq, gkv, wuq, wukv, cos_q, sin_q, cos_k, sin_k)


def _cachekv_kernel(c_ref, w_ref, kn_ref, v_ref):
    kv = _dot(c_ref[...].astype(BF16), w_ref[...])
    kn_ref[...] = kv[:, :HEADS * QK_NOPE].astype(BF16)
    v_ref[...] = kv[:, HEADS * QK_NOPE:].astype(BF16)


def _cache_kv(ckv_cache, wukv):
    rows = ckv_cache.shape[0]
    tm = 512
    return pl.pallas_call(
        _cachekv_kernel,
        grid=(rows // tm,),
        in_specs=[pl.BlockSpec((tm, KV_LORA), lambda i: (i, 0)), _full(wukv.shape)],
        out_specs=[pl.BlockSpec((tm, HEADS * QK_NOPE), lambda i: (i, 0)),
                   pl.BlockSpec((tm, HEADS * V_DIM), lambda i: (i, 0))],
        out_shape=[jax.ShapeDtypeStruct((rows, HEADS * QK_NOPE), BF16),
                   jax.ShapeDtypeStruct((rows, HEADS * V_DIM), BF16)],
        compiler_params=_cparams(("arbitrary",)),
        name="cache_kv",
    )(ckv_cache, wukv)


def _attn_kernel(*refs, has_cache):
    if has_cache:
        qn, qr, kn, kr, v, knc, krc, vc, o_ref, s_scr = refs
        streams = [(knc, krc, vc), (kn, kr, v)]
    else:
        qn, qr, kn, kr, v, o_ref, s_scr = refs
        streams = [(kn, kr, v)]
    chunks = [(k1, k2, vv, c * KEY_BLK) for k1, k2, vv in streams for c in range(k1.shape[0] // KEY_BLK)]
    tq = qn.shape[0]
    a = (QK_NOPE + QK_ROPE) ** -0.5 * math.log2(math.e)
    for hh in range(HEAD_GRP):
        cols = slice(hh * LANE, (hh + 1) * LANE)
        pair_cols = slice((hh // 2) * LANE, (hh // 2 + 1) * LANE)
        kr_cols = slice((hh % 2) * LANE, (hh % 2 + 1) * LANE)
        q = jnp.concatenate([qn[:, cols], qr[:, pair_cols]], axis=1)
        mx = jnp.full((tq, LANE), -jnp.inf, F32)
        for n, (k1, k2, _, r0) in enumerate(chunks):
            k = jnp.concatenate([k1[r0:r0 + KEY_BLK, cols], k2[r0:r0 + KEY_BLK, kr_cols]], axis=1)
            s = lax.dot_general(q, k, NT_DIMS, preferred_element_type=F32)
            s_scr[hh, :, n * KEY_BLK:(n + 1) * KEY_BLK] = s
            for c in range(KEY_BLK // LANE):
                mx = jnp.maximum(mx, s[:, c * LANE:(c + 1) * LANE])
        mb = jnp.max(mx, axis=-1, keepdims=True) * a
        den = jnp.zeros((tq, LANE), F32)
        o = jnp.zeros((tq, V_DIM), F32)
        for n, (_, _, vv, r0) in enumerate(chunks):
            p = jnp.exp2(s_scr[hh, :, n * KEY_BLK:(n + 1) * KEY_BLK] * a - mb)
            for c in range(KEY_BLK // LANE):
                den = den + p[:, c * LANE:(c + 1) * LANE]
            o = o + _dot(p.astype(BF16), vv[r0:r0 + KEY_BLK, cols])
        o_ref[:, cols] = (o / jnp.sum(den, axis=-1, keepdims=True)).astype(BF16)


def _attention(qn, qr, kn, kr2, v, *, row0, n_seq, seq, tq, cache=None):
    nq = seq // tq
    grp = HEAD_GRP * LANE
    qblk = lambda b, h, i: row0 // tq + b * nq + i
    kblk = lambda b: row0 // seq + b
    in_specs = [pl.BlockSpec((tq, grp), lambda b, h, i: (qblk(b, h, i), h)),
                pl.BlockSpec((tq, grp // 2), lambda b, h, i: (qblk(b, h, i), h)),
                pl.BlockSpec((seq, grp), lambda b, h, i: (kblk(b), h)),
                pl.BlockSpec((seq, 2 * LANE), lambda b, h, i: (kblk(b), 0)),
                pl.BlockSpec((seq, grp), lambda b, h, i: (kblk(b), h))]
    args = [qn, qr, kn, kr2, v]
    if cache is not None:
        knc, kr2c, vc = cache
        in_specs += [pl.BlockSpec((PAST_LEN, grp), lambda b, h, i: (b, h)),
                     pl.BlockSpec((PAST_LEN, 2 * LANE), lambda b, h, i: (b, 0)),
                     pl.BlockSpec((PAST_LEN, grp), lambda b, h, i: (b, h))]
        args += [knc, kr2c, vc]
    return pl.pallas_call(
        functools.partial(_attn_kernel, has_cache=cache is not None),
        grid=(n_seq, HEADS // HEAD_GRP, nq),
        in_specs=in_specs,
        out_specs=pl.BlockSpec((tq, grp), lambda b, h, i: (b * nq + i, h)),
        out_shape=jax.ShapeDtypeStruct((n_seq * seq, HEADS * V_DIM), BF16),
        scratch_shapes=[pltpu.VMEM((HEAD_GRP, tq, seq + (PAST_LEN if cache is not None else 0)), F32)],
        compiler_params=_cparams(("arbitrary", "arbitrary", "arbitrary")),
        name="attn_latent" if cache is not None else "attn_context",
    )(*args)


def _router_kernel(op_ref, os_ref, xp_ref, xs_ref, gate1, g1, sh2, sc2, g2, wout, wr_hi, wr_lo, tri,
                   x3_ref, h_ref, ri_ref, rf_ref, cnt_ref, carry, *, prompt_steps):
    step = pl.program_id(0)

    @pl.when(step == 0)
    def _():
        carry[...] = jnp.zeros_like(carry)

    o_all = _stream_rows(op_ref, os_ref, prompt_steps)
    x_all = _stream_rows(xp_ref, xs_ref, prompt_steps)
    sub = x_all.shape[0] // ROUTER_SUB
    dg = lambda a, b: lax.dot_general(a, b, NT_DIMS, preferred_element_type=F32)
    logits = []
    for r in range(ROUTER_SUB):
        rows = slice(r * sub, (r + 1) * sub)
        x3 = _gated(x_all[rows], _dot(o_all[rows], wout[...]), g1[...], gate1[0])
        x3_ref[rows, :] = x3
        h = _adaln(x3, g2[...], sc2[0], sh2[0])
        _store_row_tiles(h_ref, lambda cols, h=h: h[:, cols], r * sub, sub)
        h_hi = h.astype(BF16)
        h_lo = (h - h_hi.astype(F32)).astype(BF16)
        logits.append(dg(wr_hi[...], h_hi) + dg(wr_hi[...], h_lo) + dg(wr_lo[...], h_hi))
    lg = jnp.concatenate(logits, axis=1)
    eidx = lax.broadcasted_iota(I32, lg.shape, 0).astype(F32)
    m1 = jnp.max(lg, axis=0, keepdims=True)
    i1 = jnp.min(jnp.where(lg == m1, eidx, float(N_EXP)), axis=0, keepdims=True)
    sel1 = eidx == i1
    lg2 = jnp.where(sel1, -jnp.inf, lg)
    m2 = jnp.max(lg2, axis=0, keepdims=True)
    i2 = jnp.min(jnp.where(lg2 == m2, eidx, float(N_EXP)), axis=0, keepdims=True)
    sel2 = eidx == i2
    e = jnp.exp(m2 - m1)
    w1 = 1.0 / (1.0 + e)
    w2 = e / (1.0 + e)
    picked = jnp.where(sel1 | sel2, 1.0, 0.0)
    rank = _dot(picked.astype(BF16), tri[...]) + carry[:, 0:1]
    r1 = jnp.sum(jnp.where(sel1, rank, 0.0), axis=0, keepdims=True)
    r2 = jnp.sum(jnp.where(sel2, rank, 0.0), axis=0, keepdims=True)
    carry[...] = carry[...] + jnp.sum(picked, axis=1, keepdims=True)
    cnt_ref[...] = carry[...]
    ri_ref[...] = jnp.where(eidx == 0.0, i1, jnp.where(eidx == 1.0, i2, jnp.where(eidx == 2.0, r1, r2))).astype(I32)
    rf_ref[...] = jnp.where(eidx == 0.0, w1, w2)


def _attn_out_router(o_p, o_s, xp, xs, mods, g1, g2, wout, wr_hi, wr_lo):
    tm = 512
    grp = _grp_bm(tm)
    n_p = NP // tm
    tri = jnp.asarray(np.triu(np.ones((tm, tm), np.float32), 1), BF16)
    row = lambda w: pl.BlockSpec((tm, w), lambda i: (i, 0))
    col = pl.BlockSpec((N_EXP, tm), lambda i: (0, i))
    return pl.pallas_call(
        functools.partial(_router_kernel, prompt_steps=n_p),
        grid=(N // tm,),
        in_specs=_stream_specs(tm, HEADS * V_DIM) + _stream_specs(tm, D) + [_mod_spec(2, grp), _full((1, D)),
                  _mod_spec(3, grp), _mod_spec(4, grp), _full((1, D)),
                  _full((HEADS * V_DIM, D)), _full((N_EXP, D)), _full((N_EXP, D)), _full((tm, tm))],
        out_specs=[row(D), pl.BlockSpec((tm * ROW_TILE, LANE), lambda i: (i, 0)), col, col, _full((N_EXP, LANE))],
        out_shape=[jax.ShapeDtypeStruct((N, D), F32), jax.ShapeDtypeStruct((N * ROW_TILE, LANE), F32),
                   jax.ShapeDtypeStruct((N_EXP, N), I32), jax.ShapeDtypeStruct((N_EXP, N), F32),
                   jax.ShapeDtypeStruct((N_EXP, LANE), F32)],
        scratch_shapes=[pltpu.VMEM((N_EXP, LANE), F32)],
        compiler_params=_cparams(("arbitrary",)),
        name="attn_out_router",
    )(o_p, o_s, xp, xs, mods, g1, mods, mods, g2, wout, wr_hi, wr_lo, tri)


def _tile_rows(r):
    return r * ROW_TILE if isinstance(r, int) else pl.multiple_of(r * ROW_TILE, ROW_TILE)


def _moe_kernel(te, nu, src0_ref, src1_ref, dstp_ref, dstc_ref, h_hbm, wgu, wd, y_hbm,
                hsbuf, ybuf, hbf, acc, sem_g, sem_s):
    i = pl.program_id(0)
    last = pl.num_programs(0) - 1
    cur = i % 2
    nxt = 1 - cur
    buf_rows = TM_E * ROW_TILE

    def gather(idx_ref, r, slot):
        i_src = pl.multiple_of(idx_ref[0, 0, r] * ROW_TILE, ROW_TILE)
        return pltpu.make_async_copy(h_hbm.at[pl.ds(i_src, ROW_TILE), :],
                                     hsbuf.at[slot, pl.ds(_tile_rows(r), ROW_TILE), :], sem_g.at[slot])

    def scatter(idx_ref, r, slot):
        i_dst = pl.multiple_of(idx_ref[0, 0, r] * ROW_TILE, ROW_TILE)
        return pltpu.make_async_copy(ybuf.at[slot, pl.ds(_tile_rows(r), ROW_TILE), :],
                                     y_hbm.at[pl.ds(i_dst, ROW_TILE), :], sem_s.at[slot])

    def wait_gather(slot):
        pltpu.make_async_copy(h_hbm.at[pl.ds(0, buf_rows), :], hsbuf.at[slot], sem_g.at[slot]).wait()

    def wait_scatter(slot):
        pltpu.make_async_copy(ybuf.at[slot], y_hbm.at[pl.ds(0, buf_rows), :], sem_s.at[slot]).wait()

    def for_rows(fn):
        def body(g, carry):
            for k in range(ROW_TILE):
                fn(g * ROW_TILE + k, k % 2)
            return carry
        lax.fori_loop(0, TM_E // ROW_TILE, body, 0)

    def side_traffic(r, queue):
        gather(src1_ref, r, nxt).start(priority=queue)
        scatter(dstp_ref, r, nxt).start(priority=queue)

    @pl.when(i == 0)
    def _():
        ybuf[1] = jnp.zeros((buf_rows, LANE), F32)
        for_rows(lambda r, queue: gather(src0_ref, r, 0).start(priority=queue))

    wait_gather(cur)

    @pl.when(i < nu[0])
    def _():
        for c in range(ROW_TILE):
            hbf[:, c * LANE:(c + 1) * LANE] = hsbuf[cur, pl.ds(c, TM_E, stride=ROW_TILE), :].astype(BF16)
        w_gu = wgu.at[0]
        w_d = wd.at[0]
        _swiglu_block(hbf, w_gu, w_d, acc, 0)
        per_iter = TM_E // SIDE_ITERS

        def body(t, carry):
            _swiglu_block(hbf, w_gu, w_d, acc, 1 + 2 * t)
            _swiglu_block(hbf, w_gu, w_d, acc, 2 + 2 * t)
            for k in range(per_iter):
                side_traffic(t * per_iter + k, k % 2)
            return carry

        lax.fori_loop(0, SIDE_ITERS, body, 0)
        for j in range(1 + 2 * SIDE_ITERS, N_FF):
            _swiglu_block(hbf, w_gu, w_d, acc, j)
        _store_row_tiles(ybuf.at[cur], lambda cols: acc[:, cols])

    @pl.when(i >= nu[0])
    def _():
        for_rows(side_traffic)
        ybuf[cur] = jnp.zeros((buf_rows, LANE), F32)

    wait_scatter(nxt)

    @pl.when(i == last)
    def _():
        for_rows(lambda r, queue: scatter(dstc_ref, r, cur).start(priority=queue))
        wait_scatter(cur)
        wait_gather(nxt)


def _moe_experts(tile_expert, n_used, src_tbl, dst_tbl, h, wgu, wd):
    smem = lambda off: pl.BlockSpec((1, 1, TM_E), lambda i, te, nu: (i + off, 0, 0), memory_space=pltpu.SMEM)
    grid_spec = pltpu.PrefetchScalarGridSpec(
        num_scalar_prefetch=2,
        grid=(N_TILES,),
        in_specs=[smem(0), smem(1), smem(0), smem(1),
                  pl.BlockSpec(memory_space=pl.ANY),
                  pl.BlockSpec((1,) + wgu.shape[1:], lambda i, te, nu: (te[i], 0, 0)),
                  pl.BlockSpec((1,) + wd.shape[1:], lambda i, te, nu: (te[i], 0, 0))],
        out_specs=pl.BlockSpec(memory_space=pl.ANY),
        scratch_shapes=[pltpu.VMEM((2, TM_E * ROW_TILE, LANE), F32), pltpu.VMEM((2, TM_E * ROW_TILE, LANE), F32),
                        pltpu.VMEM((TM_E, D), BF16), pltpu.VMEM((TM_E, D), F32),
                        pltpu.SemaphoreType.DMA((2,)), pltpu.SemaphoreType.DMA((2,))],
    )
    return pl.pallas_call(
        _moe_kernel,
        grid_spec=grid_spec,
        out_shape=jax.ShapeDtypeStruct(((2 * N + DUMP_ROWS) * ROW_TILE, LANE), F32),
        compiler_params=_cparams(("arbitrary",)),
        name="moe_experts",
    )(tile_expert, n_used, src_tbl, src_tbl, dst_tbl, dst_tbl, h, wgu, wd)


def _combine_kernel(y1_ref, y2_ref, x_ref, w1_ref, w2_ref, gate2, g3, op_ref, os_ref, *, prompt_steps):
    rows = x_ref.shape[0]
    w1 = w1_ref[...]
    w2 = w2_ref[...]
    f = jnp.concatenate([w1 * _load_row_tiles(y1_ref, c, rows) + w2 * _load_row_tiles(y2_ref, c, rows)
                         for c in range(ROW_TILE)], axis=1)
    out = _gated(x_ref[...], f, g3[...], gate2[0])
    step = pl.program_id(0)

    @pl.when(step < prompt_steps)
    def _():
        op_ref[...] = out

    @pl.when(step >= prompt_steps)
    def _():
        os_ref[...] = out


def _moe_combine(y, x, w1, w2, mods, g3):
    tm = 512
    grp = _grp_bm(tm)
    nb = N // tm
    n_p = NP // tm
    return pl.pallas_call(
        functools.partial(_combine_kernel, prompt_steps=n_p),
        grid=(nb,),
        in_specs=[pl.BlockSpec((tm * ROW_TILE, LANE), lambda i: (i, 0)),
                  pl.BlockSpec((tm * ROW_TILE, LANE), lambda i: (nb + i, 0)),
                  pl.BlockSpec((tm, D), lambda i: (i, 0)),
                  pl.BlockSpec((tm, 1), lambda i: (i, 0)), pl.BlockSpec((tm, 1), lambda i: (i, 0)),
                  _mod_spec(5, grp), _full((1, D))],
        out_specs=[pl.BlockSpec((tm, D), lambda i: (jnp.minimum(i, n_p - 1), 0)),
                   pl.BlockSpec((tm, D), lambda i: (jnp.maximum(i - n_p, 0), 0))],
        out_shape=[jax.ShapeDtypeStruct((NP, D), F32), jax.ShapeDtypeStruct((NS, D), F32)],
        compiler_params=_cparams(("arbitrary",)),
        name="moe_combine",
    )(y, y, x, w1, w2, mods, g3)


def _block_diag(w):
    hh, a, b = w.shape
    eye = jnp.eye(hh, dtype=w.dtype)
    return jnp.einsum('hab,hk->hakb', w, eye).reshape(hh * a, hh * b)


def _s5_matrices(a_re, a_im, log_dt, b_re, b_im, c_re, c_im):
    dt = jnp.exp(log_dt)[:, None]
    mag = jnp.exp(a_re * dt)
    abr = mag * jnp.cos(a_im * dt)
    abi = mag * jnp.sin(a_im * dt)
    den = a_re * a_re + a_im * a_im
    cr = ((abr - 1.0) * a_re + abi * a_im) / den
    ci = (abi * a_re - (abr - 1.0) * a_im) / den
    bbr = cr[..., None] * b_re - ci[..., None] * b_im
    bbi = cr[..., None] * b_im + ci[..., None] * b_re
    hg = S5_G // 2
    eye = jnp.eye(hg, dtype=F32)
    bms, cms = [], []
    for j in range(2):
        sl = slice(j * hg, (j + 1) * hg)
        bd = lambda m: jnp.einsum('gpc,gh->gchp', m[sl], eye).reshape(hg * S5_CH, hg * S5_P)
        bms.append(jnp.concatenate([bd(bbr), bd(bbi)], axis=1))
        cd = lambda m: jnp.einsum('gcp,gh->gphc', m[sl], eye).reshape(hg * S5_P, hg * S5_CH)
        cms.append(jnp.concatenate([cd(c_re), cd(-c_im)], axis=0))
    return (jnp.stack(bms).astype(BF16), abr.reshape(1, S5_N), abi.reshape(1, S5_N),
            jnp.stack(cms).astype(BF16))


def _rope_tables(tm):
    rows = DEC_SEQ // GRID_W
    row = jnp.repeat(jnp.arange(rows, dtype=F32), GRID_W)
    col = jnp.tile(jnp.arange(GRID_W, dtype=F32), rows)
    nf = QK_ROPE // 4
    inv = ROPE_THETA ** (-jnp.arange(nf, dtype=F32) / nf)
    ang = jnp.concatenate([row[:, None] * inv, col[:, None] * inv], axis=-1)
    cos = jnp.repeat(jnp.cos(ang), 2, axis=-1)
    sin = jnp.stack([-jnp.sin(ang), jnp.sin(ang)], axis=-1).reshape(DEC_SEQ, QK_ROPE)
    ident = lambda t, one: jnp.concatenate([t, jnp.full((tm, t.shape[1]), one, F32)], axis=0)
    cos_q = ident(jnp.tile(cos, (1, HEADS)), 1.0)
    sin_q = ident(jnp.tile(sin, (1, HEADS)), 0.0)
    pad = lambda t, one: jnp.concatenate([t, jnp.full((DEC_SEQ, LANE - QK_ROPE), one, F32)], axis=1)
    return cos_q, sin_q, ident(pad(cos, 1.0), 1.0), ident(pad(sin, 0.0), 0.0)


def _group_states(prompt_state, sample_state):
    w = sample_state.shape[-1]
    return jnp.concatenate([prompt_state.reshape(2, SUB, w), sample_state.reshape(1, SUB, w)], axis=0)


def _layer_ab(xp, xs, m, ng, j, state_lru, state_s5_re, state_s5_im, p):
    xz = _ab_inproj(xp, xs, ng[0:1], m, p['ab_w_in'][j].astype(BF16))
    zeros = lambda w: jnp.zeros((BATCH, w), F32)
    per_dir = []
    for d in range(2):
        wg = jnp.concatenate([_block_diag(p['lru_wa'][j, d]), _block_diag(p['lru_wx'][j, d])], axis=1).astype(BF16)
        bg = jnp.concatenate([p['lru_ba'][j, d], p['lru_bx'][j, d]])[None]
        bm, ar, ai, cm = _s5_matrices(p['s5_a_re'][j, d], p['s5_a_im'][j, d], p['s5_log_dt'][j, d],
                                      p['s5_b_re'][j, d], p['s5_b_im'][j, d], p['s5_c_re'][j, d], p['s5_c_im'][j, d])
        h0l = _group_states(zeros(LRU_W), state_lru[:, j, d])
        h0r = _group_states(zeros(S5_N), state_s5_re[:, j, d].reshape(DEC_BATCH, S5_N))
        h0i = _group_states(zeros(S5_N), state_s5_im[:, j, d].reshape(DEC_BATCH, S5_N))
        per_dir.append((h0l, h0r, h0i, p['ab_conv_w'][j], p['ab_conv_b'][j][None], wg, bg,
                        p['lru_lambda'][j, d][None], bm, ar, ai, cm))
    (haf, yf, llf, lrf, lif), (hab, yr, llb, lrb, lib) = _ab_scan(xz, per_dir)
    x = _ab_out(haf, hab, yf, yr, xz, xp, xs, m, ng[1:2], p['s5_d'][j][None], p['s5_w_glu'][j].astype(BF16),
                p['s5_b_glu'][j][None], p['ab_w_out'][j].astype(BF16))
    streams = _ffn(x, m, ng[2:3], ng[3:4], p['ffn_w_gate_up'][j].astype(BF16), p['ffn_w_down'][j].astype(BF16))
    prompt = lambda f, b, w: jnp.stack([f[:2].reshape(BATCH, w), b[:2].reshape(BATCH, w)], axis=1)
    lru = prompt(llf, llb, LRU_W)
    s5r = prompt(lrf, lrb, S5_N).reshape(BATCH, 2, S5_G, S5_P)
    s5i = prompt(lif, lib, S5_N).reshape(BATCH, 2, S5_G, S5_P)
    return tuple(streams), lru, s5r, s5i


def _head_major(w, parts):
    k = w.shape[0]
    per_head = w.reshape(k, HEADS, -1)
    out, start = [], 0
    for width in parts:
        out.append(per_head[:, :, start:start + width].reshape(k, HEADS * width))
        start += width
    return jnp.concatenate(out, axis=1)


def _layer_mla_moe(xp, xs, m, ng, j, cache_kv_latent, cache_k_rope, p):
    w1 = jnp.concatenate([p['mla_w_in'][j], jnp.zeros((D, LANE - QK_ROPE), F32)], axis=1).astype(BF16)
    wuq = _head_major(p['mla_w_uq'][j], (QK_NOPE, QK_ROPE)).astype(BF16)
    wukv = _head_major(p['mla_w_ukv'][j], (QK_NOPE, V_DIM)).astype(BF16)
    tables = _rope_tables(MLA_TM)
    qn, qr, kn, v, kr2, ckv, krr = _mla_proj(xp, xs, ng[0:1], m, w1, p['mla_g_q'][j][None], p['mla_g_kv'][j][None],
                                              wuq, wukv, *tables)
    knc, vc = _cache_kv(cache_kv_latent[:, j].reshape(DEC_BATCH * PAST_LEN, KV_LORA), wukv)
    krc = cache_k_rope[:, j].reshape(DEC_BATCH * PAST_LEN, QK_ROPE)
    z = jnp.zeros_like(krc)
    kr2c = jnp.concatenate([krc, z, z, krc], axis=1).astype(BF16)
    o_p = _attention(qn, qr, kn, kr2, v, row0=0, n_seq=BATCH, seq=SEQ, tq=SEQ)
    o_s = _attention(qn, qr, kn, kr2, v, row0=NP, n_seq=DEC_BATCH, seq=DEC_SEQ, tq=1024, cache=(knc, kr2c, vc))
    wr_t = p['moe_w_router'][j].T
    wr_hi = wr_t.astype(BF16)
    wr_lo = (wr_t - wr_hi.astype(F32)).astype(BF16)
    x3, h, ri, rf, cnt = _attn_out_router(o_p, o_s, xp, xs, m, ng[1:2], ng[2:3], p['mla_w_out'][j].astype(BF16), wr_hi, wr_lo)
    counts = cnt[:, 0].astype(I32)
    padded = ((counts + TM_E - 1) // TM_E) * TM_E
    ends = jnp.cumsum(padded)
    offs = ends - padded
    pos1 = offs[ri[0]] + ri[2]
    pos2 = offs[ri[1]] + ri[3]
    pick_tok = jnp.arange(2 * N, dtype=I32)
    dest = jnp.full((P_ROWS,), -1, I32).at[jnp.concatenate([pos1, pos2])].set(pick_tok, unique_indices=True)
    is_pad = dest < 0
    pad_row = 2 * N + TM_E + jnp.cumsum(is_pad.astype(I32)) - 1
    src_tbl = jnp.concatenate([jnp.where(is_pad, 0, dest % N), jnp.zeros((TM_E,), I32)])
    dst_tbl = jnp.concatenate([2 * N + jnp.arange(TM_E, dtype=I32), jnp.where(is_pad, pad_row, dest)])
    n_used = (ends[-1] // TM_E).astype(I32)[None]
    tile_row = jnp.minimum(jnp.arange(N_TILES, dtype=I32), n_used - 1) * TM_E
    tile_expert = jnp.sum((tile_row[:, None] >= ends[None, :]).astype(I32), axis=1)
    y = _moe_experts(tile_expert, n_used, src_tbl.reshape(N_TILES + 1, 1, TM_E), dst_tbl.reshape(N_TILES + 1, 1, TM_E),
                     h, p['moe_w_gate_up'][j].astype(BF16), p['moe_w_down'][j].astype(BF16))
    xp, xs = _moe_combine(y, x3, rf[0][:, None], rf[1][:, None], m, ng[3:4])
    kv_new = ckv.reshape(BATCH, SEQ, KV_LORA)
    kr_new = krr[:, :QK_ROPE].reshape(BATCH, SEQ, QK_ROPE)
    return (xp.reshape(BATCH, SEQ, D), xs.reshape(DEC_BATCH, DEC_SEQ, D)), kv_new, kr_new


def kernel(x_prompt, x_sample, c, state_lru, state_s5_re, state_s5_im, cache_kv_latent, cache_k_rope, c_ctx, w_mod, b_mod, norm_gains, ab_w_in, ab_conv_w, ab_conv_b, lru_wa, lru_ba, lru_wx, lru_bx, lru_lambda, s5_a_re, s5_a_im, s5_log_dt, s5_b_re, s5_b_im, s5_c_re, s5_c_im, s5_d, s5_w_glu, s5_b_glu, ab_w_out, ffn_w_gate_up, ffn_w_down, mla_w_in, mla_g_q, mla_g_kv, mla_w_uq, mla_w_ukv, mla_w_out, moe_w_router, moe_w_gate_up, moe_w_down):
    p = dict(ab_w_in=ab_w_in, ab_conv_w=ab_conv_w, ab_conv_b=ab_conv_b, lru_wa=lru_wa, lru_ba=lru_ba,
             lru_wx=lru_wx, lru_bx=lru_bx, lru_lambda=lru_lambda, s5_a_re=s5_a_re, s5_a_im=s5_a_im,
             s5_log_dt=s5_log_dt, s5_b_re=s5_b_re, s5_b_im=s5_b_im, s5_c_re=s5_c_re, s5_c_im=s5_c_im,
             s5_d=s5_d, s5_w_glu=s5_w_glu, s5_b_glu=s5_b_glu, ab_w_out=ab_w_out, ffn_w_gate_up=ffn_w_gate_up,
             ffn_w_down=ffn_w_down, mla_w_in=mla_w_in, mla_g_q=mla_g_q, mla_g_kv=mla_g_kv, mla_w_uq=mla_w_uq,
             mla_w_ukv=mla_w_ukv, mla_w_out=mla_w_out, moe_w_router=moe_w_router, moe_w_gate_up=moe_w_gate_up,
             moe_w_down=moe_w_down)
    depth = w_mod.shape[0]
    cond = jnp.concatenate([c_ctx[None], c, jnp.zeros((2 * SUB - 1 - DEC_BATCH, D), F32)], axis=0)
    mod = _modulation(cond, w_mod, b_mod)
    ctx_tile = lambda l: jnp.broadcast_to(mod[l, 0:1], (SUB, 6 * D))
    streams = (x_prompt, x_sample)
    lru_l, s5r_l, s5i_l, kv_l, kr_l = [], [], [], [], []
    for layer in range(depth):
        j = layer // 2
        ng = norm_gains[layer]
        if layer % 2 == 0:
            m = jnp.stack([ctx_tile(layer), mod[layer, 1:1 + DEC_BATCH]])
            streams, lru, s5r, s5i = _layer_ab(*streams, m, ng, j, state_lru, state_s5_re, state_s5_im, p)
            lru_l.append(lru)
            s5r_l.append(s5r)
            s5i_l.append(s5i)
        else:
            lat = jnp.broadcast_to(mod[layer, 1:1 + DEC_BATCH, None, :], (DEC_BATCH, SUB, 6 * D))
            m = jnp.concatenate([ctx_tile(layer)[None], lat], axis=0)
            streams, kv_new, kr_new = _layer_mla_moe(streams[0].reshape(NP, D), streams[1].reshape(NS, D), m, ng, j,
                                                     cache_kv_latent, cache_k_rope, p)
            kv_l.append(kv_new)
            kr_l.append(kr_new)
    return (streams[0], streams[1],
            jnp.stack(lru_l, axis=1), jnp.stack(s5r_l, axis=1), jnp.stack(s5i_l, axis=1),
            jnp.stack(kv_l, axis=1), jnp.stack(kr_l, axis=1))
```

```python
import functools
import math

import numpy as np
import jax
import jax.numpy as jnp
from jax import lax
from jax.experimental import pallas as pl
from jax.experimental.pallas import tpu as pltpu

F32 = jnp.float32
BF16 = jnp.bfloat16
I32 = jnp.int32

D = 1024
BATCH, SEQ = 16, 256
DEC_BATCH, DEC_SEQ = 8, 2048
PAST_LEN = 256
GRID_W = 64
LRU_W = 512
LRU_HEADS = 8
LRU_C = 8.0
CONV_W = 4
S5_W = 512
S5_CH = 16
S5_G = 32
S5_P = 64
S5_N = S5_G * S5_P
HEADS = 8
QK_NOPE, QK_ROPE, V_DIM = 128, 64, 128
Q_LORA, KV_LORA = 384, 256
ROPE_THETA = 10000.0
D_FF = 2816
N_EXP = 8
EPS = 1e-6

NP = BATCH * SEQ
NS = DEC_BATCH * DEC_SEQ
N = NP + NS
SUB = 8
LANE = 128
ROW_TILE = D // LANE
T_CHUNK = 64
R_CHUNK = T_CHUNK * SUB
FF_BLK = 256
KEY_BLK = 256
HEAD_GRP = 4
MLA_TM = 512
MLA_SUB = 2
ROUTER_SUB = 1
ABOUT_SUB = 1
ABOUT_T = 64
INPROJ_SUB = 2
INPROJ_T = 64
FFN_T = 128
SIDE_ITERS = 4
N_FF = D_FF // FF_BLK
TM_E = 512
N_TILES = -(-(2 * N + N_EXP * (TM_E - 1)) // TM_E)
P_ROWS = N_TILES * TM_E
DUMP_ROWS = P_ROWS - 2 * N + TM_E
VMEM_LIMIT = 56 * 1024 * 1024

NT_DIMS = (((1,), (1,)), ((), ()))


def _cparams(sem):
    return pltpu.CompilerParams(dimension_semantics=sem, vmem_limit_bytes=VMEM_LIMIT)


def _dot(a, b):
    return jnp.dot(a, b, preferred_element_type=F32)


def _sigmoid(x):
    return 1.0 / (1.0 + jnp.exp(-x))


def _neg_expm1_2x(log_a, a):
    series = -2.0 * log_a * (1.0 + log_a * (1.0 + log_a * (2.0 / 3.0) * (1.0 + log_a * 0.5)))
    return jnp.where(log_a > -0.01, series, (1.0 - a) * (1.0 + a))


def _sqrt_nonneg(v):
    return jnp.where(v > 0.0, v * lax.rsqrt(v), 0.0)


def _gelu(x):
    k = math.sqrt(2.0 / math.pi)
    half = 0.5 * x
    return half + half * jnp.tanh(x * (k + (k * 0.044715) * (x * x)))


def _rms(x, g):
    ms = jnp.mean(x * x, axis=-1, keepdims=True)
    return x * lax.rsqrt(ms + EPS) * g


def _rows8(y, fn):
    r, c = y.shape
    return fn(y.reshape(r // SUB, SUB, c)).reshape(r, c)


def _adaln(x, g, scale, shift):
    return _rows8(_rms(x, g), lambda y: y * (1.0 + scale)[None] + shift[None])


def _gated(x, y, g, gate):
    return x + _rows8(_rms(y, g), lambda z: z * gate[None])


def _store_row_tiles(ref, piece, row0=0, rows=None):
    rows = ref.shape[0] // ROW_TILE if rows is None else rows
    for c in range(ROW_TILE):
        ref[pl.ds(row0 * ROW_TILE + c, rows, stride=ROW_TILE), :] = piece(slice(c * LANE, (c + 1) * LANE))


def _load_row_tiles(ref, c, rows):
    return ref[pl.ds(c, rows, stride=ROW_TILE), :]


def _full(shape):
    nd = len(shape)
    return pl.BlockSpec(shape, lambda *_: (0,) * nd)


def _mod_spec(k, grp):
    return pl.BlockSpec((1, SUB, D), lambda i, *_: (grp(i), 0, k))


def _grp_tm(tm):
    return lambda i: (i * tm >= NP).astype(I32)


def _grp_bm(tm):
    return lambda i: jnp.where(i * tm < NP, 0, 1 + (i * tm - NP) // DEC_SEQ)


def _mod_kernel(c_ref, w_ref, b_ref, o_ref):
    c = c_ref[...]
    s = c * _sigmoid(c)
    o_ref[0] = _dot(s.astype(BF16), w_ref[0].astype(BF16)) + b_ref[0]


def _modulation(cond, w_mod, b_mod):
    depth = w_mod.shape[0]
    rows = cond.shape[0]
    return pl.pallas_call(
        _mod_kernel,
        grid=(depth, 6),
        in_specs=[_full((rows, D)),
                  pl.BlockSpec((1, D, D), lambda l, j: (l, 0, j)),
                  pl.BlockSpec((1, 1, D), lambda l, j: (l, 0, j))],
        out_specs=pl.BlockSpec((1, rows, D), lambda l, j: (l, 0, j)),
        out_shape=jax.ShapeDtypeStruct((depth, rows, 6 * D), F32),
        compiler_params=_cparams(("arbitrary", "arbitrary")),
        name="modulation",
    )(cond, w_mod, b_mod.reshape(depth, 1, 6 * D))


def _time_major_copies(step, t_steps, xp_hbm, xs_hbm, buf, sem, to_hbm):
    p_steps = (BATCH // SUB) * (SEQ // t_steps)
    per_group = SEQ // t_steps

    def issue(hbm, seq0, t0):
        t0 = pl.multiple_of(t0, t_steps)
        for b in range(SUB):
            rows = hbm.at[seq0 + b, pl.ds(t0, t_steps), :]
            tile = buf.at[:, b, :]
            (pltpu.make_async_copy(tile, rows, sem) if to_hbm else pltpu.make_async_copy(rows, tile, sem)).start()

    @pl.when(step < p_steps)
    def _():
        issue(xp_hbm, (step // per_group) * SUB, (step % per_group) * t_steps)

    @pl.when(step >= p_steps)
    def _():
        issue(xs_hbm, 0, (step - p_steps) * t_steps)


def _time_major_wait(t_steps, xs_hbm, buf, sem, to_hbm):
    for b in range(SUB):
        rows = xs_hbm.at[0, pl.ds(0, t_steps), :]
        tile = buf.at[:, b, :]
        (pltpu.make_async_copy(tile, rows, sem) if to_hbm else pltpu.make_async_copy(rows, tile, sem)).wait()


def _time_major_read(t_steps, xp_hbm, xs_hbm, buf, sem):
    step = pl.program_id(0)
    slot = step % 2

    @pl.when(step == 0)
    def _():
        _time_major_copies(step, t_steps, xp_hbm, xs_hbm, buf.at[0], sem.at[0], False)

    @pl.when(step + 1 < pl.num_programs(0))
    def _():
        _time_major_copies(step + 1, t_steps, xp_hbm, xs_hbm, buf.at[1 - slot], sem.at[1 - slot], False)

    _time_major_wait(t_steps, xs_hbm, buf.at[slot], sem.at[slot], False)
    return slot


def _inproj_kernel(xp_hbm, xs_hbm, g_ref, sh_ref, sc_ref, w_ref, o_ref, buf, sem):
    slot = _time_major_read(INPROJ_T, xp_hbm, xs_hbm, buf, sem)
    sub_t = INPROJ_T // INPROJ_SUB
    for r in range(INPROJ_SUB):
        rows = slice(r * sub_t * SUB, (r + 1) * sub_t * SUB)
        x = buf[slot, r * sub_t:(r + 1) * sub_t].reshape(sub_t * SUB, D)
        h = _adaln(x, g_ref[...], sc_ref[0], sh_ref[0])
        o_ref[rows, :] = _dot(h.astype(BF16), w_ref[...]).astype(BF16)


def _ab_inproj(xp, xs, gain, mods, w_in):
    tm = INPROJ_T * SUB
    nout = w_in.shape[1]
    grp = _grp_tm(tm)
    return pl.pallas_call(
        _inproj_kernel,
        grid=(N // tm,),
        in_specs=[pl.BlockSpec(memory_space=pl.ANY), pl.BlockSpec(memory_space=pl.ANY),
                  _full((1, D)),
                  _mod_spec(0, grp), _mod_spec(1, grp),
                  _full((D, nout))],
        out_specs=pl.BlockSpec((tm, nout), lambda i: (i, 0)),
        out_shape=jax.ShapeDtypeStruct((N, nout), BF16),
        scratch_shapes=[pltpu.VMEM((2, INPROJ_T, SUB, D), F32), pltpu.SemaphoreType.DMA((2,))],
        compiler_params=_cparams(("arbitrary",)),
        name="ab_inproj",
    )(xp, xs, gain, mods, mods, w_in)


def _scan_table(reverse):
    cols = []
    groups = [(0, SEQ // T_CHUNK, 0), (1, SEQ // T_CHUNK, SEQ // T_CHUNK),
              (2, DEC_SEQ // T_CHUNK, NP // R_CHUNK)]
    for g, nc, base in groups:
        order = range(nc - 1, -1, -1) if reverse else range(nc)
        for k, c in enumerate(order):
            cols.append((base + c, g, int(k == 0), int(c > 0), int(c < nc - 1)))
    return np.asarray(cols, np.int32).T.copy()


N_SCAN_IN, N_SCAN_OUT, N_SCAN_SCRATCH = 16, 5, 8
SCAN_TBL_ROWS = 5


def _scan_kernel(tbl, *refs):
    ins = [refs[d * N_SCAN_IN:(d + 1) * N_SCAN_IN] for d in range(2)]
    o0 = 2 * N_SCAN_IN
    outs = [refs[o0 + d * N_SCAN_OUT:o0 + (d + 1) * N_SCAN_OUT] for d in range(2)]
    s0 = o0 + 2 * N_SCAN_OUT
    scr = [refs[s0 + d * N_SCAN_SCRATCH:s0 + (d + 1) * N_SCAN_SCRATCH] for d in range(2)]
    s = pl.program_id(0)
    for d in range(2):
        h0l_ref, h0r_ref, h0i_ref = ins[d][4:7]
        hl, sre, sim = scr[d][5:8]

        @pl.when(tbl[d * SCAN_TBL_ROWS + 2, s] == 1)
        def _(h0l_ref=h0l_ref, h0r_ref=h0r_ref, h0i_ref=h0i_ref, hl=hl, sre=sre, sim=sim):
            hl[...] = h0l_ref[0]
            sre[...] = h0r_ref[0]
            sim[...] = h0i_ref[0]

    fwd, bwd = [_ScanChunk(tbl, d * SCAN_TBL_ROWS, ins[d], outs[d], scr[d], reverse=(d == 1)) for d in range(2)]
    fwd.s5_project()
    fwd.conv()
    fwd.gates()
    bwd.s5_project()
    fwd.s5_recurrence()
    bwd.conv()
    fwd.s5_readout()
    bwd.gates()
    fwd.lru_coefficients()
    fwd.lru_recurrence()
    bwd.s5_recurrence()
    bwd.s5_readout()
    bwd.lru_coefficients()
    bwd.lru_recurrence()


class _ScanChunk:
    def __init__(self, tbl, row0, ins, outs, scratch, reverse):
        (self.xa_ref, self.xp_ref, self.xn_ref, self.xb_ref, _, _, _, self.cw_ref, self.cb_ref, self.wg_ref,
         self.bg_ref, self.lam_ref, self.bm_ref, self.ar_ref, self.ai_ref, self.cm_ref) = ins
        self.ha_ref, self.y_ref, self.ll_ref, self.lr_ref, self.li_ref = outs
        self.ext, self.abuf, self.bbuf, self.hre, self.him, self.hl, self.sre, self.sim = scratch
        step = pl.program_id(0)
        self.has_prev = tbl[row0 + 3, step] == 1
        self.has_next = tbl[row0 + 4, step] == 1
        self.order = range(T_CHUNK - 1, -1, -1) if reverse else range(T_CHUNK)
        self.half = S5_N // 2

    def s5_project(self):
        ub = self.xb_ref[...]
        half = self.half
        for j in range(2):
            bu = _dot(ub[:, j * 256:(j + 1) * 256], self.bm_ref[j])
            self.hre[:, j * half:(j + 1) * half] = bu[:, :half]
            self.him[:, j * half:(j + 1) * half] = bu[:, half:]

    def conv(self):
        ext = self.ext
        ext[0:2 * SUB] = jnp.where(self.has_prev, self.xp_ref[...].astype(F32), 0.0)
        ext[2 * SUB:2 * SUB + R_CHUNK] = self.xa_ref[...].astype(F32)
        ext[2 * SUB + R_CHUNK:3 * SUB + R_CHUNK] = jnp.where(self.has_next, self.xn_ref[0:SUB, :].astype(F32), 0.0)
        xa = self.cb_ref[...] + self.cw_ref[0:1] * ext[0:R_CHUNK]
        for k in range(1, CONV_W):
            xa = xa + self.cw_ref[k:k + 1] * ext[k * SUB:k * SUB + R_CHUNK]
        self.xa = xa

    def gates(self):
        self.gz = _dot(self.xa.astype(BF16), self.wg_ref[...]) + self.bg_ref[...]

    def s5_recurrence(self):
        cblk = 4 * LANE
        for cb in range(S5_N // cblk):
            cols = slice(cb * cblk, (cb + 1) * cblk)
            ar = jnp.broadcast_to(self.ar_ref[:, cols], (SUB, cblk))
            ai = jnp.broadcast_to(self.ai_ref[:, cols], (SUB, cblk))
            hr = self.sre[:, cols]
            hi = self.sim[:, cols]
            for t in self.order:
                rows = slice(t * SUB, (t + 1) * SUB)
                nr = ar * hr - ai * hi + self.hre[rows, cols]
                ni = ar * hi + ai * hr + self.him[rows, cols]
                hr, hi = nr, ni
                self.hre[rows, cols] = hr
                self.him[rows, cols] = hi
            self.sre[:, cols] = hr
            self.sim[:, cols] = hi
        self.lr_ref[0] = self.sre[...]
        self.li_ref[0] = self.sim[...]

    def s5_readout(self):
        half = self.half
        for j in range(2):
            hc = jnp.concatenate([self.hre[:, j * half:(j + 1) * half], self.him[:, j * half:(j + 1) * half]],
                                 axis=1).astype(BF16)
            self.y_ref[:, j * 256:(j + 1) * 256] = _dot(hc, self.cm_ref[j]).astype(BF16)

    def lru_coefficients(self):
        r = 0.5 * jnp.tanh(0.5 * self.gz[:, :LRU_W]) + 0.5
        i = 0.5 * jnp.tanh(0.5 * self.gz[:, LRU_W:]) + 0.5
        lam = self.lam_ref[...]
        log_sig = jnp.minimum(lam, 0.0) - jnp.log1p(jnp.exp(-jnp.abs(lam)))
        log_a = LRU_C * r * log_sig
        a = jnp.exp(log_a)
        self.abuf[...] = a
        self.bbuf[...] = _sqrt_nonneg(_neg_expm1_2x(log_a, a)) * (i * self.xa)

    def lru_recurrence(self):
        h = self.hl[...]
        for t in self.order:
            rows = slice(t * SUB, (t + 1) * SUB)
            h = self.abuf[rows] * h + self.bbuf[rows]
            self.bbuf[rows] = h
        self.hl[...] = h
        self.ll_ref[0] = h
        self.ha_ref[...] = self.bbuf[...].astype(BF16)


def _ab_scan(xz, per_dir):
    tbl = jnp.asarray(np.concatenate([_scan_table(False), _scan_table(True)], axis=0))
    n_steps = tbl.shape[1]
    in_specs, out_specs, scratch, out_shape, args = [], [], [], [], []
    for d in range(2):
        blk = lambda s, t, d=d: t[d * SCAN_TBL_ROWS, s]
        grp = lambda s, t, d=d: t[d * SCAN_TBL_ROWS + 1, s]
        state_spec = lambda w, grp=grp: pl.BlockSpec((1, SUB, w), lambda s, t: (grp(s, t), 0, 0))
        const = lambda shape: pl.BlockSpec(shape, lambda s, t: (0,) * len(shape))
        in_specs += [
            pl.BlockSpec((R_CHUNK, LRU_W), lambda s, t, blk=blk: (blk(s, t), 0)),
            pl.BlockSpec((2 * SUB, LRU_W),
                         lambda s, t, blk=blk: (jnp.maximum(blk(s, t) * (T_CHUNK // 2) - 1, 0), 0)),
            pl.BlockSpec((2 * SUB, LRU_W),
                         lambda s, t, blk=blk: (jnp.minimum((blk(s, t) + 1) * (T_CHUNK // 2), N // (2 * SUB) - 1), 0)),
            pl.BlockSpec((R_CHUNK, S5_W), lambda s, t, blk=blk: (blk(s, t), 2)),
            state_spec(LRU_W), state_spec(S5_N), state_spec(S5_N),
            const((CONV_W, LRU_W)), const((1, LRU_W)),
            const((LRU_W, 2 * LRU_W)), const((1, 2 * LRU_W)), const((1, LRU_W)),
            const((2, 256, S5_N)), const((1, S5_N)), const((1, S5_N)), const((2, S5_N, 256)),
        ]
        out_specs += [
            pl.BlockSpec((R_CHUNK, LRU_W), lambda s, t, blk=blk: (blk(s, t), 0)),
            pl.BlockSpec((R_CHUNK, S5_W), lambda s, t, blk=blk: (blk(s, t), 0)),
            state_spec(LRU_W), state_spec(S5_N), state_spec(S5_N),
        ]
        scratch += [
            pltpu.VMEM((R_CHUNK + 3 * SUB, LRU_W), F32),
            pltpu.VMEM((R_CHUNK, LRU_W), F32), pltpu.VMEM((R_CHUNK, LRU_W), F32),
            pltpu.VMEM((R_CHUNK, S5_N), F32), pltpu.VMEM((R_CHUNK, S5_N), F32),
            pltpu.VMEM((SUB, LRU_W), F32), pltpu.VMEM((SUB, S5_N), F32), pltpu.VMEM((SUB, S5_N), F32),
        ]
        out_shape += [jax.ShapeDtypeStruct((N, LRU_W), BF16), jax.ShapeDtypeStruct((N, S5_W), BF16),
                      jax.ShapeDtypeStruct((3, SUB, LRU_W), F32),
                      jax.ShapeDtypeStruct((3, SUB, S5_N), F32), jax.ShapeDtypeStruct((3, SUB, S5_N), F32)]
        args += [xz, xz, xz, xz, *per_dir[d]]
    grid_spec = pltpu.PrefetchScalarGridSpec(num_scalar_prefetch=1, grid=(n_steps,), in_specs=in_specs,
                                             out_specs=out_specs, scratch_shapes=scratch)
    outs = pl.pallas_call(
        _scan_kernel,
        grid_spec=grid_spec,
        out_shape=out_shape,
        compiler_params=_cparams(("arbitrary",)),
        name="ab_scan",
    )(tbl, *args)
    return outs[:N_SCAN_OUT], outs[N_SCAN_OUT:]


def _about_kernel(haf, hab, yf, yr, ga, xb, xp_hbm, xs_hbm, gate, g1, d_ref, wglu, bglu, wout, o_ref, buf, sem):
    slot = _time_major_read(ABOUT_T, xp_hbm, xs_hbm, buf, sem)
    x_all = buf[slot].reshape(ABOUT_T * SUB, D)
    sub = ABOUT_T * SUB // ABOUT_SUB
    for r in range(ABOUT_SUB):
        rows = slice(r * sub, (r + 1) * sub)
        f32 = lambda ref: ref[rows, :].astype(F32)
        ya = (f32(haf) + f32(hab)) * _gelu(f32(ga))
        yb0 = _gelu(f32(yf) + f32(yr) + d_ref[...] * f32(xb))
        half_yb0 = 0.5 * yb0
        yb = half_yb0 + half_yb0 * jnp.tanh(0.5 * (_dot(yb0.astype(BF16), wglu[...]) + bglu[...]))
        out = _dot(ya.astype(BF16), wout[0:LRU_W]) + _dot(yb.astype(BF16), wout[LRU_W:LRU_W + S5_W])
        o_ref[rows, :] = _gated(x_all[rows], out, g1[...], gate[0])


def _ab_out(haf, hab, yf, yr, xz, xp, xs, mods, g1, s5_d, wglu, bglu, wout):
    tm = ABOUT_T * SUB
    grp = _grp_tm(tm)
    half = lambda c: pl.BlockSpec((tm, LRU_W), lambda i: (i, c))
    return pl.pallas_call(
        _about_kernel,
        grid=(N // tm,),
        in_specs=[half(0), half(0), half(0), half(0), half(1), half(2),
                  pl.BlockSpec(memory_space=pl.ANY), pl.BlockSpec(memory_space=pl.ANY),
                  _mod_spec(2, grp), _full((1, D)), _full((1, S5_W)),
                  _full((S5_W, S5_W)), _full((1, S5_W)), _full((LRU_W + S5_W, D))],
        out_specs=pl.BlockSpec((tm, D), lambda i: (i, 0)),
        out_shape=jax.ShapeDtypeStruct((N, D), F32),
        scratch_shapes=[pltpu.VMEM((2, ABOUT_T, SUB, D), F32), pltpu.SemaphoreType.DMA((2,))],
        compiler_params=_cparams(("arbitrary",)),
        name="ab_out",
    )(haf, hab, yf, yr, xz, xz, xp, xs, mods, g1, s5_d, wglu, bglu, wout)


def _swiglu_block(hbf, wgu, wd, acc, j):
    static = isinstance(j, int)
    blk = lambda start: pl.ds(start if static else pl.multiple_of(start, FF_BLK), FF_BLK)
    h = hbf[...]
    g = _dot(h, wgu[:, blk(j * FF_BLK)])
    u = _dot(h, wgu[:, blk(D_FF + j * FF_BLK)])
    act = (g * _sigmoid(g)) * u
    part = _dot(act.astype(BF16), wd[blk(j * FF_BLK), :])
    if static and j == 0:
        acc[...] = part
    else:
        acc[...] += part


def _ffn_kernel(x_ref, sh, sc, gt, g2, g3, wgu, wd, op_hbm, os_hbm, hbf, acc, obuf, sem):
    step = pl.program_id(0)
    slot = step % 2
    hbf[...] = _adaln(x_ref[...], g2[...], sc[0], sh[0]).astype(BF16)
    for j in range(N_FF):
        _swiglu_block(hbf, wgu, wd, acc, j)
    obuf[slot] = _gated(x_ref[...], acc[...], g3[...], gt[0]).reshape(FFN_T, SUB, D)
    _time_major_copies(step, FFN_T, op_hbm, os_hbm, obuf.at[slot], sem.at[slot], True)

    @pl.when(step > 0)
    def _():
        _time_major_wait(FFN_T, os_hbm, obuf.at[1 - slot], sem.at[1 - slot], True)

    @pl.when(step == pl.num_programs(0) - 1)
    def _():
        _time_major_wait(FFN_T, os_hbm, obuf.at[slot], sem.at[slot], True)


def _ffn(x, mods, g2, g3, wgu, wd):
    tm = FFN_T * SUB
    grp = _grp_tm(tm)
    once = lambda shape: pl.BlockSpec(shape, lambda i: (0,) * len(shape), pipeline_mode=pl.Buffered(1))
    return pl.pallas_call(
        _ffn_kernel,
        grid=(N // tm,),
        in_specs=[pl.BlockSpec((tm, D), lambda i: (i, 0)),
                  _mod_spec(3, grp), _mod_spec(4, grp), _mod_spec(5, grp),
                  _full((1, D)), _full((1, D)),
                  once(wgu.shape), once(wd.shape)],
        out_specs=[pl.BlockSpec(memory_space=pl.ANY), pl.BlockSpec(memory_space=pl.ANY)],
        out_shape=[jax.ShapeDtypeStruct((BATCH, SEQ, D), F32), jax.ShapeDtypeStruct((DEC_BATCH, DEC_SEQ, D), F32)],
        scratch_shapes=[pltpu.VMEM((tm, D), BF16), pltpu.VMEM((tm, D), F32),
                        pltpu.VMEM((2, FFN_T, SUB, D), F32), pltpu.SemaphoreType.DMA((2,))],
        compiler_params=_cparams(("arbitrary",)),
        name="ffn",
    )(x, mods, mods, mods, g2, g3, wgu, wd)


def _pair_swap(x):
    outs = []
    for c in range(x.shape[1] // LANE):
        xc = x[:, c * LANE:(c + 1) * LANE]
        even = lax.broadcasted_iota(I32, xc.shape, 1) % 2 == 0
        outs.append(jnp.where(even, pltpu.roll(xc, LANE - 1, 1), pltpu.roll(xc, 1, 1)))
    return outs[0] if len(outs) == 1 else jnp.concatenate(outs, axis=1)


def _stream_specs(tm, width):
    n_p = NP // tm
    return [pl.BlockSpec((tm, width), lambda i: (jnp.minimum(i, n_p - 1), 0)),
            pl.BlockSpec((tm, width), lambda i: (jnp.maximum(i - n_p, 0), 0))]


def _stream_rows(p_ref, s_ref, prompt_steps):
    return jnp.where(pl.program_id(0) < prompt_steps, p_ref[...], s_ref[...])


def _mlaproj_kernel(xp_ref, xs_ref, g0, sh, sc, w1, gq, gkv, wuq, wukv, cq_ref, sq_ref, ck_ref, sk_ref,
                    qn_ref, qr_ref, kn_ref, v_ref, kr2_ref, ckv_ref, krr_ref, *, prompt_steps):
    x = _stream_rows(xp_ref, xs_ref, prompt_steps)
    sub = x.shape[0] // MLA_SUB
    cache_rows = []
    for r in range(MLA_SUB):
        rows = slice(r * sub, (r + 1) * sub)
        h = _adaln(x[rows], g0[...], sc[0], sh[0])
        dn = _dot(h.astype(BF16), w1[...])
        cq = _rms(dn[:, :Q_LORA], gq[...])
        ckv = _rms(dn[:, Q_LORA:Q_LORA + KV_LORA], gkv[...])
        krp = dn[:, Q_LORA + KV_LORA:]
        cache_rows.append((rows, ckv, krp))
        q = _dot(cq.astype(BF16), wuq[...])
        qn_ref[rows, :] = q[:, :HEADS * QK_NOPE].astype(BF16)
        qr = q[:, HEADS * QK_NOPE:]
        qr_ref[rows, :] = (qr * cq_ref[rows, :] + _pair_swap(qr) * sq_ref[rows, :]).astype(BF16)
        kv = _dot(ckv.astype(BF16), wukv[...])
        kn_ref[rows, :] = kv[:, :HEADS * QK_NOPE].astype(BF16)
        v_ref[rows, :] = kv[:, HEADS * QK_NOPE:].astype(BF16)
        kr = krp * ck_ref[rows, :] + _pair_swap(krp) * sk_ref[rows, :]
        kr2_ref[rows, :] = jnp.concatenate([kr, pltpu.roll(kr, QK_ROPE, 1)], axis=1).astype(BF16)

    @pl.when(pl.program_id(0) < prompt_steps)
    def _():
        for rows, ckv, krp in cache_rows:
            ckv_ref[rows, :] = ckv
            krr_ref[rows, :] = krp


def _mla_proj(xp, xs, g0, mods, w1, gq, gkv, wuq, wukv, cos_q, sin_q, cos_k, sin_k):
    tm = MLA_TM
    grp = _grp_bm(tm)
    n_pos = DEC_SEQ // tm
    tab = lambda w: pl.BlockSpec((tm, w), lambda i: (jnp.where(i * tm < NP, n_pos, (i - NP // tm) % n_pos), 0))
    row = lambda w: pl.BlockSpec((tm, w), lambda i: (i, 0))
    shp = lambda w, dt: jax.ShapeDtypeStruct((N, w), dt)
    n_p = NP // tm
    prow = lambda w: pl.BlockSpec((tm, w), lambda i: (jnp.minimum(i, n_p - 1), 0))
    return pl.pallas_call(
        functools.partial(_mlaproj_kernel, prompt_steps=n_p),
        grid=(N // tm,),
        in_specs=_stream_specs(tm, D) + [_full((1, D)), _mod_spec(0, grp), _mod_spec(1, grp),
                  _full(w1.shape), _full((1, Q_LORA)), _full((1, KV_LORA)),
                  _full(wuq.shape), _full(wukv.shape),
                  tab(HEADS * QK_ROPE), tab(HEADS * QK_ROPE), tab(LANE), tab(LANE)],
        out_specs=[row(HEADS * QK_NOPE), row(HEADS * QK_ROPE), row(HEADS * QK_NOPE), row(HEADS * V_DIM),
                   row(2 * LANE), prow(KV_LORA), prow(LANE)],
        out_shape=[shp(HEADS * QK_NOPE, BF16), shp(HEADS * QK_ROPE, BF16), shp(HEADS * QK_NOPE, BF16),
                   shp(HEADS * V_DIM, BF16), shp(2 * LANE, BF16),
                   jax.ShapeDtypeStruct((NP, KV_LORA), F32), jax.ShapeDtypeStruct((NP, LANE), F32)],
        compiler_params=_cparams(("arbitrary",)),
        name="mla_proj",
    )(xp, xs, g0, mods, mods, w1, gq, gkv, wuq, wukv, cos_q, sin_q, cos_k, sin_k)


def _cachekv_kernel(c_ref, w_ref, kn_ref, v_ref):
    kv = _dot(c_ref[...].astype(BF16), w_ref[...])
    kn_ref[...] = kv[:, :HEADS * QK_NOPE].astype(BF16)
    v_ref[...] = kv[:, HEADS * QK_NOPE:].astype(BF16)


def _cache_kv(ckv_cache, wukv):
    rows = ckv_cache.shape[0]
    tm = 512
    return pl.pallas_call(
        _cachekv_kernel,
        grid=(rows // tm,),
        in_specs=[pl.BlockSpec((tm, KV_LORA), lambda i: (i, 0)), _full(wukv.shape)],
        out_specs=[pl.BlockSpec((tm, HEADS * QK_NOPE), lambda i: (i, 0)),
                   pl.BlockSpec((tm, HEADS * V_DIM), lambda i: (i, 0))],
        out_shape=[jax.ShapeDtypeStruct((rows, HEADS * QK_NOPE), BF16),
                   jax.ShapeDtypeStruct((rows, HEADS * V_DIM), BF16)],
        compiler_params=_cparams(("arbitrary",)),
        name="cache_kv",
    )(ckv_cache, wukv)


def _attn_kernel(*refs, has_cache):
    if has_cache:
        qn, qr, kn, kr, v, knc, krc, vc, o_ref, s_scr = refs
        streams = [(knc, krc, vc), (kn, kr, v)]
    else:
        qn, qr, kn, kr, v, o_ref, s_scr = refs
        streams = [(kn, kr, v)]
    chunks = [(k1, k2, vv, c * KEY_BLK) for k1, k2, vv in streams for c in range(k1.shape[0] // KEY_BLK)]
    tq = qn.shape[0]
    a = (QK_NOPE + QK_ROPE) ** -0.5 * math.log2(math.e)
    for hh in range(HEAD_GRP):
        cols = slice(hh * LANE, (hh + 1) * LANE)
        pair_cols = slice((hh // 2) * LANE, (hh // 2 + 1) * LANE)
        kr_cols = slice((hh % 2) * LANE, (hh % 2 + 1) * LANE)
        q = jnp.concatenate([qn[:, cols], qr[:, pair_cols]], axis=1)
        mx = jnp.full((tq, LANE), -jnp.inf, F32)
        for n, (k1, k2, _, r0) in enumerate(chunks):
            k = jnp.concatenate([k1[r0:r0 + KEY_BLK, cols], k2[r0:r0 + KEY_BLK, kr_cols]], axis=1)
            s = lax.dot_general(q, k, NT_DIMS, preferred_element_type=F32)
            s_scr[hh, :, n * KEY_BLK:(n + 1) * KEY_BLK] = s
            for c in range(KEY_BLK // LANE):
                mx = jnp.maximum(mx, s[:, c * LANE:(c + 1) * LANE])
        mb = jnp.max(mx, axis=-1, keepdims=True) * a
        den = jnp.zeros((tq, LANE), F32)
        o = jnp.zeros((tq, V_DIM), F32)
        for n, (_, _, vv, r0) in enumerate(chunks):
            p = jnp.exp2(s_scr[hh, :, n * KEY_BLK:(n + 1) * KEY_BLK] * a - mb)
            for c in range(KEY_BLK // LANE):
                den = den + p[:, c * LANE:(c + 1) * LANE]
            o = o + _dot(p.astype(BF16), vv[r0:r0 + KEY_BLK, cols])
        o_ref[:, cols] = (o / jnp.sum(den, axis=-1, keepdims=True)).astype(BF16)


def _attention(qn, qr, kn, kr2, v, *, row0, n_seq, seq, tq, cache=None):
    nq = seq // tq
    grp = HEAD_GRP * LANE
    qblk = lambda b, h, i: row0 // tq + b * nq + i
    kblk = lambda b: row0 // seq + b
    in_specs = [pl.BlockSpec((tq, grp), lambda b, h, i: (qblk(b, h, i), h)),
                pl.BlockSpec((tq, grp // 2), lambda b, h, i: (qblk(b, h, i), h)),
                pl.BlockSpec((seq, grp), lambda b, h, i: (kblk(b), h)),
                pl.BlockSpec((seq, 2 * LANE), lambda b, h, i: (kblk(b), 0)),
                pl.BlockSpec((seq, grp), lambda b, h, i: (kblk(b), h))]
    args = [qn, qr, kn, kr2, v]
    if cache is not None:
        knc, kr2c, vc = cache
        in_specs += [pl.BlockSpec((PAST_LEN, grp), lambda b, h, i: (b, h)),
                     pl.BlockSpec((PAST_LEN, 2 * LANE), lambda b, h, i: (b, 0)),
                     pl.BlockSpec((PAST_LEN, grp), lambda b, h, i: (b, h))]
        args += [knc, kr2c, vc]
    return pl.pallas_call(
        functools.partial(_attn_kernel, has_cache=cache is not None),
        grid=(n_seq, HEADS // HEAD_GRP, nq),
        in_specs=in_specs,
        out_specs=pl.BlockSpec((tq, grp), lambda b, h, i: (b * nq + i, h)),
        out_shape=jax.ShapeDtypeStruct((n_seq * seq, HEADS * V_DIM), BF16),
        scratch_shapes=[pltpu.VMEM((HEAD_GRP, tq, seq + (PAST_LEN if cache is not None else 0)), F32)],
        compiler_params=_cparams(("arbitrary", "arbitrary", "arbitrary")),
        name="attn_latent" if cache is not None else "attn_context",
    )(*args)


def _router_kernel(op_ref, os_ref, xp_ref, xs_ref, gate1, g1, sh2, sc2, g2, wout, wr_hi, wr_lo, tri,
                   x3_ref, h_ref, ri_ref, rf_ref, cnt_ref, carry, *, prompt_steps):
    step = pl.program_id(0)

    @pl.when(step == 0)
    def _():
        carry[...] = jnp.zeros_like(carry)

    o_all = _stream_rows(op_ref, os_ref, prompt_steps)
    x_all = _stream_rows(xp_ref, xs_ref, prompt_steps)
    sub = x_all.shape[0] // ROUTER_SUB
    dg = lambda a, b: lax.dot_general(a, b, NT_DIMS, preferred_element_type=F32)
    logits = []
    for r in range(ROUTER_SUB):
        rows = slice(r * sub, (r + 1) * sub)
        x3 = _gated(x_all[rows], _dot(o_all[rows], wout[...]), g1[...], gate1[0])
        x3_ref[rows, :] = x3
        h = _adaln(x3, g2[...], sc2[0], sh2[0])
        _store_row_tiles(h_ref, lambda cols, h=h: h[:, cols], r * sub, sub)
        h_hi = h.astype(BF16)
        h_lo = (h - h_hi.astype(F32)).astype(BF16)
        logits.append(dg(wr_hi[...], h_hi) + dg(wr_hi[...], h_lo) + dg(wr_lo[...], h_hi))
    lg = jnp.concatenate(logits, axis=1)
    eidx = lax.broadcasted_iota(I32, lg.shape, 0).astype(F32)
    m1 = jnp.max(lg, axis=0, keepdims=True)
    i1 = jnp.min(jnp.where(lg == m1, eidx, float(N_EXP)), axis=0, keepdims=True)
    sel1 = eidx == i1
    lg2 = jnp.where(sel1, -jnp.inf, lg)
    m2 = jnp.max(lg2, axis=0, keepdims=True)
    i2 = jnp.min(jnp.where(lg2 == m2, eidx, float(N_EXP)), axis=0, keepdims=True)
    sel2 = eidx == i2
    e = jnp.exp(m2 - m1)
    w1 = 1.0 / (1.0 + e)
    w2 = e / (1.0 + e)
    picked = jnp.where(sel1 | sel2, 1.0, 0.0)
    rank = _dot(picked.astype(BF16), tri[...]) + carry[:, 0:1]
    r1 = jnp.sum(jnp.where(sel1, rank, 0.0), axis=0, keepdims=True)
    r2 = jnp.sum(jnp.where(sel2, rank, 0.0), axis=0, keepdims=True)
    carry[...] = carry[...] + jnp.sum(picked, axis=1, keepdims=True)
    cnt_ref[...] = carry[...]
    ri_ref[...] = jnp.where(eidx == 0.0, i1, jnp.where(eidx == 1.0, i2, jnp.where(eidx == 2.0, r1, r2))).astype(I32)
    rf_ref[...] = jnp.where(eidx == 0.0, w1, w2)


def _attn_out_router(o_p, o_s, xp, xs, mods, g1, g2, wout, wr_hi, wr_lo):
    tm = 512
    grp = _grp_bm(tm)
    n_p = NP // tm
    tri = jnp.asarray(np.triu(np.ones((tm, tm), np.float32), 1), BF16)
    row = lambda w: pl.BlockSpec((tm, w), lambda i: (i, 0))
    col = pl.BlockSpec((N_EXP, tm), lambda i: (0, i))
    return pl.pallas_call(
        functools.partial(_router_kernel, prompt_steps=n_p),
        grid=(N // tm,),
        in_specs=_stream_specs(tm, HEADS * V_DIM) + _stream_specs(tm, D) + [_mod_spec(2, grp), _full((1, D)),
                  _mod_spec(3, grp), _mod_spec(4, grp), _full((1, D)),
                  _full((HEADS * V_DIM, D)), _full((N_EXP, D)), _full((N_EXP, D)), _full((tm, tm))],
        out_specs=[row(D), pl.BlockSpec((tm * ROW_TILE, LANE), lambda i: (i, 0)), col, col, _full((N_EXP, LANE))],
        out_shape=[jax.ShapeDtypeStruct((N, D), F32), jax.ShapeDtypeStruct((N * ROW_TILE, LANE), F32),
                   jax.ShapeDtypeStruct((N_EXP, N), I32), jax.ShapeDtypeStruct((N_EXP, N), F32),
                   jax.ShapeDtypeStruct((N_EXP, LANE), F32)],
        scratch_shapes=[pltpu.VMEM((N_EXP, LANE), F32)],
        compiler_params=_cparams(("arbitrary",)),
        name="attn_out_router",
    )(o_p, o_s, xp, xs, mods, g1, mods, mods, g2, wout, wr_hi, wr_lo, tri)


def _tile_rows(r):
    return r * ROW_TILE if isinstance(r, int) else pl.multiple_of(r * ROW_TILE, ROW_TILE)


def _moe_kernel(te, nu, src0_ref, src1_ref, dstp_ref, dstc_ref, h_hbm, wgu, wd, y_hbm,
                hsbuf, ybuf, hbf, acc, sem_g, sem_s):
    i = pl.program_id(0)
    last = pl.num_programs(0) - 1
    cur = i % 2
    nxt = 1 - cur
    buf_rows = TM_E * ROW_TILE

    def gather(idx_ref, r, slot):
        i_src = pl.multiple_of(idx_ref[0, 0, r] * ROW_TILE, ROW_TILE)
        return pltpu.make_async_copy(h_hbm.at[pl.ds(i_src, ROW_TILE), :],
                                     hsbuf.at[slot, pl.ds(_tile_rows(r), ROW_TILE), :], sem_g.at[slot])

    def scatter(idx_ref, r, slot):
        i_dst = pl.multiple_of(idx_ref[0, 0, r] * ROW_TILE, ROW_TILE)
        return pltpu.make_async_copy(ybuf.at[slot, pl.ds(_tile_rows(r), ROW_TILE), :],
                                     y_hbm.at[pl.ds(i_dst, ROW_TILE), :], sem_s.at[slot])

    def wait_gather(slot):
        pltpu.make_async_copy(h_hbm.at[pl.ds(0, buf_rows), :], hsbuf.at[slot], sem_g.at[slot]).wait()

    def wait_scatter(slot):
        pltpu.make_async_copy(ybuf.at[slot], y_hbm.at[pl.ds(0, buf_rows), :], sem_s.at[slot]).wait()

    def for_rows(fn):
        def body(g, carry):
            for k in range(ROW_TILE):
                fn(g * ROW_TILE + k, k % 2)
            return carry
        lax.fori_loop(0, TM_E // ROW_TILE, body, 0)

    def side_traffic(r, queue):
        gather(src1_ref, r, nxt).start(priority=queue)
        scatter(dstp_ref, r, nxt).start(priority=queue)

    @pl.when(i == 0)
    def _():
        ybuf[1] = jnp.zeros((buf_rows, LANE), F32)
        for_rows(lambda r, queue: gather(src0_ref, r, 0).start(priority=queue))

    wait_gather(cur)

    @pl.when(i < nu[0])
    def _():
        for c in range(ROW_TILE):
            hbf[:, c * LANE:(c + 1) * LANE] = hsbuf[cur, pl.ds(c, TM_E, stride=ROW_TILE), :].astype(BF16)
        w_gu = wgu.at[0]
        w_d = wd.at[0]
        _swiglu_block(hbf, w_gu, w_d, acc, 0)
        per_iter = TM_E // SIDE_ITERS

        def body(t, carry):
            _swiglu_block(hbf, w_gu, w_d, acc, 1 + 2 * t)
            _swiglu_block(hbf, w_gu, w_d, acc, 2 + 2 * t)
            for k in range(per_iter):
                side_traffic(t * per_iter + k, k % 2)
            return carry

        lax.fori_loop(0, SIDE_ITERS, body, 0)
        for j in range(1 + 2 * SIDE_ITERS, N_FF):
            _swiglu_block(hbf, w_gu, w_d, acc, j)
        _store_row_tiles(ybuf.at[cur], lambda cols: acc[:, cols])

    @pl.when(i >= nu[0])
    def _():
        for_rows(side_traffic)
        ybuf[cur] = jnp.zeros((buf_rows, LANE), F32)

    wait_scatter(nxt)

    @pl.when(i == last)
    def _():
        for_rows(lambda r, queue: scatter(dstc_ref, r, cur).start(priority=queue))
        wait_scatter(cur)
        wait_gather(nxt)


def _moe_experts(tile_expert, n_used, src_tbl, dst_tbl, h, wgu, wd):
    smem = lambda off: pl.BlockSpec((1, 1, TM_E), lambda i, te, nu: (i + off, 0, 0), memory_space=pltpu.SMEM)
    grid_spec = pltpu.PrefetchScalarGridSpec(
        num_scalar_prefetch=2,
        grid=(N_TILES,),
        in_specs=[smem(0), smem(1), smem(0), smem(1),
                  pl.BlockSpec(memory_space=pl.ANY),
                  pl.BlockSpec((1,) + wgu.shape[1:], lambda i, te, nu: (te[i], 0, 0)),
                  pl.BlockSpec((1,) + wd.shape[1:], lambda i, te, nu: (te[i], 0, 0))],
        out_specs=pl.BlockSpec(memory_space=pl.ANY),
        scratch_shapes=[pltpu.VMEM((2, TM_E * ROW_TILE, LANE), F32), pltpu.VMEM((2, TM_E * ROW_TILE, LANE), F32),
                        pltpu.VMEM((TM_E, D), BF16), pltpu.VMEM((TM_E, D), F32),
                        pltpu.SemaphoreType.DMA((2,)), pltpu.SemaphoreType.DMA((2,))],
    )
    return pl.pallas_call(
        _moe_kernel,
        grid_spec=grid_spec,
        out_shape=jax.ShapeDtypeStruct(((2 * N + DUMP_ROWS) * ROW_TILE, LANE), F32),
        compiler_params=_cparams(("arbitrary",)),
        name="moe_experts",
    )(tile_expert, n_used, src_tbl, src_tbl, dst_tbl, dst_tbl, h, wgu, wd)


def _combine_kernel(y1_ref, y2_ref, x_ref, w1_ref, w2_ref, gate2, g3, op_ref, os_ref, *, prompt_steps):
    rows = x_ref.shape[0]
    w1 = w1_ref[...]
    w2 = w2_ref[...]
    f = jnp.concatenate([w1 * _load_row_tiles(y1_ref, c, rows) + w2 * _load_row_tiles(y2_ref, c, rows)
                         for c in range(ROW_TILE)], axis=1)
    out = _gated(x_ref[...], f, g3[...], gate2[0])
    step = pl.program_id(0)

    @pl.when(step < prompt_steps)
    def _():
        op_ref[...] = out

    @pl.when(step >= prompt_steps)
    def _():
        os_ref[...] = out


def _moe_combine(y, x, w1, w2, mods, g3):
    tm = 512
    grp = _grp_bm(tm)
    nb = N // tm
    n_p = NP // tm
    return pl.pallas_call(
        functools.partial(_combine_kernel, prompt_steps=n_p),
        grid=(nb,),
        in_specs=[pl.BlockSpec((tm * ROW_TILE, LANE), lambda i: (i, 0)),
                  pl.BlockSpec((tm * ROW_TILE, LANE), lambda i: (nb + i, 0)),
                  pl.BlockSpec((tm, D), lambda i: (i, 0)),
                  pl.BlockSpec((tm, 1), lambda i: (i, 0)), pl.BlockSpec((tm, 1), lambda i: (i, 0)),
                  _mod_spec(5, grp), _full((1, D))],
        out_specs=[pl.BlockSpec((tm, D), lambda i: (jnp.minimum(i, n_p - 1), 0)),
                   pl.BlockSpec((tm, D), lambda i: (jnp.maximum(i - n_p, 0), 0))],
        out_shape=[jax.ShapeDtypeStruct((NP, D), F32), jax.ShapeDtypeStruct((NS, D), F32)],
        compiler_params=_cparams(("arbitrary",)),
        name="moe_combine",
    )(y, y, x, w1, w2, mods, g3)


def _block_diag(w):
    hh, a, b = w.shape
    eye = jnp.eye(hh, dtype=w.dtype)
    return jnp.einsum('hab,hk->hakb', w, eye).reshape(hh * a, hh * b)


def _s5_matrices(a_re, a_im, log_dt, b_re, b_im, c_re, c_im):
    dt = jnp.exp(log_dt)[:, None]
    mag = jnp.exp(a_re * dt)
    abr = mag * jnp.cos(a_im * dt)
    abi = mag * jnp.sin(a_im * dt)
    den = a_re * a_re + a_im * a_im
    cr = ((abr - 1.0) * a_re + abi * a_im) / den
    ci = (abi * a_re - (abr - 1.0) * a_im) / den
    bbr = cr[..., None] * b_re - ci[..., None] * b_im
    bbi = cr[..., None] * b_im + ci[..., None] * b_re
    hg = S5_G // 2
    eye = jnp.eye(hg, dtype=F32)
    bms, cms = [], []
    for j in range(2):
        sl = slice(j * hg, (j + 1) * hg)
        bd = lambda m: jnp.einsum('gpc,gh->gchp', m[sl], eye).reshape(hg * S5_CH, hg * S5_P)
        bms.append(jnp.concatenate([bd(bbr), bd(bbi)], axis=1))
        cd = lambda m: jnp.einsum('gcp,gh->gphc', m[sl], eye).reshape(hg * S5_P, hg * S5_CH)
        cms.append(jnp.concatenate([cd(c_re), cd(-c_im)], axis=0))
    return (jnp.stack(bms).astype(BF16), abr.reshape(1, S5_N), abi.reshape(1, S5_N),
            jnp.stack(cms).astype(BF16))


def _rope_tables(tm):
    rows = DEC_SEQ // GRID_W
    row = jnp.repeat(jnp.arange(rows, dtype=F32), GRID_W)
    col = jnp.tile(jnp.arange(GRID_W, dtype=F32), rows)
    nf = QK_ROPE // 4
    inv = ROPE_THETA ** (-jnp.arange(nf, dtype=F32) / nf)
    ang = jnp.concatenate([row[:, None] * inv, col[:, None] * inv], axis=-1)
    cos = jnp.repeat(jnp.cos(ang), 2, axis=-1)
    sin = jnp.stack([-jnp.sin(ang), jnp.sin(ang)], axis=-1).reshape(DEC_SEQ, QK_ROPE)
    ident = lambda t, one: jnp.concatenate([t, jnp.full((tm, t.shape[1]), one, F32)], axis=0)
    cos_q = ident(jnp.tile(cos, (1, HEADS)), 1.0)
    sin_q = ident(jnp.tile(sin, (1, HEADS)), 0.0)
    pad = lambda t, one: jnp.concatenate([t, jnp.full((DEC_SEQ, LANE - QK_ROPE), one, F32)], axis=1)
    return cos_q, sin_q, ident(pad(cos, 1.0), 1.0), ident(pad(sin, 0.0), 0.0)


def _group_states(prompt_state, sample_state):
    w = sample_state.shape[-1]
    return jnp.concatenate([prompt_state.reshape(2, SUB, w), sample_state.reshape(1, SUB, w)], axis=0)


def _layer_ab(xp, xs, m, ng, j, state_lru, state_s5_re, state_s5_im, p):
    xz = _ab_inproj(xp, xs, ng[0:1], m, p['ab_w_in'][j].astype(BF16))
    zeros = lambda w: jnp.zeros((BATCH, w), F32)
    per_dir = []
    for d in range(2):
        wg = jnp.concatenate([_block_diag(p['lru_wa'][j, d]), _block_diag(p['lru_wx'][j, d])], axis=1).astype(BF16)
        bg = jnp.concatenate([p['lru_ba'][j, d], p['lru_bx'][j, d]])[None]
        bm, ar, ai, cm = _s5_matrices(p['s5_a_re'][j, d], p['s5_a_im'][j, d], p['s5_log_dt'][j, d],
                                      p['s5_b_re'][j, d], p['s5_b_im'][j, d], p['s5_c_re'][j, d], p['s5_c_im'][j, d])
        h0l = _group_states(zeros(LRU_W), state_lru[:, j, d])
        h0r = _group_states(zeros(S5_N), state_s5_re[:, j, d].reshape(DEC_BATCH, S5_N))
        h0i = _group_states(zeros(S5_N), state_s5_im[:, j, d].reshape(DEC_BATCH, S5_N))
        per_dir.append((h0l, h0r, h0i, p['ab_conv_w'][j], p['ab_conv_b'][j][None], wg, bg,
                        p['lru_lambda'][j, d][None], bm, ar, ai, cm))
    (haf, yf, llf, lrf, lif), (hab, yr, llb, lrb, lib) = _ab_scan(xz, per_dir)
    x = _ab_out(haf, hab, yf, yr, xz, xp, xs, m, ng[1:2], p['s5_d'][j][None], p['s5_w_glu'][j].astype(BF16),
                p['s5_b_glu'][j][None], p['ab_w_out'][j].astype(BF16))
    streams = _ffn(x, m, ng[2:3], ng[3:4], p['ffn_w_gate_up'][j].astype(BF16), p['ffn_w_down'][j].astype(BF16))
    prompt = lambda f, b, w: jnp.stack([f[:2].reshape(BATCH, w), b[:2].reshape(BATCH, w)], axis=1)
    lru = prompt(llf, llb, LRU_W)
    s5r = prompt(lrf, lrb, S5_N).reshape(BATCH, 2, S5_G, S5_P)
    s5i = prompt(lif, lib, S5_N).reshape(BATCH, 2, S5_G, S5_P)
    return tuple(streams), lru, s5r, s5i


def _head_major(w, parts):
    k = w.shape[0]
    per_head = w.reshape(k, HEADS, -1)
    out, start = [], 0
    for width in parts:
        out.append(per_head[:, :, start:start + width].reshape(k, HEADS * width))
        start += width
    return jnp.concatenate(out, axis=1)


def _layer_mla_moe(xp, xs, m, ng, j, cache_kv_latent, cache_k_rope, p):
    w1 = jnp.concatenate([p['mla_w_in'][j], jnp.zeros((D, LANE - QK_ROPE), F32)], axis=1).astype(BF16)
    wuq = _head_major(p['mla_w_uq'][j], (QK_NOPE, QK_ROPE)).astype(BF16)
    wukv = _head_major(p['mla_w_ukv'][j], (QK_NOPE, V_DIM)).astype(BF16)
    tables = _rope_tables(MLA_TM)
    qn, qr, kn, v, kr2, ckv, krr = _mla_proj(xp, xs, ng[0:1], m, w1, p['mla_g_q'][j][None], p['mla_g_kv'][j][None],
                                              wuq, wukv, *tables)
    knc, vc = _cache_kv(cache_kv_latent[:, j].reshape(DEC_BATCH * PAST_LEN, KV_LORA), wukv)
    krc = cache_k_rope[:, j].reshape(DEC_BATCH * PAST_LEN, QK_ROPE)
    z = jnp.zeros_like(krc)
    kr2c = jnp.concatenate([krc, z, z, krc], axis=1).astype(BF16)
    o_p = _attention(qn, qr, kn, kr2, v, row0=0, n_seq=BATCH, seq=SEQ, tq=SEQ)
    o_s = _attention(qn, qr, kn, kr2, v, row0=NP, n_seq=DEC_BATCH, seq=DEC_SEQ, tq=1024, cache=(knc, kr2c, vc))
    wr_t = p['moe_w_router'][j].T
    wr_hi = wr_t.astype(BF16)
    wr_lo = (wr_t - wr_hi.astype(F32)).astype(BF16)
    x3, h, ri, rf, cnt = _attn_out_router(o_p, o_s, xp, xs, m, ng[1:2], ng[2:3], p['mla_w_out'][j].astype(BF16), wr_hi, wr_lo)
    counts = cnt[:, 0].astype(I32)
    padded = ((counts + TM_E - 1) // TM_E) * TM_E
    ends = jnp.cumsum(padded)
    offs = ends - padded
    pos1 = offs[ri[0]] + ri[2]
    pos2 = offs[ri[1]] + ri[3]
    pick_tok = jnp.arange(2 * N, dtype=I32)
    dest = jnp.full((P_ROWS,), -1, I32).at[jnp.concatenate([pos1, pos2])].set(pick_tok, unique_indices=True)
    is_pad = dest < 0
    pad_row = 2 * N + TM_E + jnp.cumsum(is_pad.astype(I32)) - 1
    src_tbl = jnp.concatenate([jnp.where(is_pad, 0, dest % N), jnp.zeros((TM_E,), I32)])
    dst_tbl = jnp.concatenate([2 * N + jnp.arange(TM_E, dtype=I32), jnp.where(is_pad, pad_row, dest)])
    n_used = (ends[-1] // TM_E).astype(I32)[None]
    tile_row = jnp.minimum(jnp.arange(N_TILES, dtype=I32), n_used - 1) * TM_E
    tile_expert = jnp.sum((tile_row[:, None] >= ends[None, :]).astype(I32), axis=1)
    y = _moe_experts(tile_expert, n_used, src_tbl.reshape(N_TILES + 1, 1, TM_E), dst_tbl.reshape(N_TILES + 1, 1, TM_E),
                     h, p['moe_w_gate_up'][j].astype(BF16), p['moe_w_down'][j].astype(BF16))
    xp, xs = _moe_combine(y, x3, rf[0][:, None], rf[1][:, None], m, ng[3:4])
    kv_new = ckv.reshape(BATCH, SEQ, KV_LORA)
    kr_new = krr[:, :QK_ROPE].reshape(BATCH, SEQ, QK_ROPE)
    return (xp.reshape(BATCH, SEQ, D), xs.reshape(DEC_BATCH, DEC_SEQ, D)), kv_new, kr_new


def kernel(x_prompt, x_sample, c, state_lru, state_s5_re, state_s5_im, cache_kv_latent, cache_k_rope, c_ctx, w_mod, b_mod, norm_gains, ab_w_in, ab_conv_w, ab_conv_b, lru_wa, lru_ba, lru_wx, lru_bx, lru_lambda, s5_a_re, s5_a_im, s5_log_dt, s5_b_re, s5_b_im, s5_c_re, s5_c_im, s5_d, s5_w_glu, s5_b_glu, ab_w_out, ffn_w_gate_up, ffn_w_down, mla_w_in, mla_g_q, mla_g_kv, mla_w_uq, mla_w_ukv, mla_w_out, moe_w_router, moe_w_gate_up, moe_w_down):
    p = dict(ab_w_in=ab_w_in, ab_conv_w=ab_conv_w, ab_conv_b=ab_conv_b, lru_wa=lru_wa, lru_ba=lru_ba,
             lru_wx=lru_wx, lru_bx=lru_bx, lru_lambda=lru_lambda, s5_a_re=s5_a_re, s5_a_im=s5_a_im,
             s5_log_dt=s5_log_dt, s5_b_re=s5_b_re, s5_b_im=s5_b_im, s5_c_re=s5_c_re, s5_c_im=s5_c_im,
             s5_d=s5_d, s5_w_glu=s5_w_glu, s5_b_glu=s5_b_glu, ab_w_out=ab_w_out, ffn_w_gate_up=ffn_w_gate_up,
             ffn_w_down=ffn_w_down, mla_w_in=mla_w_in, mla_g_q=mla_g_q, mla_g_kv=mla_g_kv, mla_w_uq=mla_w_uq,
             mla_w_ukv=mla_w_ukv, mla_w_out=mla_w_out, moe_w_router=moe_w_router, moe_w_gate_up=moe_w_gate_up,
             moe_w_down=moe_w_down)
    depth = w_mod.shape[0]
    cond = jnp.concatenate([c_ctx[None], c, jnp.zeros((2 * SUB - 1 - DEC_BATCH, D), F32)], axis=0)
    mod = _modulation(cond, w_mod, b_mod)
    ctx_tile = lambda l: jnp.broadcast_to(mod[l, 0:1], (SUB, 6 * D))
    streams = (x_prompt, x_sample)
    lru_l, s5r_l, s5i_l, kv_l, kr_l = [], [], [], [], []
    for layer in range(depth):
        j = layer // 2
        ng = norm_gains[layer]
        if layer % 2 == 0:
            m = jnp.stack([ctx_tile(layer), mod[layer, 1:1 + DEC_BATCH]])
            streams, lru, s5r, s5i = _layer_ab(*streams, m, ng, j, state_lru, state_s5_re, state_s5_im, p)
            lru_l.append(lru)
            s5r_l.append(s5r)
            s5i_l.append(s5i)
        else:
            lat = jnp.broadcast_to(mod[layer, 1:1 + DEC_BATCH, None, :], (DEC_BATCH, SUB, 6 * D))
            m = jnp.concatenate([ctx_tile(layer)[None], lat], axis=0)
            streams, kv_new, kr_new = _layer_mla_moe(streams[0].reshape(NP, D), streams[1].reshape(NS, D), m, ng, j,
                                                     cache_kv_latent, cache_k_rope, p)
            kv_l.append(kv_new)
            kr_l.append(kr_new)
    return (streams[0], streams[1],
            jnp.stack(lru_l, axis=1), jnp.stack(s5r_l, axis=1), jnp.stack(s5i_l, axis=1),
            jnp.stack(kv_l, axis=1), jnp.stack(kr_l, axis=1))
```

```python
import functools
import math

import numpy as np
import jax
import jax.numpy as jnp
from jax import lax
from jax.experimental import pallas as pl
from jax.experimental.pallas import tpu as pltpu

F32 = jnp.float32
BF16 = jnp.bfloat16
I32 = jnp.int32

D = 1024
BATCH, SEQ = 16, 256
DEC_BATCH, DEC_SEQ = 8, 2048
PAST_LEN = 256
GRID_W = 64
LRU_W = 512
LRU_HEADS = 8
LRU_C = 8.0
CONV_W = 4
S5_W = 512
S5_CH = 16
S5_G = 32
S5_P = 64
S5_N = S5_G * S5_P
HEADS = 8
QK_NOPE, QK_ROPE, V_DIM = 128, 64, 128
Q_LORA, KV_LORA = 384, 256
ROPE_THETA = 10000.0
D_FF = 2816
N_EXP = 8
EPS = 1e-6

NP = BATCH * SEQ
NS = DEC_BATCH * DEC_SEQ
N = NP + NS
SUB = 8
LANE = 128
ROW_TILE = D // LANE
T_CHUNK = 64
R_CHUNK = T_CHUNK * SUB
FF_BLK = 256
KEY_BLK = 256
HEAD_GRP = 4
MLA_TM = 512
MLA_SUB = 2
ROUTER_SUB = 1
ABOUT_SUB = 1
ABOUT_T = 64
INPROJ_SUB = 2
INPROJ_T = 64
FFN_T = 128
SIDE_ITERS = 4
INV_BLK = 2048
N_FF = D_FF // FF_BLK
TM_E = 512
P_ROWS = 2 * N + N_EXP * TM_E
N_TILES = P_ROWS // TM_E
DUMP_ROWS = P_ROWS - 2 * N + TM_E
VMEM_LIMIT = 56 * 1024 * 1024

NT_DIMS = (((1,), (1,)), ((), ()))


def _cparams(sem):
    return pltpu.CompilerParams(dimension_semantics=sem, vmem_limit_bytes=VMEM_LIMIT)


def _dot(a, b):
    return jnp.dot(a, b, preferred_element_type=F32)


def _sigmoid(x):
    return 1.0 / (1.0 + jnp.exp(-x))


def _neg_expm1_2x(log_a, a):
    series = -2.0 * log_a * (1.0 + log_a * (1.0 + log_a * (2.0 / 3.0) * (1.0 + log_a * 0.5)))
    return jnp.where(log_a > -0.01, series, (1.0 - a) * (1.0 + a))


def _sqrt_nonneg(v):
    return jnp.where(v > 0.0, v * lax.rsqrt(v), 0.0)


def _gelu(x):
    k = math.sqrt(2.0 / math.pi)
    half = 0.5 * x
    return half + half * jnp.tanh(x * (k + (k * 0.044715) * (x * x)))


def _rms(x, g):
    ms = jnp.mean(x * x, axis=-1, keepdims=True)
    return x * lax.rsqrt(ms + EPS) * g


def _rows8(y, fn):
    r, c = y.shape
    return fn(y.reshape(r // SUB, SUB, c)).reshape(r, c)


def _adaln(x, g, scale, shift):
    return _rows8(_rms(x, g), lambda y: y * (1.0 + scale)[None] + shift[None])


def _gated(x, y, g, gate):
    return x + _rows8(_rms(y, g), lambda z: z * gate[None])


def _store_row_tiles(ref, piece, row0=0, rows=None):
    rows = ref.shape[0] // ROW_TILE if rows is None else rows
    for c in range(ROW_TILE):
        ref[pl.ds(row0 * ROW_TILE + c, rows, stride=ROW_TILE), :] = piece(slice(c * LANE, (c + 1) * LANE))


def _load_row_tiles(ref, c, rows):
    return ref[pl.ds(c, rows, stride=ROW_TILE), :]


def _full(shape):
    nd = len(shape)
    return pl.BlockSpec(shape, lambda *_: (0,) * nd)


def _mod_spec(k, grp):
    return pl.BlockSpec((1, SUB, D), lambda i, *_: (grp(i), 0, k))


def _grp_tm(tm):
    return lambda i: (i * tm >= NP).astype(I32)


def _grp_bm(tm):
    return lambda i: jnp.where(i * tm < NP, 0, 1 + (i * tm - NP) // DEC_SEQ)


def _mod_kernel(c_ref, w_ref, b_ref, o_ref):
    c = c_ref[...]
    s = c * _sigmoid(c)
    o_ref[0] = _dot(s.astype(BF16), w_ref[0].astype(BF16)) + b_ref[0]


def _modulation(cond, w_mod, b_mod):
    depth = w_mod.shape[0]
    rows = cond.shape[0]
    return pl.pallas_call(
        _mod_kernel,
        grid=(depth, 6),
        in_specs=[_full((rows, D)),
                  pl.BlockSpec((1, D, D), lambda l, j: (l, 0, j)),
                  pl.BlockSpec((1, 1, D), lambda l, j: (l, 0, j))],
        out_specs=pl.BlockSpec((1, rows, D), lambda l, j: (l, 0, j)),
        out_shape=jax.ShapeDtypeStruct((depth, rows, 6 * D), F32),
        compiler_params=_cparams(("arbitrary", "arbitrary")),
        name="modulation",
    )(cond, w_mod, b_mod.reshape(depth, 1, 6 * D))


def _time_major_copies(step, t_steps, xp_hbm, xs_hbm, buf, sem, to_hbm):
    p_steps = (BATCH // SUB) * (SEQ // t_steps)
    per_group = SEQ // t_steps

    def issue(hbm, seq0, t0):
        t0 = pl.multiple_of(t0, t_steps)
        for b in range(SUB):
            rows = hbm.at[seq0 + b, pl.ds(t0, t_steps), :]
            tile = buf.at[:, b, :]
            (pltpu.make_async_copy(tile, rows, sem) if to_hbm else pltpu.make_async_copy(rows, tile, sem)).start()

    @pl.when(step < p_steps)
    def _():
        issue(xp_hbm, (step // per_group) * SUB, (step % per_group) * t_steps)

    @pl.when(step >= p_steps)
    def _():
        issue(xs_hbm, 0, (step - p_steps) * t_steps)


def _time_major_wait(t_steps, xs_hbm, buf, sem, to_hbm):
    for b in range(SUB):
        rows = xs_hbm.at[0, pl.ds(0, t_steps), :]
        tile = buf.at[:, b, :]
        (pltpu.make_async_copy(tile, rows, sem) if to_hbm else pltpu.make_async_copy(rows, tile, sem)).wait()


def _time_major_read(t_steps, xp_hbm, xs_hbm, buf, sem):
    step = pl.program_id(0)
    slot = step % 2

    @pl.when(step == 0)
    def _():
        _time_major_copies(step, t_steps, xp_hbm, xs_hbm, buf.at[0], sem.at[0], False)

    @pl.when(step + 1 < pl.num_programs(0))
    def _():
        _time_major_copies(step + 1, t_steps, xp_hbm, xs_hbm, buf.at[1 - slot], sem.at[1 - slot], False)

    _time_major_wait(t_steps, xs_hbm, buf.at[slot], sem.at[slot], False)
    return slot


def _inproj_kernel(xp_hbm, xs_hbm, g_ref, sh_ref, sc_ref, w_ref, o_ref, buf, sem):
    slot = _time_major_read(INPROJ_T, xp_hbm, xs_hbm, buf, sem)
    sub_t = INPROJ_T // INPROJ_SUB
    for r in range(INPROJ_SUB):
        rows = slice(r * sub_t * SUB, (r + 1) * sub_t * SUB)
        x = buf[slot, r * sub_t:(r + 1) * sub_t].reshape(sub_t * SUB, D)
        h = _adaln(x, g_ref[...], sc_ref[0], sh_ref[0])
        o_ref[rows, :] = _dot(h.astype(BF16), w_ref[...])


def _ab_inproj(xp, xs, gain, mods, w_in):
    tm = INPROJ_T * SUB
    nout = w_in.shape[1]
    grp = _grp_tm(tm)
    return pl.pallas_call(
        _inproj_kernel,
        grid=(N // tm,),
        in_specs=[pl.BlockSpec(memory_space=pl.ANY), pl.BlockSpec(memory_space=pl.ANY),
                  _full((1, D)),
                  _mod_spec(0, grp), _mod_spec(1, grp),
                  _full((D, nout))],
        out_specs=pl.BlockSpec((tm, nout), lambda i: (i, 0)),
        out_shape=jax.ShapeDtypeStruct((N, nout), F32),
        scratch_shapes=[pltpu.VMEM((2, INPROJ_T, SUB, D), F32), pltpu.SemaphoreType.DMA((2,))],
        compiler_params=_cparams(("arbitrary",)),
        name="ab_inproj",
    )(xp, xs, gain, mods, mods, w_in)


def _scan_table(reverse):
    cols = []
    groups = [(0, SEQ // T_CHUNK, 0), (1, SEQ // T_CHUNK, SEQ // T_CHUNK),
              (2, DEC_SEQ // T_CHUNK, NP // R_CHUNK)]
    for g, nc, base in groups:
        order = range(nc - 1, -1, -1) if reverse else range(nc)
        for k, c in enumerate(order):
            cols.append((base + c, g, int(k == 0), int(c > 0), int(c < nc - 1)))
    return np.asarray(cols, np.int32).T.copy()


N_SCAN_IN, N_SCAN_OUT, N_SCAN_SCRATCH = 16, 5, 8
SCAN_TBL_ROWS = 5


def _scan_kernel(tbl, *refs):
    ins = [refs[d * N_SCAN_IN:(d + 1) * N_SCAN_IN] for d in range(2)]
    o0 = 2 * N_SCAN_IN
    outs = [refs[o0 + d * N_SCAN_OUT:o0 + (d + 1) * N_SCAN_OUT] for d in range(2)]
    s0 = o0 + 2 * N_SCAN_OUT
    scr = [refs[s0 + d * N_SCAN_SCRATCH:s0 + (d + 1) * N_SCAN_SCRATCH] for d in range(2)]
    s = pl.program_id(0)
    for d in range(2):
        h0l_ref, h0r_ref, h0i_ref = ins[d][4:7]
        hl, sre, sim = scr[d][5:8]

        @pl.when(tbl[d * SCAN_TBL_ROWS + 2, s] == 1)
        def _(h0l_ref=h0l_ref, h0r_ref=h0r_ref, h0i_ref=h0i_ref, hl=hl, sre=sre, sim=sim):
            hl[...] = h0l_ref[0]
            sre[...] = h0r_ref[0]
            sim[...] = h0i_ref[0]

    fwd, bwd = [_ScanChunk(tbl, d * SCAN_TBL_ROWS, ins[d], outs[d], scr[d], reverse=(d == 1)) for d in range(2)]
    fwd.s5_project()
    fwd.conv()
    fwd.gates()
    bwd.s5_project()
    fwd.s5_recurrence()
    bwd.conv()
    fwd.s5_readout()
    bwd.gates()
    fwd.lru_coefficients()
    fwd.lru_recurrence()
    bwd.s5_recurrence()
    bwd.s5_readout()
    bwd.lru_coefficients()
    bwd.lru_recurrence()


class _ScanChunk:
    def __init__(self, tbl, row0, ins, outs, scratch, reverse):
        (self.xa_ref, self.xp_ref, self.xn_ref, self.xb_ref, _, _, _, self.cw_ref, self.cb_ref, self.wg_ref,
         self.bg_ref, self.lam_ref, self.bm_ref, self.ar_ref, self.ai_ref, self.cm_ref) = ins
        self.ha_ref, self.y_ref, self.ll_ref, self.lr_ref, self.li_ref = outs
        self.ext, self.abuf, self.bbuf, self.hre, self.him, self.hl, self.sre, self.sim = scratch
        step = pl.program_id(0)
        self.has_prev = tbl[row0 + 3, step] == 1
        self.has_next = tbl[row0 + 4, step] == 1
        self.order = range(T_CHUNK - 1, -1, -1) if reverse else range(T_CHUNK)
        self.half = S5_N // 2

    def s5_project(self):
        ub = self.xb_ref[...].astype(BF16)
        half = self.half
        for j in range(2):
            bu = _dot(ub[:, j * 256:(j + 1) * 256], self.bm_ref[j])
            self.hre[:, j * half:(j + 1) * half] = bu[:, :half]
            self.him[:, j * half:(j + 1) * half] = bu[:, half:]

    def conv(self):
        ext = self.ext
        ext[0:2 * SUB] = jnp.where(self.has_prev, self.xp_ref[...], 0.0)
        ext[2 * SUB:2 * SUB + R_CHUNK] = self.xa_ref[...]
        ext[2 * SUB + R_CHUNK:3 * SUB + R_CHUNK] = jnp.where(self.has_next, self.xn_ref[...], 0.0)
        xa = self.cb_ref[...] + self.cw_ref[0:1] * ext[0:R_CHUNK]
        for k in range(1, CONV_W):
            xa = xa + self.cw_ref[k:k + 1] * ext[k * SUB:k * SUB + R_CHUNK]
        self.xa = xa

    def gates(self):
        self.gz = _dot(self.xa.astype(BF16), self.wg_ref[...]) + self.bg_ref[...]

    def s5_recurrence(self):
        cblk = 4 * LANE
        for cb in range(S5_N // cblk):
            cols = slice(cb * cblk, (cb + 1) * cblk)
            ar = jnp.broadcast_to(self.ar_ref[:, cols], (SUB, cblk))
            ai = jnp.broadcast_to(self.ai_ref[:, cols], (SUB, cblk))
            hr = self.sre[:, cols]
            hi = self.sim[:, cols]
            for t in self.order:
                rows = slice(t * SUB, (t + 1) * SUB)
                nr = ar * hr - ai * hi + self.hre[rows, cols]
                ni = ar * hi + ai * hr + self.him[rows, cols]
                hr, hi = nr, ni
                self.hre[rows, cols] = hr
                self.him[rows, cols] = hi
            self.sre[:, cols] = hr
            self.sim[:, cols] = hi
        self.lr_ref[0] = self.sre[...]
        self.li_ref[0] = self.sim[...]

    def s5_readout(self):
        half = self.half
        for j in range(2):
            hc = jnp.concatenate([self.hre[:, j * half:(j + 1) * half], self.him[:, j * half:(j + 1) * half]],
                                 axis=1).astype(BF16)
            self.y_ref[:, j * 256:(j + 1) * 256] = _dot(hc, self.cm_ref[j])

    def lru_coefficients(self):
        r = 0.5 * jnp.tanh(0.5 * self.gz[:, :LRU_W]) + 0.5
        i = 0.5 * jnp.tanh(0.5 * self.gz[:, LRU_W:]) + 0.5
        lam = self.lam_ref[...]
        log_sig = jnp.minimum(lam, 0.0) - jnp.log1p(jnp.exp(-jnp.abs(lam)))
        log_a = LRU_C * r * log_sig
        a = jnp.exp(log_a)
        self.abuf[...] = a
        self.bbuf[...] = _sqrt_nonneg(_neg_expm1_2x(log_a, a)) * (i * self.xa)

    def lru_recurrence(self):
        h = self.hl[...]
        for t in self.order:
            rows = slice(t * SUB, (t + 1) * SUB)
            h = self.abuf[rows] * h + self.bbuf[rows]
            self.ha_ref[rows, :] = h
        self.hl[...] = h
        self.ll_ref[0] = h


def _ab_scan(xz, per_dir):
    tbl = jnp.asarray(np.concatenate([_scan_table(False), _scan_table(True)], axis=0))
    n_steps = tbl.shape[1]
    in_specs, out_specs, scratch, out_shape, args = [], [], [], [], []
    for d in range(2):
        blk = lambda s, t, d=d: t[d * SCAN_TBL_ROWS, s]
        grp = lambda s, t, d=d: t[d * SCAN_TBL_ROWS + 1, s]
        state_spec = lambda w, grp=grp: pl.BlockSpec((1, SUB, w), lambda s, t: (grp(s, t), 0, 0))
        const = lambda shape: pl.BlockSpec(shape, lambda s, t: (0,) * len(shape))
        in_specs += [
            pl.BlockSpec((R_CHUNK, LRU_W), lambda s, t, blk=blk: (blk(s, t), 0)),
            pl.BlockSpec((2 * SUB, LRU_W),
                         lambda s, t, blk=blk: (jnp.maximum(blk(s, t) * (T_CHUNK // 2) - 1, 0), 0)),
            pl.BlockSpec((SUB, LRU_W),
                         lambda s, t, blk=blk: (jnp.minimum((blk(s, t) + 1) * T_CHUNK, N // SUB - 1), 0)),
            pl.BlockSpec((R_CHUNK, S5_W), lambda s, t, blk=blk: (blk(s, t), 2)),
            state_spec(LRU_W), state_spec(S5_N), state_spec(S5_N),
            const((CONV_W, LRU_W)), const((1, LRU_W)),
            const((LRU_W, 2 * LRU_W)), const((1, 2 * LRU_W)), const((1, LRU_W)),
            const((2, 256, S5_N)), const((1, S5_N)), const((1, S5_N)), const((2, S5_N, 256)),
        ]
        out_specs += [
            pl.BlockSpec((R_CHUNK, LRU_W), lambda s, t, blk=blk: (blk(s, t), 0)),
            pl.BlockSpec((R_CHUNK, S5_W), lambda s, t, blk=blk: (blk(s, t), 0)),
            state_spec(LRU_W), state_spec(S5_N), state_spec(S5_N),
        ]
        scratch += [
            pltpu.VMEM((R_CHUNK + 3 * SUB, LRU_W), F32),
            pltpu.VMEM((R_CHUNK, LRU_W), F32), pltpu.VMEM((R_CHUNK, LRU_W), F32),
            pltpu.VMEM((R_CHUNK, S5_N), F32), pltpu.VMEM((R_CHUNK, S5_N), F32),
            pltpu.VMEM((SUB, LRU_W), F32), pltpu.VMEM((SUB, S5_N), F32), pltpu.VMEM((SUB, S5_N), F32),
        ]
        out_shape += [jax.ShapeDtypeStruct((N, LRU_W), F32), jax.ShapeDtypeStruct((N, S5_W), F32),
                      jax.ShapeDtypeStruct((3, SUB, LRU_W), F32),
                      jax.ShapeDtypeStruct((3, SUB, S5_N), F32), jax.ShapeDtypeStruct((3, SUB, S5_N), F32)]
        args += [xz, xz, xz, xz, *per_dir[d]]
    grid_spec = pltpu.PrefetchScalarGridSpec(num_scalar_prefetch=1, grid=(n_steps,), in_specs=in_specs,
                                             out_specs=out_specs, scratch_shapes=scratch)
    outs = pl.pallas_call(
        _scan_kernel,
        grid_spec=grid_spec,
        out_shape=out_shape,
        compiler_params=_cparams(("arbitrary",)),
        name="ab_scan",
    )(tbl, *args)
    return outs[:N_SCAN_OUT], outs[N_SCAN_OUT:]


def _about_kernel(haf, hab, yf, yr, ga, xb, xp_hbm, xs_hbm, gate, g1, d_ref, wglu, bglu, wout, o_ref, buf, sem):
    slot = _time_major_read(ABOUT_T, xp_hbm, xs_hbm, buf, sem)
    x_all = buf[slot].reshape(ABOUT_T * SUB, D)
    sub = ABOUT_T * SUB // ABOUT_SUB
    for r in range(ABOUT_SUB):
        rows = slice(r * sub, (r + 1) * sub)
        ya = (haf[rows, :] + hab[rows, :]) * _gelu(ga[rows, :])
        yb0 = _gelu(yf[rows, :] + yr[rows, :] + d_ref[...] * xb[rows, :])
        half_yb0 = 0.5 * yb0
        yb = half_yb0 + half_yb0 * jnp.tanh(0.5 * (_dot(yb0.astype(BF16), wglu[...]) + bglu[...]))
        out = _dot(ya.astype(BF16), wout[0:LRU_W]) + _dot(yb.astype(BF16), wout[LRU_W:LRU_W + S5_W])
        o_ref[rows, :] = _gated(x_all[rows], out, g1[...], gate[0])


def _ab_out(haf, hab, yf, yr, xz, xp, xs, mods, g1, s5_d, wglu, bglu, wout):
    tm = ABOUT_T * SUB
    grp = _grp_tm(tm)
    half = lambda c: pl.BlockSpec((tm, LRU_W), lambda i: (i, c))
    return pl.pallas_call(
        _about_kernel,
        grid=(N // tm,),
        in_specs=[half(0), half(0), half(0), half(0), half(1), half(2),
                  pl.BlockSpec(memory_space=pl.ANY), pl.BlockSpec(memory_space=pl.ANY),
                  _mod_spec(2, grp), _full((1, D)), _full((1, S5_W)),
                  _full((S5_W, S5_W)), _full((1, S5_W)), _full((LRU_W + S5_W, D))],
        out_specs=pl.BlockSpec((tm, D), lambda i: (i, 0)),
        out_shape=jax.ShapeDtypeStruct((N, D), F32),
        scratch_shapes=[pltpu.VMEM((2, ABOUT_T, SUB, D), F32), pltpu.SemaphoreType.DMA((2,))],
        compiler_params=_cparams(("arbitrary",)),
        name="ab_out",
    )(haf, hab, yf, yr, xz, xz, xp, xs, mods, g1, s5_d, wglu, bglu, wout)


def _swiglu_block(hbf, wgu, wd, acc, j):
    static = isinstance(j, int)
    blk = lambda start: pl.ds(start if static else pl.multiple_of(start, FF_BLK), FF_BLK)
    h = hbf[...]
    g = _dot(h, wgu[:, blk(j * FF_BLK)])
    u = _dot(h, wgu[:, blk(D_FF + j * FF_BLK)])
    act = (g * _sigmoid(g)) * u
    part = _dot(act.astype(BF16), wd[blk(j * FF_BLK), :])
    if static and j == 0:
        acc[...] = part
    else:
        acc[...] += part


def _ffn_kernel(x_ref, sh, sc, gt, g2, g3, wgu, wd, op_hbm, os_hbm, hbf, acc, obuf, sem):
    step = pl.program_id(0)
    slot = step % 2
    hbf[...] = _adaln(x_ref[...], g2[...], sc[0], sh[0]).astype(BF16)
    for j in range(N_FF):
        _swiglu_block(hbf, wgu, wd, acc, j)
    obuf[slot] = _gated(x_ref[...], acc[...], g3[...], gt[0]).reshape(FFN_T, SUB, D)
    _time_major_copies(step, FFN_T, op_hbm, os_hbm, obuf.at[slot], sem.at[slot], True)

    @pl.when(step > 0)
    def _():
        _time_major_wait(FFN_T, os_hbm, obuf.at[1 - slot], sem.at[1 - slot], True)

    @pl.when(step == pl.num_programs(0) - 1)
    def _():
        _time_major_wait(FFN_T, os_hbm, obuf.at[slot], sem.at[slot], True)


def _ffn(x, mods, g2, g3, wgu, wd):
    tm = FFN_T * SUB
    grp = _grp_tm(tm)
    once = lambda shape: pl.BlockSpec(shape, lambda i: (0,) * len(shape), pipeline_mode=pl.Buffered(1))
    return pl.pallas_call(
        _ffn_kernel,
        grid=(N // tm,),
        in_specs=[pl.BlockSpec((tm, D), lambda i: (i, 0)),
                  _mod_spec(3, grp), _mod_spec(4, grp), _mod_spec(5, grp),
                  _full((1, D)), _full((1, D)),
                  once(wgu.shape), once(wd.shape)],
        out_specs=[pl.BlockSpec(memory_space=pl.ANY), pl.BlockSpec(memory_space=pl.ANY)],
        out_shape=[jax.ShapeDtypeStruct((BATCH, SEQ, D), F32), jax.ShapeDtypeStruct((DEC_BATCH, DEC_SEQ, D), F32)],
        scratch_shapes=[pltpu.VMEM((tm, D), BF16), pltpu.VMEM((tm, D), F32),
                        pltpu.VMEM((2, FFN_T, SUB, D), F32), pltpu.SemaphoreType.DMA((2,))],
        compiler_params=_cparams(("arbitrary",)),
        name="ffn",
    )(x, mods, mods, mods, g2, g3, wgu, wd)


def _pair_swap(x):
    outs = []
    for c in range(x.shape[1] // LANE):
        xc = x[:, c * LANE:(c + 1) * LANE]
        even = lax.broadcasted_iota(I32, xc.shape, 1) % 2 == 0
        outs.append(jnp.where(even, pltpu.roll(xc, LANE - 1, 1), pltpu.roll(xc, 1, 1)))
    return outs[0] if len(outs) == 1 else jnp.concatenate(outs, axis=1)


def _stream_specs(tm, width):
    n_p = NP // tm
    return [pl.BlockSpec((tm, width), lambda i: (jnp.minimum(i, n_p - 1), 0)),
            pl.BlockSpec((tm, width), lambda i: (jnp.maximum(i - n_p, 0), 0))]


def _stream_rows(p_ref, s_ref, prompt_steps):
    return jnp.where(pl.program_id(0) < prompt_steps, p_ref[...], s_ref[...])


def _mlaproj_kernel(xp_ref, xs_ref, g0, sh, sc, w1, gq, gkv, wuq, wukv, cq_ref, sq_ref, ck_ref, sk_ref,
                    qn_ref, qr_ref, kn_ref, v_ref, kr2_ref, ckv_ref, krr_ref, *, prompt_steps):
    x = _stream_rows(xp_ref, xs_ref, prompt_steps)
    sub = x.shape[0] // MLA_SUB
    cache_rows = []
    for r in range(MLA_SUB):
        rows = slice(r * sub, (r + 1) * sub)
        h = _adaln(x[rows], g0[...], sc[0], sh[0])
        dn = _dot(h.astype(BF16), w1[...])
        cq = _rms(dn[:, :Q_LORA], gq[...])
        ckv = _rms(dn[:, Q_LORA:Q_LORA + KV_LORA], gkv[...])
        krp = dn[:, Q_LORA + KV_LORA:]
        cache_rows.append((rows, ckv, krp))
        q = _dot(cq.astype(BF16), wuq[...])
        qn_ref[rows, :] = q[:, :HEADS * QK_NOPE].astype(BF16)
        qr = q[:, HEADS * QK_NOPE:]
        qr_ref[rows, :] = (qr * cq_ref[rows, :] + _pair_swap(qr) * sq_ref[rows, :]).astype(BF16)
        kv = _dot(ckv.astype(BF16), wukv[...])
        kn_ref[rows, :] = kv[:, :HEADS * QK_NOPE].astype(BF16)
        v_ref[rows, :] = kv[:, HEADS * QK_NOPE:].astype(BF16)
        kr = krp * ck_ref[rows, :] + _pair_swap(krp) * sk_ref[rows, :]
        kr2_ref[rows, :] = jnp.concatenate([kr, pltpu.roll(kr, QK_ROPE, 1)], axis=1).astype(BF16)

    @pl.when(pl.program_id(0) < prompt_steps)
    def _():
        for rows, ckv, krp in cache_rows:
            ckv_ref[rows, :] = ckv
            krr_ref[rows, :] = krp


def _mla_proj(xp, xs, g0, mods, w1, gq, gkv, wuq, wukv, cos_q, sin_q, cos_k, sin_k):
    tm = MLA_TM
    grp = _grp_bm(tm)
    n_pos = DEC_SEQ // tm
    tab = lambda w: pl.BlockSpec((tm, w), lambda i: (jnp.where(i * tm < NP, n_pos, (i - NP // tm) % n_pos), 0))
    row = lambda w: pl.BlockSpec((tm, w), lambda i: (i, 0))
    shp = lambda w, dt: jax.ShapeDtypeStruct((N, w), dt)
    n_p = NP // tm
    prow = lambda w: pl.BlockSpec((tm, w), lambda i: (jnp.minimum(i, n_p - 1), 0))
    return pl.pallas_call(
        functools.partial(_mlaproj_kernel, prompt_steps=n_p),
        grid=(N // tm,),
        in_specs=_stream_specs(tm, D) + [_full((1, D)), _mod_spec(0, grp), _mod_spec(1, grp),
                  _full(w1.shape), _full((1, Q_LORA)), _full((1, KV_LORA)),
                  _full(wuq.shape), _full(wukv.shape),
                  tab(HEADS * QK_ROPE), tab(HEADS * QK_ROPE), tab(LANE), tab(LANE)],
        out_specs=[row(HEADS * QK_NOPE), row(HEADS * QK_ROPE), row(HEADS * QK_NOPE), row(HEADS * V_DIM),
                   row(2 * LANE), prow(KV_LORA), prow(LANE)],
        out_shape=[shp(HEADS * QK_NOPE, BF16), shp(HEADS * QK_ROPE, BF16), shp(HEADS * QK_NOPE, BF16),
                   shp(HEADS * V_DIM, BF16), shp(2 * LANE, BF16),
                   jax.ShapeDtypeStruct((NP, KV_LORA), F32), jax.ShapeDtypeStruct((NP, LANE), F32)],
        compiler_params=_cparams(("arbitrary",)),
        name="mla_proj",
    )(xp, xs, g0, mods, mods, w1, gq, gkv, wuq, wukv, cos_q, sin_q, cos_k, sin_k)


def _cachekv_kernel(c_ref, w_ref, kn_ref, v_ref):
    kv = _dot(c_ref[...].astype(BF16), w_ref[...])
    kn_ref[...] = kv[:, :HEADS * QK_NOPE].astype(BF16)
    v_ref[...] = kv[:, HEADS * QK_NOPE:].astype(BF16)


def _cache_kv(ckv_cache, wukv):
    rows = ckv_cache.shape[0]
    tm = 512
    return pl.pallas_call(
        _cachekv_kernel,
        grid=(rows // tm,),
        in_specs=[pl.BlockSpec((tm, KV_LORA), lambda i: (i, 0)), _full(wukv.shape)],
        out_specs=[pl.BlockSpec((tm, HEADS * QK_NOPE), lambda i: (i, 0)),
                   pl.BlockSpec((tm, HEADS * V_DIM), lambda i: (i, 0))],
        out_shape=[jax.ShapeDtypeStruct((rows, HEADS * QK_NOPE), BF16),
                   jax.ShapeDtypeStruct((rows, HEADS * V_DIM), BF16)],
        compiler_params=_cparams(("arbitrary",)),
        name="cache_kv",
    )(ckv_cache, wukv)


def _attn_kernel(*refs, has_cache):
    if has_cache:
        qn, qr, kn, kr, v, knc, krc, vc, o_ref, s_scr = refs
        streams = [(knc, krc, vc), (kn, kr, v)]
    else:
        qn, qr, kn, kr, v, o_ref, s_scr = refs
        streams = [(kn, kr, v)]
    chunks = [(k1, k2, vv, c * KEY_BLK) for k1, k2, vv in streams for c in range(k1.shape[0] // KEY_BLK)]
    tq = qn.shape[0]
    a = (QK_NOPE + QK_ROPE) ** -0.5 * math.log2(math.e)
    for hh in range(HEAD_GRP):
        cols = slice(hh * LANE, (hh + 1) * LANE)
        pair_cols = slice((hh // 2) * LANE, (hh // 2 + 1) * LANE)
        kr_cols = slice((hh % 2) * LANE, (hh % 2 + 1) * LANE)
        q = jnp.concatenate([qn[:, cols], qr[:, pair_cols]], axis=1)
        mx = jnp.full((tq, LANE), -jnp.inf, F32)
        for n, (k1, k2, _, r0) in enumerate(chunks):
            k = jnp.concatenate([k1[r0:r0 + KEY_BLK, cols], k2[r0:r0 + KEY_BLK, kr_cols]], axis=1)
            s = lax.dot_general(q, k, NT_DIMS, preferred_element_type=F32)
            s_scr[hh, :, n * KEY_BLK:(n + 1) * KEY_BLK] = s
            for c in range(KEY_BLK // LANE):
                mx = jnp.maximum(mx, s[:, c * LANE:(c + 1) * LANE])
        mb = jnp.max(mx, axis=-1, keepdims=True) * a
        den = jnp.zeros((tq, LANE), F32)
        o = jnp.zeros((tq, V_DIM), F32)
        for n, (_, _, vv, r0) in enumerate(chunks):
            p = jnp.exp2(s_scr[hh, :, n * KEY_BLK:(n + 1) * KEY_BLK] * a - mb)
            for c in range(KEY_BLK // LANE):
                den = den + p[:, c * LANE:(c + 1) * LANE]
            o = o + _dot(p.astype(BF16), vv[r0:r0 + KEY_BLK, cols])
        o_ref[:, cols] = (o / jnp.sum(den, axis=-1, keepdims=True)).astype(BF16)


def _attention(qn, qr, kn, kr2, v, *, row0, n_seq, seq, tq, cache=None):
    nq = seq // tq
    grp = HEAD_GRP * LANE
    qblk = lambda b, h, i: row0 // tq + b * nq + i
    kblk = lambda b: row0 // seq + b
    in_specs = [pl.BlockSpec((tq, grp), lambda b, h, i: (qblk(b, h, i), h)),
                pl.BlockSpec((tq, grp // 2), lambda b, h, i: (qblk(b, h, i), h)),
                pl.BlockSpec((seq, grp), lambda b, h, i: (kblk(b), h)),
                pl.BlockSpec((seq, 2 * LANE), lambda b, h, i: (kblk(b), 0)),
                pl.BlockSpec((seq, grp), lambda b, h, i: (kblk(b), h))]
    args = [qn, qr, kn, kr2, v]
    if cache is not None:
        knc, kr2c, vc = cache
        in_specs += [pl.BlockSpec((PAST_LEN, grp), lambda b, h, i: (b, h)),
                     pl.BlockSpec((PAST_LEN, 2 * LANE), lambda b, h, i: (b, 0)),
                     pl.BlockSpec((PAST_LEN, grp), lambda b, h, i: (b, h))]
        args += [knc, kr2c, vc]
    return pl.pallas_call(
        functools.partial(_attn_kernel, has_cache=cache is not None),
        grid=(n_seq, HEADS // HEAD_GRP, nq),
        in_specs=in_specs,
        out_specs=pl.BlockSpec((tq, grp), lambda b, h, i: (b * nq + i, h)),
        out_shape=jax.ShapeDtypeStruct((n_seq * seq, HEADS * V_DIM), BF16),
        scratch_shapes=[pltpu.VMEM((HEAD_GRP, tq, seq + (PAST_LEN if cache is not None else 0)), F32)],
        compiler_params=_cparams(("arbitrary", "arbitrary", "arbitrary")),
        name="attn_latent" if cache is not None else "attn_context",
    )(*args)


def _router_kernel(op_ref, os_ref, xp_ref, xs_ref, gate1, g1, sh2, sc2, g2, wout, wr_hi, wr_lo, tri,
                   x3_ref, h_ref, ri_ref, rf_ref, cnt_ref, carry, *, prompt_steps):
    step = pl.program_id(0)

    @pl.when(step == 0)
    def _():
        carry[...] = jnp.zeros_like(carry)

    o_all = _stream_rows(op_ref, os_ref, prompt_steps)
    x_all = _stream_rows(xp_ref, xs_ref, prompt_steps)
    sub = x_all.shape[0] // ROUTER_SUB
    dg = lambda a, b: lax.dot_general(a, b, NT_DIMS, preferred_element_type=F32)
    logits = []
    for r in range(ROUTER_SUB):
        rows = slice(r * sub, (r + 1) * sub)
        x3 = _gated(x_all[rows], _dot(o_all[rows], wout[...]), g1[...], gate1[0])
        x3_ref[rows, :] = x3
        h = _adaln(x3, g2[...], sc2[0], sh2[0])
        _store_row_tiles(h_ref, lambda cols, h=h: h[:, cols], r * sub, sub)
        h_hi = h.astype(BF16)
        h_lo = (h - h_hi.astype(F32)).astype(BF16)
        logits.append(dg(wr_hi[...], h_hi) + dg(wr_hi[...], h_lo) + dg(wr_lo[...], h_hi))
    lg = jnp.concatenate(logits, axis=1)
    eidx = lax.broadcasted_iota(I32, lg.shape, 0).astype(F32)
    m1 = jnp.max(lg, axis=0, keepdims=True)
    i1 = jnp.min(jnp.where(lg == m1, eidx, float(N_EXP)), axis=0, keepdims=True)
    sel1 = eidx == i1
    lg2 = jnp.where(sel1, -jnp.inf, lg)
    m2 = jnp.max(lg2, axis=0, keepdims=True)
    i2 = jnp.min(jnp.where(lg2 == m2, eidx, float(N_EXP)), axis=0, keepdims=True)
    sel2 = eidx == i2
    e = jnp.exp(m2 - m1)
    w1 = 1.0 / (1.0 + e)
    w2 = e / (1.0 + e)
    picked = jnp.where(sel1 | sel2, 1.0, 0.0)
    rank = _dot(picked.astype(BF16), tri[...]) + carry[:, 0:1]
    r1 = jnp.sum(jnp.where(sel1, rank, 0.0), axis=0, keepdims=True)
    r2 = jnp.sum(jnp.where(sel2, rank, 0.0), axis=0, keepdims=True)
    carry[...] = carry[...] + jnp.sum(picked, axis=1, keepdims=True)
    cnt_ref[...] = carry[...]
    ri_ref[...] = jnp.where(eidx == 0.0, i1, jnp.where(eidx == 1.0, i2, jnp.where(eidx == 2.0, r1, r2))).astype(I32)
    rf_ref[...] = jnp.where(eidx == 0.0, w1, w2)


def _attn_out_router(o_p, o_s, xp, xs, mods, g1, g2, wout, wr_hi, wr_lo):
    tm = 512
    grp = _grp_bm(tm)
    n_p = NP // tm
    tri = jnp.asarray(np.triu(np.ones((tm, tm), np.float32), 1), BF16)
    row = lambda w: pl.BlockSpec((tm, w), lambda i: (i, 0))
    col = pl.BlockSpec((N_EXP, tm), lambda i: (0, i))
    return pl.pallas_call(
        functools.partial(_router_kernel, prompt_steps=n_p),
        grid=(N // tm,),
        in_specs=_stream_specs(tm, HEADS * V_DIM) + _stream_specs(tm, D) + [_mod_spec(2, grp), _full((1, D)),
                  _mod_spec(3, grp), _mod_spec(4, grp), _full((1, D)),
                  _full((HEADS * V_DIM, D)), _full((N_EXP, D)), _full((N_EXP, D)), _full((tm, tm))],
        out_specs=[row(D), pl.BlockSpec((tm * ROW_TILE, LANE), lambda i: (i, 0)), col, col, _full((N_EXP, LANE))],
        out_shape=[jax.ShapeDtypeStruct((N, D), F32), jax.ShapeDtypeStruct((N * ROW_TILE, LANE), F32),
                   jax.ShapeDtypeStruct((N_EXP, N), I32), jax.ShapeDtypeStruct((N_EXP, N), F32),
                   jax.ShapeDtypeStruct((N_EXP, LANE), F32)],
        scratch_shapes=[pltpu.VMEM((N_EXP, LANE), F32)],
        compiler_params=_cparams(("arbitrary",)),
        name="attn_out_router",
    )(o_p, o_s, xp, xs, mods, g1, mods, mods, g2, wout, wr_hi, wr_lo, tri)


def _invert_kernel(pos_ref, init_hbm, out_hbm, dest, sem):
    step = pl.program_id(0)

    @pl.when(step == 0)
    def _():
        fill = pltpu.make_async_copy(init_hbm, dest, sem)
        fill.start()
        fill.wait()

    base = step * INV_BLK

    def body(q, carry):
        dest[pos_ref[0, 0, q]] = base + q
        return carry

    lax.fori_loop(0, INV_BLK, body, 0, unroll=8)

    @pl.when(step == pl.num_programs(0) - 1)
    def _():
        out = pltpu.make_async_copy(dest, out_hbm, sem)
        out.start()
        out.wait()


def _invert_positions(pos):
    n = pos.shape[0]
    return pl.pallas_call(
        _invert_kernel,
        grid=(n // INV_BLK,),
        in_specs=[pl.BlockSpec((1, 1, INV_BLK), lambda i: (i, 0, 0), memory_space=pltpu.SMEM),
                  pl.BlockSpec(memory_space=pl.ANY)],
        out_specs=pl.BlockSpec(memory_space=pl.ANY),
        out_shape=jax.ShapeDtypeStruct((P_ROWS,), I32),
        scratch_shapes=[pltpu.SMEM((P_ROWS,), I32), pltpu.SemaphoreType.DMA],
        compiler_params=_cparams(("arbitrary",)),
        name="invert_positions",
    )(pos.reshape(n // INV_BLK, 1, INV_BLK), jnp.full((P_ROWS,), -1, I32))


def _tile_rows(r):
    return r * ROW_TILE if isinstance(r, int) else pl.multiple_of(r * ROW_TILE, ROW_TILE)


def _moe_kernel(te, nu, src0_ref, src1_ref, dstp_ref, dstc_ref, h_hbm, wgu, wd, y_hbm,
                hsbuf, ybuf, hbf, acc, sem_g, sem_s):
    i = pl.program_id(0)
    last = pl.num_programs(0) - 1
    cur = i % 2
    nxt = 1 - cur
    buf_rows = TM_E * ROW_TILE

    def gather(idx_ref, r, slot):
        i_src = pl.multiple_of(idx_ref[0, 0, r] * ROW_TILE, ROW_TILE)
        return pltpu.make_async_copy(h_hbm.at[pl.ds(i_src, ROW_TILE), :],
                                     hsbuf.at[slot, pl.ds(_tile_rows(r), ROW_TILE), :], sem_g.at[slot])

    def scatter(idx_ref, r, slot):
        i_dst = pl.multiple_of(idx_ref[0, 0, r] * ROW_TILE, ROW_TILE)
        return pltpu.make_async_copy(ybuf.at[slot, pl.ds(_tile_rows(r), ROW_TILE), :],
                                     y_hbm.at[pl.ds(i_dst, ROW_TILE), :], sem_s.at[slot])

    def wait_gather(slot):
        pltpu.make_async_copy(h_hbm.at[pl.ds(0, buf_rows), :], hsbuf.at[slot], sem_g.at[slot]).wait()

    def wait_scatter(slot):
        pltpu.make_async_copy(ybuf.at[slot], y_hbm.at[pl.ds(0, buf_rows), :], sem_s.at[slot]).wait()

    def for_rows(fn):
        def body(g, carry):
            for k in range(ROW_TILE):
                fn(g * ROW_TILE + k, k % 2)
            return carry
        lax.fori_loop(0, TM_E // ROW_TILE, body, 0)

    def side_traffic(r, queue):
        gather(src1_ref, r, nxt).start(priority=queue)
        scatter(dstp_ref, r, nxt).start(priority=queue)

    @pl.when(i == 0)
    def _():
        ybuf[1] = jnp.zeros((buf_rows, LANE), F32)
        for_rows(lambda r, queue: gather(src0_ref, r, 0).start(priority=queue))

    wait_gather(cur)

    @pl.when(i < nu[0])
    def _():
        for c in range(ROW_TILE):
            hbf[:, c * LANE:(c + 1) * LANE] = hsbuf[cur, pl.ds(c, TM_E, stride=ROW_TILE), :].astype(BF16)
        w_gu = wgu.at[0]
        w_d = wd.at[0]
        _swiglu_block(hbf, w_gu, w_d, acc, 0)
        per_iter = TM_E // SIDE_ITERS

        def body(t, carry):
            _swiglu_block(hbf, w_gu, w_d, acc, 1 + 2 * t)
            _swiglu_block(hbf, w_gu, w_d, acc, 2 + 2 * t)
            for k in range(per_iter):
                side_traffic(t * per_iter + k, k % 2)
            return carry

        lax.fori_loop(0, SIDE_ITERS, body, 0)
        for j in range(1 + 2 * SIDE_ITERS, N_FF):
            _swiglu_block(hbf, w_gu, w_d, acc, j)
        _store_row_tiles(ybuf.at[cur], lambda cols: acc[:, cols])

    @pl.when(i >= nu[0])
    def _():
        for_rows(side_traffic)
        ybuf[cur] = jnp.zeros((buf_rows, LANE), F32)

    wait_scatter(nxt)

    @pl.when(i == last)
    def _():
        for_rows(lambda r, queue: scatter(dstc_ref, r, cur).start(priority=queue))
        wait_scatter(cur)
        wait_gather(nxt)


def _moe_experts(tile_expert, n_used, src_tbl, dst_tbl, h, wgu, wd):
    smem = lambda off: pl.BlockSpec((1, 1, TM_E), lambda i, te, nu: (i + off, 0, 0), memory_space=pltpu.SMEM)
    grid_spec = pltpu.PrefetchScalarGridSpec(
        num_scalar_prefetch=2,
        grid=(N_TILES,),
        in_specs=[smem(0), smem(1), smem(0), smem(1),
                  pl.BlockSpec(memory_space=pl.ANY),
                  pl.BlockSpec((1,) + wgu.shape[1:], lambda i, te, nu: (te[i], 0, 0)),
                  pl.BlockSpec((1,) + wd.shape[1:], lambda i, te, nu: (te[i], 0, 0))],
        out_specs=pl.BlockSpec(memory_space=pl.ANY),
        scratch_shapes=[pltpu.VMEM((2, TM_E * ROW_TILE, LANE), F32), pltpu.VMEM((2, TM_E * ROW_TILE, LANE), F32),
                        pltpu.VMEM((TM_E, D), BF16), pltpu.VMEM((TM_E, D), F32),
                        pltpu.SemaphoreType.DMA((2,)), pltpu.SemaphoreType.DMA((2,))],
    )
    return pl.pallas_call(
        _moe_kernel,
        grid_spec=grid_spec,
        out_shape=jax.ShapeDtypeStruct(((2 * N + DUMP_ROWS) * ROW_TILE, LANE), F32),
        compiler_params=_cparams(("arbitrary",)),
        name="moe_experts",
    )(tile_expert, n_used, src_tbl, src_tbl, dst_tbl, dst_tbl, h, wgu, wd)


def _combine_kernel(y1_ref, y2_ref, x_ref, w1_ref, w2_ref, gate2, g3, op_ref, os_ref, *, prompt_steps):
    rows = x_ref.shape[0]
    w1 = w1_ref[...]
    w2 = w2_ref[...]
    f = jnp.concatenate([w1 * _load_row_tiles(y1_ref, c, rows) + w2 * _load_row_tiles(y2_ref, c, rows)
                         for c in range(ROW_TILE)], axis=1)
    out = _gated(x_ref[...], f, g3[...], gate2[0])
    step = pl.program_id(0)

    @pl.when(step < prompt_steps)
    def _():
        op_ref[...] = out

    @pl.when(step >= prompt_steps)
    def _():
        os_ref[...] = out


def _moe_combine(y, x, w1, w2, mods, g3):
    tm = 512
    grp = _grp_bm(tm)
    nb = N // tm
    n_p = NP // tm
    return pl.pallas_call(
        functools.partial(_combine_kernel, prompt_steps=n_p),
        grid=(nb,),
        in_specs=[pl.BlockSpec((tm * ROW_TILE, LANE), lambda i: (i, 0)),
                  pl.BlockSpec((tm * ROW_TILE, LANE), lambda i: (nb + i, 0)),
                  pl.BlockSpec((tm, D), lambda i: (i, 0)),
                  pl.BlockSpec((tm, 1), lambda i: (i, 0)), pl.BlockSpec((tm, 1), lambda i: (i, 0)),
                  _mod_spec(5, grp), _full((1, D))],
        out_specs=[pl.BlockSpec((tm, D), lambda i: (jnp.minimum(i, n_p - 1), 0)),
                   pl.BlockSpec((tm, D), lambda i: (jnp.maximum(i - n_p, 0), 0))],
        out_shape=[jax.ShapeDtypeStruct((NP, D), F32), jax.ShapeDtypeStruct((NS, D), F32)],
        compiler_params=_cparams(("arbitrary",)),
        name="moe_combine",
    )(y, y, x, w1, w2, mods, g3)


def _block_diag(w):
    hh, a, b = w.shape
    eye = jnp.eye(hh, dtype=w.dtype)
    return jnp.einsum('hab,hk->hakb', w, eye).reshape(hh * a, hh * b)


def _s5_matrices(a_re, a_im, log_dt, b_re, b_im, c_re, c_im):
    dt = jnp.exp(log_dt)[:, None]
    mag = jnp.exp(a_re * dt)
    abr = mag * jnp.cos(a_im * dt)
    abi = mag * jnp.sin(a_im * dt)
    den = a_re * a_re + a_im * a_im
    cr = ((abr - 1.0) * a_re + abi * a_im) / den
    ci = (abi * a_re - (abr - 1.0) * a_im) / den
    bbr = cr[..., None] * b_re - ci[..., None] * b_im
    bbi = cr[..., None] * b_im + ci[..., None] * b_re
    hg = S5_G // 2
    eye = jnp.eye(hg, dtype=F32)
    bms, cms = [], []
    for j in range(2):
        sl = slice(j * hg, (j + 1) * hg)
        bd = lambda m: jnp.einsum('gpc,gh->gchp', m[sl], eye).reshape(hg * S5_CH, hg * S5_P)
        bms.append(jnp.concatenate([bd(bbr), bd(bbi)], axis=1))
        cd = lambda m: jnp.einsum('gcp,gh->gphc', m[sl], eye).reshape(hg * S5_P, hg * S5_CH)
        cms.append(jnp.concatenate([cd(c_re), cd(-c_im)], axis=0))
    return (jnp.stack(bms).astype(BF16), abr.reshape(1, S5_N), abi.reshape(1, S5_N),
            jnp.stack(cms).astype(BF16))


def _rope_tables(tm):
    rows = DEC_SEQ // GRID_W
    row = jnp.repeat(jnp.arange(rows, dtype=F32), GRID_W)
    col = jnp.tile(jnp.arange(GRID_W, dtype=F32), rows)
    nf = QK_ROPE // 4
    inv = ROPE_THETA ** (-jnp.arange(nf, dtype=F32) / nf)
    ang = jnp.concatenate([row[:, None] * inv, col[:, None] * inv], axis=-1)
    cos = jnp.repeat(jnp.cos(ang), 2, axis=-1)
    sin = jnp.stack([-jnp.sin(ang), jnp.sin(ang)], axis=-1).reshape(DEC_SEQ, QK_ROPE)
    ident = lambda t, one: jnp.concatenate([t, jnp.full((tm, t.shape[1]), one, F32)], axis=0)
    cos_q = ident(jnp.tile(cos, (1, HEADS)), 1.0)
    sin_q = ident(jnp.tile(sin, (1, HEADS)), 0.0)
    pad = lambda t, one: jnp.concatenate([t, jnp.full((DEC_SEQ, LANE - QK_ROPE), one, F32)], axis=1)
    return cos_q, sin_q, ident(pad(cos, 1.0), 1.0), ident(pad(sin, 0.0), 0.0)


def _group_states(prompt_state, sample_state):
    w = sample_state.shape[-1]
    return jnp.concatenate([prompt_state.reshape(2, SUB, w), sample_state.reshape(1, SUB, w)], axis=0)


def _layer_ab(xp, xs, m, ng, j, state_lru, state_s5_re, state_s5_im, p):
    xz = _ab_inproj(xp, xs, ng[0:1], m, p['ab_w_in'][j].astype(BF16))
    zeros = lambda w: jnp.zeros((BATCH, w), F32)
    per_dir = []
    for d in range(2):
        wg = jnp.concatenate([_block_diag(p['lru_wa'][j, d]), _block_diag(p['lru_wx'][j, d])], axis=1).astype(BF16)
        bg = jnp.concatenate([p['lru_ba'][j, d], p['lru_bx'][j, d]])[None]
        bm, ar, ai, cm = _s5_matrices(p['s5_a_re'][j, d], p['s5_a_im'][j, d], p['s5_log_dt'][j, d],
                                      p['s5_b_re'][j, d], p['s5_b_im'][j, d], p['s5_c_re'][j, d], p['s5_c_im'][j, d])
        h0l = _group_states(zeros(LRU_W), state_lru[:, j, d])
        h0r = _group_states(zeros(S5_N), state_s5_re[:, j, d].reshape(DEC_BATCH, S5_N))
        h0i = _group_states(zeros(S5_N), state_s5_im[:, j, d].reshape(DEC_BATCH, S5_N))
        per_dir.append((h0l, h0r, h0i, p['ab_conv_w'][j], p['ab_conv_b'][j][None], wg, bg,
                        p['lru_lambda'][j, d][None], bm, ar, ai, cm))
    (haf, yf, llf, lrf, lif), (hab, yr, llb, lrb, lib) = _ab_scan(xz, per_dir)
    x = _ab_out(haf, hab, yf, yr, xz, xp, xs, m, ng[1:2], p['s5_d'][j][None], p['s5_w_glu'][j].astype(BF16),
                p['s5_b_glu'][j][None], p['ab_w_out'][j].astype(BF16))
    streams = _ffn(x, m, ng[2:3], ng[3:4], p['ffn_w_gate_up'][j].astype(BF16), p['ffn_w_down'][j].astype(BF16))
    prompt = lambda f, b, w: jnp.stack([f[:2].reshape(BATCH, w), b[:2].reshape(BATCH, w)], axis=1)
    lru = prompt(llf, llb, LRU_W)
    s5r = prompt(lrf, lrb, S5_N).reshape(BATCH, 2, S5_G, S5_P)
    s5i = prompt(lif, lib, S5_N).reshape(BATCH, 2, S5_G, S5_P)
    return tuple(streams), lru, s5r, s5i


def _head_major(w, parts):
    k = w.shape[0]
    per_head = w.reshape(k, HEADS, -1)
    out, start = [], 0
    for width in parts:
        out.append(per_head[:, :, start:start + width].reshape(k, HEADS * width))
        start += width
    return jnp.concatenate(out, axis=1)


def _layer_mla_moe(xp, xs, m, ng, j, cache_kv_latent, cache_k_rope, p):
    w1 = jnp.concatenate([p['mla_w_in'][j], jnp.zeros((D, LANE - QK_ROPE), F32)], axis=1).astype(BF16)
    wuq = _head_major(p['mla_w_uq'][j], (QK_NOPE, QK_ROPE)).astype(BF16)
    wukv = _head_major(p['mla_w_ukv'][j], (QK_NOPE, V_DIM)).astype(BF16)
    tables = _rope_tables(MLA_TM)
    qn, qr, kn, v, kr2, ckv, krr = _mla_proj(xp, xs, ng[0:1], m, w1, p['mla_g_q'][j][None], p['mla_g_kv'][j][None],
                                              wuq, wukv, *tables)
    knc, vc = _cache_kv(cache_kv_latent[:, j].reshape(DEC_BATCH * PAST_LEN, KV_LORA), wukv)
    krc = cache_k_rope[:, j].reshape(DEC_BATCH * PAST_LEN, QK_ROPE)
    z = jnp.zeros_like(krc)
    kr2c = jnp.concatenate([krc, z, z, krc], axis=1).astype(BF16)
    o_p = _attention(qn, qr, kn, kr2, v, row0=0, n_seq=BATCH, seq=SEQ, tq=SEQ)
    o_s = _attention(qn, qr, kn, kr2, v, row0=NP, n_seq=DEC_BATCH, seq=DEC_SEQ, tq=1024, cache=(knc, kr2c, vc))
    wr_t = p['moe_w_router'][j].T
    wr_hi = wr_t.astype(BF16)
    wr_lo = (wr_t - wr_hi.astype(F32)).astype(BF16)
    x3, h, ri, rf, cnt = _attn_out_router(o_p, o_s, xp, xs, m, ng[1:2], ng[2:3], p['mla_w_out'][j].astype(BF16), wr_hi, wr_lo)
    counts = cnt[:, 0].astype(I32)
    padded = ((counts + TM_E - 1) // TM_E) * TM_E
    ends = jnp.cumsum(padded)
    offs = ends - padded
    pos1 = offs[ri[0]] + ri[2]
    pos2 = offs[ri[1]] + ri[3]
    dest = _invert_positions(jnp.concatenate([pos1, pos2]))
    is_pad = dest < 0
    pad_row = 2 * N + TM_E + jnp.cumsum(is_pad.astype(I32)) - 1
    src_tbl = jnp.concatenate([jnp.where(is_pad, 0, dest % N), jnp.zeros((TM_E,), I32)])
    dst_tbl = jnp.concatenate([2 * N + jnp.arange(TM_E, dtype=I32), jnp.where(is_pad, pad_row, dest)])
    n_used = (ends[-1] // TM_E).astype(I32)[None]
    tile_row = jnp.minimum(jnp.arange(N_TILES, dtype=I32), n_used - 1) * TM_E
    tile_expert = jnp.sum((tile_row[:, None] >= ends[None, :]).astype(I32), axis=1)
    y = _moe_experts(tile_expert, n_used, src_tbl.reshape(N_TILES + 1, 1, TM_E), dst_tbl.reshape(N_TILES + 1, 1, TM_E),
                     h, p['moe_w_gate_up'][j].astype(BF16), p['moe_w_down'][j].astype(BF16))
    xp, xs = _moe_combine(y, x3, rf[0][:, None], rf[1][:, None], m, ng[3:4])
    kv_new = ckv.reshape(BATCH, SEQ, KV_LORA)
    kr_new = krr[:, :QK_ROPE].reshape(BATCH, SEQ, QK_ROPE)
    return (xp.reshape(BATCH, SEQ, D), xs.reshape(DEC_BATCH, DEC_SEQ, D)), kv_new, kr_new


def kernel(x_prompt, x_sample, c, state_lru, state_s5_re, state_s5_im, cache_kv_latent, cache_k_rope, c_ctx, w_mod, b_mod, norm_gains, ab_w_in, ab_conv_w, ab_conv_b, lru_wa, lru_ba, lru_wx, lru_bx, lru_lambda, s5_a_re, s5_a_im, s5_log_dt, s5_b_re, s5_b_im, s5_c_re, s5_c_im, s5_d, s5_w_glu, s5_b_glu, ab_w_out, ffn_w_gate_up, ffn_w_down, mla_w_in, mla_g_q, mla_g_kv, mla_w_uq, mla_w_ukv, mla_w_out, moe_w_router, moe_w_gate_up, moe_w_down):
    p = dict(ab_w_in=ab_w_in, ab_conv_w=ab_conv_w, ab_conv_b=ab_conv_b, lru_wa=lru_wa, lru_ba=lru_ba,
             lru_wx=lru_wx, lru_bx=lru_bx, lru_lambda=lru_lambda, s5_a_re=s5_a_re, s5_a_im=s5_a_im,
             s5_log_dt=s5_log_dt, s5_b_re=s5_b_re, s5_b_im=s5_b_im, s5_c_re=s5_c_re, s5_c_im=s5_c_im,
             s5_d=s5_d, s5_w_glu=s5_w_glu, s5_b_glu=s5_b_glu, ab_w_out=ab_w_out, ffn_w_gate_up=ffn_w_gate_up,
             ffn_w_down=ffn_w_down, mla_w_in=mla_w_in, mla_g_q=mla_g_q, mla_g_kv=mla_g_kv, mla_w_uq=mla_w_uq,
             mla_w_ukv=mla_w_ukv, mla_w_out=mla_w_out, moe_w_router=moe_w_router, moe_w_gate_up=moe_w_gate_up,
             moe_w_down=moe_w_down)
    depth = w_mod.shape[0]
    cond = jnp.concatenate([c_ctx[None], c, jnp.zeros((2 * SUB - 1 - DEC_BATCH, D), F32)], axis=0)
    mod = _modulation(cond, w_mod, b_mod)
    ctx_tile = lambda l: jnp.broadcast_to(mod[l, 0:1], (SUB, 6 * D))
    streams = (x_prompt, x_sample)
    lru_l, s5r_l, s5i_l, kv_l, kr_l = [], [], [], [], []
    for layer in range(depth):
        j = layer // 2
        ng = norm_gains[layer]
        if layer % 2 == 0:
            m = jnp.stack([ctx_tile(layer), mod[layer, 1:1 + DEC_BATCH]])
            streams, lru, s5r, s5i = _layer_ab(*streams, m, ng, j, state_lru, state_s5_re, state_s5_im, p)
            lru_l.append(lru)
            s5r_l.append(s5r)
            s5i_l.append(s5i)
        else:
            lat = jnp.broadcast_to(mod[layer, 1:1 + DEC_BATCH, None, :], (DEC_BATCH, SUB, 6 * D))
            m = jnp.concatenate([ctx_tile(layer)[None], lat], axis=0)
            streams, kv_new, kr_new = _layer_mla_moe(streams[0].reshape(NP, D), streams[1].reshape(NS, D), m, ng, j,
                                                     cache_kv_latent, cache_k_rope, p)
            kv_l.append(kv_new)
            kr_l.append(kr_new)
    return (streams[0], streams[1],
            jnp.stack(lru_l, axis=1), jnp.stack(s5r_l, axis=1), jnp.stack(s5i_l, axis=1),
            jnp.stack(kv_l, axis=1), jnp.stack(kr_l, axis=1))
```

```python
import functools
import math

import numpy as np
import jax
import jax.numpy as jnp
from jax import lax
from jax.experimental import pallas as pl
from jax.experimental.pallas import tpu as pltpu

F32 = jnp.float32
BF16 = jnp.bfloat16
I32 = jnp.int32

D = 1024
BATCH, SEQ = 16, 256
DEC_BATCH, DEC_SEQ = 8, 2048
PAST_LEN = 256
GRID_W = 64
LRU_W = 512
LRU_HEADS = 8
LRU_C = 8.0
CONV_W = 4
S5_W = 512
S5_CH = 16
S5_G = 32
S5_P = 64
S5_N = S5_G * S5_P
HEADS = 8
QK_NOPE, QK_ROPE, V_DIM = 128, 64, 128
Q_LORA, KV_LORA = 384, 256
ROPE_THETA = 10000.0
D_FF = 2816
N_EXP = 8
EPS = 1e-6

NP = BATCH * SEQ
NS = DEC_BATCH * DEC_SEQ
N = NP + NS
SUB = 8
LANE = 128
ROW_TILE = D // LANE
T_CHUNK = 64
R_CHUNK = T_CHUNK * SUB
FF_BLK = 256
KEY_BLK = 256
HEAD_GRP = 4
MLA_TM = 1024
MLA_SUB = 4
ROUTER_SUB = 1
ABOUT_SUB = 1
ABOUT_T = 64
INPROJ_SUB = 2
INPROJ_T = 64
FFN_T = 128
SIDE_ITERS = 4
INV_BLK = 4096
N_FF = D_FF // FF_BLK
TM_E = 512
P_ROWS = 2 * N + N_EXP * TM_E
N_TILES = P_ROWS // TM_E
DUMP_ROWS = P_ROWS - 2 * N + TM_E
VMEM_LIMIT = 56 * 1024 * 1024

NT_DIMS = (((1,), (1,)), ((), ()))


def _cparams(sem):
    return pltpu.CompilerParams(dimension_semantics=sem, vmem_limit_bytes=VMEM_LIMIT)


def _dot(a, b):
    return jnp.dot(a, b, preferred_element_type=F32)


def _sigmoid(x):
    return 1.0 / (1.0 + jnp.exp(-x))


def _neg_expm1_2x(log_a, a):
    series = -2.0 * log_a * (1.0 + log_a * (1.0 + log_a * (2.0 / 3.0) * (1.0 + log_a * 0.5)))
    return jnp.where(log_a > -0.01, series, (1.0 - a) * (1.0 + a))


def _sqrt_nonneg(v):
    return jnp.where(v > 0.0, v * lax.rsqrt(v), 0.0)


def _gelu(x):
    k = math.sqrt(2.0 / math.pi)
    half = 0.5 * x
    return half + half * jnp.tanh(x * (k + (k * 0.044715) * (x * x)))


def _rms(x, g):
    ms = jnp.mean(x * x, axis=-1, keepdims=True)
    return x * lax.rsqrt(ms + EPS) * g


def _rows8(y, fn):
    r, c = y.shape
    return fn(y.reshape(r // SUB, SUB, c)).reshape(r, c)


def _adaln(x, g, scale, shift):
    return _rows8(_rms(x, g), lambda y: y * (1.0 + scale)[None] + shift[None])


def _gated(x, y, g, gate):
    return x + _rows8(_rms(y, g), lambda z: z * gate[None])


def _store_row_tiles(ref, piece, row0=0, rows=None):
    rows = ref.shape[0] // ROW_TILE if rows is None else rows
    for c in range(ROW_TILE):
        ref[pl.ds(row0 * ROW_TILE + c, rows, stride=ROW_TILE), :] = piece(slice(c * LANE, (c + 1) * LANE))


def _load_row_tiles(ref, c, rows):
    return ref[pl.ds(c, rows, stride=ROW_TILE), :]


def _full(shape):
    nd = len(shape)
    return pl.BlockSpec(shape, lambda *_: (0,) * nd)


def _mod_spec(k, grp):
    return pl.BlockSpec((1, SUB, D), lambda i, *_: (grp(i), 0, k))


def _grp_tm(tm):
    return lambda i: (i * tm >= NP).astype(I32)


def _grp_bm(tm):
    return lambda i: jnp.where(i * tm < NP, 0, 1 + (i * tm - NP) // DEC_SEQ)


def _mod_kernel(c_ref, w_ref, b_ref, o_ref):
    c = c_ref[...]
    s = c * _sigmoid(c)
    o_ref[0] = _dot(s.astype(BF16), w_ref[0].astype(BF16)) + b_ref[0]


def _modulation(cond, w_mod, b_mod):
    depth = w_mod.shape[0]
    rows = cond.shape[0]
    return pl.pallas_call(
        _mod_kernel,
        grid=(depth, 6),
        in_specs=[_full((rows, D)),
                  pl.BlockSpec((1, D, D), lambda l, j: (l, 0, j)),
                  pl.BlockSpec((1, 1, D), lambda l, j: (l, 0, j))],
        out_specs=pl.BlockSpec((1, rows, D), lambda l, j: (l, 0, j)),
        out_shape=jax.ShapeDtypeStruct((depth, rows, 6 * D), F32),
        compiler_params=_cparams(("arbitrary", "arbitrary")),
        name="modulation",
    )(cond, w_mod, b_mod.reshape(depth, 1, 6 * D))


def _time_major_copies(step, t_steps, xp_hbm, xs_hbm, buf, sem, to_hbm):
    p_steps = (BATCH // SUB) * (SEQ // t_steps)
    per_group = SEQ // t_steps

    def issue(hbm, seq0, t0):
        t0 = pl.multiple_of(t0, t_steps)
        for b in range(SUB):
            rows = hbm.at[seq0 + b, pl.ds(t0, t_steps), :]
            tile = buf.at[:, b, :]
            (pltpu.make_async_copy(tile, rows, sem) if to_hbm else pltpu.make_async_copy(rows, tile, sem)).start()

    @pl.when(step < p_steps)
    def _():
        issue(xp_hbm, (step // per_group) * SUB, (step % per_group) * t_steps)

    @pl.when(step >= p_steps)
    def _():
        issue(xs_hbm, 0, (step - p_steps) * t_steps)


def _time_major_wait(t_steps, xs_hbm, buf, sem, to_hbm):
    for b in range(SUB):
        rows = xs_hbm.at[0, pl.ds(0, t_steps), :]
        tile = buf.at[:, b, :]
        (pltpu.make_async_copy(tile, rows, sem) if to_hbm else pltpu.make_async_copy(rows, tile, sem)).wait()


def _time_major_read(t_steps, xp_hbm, xs_hbm, buf, sem):
    step = pl.program_id(0)
    slot = step % 2

    @pl.when(step == 0)
    def _():
        _time_major_copies(step, t_steps, xp_hbm, xs_hbm, buf.at[0], sem.at[0], False)

    @pl.when(step + 1 < pl.num_programs(0))
    def _():
        _time_major_copies(step + 1, t_steps, xp_hbm, xs_hbm, buf.at[1 - slot], sem.at[1 - slot], False)

    _time_major_wait(t_steps, xs_hbm, buf.at[slot], sem.at[slot], False)
    return slot


def _inproj_kernel(xp_hbm, xs_hbm, g_ref, sh_ref, sc_ref, w_ref, o_ref, buf, sem):
    slot = _time_major_read(INPROJ_T, xp_hbm, xs_hbm, buf, sem)
    sub_t = INPROJ_T // INPROJ_SUB
    for r in range(INPROJ_SUB):
        rows = slice(r * sub_t * SUB, (r + 1) * sub_t * SUB)
        x = buf[slot, r * sub_t:(r + 1) * sub_t].reshape(sub_t * SUB, D)
        h = _adaln(x, g_ref[...], sc_ref[0], sh_ref[0])
        o_ref[rows, :] = _dot(h.astype(BF16), w_ref[...])


def _ab_inproj(xp, xs, gain, mods, w_in):
    tm = INPROJ_T * SUB
    nout = w_in.shape[1]
    grp = _grp_tm(tm)
    return pl.pallas_call(
        _inproj_kernel,
        grid=(N // tm,),
        in_specs=[pl.BlockSpec(memory_space=pl.ANY), pl.BlockSpec(memory_space=pl.ANY),
                  _full((1, D)),
                  _mod_spec(0, grp), _mod_spec(1, grp),
                  _full((D, nout))],
        out_specs=pl.BlockSpec((tm, nout), lambda i: (i, 0)),
        out_shape=jax.ShapeDtypeStruct((N, nout), F32),
        scratch_shapes=[pltpu.VMEM((2, INPROJ_T, SUB, D), F32), pltpu.SemaphoreType.DMA((2,))],
        compiler_params=_cparams(("arbitrary",)),
        name="ab_inproj",
    )(xp, xs, gain, mods, mods, w_in)


def _scan_table(reverse):
    cols = []
    groups = [(0, SEQ // T_CHUNK, 0), (1, SEQ // T_CHUNK, SEQ // T_CHUNK),
              (2, DEC_SEQ // T_CHUNK, NP // R_CHUNK)]
    for g, nc, base in groups:
        order = range(nc - 1, -1, -1) if reverse else range(nc)
        for k, c in enumerate(order):
            cols.append((base + c, g, int(k == 0), int(c > 0), int(c < nc - 1)))
    return np.asarray(cols, np.int32).T.copy()


N_SCAN_IN, N_SCAN_OUT, N_SCAN_SCRATCH = 16, 5, 8
SCAN_TBL_ROWS = 5


def _scan_kernel(tbl, *refs):
    ins = [refs[d * N_SCAN_IN:(d + 1) * N_SCAN_IN] for d in range(2)]
    o0 = 2 * N_SCAN_IN
    outs = [refs[o0 + d * N_SCAN_OUT:o0 + (d + 1) * N_SCAN_OUT] for d in range(2)]
    s0 = o0 + 2 * N_SCAN_OUT
    scr = [refs[s0 + d * N_SCAN_SCRATCH:s0 + (d + 1) * N_SCAN_SCRATCH] for d in range(2)]
    s = pl.program_id(0)
    for d in range(2):
        h0l_ref, h0r_ref, h0i_ref = ins[d][4:7]
        hl, sre, sim = scr[d][5:8]

        @pl.when(tbl[d * SCAN_TBL_ROWS + 2, s] == 1)
        def _(h0l_ref=h0l_ref, h0r_ref=h0r_ref, h0i_ref=h0i_ref, hl=hl, sre=sre, sim=sim):
            hl[...] = h0l_ref[0]
            sre[...] = h0r_ref[0]
            sim[...] = h0i_ref[0]

    fwd, bwd = [_ScanChunk(tbl, d * SCAN_TBL_ROWS, ins[d], outs[d], scr[d], reverse=(d == 1)) for d in range(2)]
    fwd.s5_project()
    fwd.conv()
    fwd.gates()
    bwd.s5_project()
    fwd.s5_recurrence()
    bwd.conv()
    fwd.s5_readout()
    bwd.gates()
    fwd.lru_coefficients()
    fwd.lru_recurrence()
    bwd.s5_recurrence()
    bwd.s5_readout()
    bwd.lru_coefficients()
    bwd.lru_recurrence()


class _ScanChunk:
    def __init__(self, tbl, row0, ins, outs, scratch, reverse):
        (self.xa_ref, self.xp_ref, self.xn_ref, self.xb_ref, _, _, _, self.cw_ref, self.cb_ref, self.wg_ref,
         self.bg_ref, self.lam_ref, self.bm_ref, self.ar_ref, self.ai_ref, self.cm_ref) = ins
        self.ha_ref, self.y_ref, self.ll_ref, self.lr_ref, self.li_ref = outs
        self.ext, self.abuf, self.bbuf, self.hre, self.him, self.hl, self.sre, self.sim = scratch
        step = pl.program_id(0)
        self.has_prev = tbl[row0 + 3, step] == 1
        self.has_next = tbl[row0 + 4, step] == 1
        self.order = range(T_CHUNK - 1, -1, -1) if reverse else range(T_CHUNK)
        self.half = S5_N // 2

    def s5_project(self):
        ub = self.xb_ref[...].astype(BF16)
        half = self.half
        for j in range(2):
            bu = _dot(ub[:, j * 256:(j + 1) * 256], self.bm_ref[j])
            self.hre[:, j * half:(j + 1) * half] = bu[:, :half]
            self.him[:, j * half:(j + 1) * half] = bu[:, half:]

    def conv(self):
        ext = self.ext
        ext[0:2 * SUB] = jnp.where(self.has_prev, self.xp_ref[...], 0.0)
        ext[2 * SUB:2 * SUB + R_CHUNK] = self.xa_ref[...]
        ext[2 * SUB + R_CHUNK:3 * SUB + R_CHUNK] = jnp.where(self.has_next, self.xn_ref[...], 0.0)
        xa = self.cb_ref[...] + self.cw_ref[0:1] * ext[0:R_CHUNK]
        for k in range(1, CONV_W):
            xa = xa + self.cw_ref[k:k + 1] * ext[k * SUB:k * SUB + R_CHUNK]
        self.xa = xa

    def gates(self):
        self.gz = _dot(self.xa.astype(BF16), self.wg_ref[...]) + self.bg_ref[...]

    def s5_recurrence(self):
        cblk = 4 * LANE
        for cb in range(S5_N // cblk):
            cols = slice(cb * cblk, (cb + 1) * cblk)
            ar = jnp.broadcast_to(self.ar_ref[:, cols], (SUB, cblk))
            ai = jnp.broadcast_to(self.ai_ref[:, cols], (SUB, cblk))
            hr = self.sre[:, cols]
            hi = self.sim[:, cols]
            for t in self.order:
                rows = slice(t * SUB, (t + 1) * SUB)
                nr = ar * hr - ai * hi + self.hre[rows, cols]
                ni = ar * hi + ai * hr + self.him[rows, cols]
                hr, hi = nr, ni
                self.hre[rows, cols] = hr
                self.him[rows, cols] = hi
            self.sre[:, cols] = hr
            self.sim[:, cols] = hi
        self.lr_ref[0] = self.sre[...]
        self.li_ref[0] = self.sim[...]

    def s5_readout(self):
        half = self.half
        for j in range(2):
            hc = jnp.concatenate([self.hre[:, j * half:(j + 1) * half], self.him[:, j * half:(j + 1) * half]],
                                 axis=1).astype(BF16)
            self.y_ref[:, j * 256:(j + 1) * 256] = _dot(hc, self.cm_ref[j])

    def lru_coefficients(self):
        r = 0.5 * jnp.tanh(0.5 * self.gz[:, :LRU_W]) + 0.5
        i = 0.5 * jnp.tanh(0.5 * self.gz[:, LRU_W:]) + 0.5
        lam = self.lam_ref[...]
        log_sig = jnp.minimum(lam, 0.0) - jnp.log1p(jnp.exp(-jnp.abs(lam)))
        log_a = LRU_C * r * log_sig
        a = jnp.exp(log_a)
        self.abuf[...] = a
        self.bbuf[...] = _sqrt_nonneg(_neg_expm1_2x(log_a, a)) * (i * self.xa)

    def lru_recurrence(self):
        h = self.hl[...]
        for t in self.order:
            rows = slice(t * SUB, (t + 1) * SUB)
            h = self.abuf[rows] * h + self.bbuf[rows]
            self.ha_ref[rows, :] = h
        self.hl[...] = h
        self.ll_ref[0] = h


def _ab_scan(xz, per_dir):
    tbl = jnp.asarray(np.concatenate([_scan_table(False), _scan_table(True)], axis=0))
    n_steps = tbl.shape[1]
    in_specs, out_specs, scratch, out_shape, args = [], [], [], [], []
    for d in range(2):
        blk = lambda s, t, d=d: t[d * SCAN_TBL_ROWS, s]
        grp = lambda s, t, d=d: t[d * SCAN_TBL_ROWS + 1, s]
        state_spec = lambda w, grp=grp: pl.BlockSpec((1, SUB, w), lambda s, t: (grp(s, t), 0, 0))
        const = lambda shape: pl.BlockSpec(shape, lambda s, t: (0,) * len(shape))
        in_specs += [
            pl.BlockSpec((R_CHUNK, LRU_W), lambda s, t, blk=blk: (blk(s, t), 0)),
            pl.BlockSpec((2 * SUB, LRU_W),
                         lambda s, t, blk=blk: (jnp.maximum(blk(s, t) * (T_CHUNK // 2) - 1, 0), 0)),
            pl.BlockSpec((SUB, LRU_W),
                         lambda s, t, blk=blk: (jnp.minimum((blk(s, t) + 1) * T_CHUNK, N // SUB - 1), 0)),
            pl.BlockSpec((R_CHUNK, S5_W), lambda s, t, blk=blk: (blk(s, t), 2)),
            state_spec(LRU_W), state_spec(S5_N), state_spec(S5_N),
            const((CONV_W, LRU_W)), const((1, LRU_W)),
            const((LRU_W, 2 * LRU_W)), const((1, 2 * LRU_W)), const((1, LRU_W)),
            const((2, 256, S5_N)), const((1, S5_N)), const((1, S5_N)), const((2, S5_N, 256)),
        ]
        out_specs += [
            pl.BlockSpec((R_CHUNK, LRU_W), lambda s, t, blk=blk: (blk(s, t), 0)),
            pl.BlockSpec((R_CHUNK, S5_W), lambda s, t, blk=blk: (blk(s, t), 0)),
            state_spec(LRU_W), state_spec(S5_N), state_spec(S5_N),
        ]
        scratch += [
            pltpu.VMEM((R_CHUNK + 3 * SUB, LRU_W), F32),
            pltpu.VMEM((R_CHUNK, LRU_W), F32), pltpu.VMEM((R_CHUNK, LRU_W), F32),
            pltpu.VMEM((R_CHUNK, S5_N), F32), pltpu.VMEM((R_CHUNK, S5_N), F32),
            pltpu.VMEM((SUB, LRU_W), F32), pltpu.VMEM((SUB, S5_N), F32), pltpu.VMEM((SUB, S5_N), F32),
        ]
        out_shape += [jax.ShapeDtypeStruct((N, LRU_W), F32), jax.ShapeDtypeStruct((N, S5_W), F32),
                      jax.ShapeDtypeStruct((3, SUB, LRU_W), F32),
                      jax.ShapeDtypeStruct((3, SUB, S5_N), F32), jax.ShapeDtypeStruct((3, SUB, S5_N), F32)]
        args += [xz, xz, xz, xz, *per_dir[d]]
    grid_spec = pltpu.PrefetchScalarGridSpec(num_scalar_prefetch=1, grid=(n_steps,), in_specs=in_specs,
                                             out_specs=out_specs, scratch_shapes=scratch)
    outs = pl.pallas_call(
        _scan_kernel,
        grid_spec=grid_spec,
        out_shape=out_shape,
        compiler_params=_cparams(("arbitrary",)),
        name="ab_scan",
    )(tbl, *args)
    return outs[:N_SCAN_OUT], outs[N_SCAN_OUT:]


def _about_kernel(haf, hab, yf, yr, ga, xb, xp_hbm, xs_hbm, gate, g1, d_ref, wglu, bglu, wout, o_ref, buf, sem):
    slot = _time_major_read(ABOUT_T, xp_hbm, xs_hbm, buf, sem)
    x_all = buf[slot].reshape(ABOUT_T * SUB, D)
    sub = ABOUT_T * SUB // ABOUT_SUB
    for r in range(ABOUT_SUB):
        rows = slice(r * sub, (r + 1) * sub)
        ya = (haf[rows, :] + hab[rows, :]) * _gelu(ga[rows, :])
        yb0 = _gelu(yf[rows, :] + yr[rows, :] + d_ref[...] * xb[rows, :])
        half_yb0 = 0.5 * yb0
        yb = half_yb0 + half_yb0 * jnp.tanh(0.5 * (_dot(yb0.astype(BF16), wglu[...]) + bglu[...]))
        out = _dot(ya.astype(BF16), wout[0:LRU_W]) + _dot(yb.astype(BF16), wout[LRU_W:LRU_W + S5_W])
        o_ref[rows, :] = _gated(x_all[rows], out, g1[...], gate[0])


def _ab_out(haf, hab, yf, yr, xz, xp, xs, mods, g1, s5_d, wglu, bglu, wout):
    tm = ABOUT_T * SUB
    grp = _grp_tm(tm)
    half = lambda c: pl.BlockSpec((tm, LRU_W), lambda i: (i, c))
    return pl.pallas_call(
        _about_kernel,
        grid=(N // tm,),
        in_specs=[half(0), half(0), half(0), half(0), half(1), half(2),
                  pl.BlockSpec(memory_space=pl.ANY), pl.BlockSpec(memory_space=pl.ANY),
                  _mod_spec(2, grp), _full((1, D)), _full((1, S5_W)),
                  _full((S5_W, S5_W)), _full((1, S5_W)), _full((LRU_W + S5_W, D))],
        out_specs=pl.BlockSpec((tm, D), lambda i: (i, 0)),
        out_shape=jax.ShapeDtypeStruct((N, D), F32),
        scratch_shapes=[pltpu.VMEM((2, ABOUT_T, SUB, D), F32), pltpu.SemaphoreType.DMA((2,))],
        compiler_params=_cparams(("arbitrary",)),
        name="ab_out",
    )(haf, hab, yf, yr, xz, xz, xp, xs, mods, g1, s5_d, wglu, bglu, wout)


def _swiglu_block(hbf, wgu, wd, acc, j):
    static = isinstance(j, int)
    blk = lambda start: pl.ds(start if static else pl.multiple_of(start, FF_BLK), FF_BLK)
    h = hbf[...]
    g = _dot(h, wgu[:, blk(j * FF_BLK)])
    u = _dot(h, wgu[:, blk(D_FF + j * FF_BLK)])
    act = (g * _sigmoid(g)) * u
    part = _dot(act.astype(BF16), wd[blk(j * FF_BLK), :])
    if static and j == 0:
        acc[...] = part
    else:
        acc[...] += part


def _ffn_kernel(x_ref, sh, sc, gt, g2, g3, wgu, wd, op_hbm, os_hbm, hbf, acc, obuf, sem):
    step = pl.program_id(0)
    slot = step % 2
    hbf[...] = _adaln(x_ref[...], g2[...], sc[0], sh[0]).astype(BF16)
    for j in range(N_FF):
        _swiglu_block(hbf, wgu, wd, acc, j)
    obuf[slot] = _gated(x_ref[...], acc[...], g3[...], gt[0]).reshape(FFN_T, SUB, D)
    _time_major_copies(step, FFN_T, op_hbm, os_hbm, obuf.at[slot], sem.at[slot], True)

    @pl.when(step > 0)
    def _():
        _time_major_wait(FFN_T, os_hbm, obuf.at[1 - slot], sem.at[1 - slot], True)

    @pl.when(step == pl.num_programs(0) - 1)
    def _():
        _time_major_wait(FFN_T, os_hbm, obuf.at[slot], sem.at[slot], True)


def _ffn(x, mods, g2, g3, wgu, wd):
    tm = FFN_T * SUB
    grp = _grp_tm(tm)
    once = lambda shape: pl.BlockSpec(shape, lambda i: (0,) * len(shape), pipeline_mode=pl.Buffered(1))
    return pl.pallas_call(
        _ffn_kernel,
        grid=(N // tm,),
        in_specs=[pl.BlockSpec((tm, D), lambda i: (i, 0)),
                  _mod_spec(3, grp), _mod_spec(4, grp), _mod_spec(5, grp),
                  _full((1, D)), _full((1, D)),
                  once(wgu.shape), once(wd.shape)],
        out_specs=[pl.BlockSpec(memory_space=pl.ANY), pl.BlockSpec(memory_space=pl.ANY)],
        out_shape=[jax.ShapeDtypeStruct((BATCH, SEQ, D), F32), jax.ShapeDtypeStruct((DEC_BATCH, DEC_SEQ, D), F32)],
        scratch_shapes=[pltpu.VMEM((tm, D), BF16), pltpu.VMEM((tm, D), F32),
                        pltpu.VMEM((2, FFN_T, SUB, D), F32), pltpu.SemaphoreType.DMA((2,))],
        compiler_params=_cparams(("arbitrary",)),
        name="ffn",
    )(x, mods, mods, mods, g2, g3, wgu, wd)


def _pair_swap(x):
    outs = []
    for c in range(x.shape[1] // LANE):
        xc = x[:, c * LANE:(c + 1) * LANE]
        even = lax.broadcasted_iota(I32, xc.shape, 1) % 2 == 0
        outs.append(jnp.where(even, pltpu.roll(xc, LANE - 1, 1), pltpu.roll(xc, 1, 1)))
    return outs[0] if len(outs) == 1 else jnp.concatenate(outs, axis=1)


def _stream_specs(tm, width):
    n_p = NP // tm
    return [pl.BlockSpec((tm, width), lambda i: (jnp.minimum(i, n_p - 1), 0)),
            pl.BlockSpec((tm, width), lambda i: (jnp.maximum(i - n_p, 0), 0))]


def _stream_rows(p_ref, s_ref, prompt_steps):
    return jnp.where(pl.program_id(0) < prompt_steps, p_ref[...], s_ref[...])


def _mlaproj_kernel(xp_ref, xs_ref, g0, sh, sc, w1, gq, gkv, wuq, wukv, cq_ref, sq_ref, ck_ref, sk_ref,
                    qn_ref, qr_ref, kn_ref, v_ref, kr2_ref, ckv_ref, krr_ref, *, prompt_steps):
    x = _stream_rows(xp_ref, xs_ref, prompt_steps)
    sub = x.shape[0] // MLA_SUB
    cache_rows = []
    for r in range(MLA_SUB):
        rows = slice(r * sub, (r + 1) * sub)
        h = _adaln(x[rows], g0[...], sc[0], sh[0])
        dn = _dot(h.astype(BF16), w1[...])
        cq = _rms(dn[:, :Q_LORA], gq[...])
        ckv = _rms(dn[:, Q_LORA:Q_LORA + KV_LORA], gkv[...])
        krp = dn[:, Q_LORA + KV_LORA:]
        cache_rows.append((rows, ckv, krp))
        q = _dot(cq.astype(BF16), wuq[...])
        qn_ref[rows, :] = q[:, :HEADS * QK_NOPE].astype(BF16)
        qr = q[:, HEADS * QK_NOPE:]
        qr_ref[rows, :] = (qr * cq_ref[rows, :] + _pair_swap(qr) * sq_ref[rows, :]).astype(BF16)
        kv = _dot(ckv.astype(BF16), wukv[...])
        kn_ref[rows, :] = kv[:, :HEADS * QK_NOPE].astype(BF16)
        v_ref[rows, :] = kv[:, HEADS * QK_NOPE:].astype(BF16)
        kr = krp * ck_ref[rows, :] + _pair_swap(krp) * sk_ref[rows, :]
        kr2_ref[rows, :] = jnp.concatenate([kr, pltpu.roll(kr, QK_ROPE, 1)], axis=1).astype(BF16)

    @pl.when(pl.program_id(0) < prompt_steps)
    def _():
        for rows, ckv, krp in cache_rows:
            ckv_ref[rows, :] = ckv
            krr_ref[rows, :] = krp


def _mla_proj(xp, xs, g0, mods, w1, gq, gkv, wuq, wukv, cos_q, sin_q, cos_k, sin_k):
    tm = MLA_TM
    grp = _grp_bm(tm)
    n_pos = DEC_SEQ // tm
    tab = lambda w: pl.BlockSpec((tm, w), lambda i: (jnp.where(i * tm < NP, n_pos, (i - NP // tm) % n_pos), 0))
    row = lambda w: pl.BlockSpec((tm, w), lambda i: (i, 0))
    shp = lambda w, dt: jax.ShapeDtypeStruct((N, w), dt)
    n_p = NP // tm
    prow = lambda w: pl.BlockSpec((tm, w), lambda i: (jnp.minimum(i, n_p - 1), 0))
    return pl.pallas_call(
        functools.partial(_mlaproj_kernel, prompt_steps=n_p),
        grid=(N // tm,),
        in_specs=_stream_specs(tm, D) + [_full((1, D)), _mod_spec(0, grp), _mod_spec(1, grp),
                  _full(w1.shape), _full((1, Q_LORA)), _full((1, KV_LORA)),
                  _full(wuq.shape), _full(wukv.shape),
                  tab(HEADS * QK_ROPE), tab(HEADS * QK_ROPE), tab(LANE), tab(LANE)],
        out_specs=[row(HEADS * QK_NOPE), row(HEADS * QK_ROPE), row(HEADS * QK_NOPE), row(HEADS * V_DIM),
                   row(2 * LANE), prow(KV_LORA), prow(LANE)],
        out_shape=[shp(HEADS * QK_NOPE, BF16), shp(HEADS * QK_ROPE, BF16), shp(HEADS * QK_NOPE, BF16),
                   shp(HEADS * V_DIM, BF16), shp(2 * LANE, BF16),
                   jax.ShapeDtypeStruct((NP, KV_LORA), F32), jax.ShapeDtypeStruct((NP, LANE), F32)],
        compiler_params=_cparams(("arbitrary",)),
        name="mla_proj",
    )(xp, xs, g0, mods, mods, w1, gq, gkv, wuq, wukv, cos_q, sin_q, cos_k, sin_k)


def _cachekv_kernel(c_ref, w_ref, kn_ref, v_ref):
    kv = _dot(c_ref[...].astype(BF16), w_ref[...])
    kn_ref[...] = kv[:, :HEADS * QK_NOPE].astype(BF16)
    v_ref[...] = kv[:, HEADS * QK_NOPE:].astype(BF16)


def _cache_kv(ckv_cache, wukv):
    rows = ckv_cache.shape[0]
    tm = 512
    return pl.pallas_call(
        _cachekv_kernel,
        grid=(rows // tm,),
        in_specs=[pl.BlockSpec((tm, KV_LORA), lambda i: (i, 0)), _full(wukv.shape)],
        out_specs=[pl.BlockSpec((tm, HEADS * QK_NOPE), lambda i: (i, 0)),
                   pl.BlockSpec((tm, HEADS * V_DIM), lambda i: (i, 0))],
        out_shape=[jax.ShapeDtypeStruct((rows, HEADS * QK_NOPE), BF16),
                   jax.ShapeDtypeStruct((rows, HEADS * V_DIM), BF16)],
        compiler_params=_cparams(("arbitrary",)),
        name="cache_kv",
    )(ckv_cache, wukv)


def _attn_kernel(*refs, has_cache):
    if has_cache:
        qn, qr, kn, kr, v, knc, krc, vc, o_ref, s_scr = refs
        streams = [(knc, krc, vc), (kn, kr, v)]
    else:
        qn, qr, kn, kr, v, o_ref, s_scr = refs
        streams = [(kn, kr, v)]
    chunks = [(k1, k2, vv, c * KEY_BLK) for k1, k2, vv in streams for c in range(k1.shape[0] // KEY_BLK)]
    tq = qn.shape[0]
    a = (QK_NOPE + QK_ROPE) ** -0.5 * math.log2(math.e)
    for hh in range(HEAD_GRP):
        cols = slice(hh * LANE, (hh + 1) * LANE)
        pair_cols = slice((hh // 2) * LANE, (hh // 2 + 1) * LANE)
        kr_cols = slice((hh % 2) * LANE, (hh % 2 + 1) * LANE)
        q = jnp.concatenate([qn[:, cols], qr[:, pair_cols]], axis=1)
        mx = jnp.full((tq, LANE), -jnp.inf, F32)
        for n, (k1, k2, _, r0) in enumerate(chunks):
            k = jnp.concatenate([k1[r0:r0 + KEY_BLK, cols], k2[r0:r0 + KEY_BLK, kr_cols]], axis=1)
            s = lax.dot_general(q, k, NT_DIMS, preferred_element_type=F32)
            s_scr[hh, :, n * KEY_BLK:(n + 1) * KEY_BLK] = s
            for c in range(KEY_BLK // LANE):
                mx = jnp.maximum(mx, s[:, c * LANE:(c + 1) * LANE])
        mb = jnp.max(mx, axis=-1, keepdims=True) * a
        den = jnp.zeros((tq, LANE), F32)
        o = jnp.zeros((tq, V_DIM), F32)
        for n, (_, _, vv, r0) in enumerate(chunks):
            p = jnp.exp2(s_scr[hh, :, n * KEY_BLK:(n + 1) * KEY_BLK] * a - mb)
            for c in range(KEY_BLK // LANE):
                den = den + p[:, c * LANE:(c + 1) * LANE]
            o = o + _dot(p.astype(BF16), vv[r0:r0 + KEY_BLK, cols])
        o_ref[:, cols] = (o / jnp.sum(den, axis=-1, keepdims=True)).astype(BF16)


def _attention(qn, qr, kn, kr2, v, *, row0, n_seq, seq, tq, cache=None):
    nq = seq // tq
    grp = HEAD_GRP * LANE
    qblk = lambda b, h, i: row0 // tq + b * nq + i
    kblk = lambda b: row0 // seq + b
    in_specs = [pl.BlockSpec((tq, grp), lambda b, h, i: (qblk(b, h, i), h)),
                pl.BlockSpec((tq, grp // 2), lambda b, h, i: (qblk(b, h, i), h)),
                pl.BlockSpec((seq, grp), lambda b, h, i: (kblk(b), h)),
                pl.BlockSpec((seq, 2 * LANE), lambda b, h, i: (kblk(b), 0)),
                pl.BlockSpec((seq, grp), lambda b, h, i: (kblk(b), h))]
    args = [qn, qr, kn, kr2, v]
    if cache is not None:
        knc, kr2c, vc = cache
        in_specs += [pl.BlockSpec((PAST_LEN, grp), lambda b, h, i: (b, h)),
                     pl.BlockSpec((PAST_LEN, 2 * LANE), lambda b, h, i: (b, 0)),
                     pl.BlockSpec((PAST_LEN, grp), lambda b, h, i: (b, h))]
        args += [knc, kr2c, vc]
    return pl.pallas_call(
        functools.partial(_attn_kernel, has_cache=cache is not None),
        grid=(n_seq, HEADS // HEAD_GRP, nq),
        in_specs=in_specs,
        out_specs=pl.BlockSpec((tq, grp), lambda b, h, i: (b * nq + i, h)),
        out_shape=jax.ShapeDtypeStruct((n_seq * seq, HEADS * V_DIM), BF16),
        scratch_shapes=[pltpu.VMEM((HEAD_GRP, tq, seq + (PAST_LEN if cache is not None else 0)), F32)],
        compiler_params=_cparams(("arbitrary", "arbitrary", "arbitrary")),
        name="attn_latent" if cache is not None else "attn_context",
    )(*args)


def _router_kernel(op_ref, os_ref, xp_ref, xs_ref, gate1, g1, sh2, sc2, g2, wout, wr_hi, wr_lo, tri,
                   x3_ref, h_ref, ri_ref, rf_ref, cnt_ref, carry, *, prompt_steps):
    step = pl.program_id(0)

    @pl.when(step == 0)
    def _():
        carry[...] = jnp.zeros_like(carry)

    o_all = _stream_rows(op_ref, os_ref, prompt_steps)
    x_all = _stream_rows(xp_ref, xs_ref, prompt_steps)
    sub = x_all.shape[0] // ROUTER_SUB
    dg = lambda a, b: lax.dot_general(a, b, NT_DIMS, preferred_element_type=F32)
    logits = []
    for r in range(ROUTER_SUB):
        rows = slice(r * sub, (r + 1) * sub)
        x3 = _gated(x_all[rows], _dot(o_all[rows], wout[...]), g1[...], gate1[0])
        x3_ref[rows, :] = x3
        h = _adaln(x3, g2[...], sc2[0], sh2[0])
        _store_row_tiles(h_ref, lambda cols, h=h: h[:, cols], r * sub, sub)
        h_hi = h.astype(BF16)
        h_lo = (h - h_hi.astype(F32)).astype(BF16)
        logits.append(dg(wr_hi[...], h_hi) + dg(wr_hi[...], h_lo) + dg(wr_lo[...], h_hi))
    lg = jnp.concatenate(logits, axis=1)
    eidx = lax.broadcasted_iota(I32, lg.shape, 0).astype(F32)
    m1 = jnp.max(lg, axis=0, keepdims=True)
    i1 = jnp.min(jnp.where(lg == m1, eidx, float(N_EXP)), axis=0, keepdims=True)
    sel1 = eidx == i1
    lg2 = jnp.where(sel1, -jnp.inf, lg)
    m2 = jnp.max(lg2, axis=0, keepdims=True)
    i2 = jnp.min(jnp.where(lg2 == m2, eidx, float(N_EXP)), axis=0, keepdims=True)
    sel2 = eidx == i2
    e = jnp.exp(m2 - m1)
    w1 = 1.0 / (1.0 + e)
    w2 = e / (1.0 + e)
    picked = jnp.where(sel1 | sel2, 1.0, 0.0)
    rank = _dot(picked.astype(BF16), tri[...]) + carry[:, 0:1]
    r1 = jnp.sum(jnp.where(sel1, rank, 0.0), axis=0, keepdims=True)
    r2 = jnp.sum(jnp.where(sel2, rank, 0.0), axis=0, keepdims=True)
    carry[...] = carry[...] + jnp.sum(picked, axis=1, keepdims=True)
    cnt_ref[...] = carry[...]
    ri_ref[...] = jnp.where(eidx == 0.0, i1, jnp.where(eidx == 1.0, i2, jnp.where(eidx == 2.0, r1, r2))).astype(I32)
    rf_ref[...] = jnp.where(eidx == 0.0, w1, w2)


def _attn_out_router(o_p, o_s, xp, xs, mods, g1, g2, wout, wr_hi, wr_lo):
    tm = 512
    grp = _grp_bm(tm)
    n_p = NP // tm
    tri = jnp.asarray(np.triu(np.ones((tm, tm), np.float32), 1), BF16)
    row = lambda w: pl.BlockSpec((tm, w), lambda i: (i, 0))
    col = pl.BlockSpec((N_EXP, tm), lambda i: (0, i))
    return pl.pallas_call(
        functools.partial(_router_kernel, prompt_steps=n_p),
        grid=(N // tm,),
        in_specs=_stream_specs(tm, HEADS * V_DIM) + _stream_specs(tm, D) + [_mod_spec(2, grp), _full((1, D)),
                  _mod_spec(3, grp), _mod_spec(4, grp), _full((1, D)),
                  _full((HEADS * V_DIM, D)), _full((N_EXP, D)), _full((N_EXP, D)), _full((tm, tm))],
        out_specs=[row(D), pl.BlockSpec((tm * ROW_TILE, LANE), lambda i: (i, 0)), col, col, _full((N_EXP, LANE))],
        out_shape=[jax.ShapeDtypeStruct((N, D), F32), jax.ShapeDtypeStruct((N * ROW_TILE, LANE), F32),
                   jax.ShapeDtypeStruct((N_EXP, N), I32), jax.ShapeDtypeStruct((N_EXP, N), F32),
                   jax.ShapeDtypeStruct((N_EXP, LANE), F32)],
        scratch_shapes=[pltpu.VMEM((N_EXP, LANE), F32)],
        compiler_params=_cparams(("arbitrary",)),
        name="attn_out_router",
    )(o_p, o_s, xp, xs, mods, g1, mods, mods, g2, wout, wr_hi, wr_lo, tri)


def _invert_kernel(pos_ref, init_hbm, out_hbm, dest, sem):
    step = pl.program_id(0)

    @pl.when(step == 0)
    def _():
        fill = pltpu.make_async_copy(init_hbm, dest, sem)
        fill.start()
        fill.wait()

    base = step * INV_BLK

    def body(q, carry):
        dest[pos_ref[0, 0, q]] = base + q
        return carry

    lax.fori_loop(0, INV_BLK, body, 0, unroll=8)

    @pl.when(step == pl.num_programs(0) - 1)
    def _():
        out = pltpu.make_async_copy(dest, out_hbm, sem)
        out.start()
        out.wait()


def _invert_positions(pos):
    n = pos.shape[0]
    return pl.pallas_call(
        _invert_kernel,
        grid=(n // INV_BLK,),
        in_specs=[pl.BlockSpec((1, 1, INV_BLK), lambda i: (i, 0, 0), memory_space=pltpu.SMEM),
                  pl.BlockSpec(memory_space=pl.ANY)],
        out_specs=pl.BlockSpec(memory_space=pl.ANY),
        out_shape=jax.ShapeDtypeStruct((P_ROWS,), I32),
        scratch_shapes=[pltpu.SMEM((P_ROWS,), I32), pltpu.SemaphoreType.DMA],
        compiler_params=_cparams(("arbitrary",)),
        name="invert_positions",
    )(pos.reshape(n // INV_BLK, 1, INV_BLK), jnp.full((P_ROWS,), -1, I32))


def _tile_rows(r):
    return r * ROW_TILE if isinstance(r, int) else pl.multiple_of(r * ROW_TILE, ROW_TILE)


def _moe_kernel(te, nu, src0_ref, src1_ref, dstp_ref, dstc_ref, h_hbm, wgu, wd, y_hbm,
                hsbuf, ybuf, hbf, acc, sem_g, sem_s):
    i = pl.program_id(0)
    last = pl.num_programs(0) - 1
    cur = i % 2
    nxt = 1 - cur
    buf_rows = TM_E * ROW_TILE

    def gather(idx_ref, r, slot):
        i_src = pl.multiple_of(idx_ref[0, 0, r] * ROW_TILE, ROW_TILE)
        return pltpu.make_async_copy(h_hbm.at[pl.ds(i_src, ROW_TILE), :],
                                     hsbuf.at[slot, pl.ds(_tile_rows(r), ROW_TILE), :], sem_g.at[slot])

    def scatter(idx_ref, r, slot):
        i_dst = pl.multiple_of(idx_ref[0, 0, r] * ROW_TILE, ROW_TILE)
        return pltpu.make_async_copy(ybuf.at[slot, pl.ds(_tile_rows(r), ROW_TILE), :],
                                     y_hbm.at[pl.ds(i_dst, ROW_TILE), :], sem_s.at[slot])

    def wait_gather(slot):
        pltpu.make_async_copy(h_hbm.at[pl.ds(0, buf_rows), :], hsbuf.at[slot], sem_g.at[slot]).wait()

    def wait_scatter(slot):
        pltpu.make_async_copy(ybuf.at[slot], y_hbm.at[pl.ds(0, buf_rows), :], sem_s.at[slot]).wait()

    def for_rows(fn):
        def body(g, carry):
            for k in range(ROW_TILE):
                fn(g * ROW_TILE + k, k % 2)
            return carry
        lax.fori_loop(0, TM_E // ROW_TILE, body, 0)

    def side_traffic(r, queue):
        gather(src1_ref, r, nxt).start(priority=queue)
        scatter(dstp_ref, r, nxt).start(priority=queue)

    @pl.when(i == 0)
    def _():
        ybuf[1] = jnp.zeros((buf_rows, LANE), F32)
        for_rows(lambda r, queue: gather(src0_ref, r, 0).start(priority=queue))

    wait_gather(cur)

    @pl.when(i < nu[0])
    def _():
        for c in range(ROW_TILE):
            hbf[:, c * LANE:(c + 1) * LANE] = hsbuf[cur, pl.ds(c, TM_E, stride=ROW_TILE), :].astype(BF16)
        w_gu = wgu.at[0]
        w_d = wd.at[0]
        _swiglu_block(hbf, w_gu, w_d, acc, 0)
        per_iter = TM_E // SIDE_ITERS

        def body(t, carry):
            _swiglu_block(hbf, w_gu, w_d, acc, 1 + 2 * t)
            _swiglu_block(hbf, w_gu, w_d, acc, 2 + 2 * t)
            for k in range(per_iter):
                side_traffic(t * per_iter + k, k % 2)
            return carry

        lax.fori_loop(0, SIDE_ITERS, body, 0)
        for j in range(1 + 2 * SIDE_ITERS, N_FF):
            _swiglu_block(hbf, w_gu, w_d, acc, j)
        _store_row_tiles(ybuf.at[cur], lambda cols: acc[:, cols])

    @pl.when(i >= nu[0])
    def _():
        for_rows(side_traffic)
        ybuf[cur] = jnp.zeros((buf_rows, LANE), F32)

    wait_scatter(nxt)

    @pl.when(i == last)
    def _():
        for_rows(lambda r, queue: scatter(dstc_ref, r, cur).start(priority=queue))
        wait_scatter(cur)
        wait_gather(nxt)


def _moe_experts(tile_expert, n_used, src_tbl, dst_tbl, h, wgu, wd):
    smem = lambda off: pl.BlockSpec((1, 1, TM_E), lambda i, te, nu: (i + off, 0, 0), memory_space=pltpu.SMEM)
    grid_spec = pltpu.PrefetchScalarGridSpec(
        num_scalar_prefetch=2,
        grid=(N_TILES,),
        in_specs=[smem(0), smem(1), smem(0), smem(1),
                  pl.BlockSpec(memory_space=pl.ANY),
                  pl.BlockSpec((1,) + wgu.shape[1:], lambda i, te, nu: (te[i], 0, 0)),
                  pl.BlockSpec((1,) + wd.shape[1:], lambda i, te, nu: (te[i], 0, 0))],
        out_specs=pl.BlockSpec(memory_space=pl.ANY),
        scratch_shapes=[pltpu.VMEM((2, TM_E * ROW_TILE, LANE), F32), pltpu.VMEM((2, TM_E * ROW_TILE, LANE), F32),
                        pltpu.VMEM((TM_E, D), BF16), pltpu.VMEM((TM_E, D), F32),
                        pltpu.SemaphoreType.DMA((2,)), pltpu.SemaphoreType.DMA((2,))],
    )
    return pl.pallas_call(
        _moe_kernel,
        grid_spec=grid_spec,
        out_shape=jax.ShapeDtypeStruct(((2 * N + DUMP_ROWS) * ROW_TILE, LANE), F32),
        compiler_params=_cparams(("arbitrary",)),
        name="moe_experts",
    )(tile_expert, n_used, src_tbl, src_tbl, dst_tbl, dst_tbl, h, wgu, wd)


def _combine_kernel(y1_ref, y2_ref, x_ref, w1_ref, w2_ref, gate2, g3, op_ref, os_ref, *, prompt_steps):
    rows = x_ref.shape[0]
    w1 = w1_ref[...]
    w2 = w2_ref[...]
    f = jnp.concatenate([w1 * _load_row_tiles(y1_ref, c, rows) + w2 * _load_row_tiles(y2_ref, c, rows)
                         for c in range(ROW_TILE)], axis=1)
    out = _gated(x_ref[...], f, g3[...], gate2[0])
    step = pl.program_id(0)

    @pl.when(step < prompt_steps)
    def _():
        op_ref[...] = out

    @pl.when(step >= prompt_steps)
    def _():
        os_ref[...] = out


def _moe_combine(y, x, w1, w2, mods, g3):
    tm = 512
    grp = _grp_bm(tm)
    nb = N // tm
    n_p = NP // tm
    return pl.pallas_call(
        functools.partial(_combine_kernel, prompt_steps=n_p),
        grid=(nb,),
        in_specs=[pl.BlockSpec((tm * ROW_TILE, LANE), lambda i: (i, 0)),
                  pl.BlockSpec((tm * ROW_TILE, LANE), lambda i: (nb + i, 0)),
                  pl.BlockSpec((tm, D), lambda i: (i, 0)),
                  pl.BlockSpec((tm, 1), lambda i: (i, 0)), pl.BlockSpec((tm, 1), lambda i: (i, 0)),
                  _mod_spec(5, grp), _full((1, D))],
        out_specs=[pl.BlockSpec((tm, D), lambda i: (jnp.minimum(i, n_p - 1), 0)),
                   pl.BlockSpec((tm, D), lambda i: (jnp.maximum(i - n_p, 0), 0))],
        out_shape=[jax.ShapeDtypeStruct((NP, D), F32), jax.ShapeDtypeStruct((NS, D), F32)],
        compiler_params=_cparams(("arbitrary",)),
        name="moe_combine",
    )(y, y, x, w1, w2, mods, g3)


def _block_diag(w):
    hh, a, b = w.shape
    eye = jnp.eye(hh, dtype=w.dtype)
    return jnp.einsum('hab,hk->hakb', w, eye).reshape(hh * a, hh * b)


def _s5_matrices(a_re, a_im, log_dt, b_re, b_im, c_re, c_im):
    dt = jnp.exp(log_dt)[:, None]
    mag = jnp.exp(a_re * dt)
    abr = mag * jnp.cos(a_im * dt)
    abi = mag * jnp.sin(a_im * dt)
    den = a_re * a_re + a_im * a_im
    cr = ((abr - 1.0) * a_re + abi * a_im) / den
    ci = (abi * a_re - (abr - 1.0) * a_im) / den
    bbr = cr[..., None] * b_re - ci[..., None] * b_im
    bbi = cr[..., None] * b_im + ci[..., None] * b_re
    hg = S5_G // 2
    eye = jnp.eye(hg, dtype=F32)
    bms, cms = [], []
    for j in range(2):
        sl = slice(j * hg, (j + 1) * hg)
        bd = lambda m: jnp.einsum('gpc,gh->gchp', m[sl], eye).reshape(hg * S5_CH, hg * S5_P)
        bms.append(jnp.concatenate([bd(bbr), bd(bbi)], axis=1))
        cd = lambda m: jnp.einsum('gcp,gh->gphc', m[sl], eye).reshape(hg * S5_P, hg * S5_CH)
        cms.append(jnp.concatenate([cd(c_re), cd(-c_im)], axis=0))
    return (jnp.stack(bms).astype(BF16), abr.reshape(1, S5_N), abi.reshape(1, S5_N),
            jnp.stack(cms).astype(BF16))


def _rope_tables(tm):
    rows = DEC_SEQ // GRID_W
    row = jnp.repeat(jnp.arange(rows, dtype=F32), GRID_W)
    col = jnp.tile(jnp.arange(GRID_W, dtype=F32), rows)
    nf = QK_ROPE // 4
    inv = ROPE_THETA ** (-jnp.arange(nf, dtype=F32) / nf)
    ang = jnp.concatenate([row[:, None] * inv, col[:, None] * inv], axis=-1)
    cos = jnp.repeat(jnp.cos(ang), 2, axis=-1)
    sin = jnp.stack([-jnp.sin(ang), jnp.sin(ang)], axis=-1).reshape(DEC_SEQ, QK_ROPE)
    ident = lambda t, one: jnp.concatenate([t, jnp.full((tm, t.shape[1]), one, F32)], axis=0)
    cos_q = ident(jnp.tile(cos, (1, HEADS)), 1.0)
    sin_q = ident(jnp.tile(sin, (1, HEADS)), 0.0)
    pad = lambda t, one: jnp.concatenate([t, jnp.full((DEC_SEQ, LANE - QK_ROPE), one, F32)], axis=1)
    return cos_q, sin_q, ident(pad(cos, 1.0), 1.0), ident(pad(sin, 0.0), 0.0)


def _group_states(prompt_state, sample_state):
    w = sample_state.shape[-1]
    return jnp.concatenate([prompt_state.reshape(2, SUB, w), sample_state.reshape(1, SUB, w)], axis=0)


def _layer_ab(xp, xs, m, ng, j, state_lru, state_s5_re, state_s5_im, p):
    xz = _ab_inproj(xp, xs, ng[0:1], m, p['ab_w_in'][j].astype(BF16))
    zeros = lambda w: jnp.zeros((BATCH, w), F32)
    per_dir = []
    for d in range(2):
        wg = jnp.concatenate([_block_diag(p['lru_wa'][j, d]), _block_diag(p['lru_wx'][j, d])], axis=1).astype(BF16)
        bg = jnp.concatenate([p['lru_ba'][j, d], p['lru_bx'][j, d]])[None]
        bm, ar, ai, cm = _s5_matrices(p['s5_a_re'][j, d], p['s5_a_im'][j, d], p['s5_log_dt'][j, d],
                                      p['s5_b_re'][j, d], p['s5_b_im'][j, d], p['s5_c_re'][j, d], p['s5_c_im'][j, d])
        h0l = _group_states(zeros(LRU_W), state_lru[:, j, d])
        h0r = _group_states(zeros(S5_N), state_s5_re[:, j, d].reshape(DEC_BATCH, S5_N))
        h0i = _group_states(zeros(S5_N), state_s5_im[:, j, d].reshape(DEC_BATCH, S5_N))
        per_dir.append((h0l, h0r, h0i, p['ab_conv_w'][j], p['ab_conv_b'][j][None], wg, bg,
                        p['lru_lambda'][j, d][None], bm, ar, ai, cm))
    (haf, yf, llf, lrf, lif), (hab, yr, llb, lrb, lib) = _ab_scan(xz, per_dir)
    x = _ab_out(haf, hab, yf, yr, xz, xp, xs, m, ng[1:2], p['s5_d'][j][None], p['s5_w_glu'][j].astype(BF16),
                p['s5_b_glu'][j][None], p['ab_w_out'][j].astype(BF16))
    streams = _ffn(x, m, ng[2:3], ng[3:4], p['ffn_w_gate_up'][j].astype(BF16), p['ffn_w_down'][j].astype(BF16))
    prompt = lambda f, b, w: jnp.stack([f[:2].reshape(BATCH, w), b[:2].reshape(BATCH, w)], axis=1)
    lru = prompt(llf, llb, LRU_W)
    s5r = prompt(lrf, lrb, S5_N).reshape(BATCH, 2, S5_G, S5_P)
    s5i = prompt(lif, lib, S5_N).reshape(BATCH, 2, S5_G, S5_P)
    return tuple(streams), lru, s5r, s5i


def _head_major(w, parts):
    k = w.shape[0]
    per_head = w.reshape(k, HEADS, -1)
    out, start = [], 0
    for width in parts:
        out.append(per_head[:, :, start:start + width].reshape(k, HEADS * width))
        start += width
    return jnp.concatenate(out, axis=1)


def _layer_mla_moe(xp, xs, m, ng, j, cache_kv_latent, cache_k_rope, p):
    w1 = jnp.concatenate([p['mla_w_in'][j], jnp.zeros((D, LANE - QK_ROPE), F32)], axis=1).astype(BF16)
    wuq = _head_major(p['mla_w_uq'][j], (QK_NOPE, QK_ROPE)).astype(BF16)
    wukv = _head_major(p['mla_w_ukv'][j], (QK_NOPE, V_DIM)).astype(BF16)
    tables = _rope_tables(MLA_TM)
    qn, qr, kn, v, kr2, ckv, krr = _mla_proj(xp, xs, ng[0:1], m, w1, p['mla_g_q'][j][None], p['mla_g_kv'][j][None],
                                              wuq, wukv, *tables)
    knc, vc = _cache_kv(cache_kv_latent[:, j].reshape(DEC_BATCH * PAST_LEN, KV_LORA), wukv)
    krc = cache_k_rope[:, j].reshape(DEC_BATCH * PAST_LEN, QK_ROPE)
    z = jnp.zeros_like(krc)
    kr2c = jnp.concatenate([krc, z, z, krc], axis=1).astype(BF16)
    o_p = _attention(qn, qr, kn, kr2, v, row0=0, n_seq=BATCH, seq=SEQ, tq=SEQ)
    o_s = _attention(qn, qr, kn, kr2, v, row0=NP, n_seq=DEC_BATCH, seq=DEC_SEQ, tq=1024, cache=(knc, kr2c, vc))
    wr_t = p['moe_w_router'][j].T
    wr_hi = wr_t.astype(BF16)
    wr_lo = (wr_t - wr_hi.astype(F32)).astype(BF16)
    x3, h, ri, rf, cnt = _attn_out_router(o_p, o_s, xp, xs, m, ng[1:2], ng[2:3], p['mla_w_out'][j].astype(BF16), wr_hi, wr_lo)
    counts = cnt[:, 0].astype(I32)
    padded = ((counts + TM_E - 1) // TM_E) * TM_E
    ends = jnp.cumsum(padded)
    offs = ends - padded
    pos1 = offs[ri[0]] + ri[2]
    pos2 = offs[ri[1]] + ri[3]
    dest = _invert_positions(jnp.concatenate([pos1, pos2]))
    is_pad = dest < 0
    pad_row = 2 * N + TM_E + jnp.cumsum(is_pad.astype(I32)) - 1
    src_tbl = jnp.concatenate([jnp.where(is_pad, 0, dest % N), jnp.zeros((TM_E,), I32)])
    dst_tbl = jnp.concatenate([2 * N + jnp.arange(TM_E, dtype=I32), jnp.where(is_pad, pad_row, dest)])
    n_used = (ends[-1] // TM_E).astype(I32)[None]
    tile_row = jnp.minimum(jnp.arange(N_TILES, dtype=I32), n_used - 1) * TM_E
    tile_expert = jnp.sum((tile_row[:, None] >= ends[None, :]).astype(I32), axis=1)
    y = _moe_experts(tile_expert, n_used, src_tbl.reshape(N_TILES + 1, 1, TM_E), dst_tbl.reshape(N_TILES + 1, 1, TM_E),
                     h, p['moe_w_gate_up'][j].astype(BF16), p['moe_w_down'][j].astype(BF16))
    xp, xs = _moe_combine(y, x3, rf[0][:, None], rf[1][:, None], m, ng[3:4])
    kv_new = ckv.reshape(BATCH, SEQ, KV_LORA)
    kr_new = krr[:, :QK_ROPE].reshape(BATCH, SEQ, QK_ROPE)
    return (xp.reshape(BATCH, SEQ, D), xs.reshape(DEC_BATCH, DEC_SEQ, D)), kv_new, kr_new


def kernel(x_prompt, x_sample, c, state_lru, state_s5_re, state_s5_im, cache_kv_latent, cache_k_rope, c_ctx, w_mod, b_mod, norm_gains, ab_w_in, ab_conv_w, ab_conv_b, lru_wa, lru_ba, lru_wx, lru_bx, lru_lambda, s5_a_re, s5_a_im, s5_log_dt, s5_b_re, s5_b_im, s5_c_re, s5_c_im, s5_d, s5_w_glu, s5_b_glu, ab_w_out, ffn_w_gate_up, ffn_w_down, mla_w_in, mla_g_q, mla_g_kv, mla_w_uq, mla_w_ukv, mla_w_out, moe_w_router, moe_w_gate_up, moe_w_down):
    p = dict(ab_w_in=ab_w_in, ab_conv_w=ab_conv_w, ab_conv_b=ab_conv_b, lru_wa=lru_wa, lru_ba=lru_ba,
             lru_wx=lru_wx, lru_bx=lru_bx, lru_lambda=lru_lambda, s5_a_re=s5_a_re, s5_a_im=s5_a_im,
             s5_log_dt=s5_log_dt, s5_b_re=s5_b_re, s5_b_im=s5_b_im, s5_c_re=s5_c_re, s5_c_im=s5_c_im,
             s5_d=s5_d, s5_w_glu=s5_w_glu, s5_b_glu=s5_b_glu, ab_w_out=ab_w_out, ffn_w_gate_up=ffn_w_gate_up,
             ffn_w_down=ffn_w_down, mla_w_in=mla_w_in, mla_g_q=mla_g_q, mla_g_kv=mla_g_kv, mla_w_uq=mla_w_uq,
             mla_w_ukv=mla_w_ukv, mla_w_out=mla_w_out, moe_w_router=moe_w_router, moe_w_gate_up=moe_w_gate_up,
             moe_w_down=moe_w_down)
    depth = w_mod.shape[0]
    cond = jnp.concatenate([c_ctx[None], c, jnp.zeros((2 * SUB - 1 - DEC_BATCH, D), F32)], axis=0)
    mod = _modulation(cond, w_mod, b_mod)
    ctx_tile = lambda l: jnp.broadcast_to(mod[l, 0:1], (SUB, 6 * D))
    streams = (x_prompt, x_sample)
    lru_l, s5r_l, s5i_l, kv_l, kr_l = [], [], [], [], []
    for layer in range(depth):
        j = layer // 2
        ng = norm_gains[layer]
        if layer % 2 == 0:
            m = jnp.stack([ctx_tile(layer), mod[layer, 1:1 + DEC_BATCH]])
            streams, lru, s5r, s5i = _layer_ab(*streams, m, ng, j, state_lru, state_s5_re, state_s5_im, p)
            lru_l.append(lru)
            s5r_l.append(s5r)
            s5i_l.append(s5i)
        else:
            lat = jnp.broadcast_to(mod[layer, 1:1 + DEC_BATCH, None, :], (DEC_BATCH, SUB, 6 * D))
            m = jnp.concatenate([ctx_tile(layer)[None], lat], axis=0)
            streams, kv_new, kr_new = _layer_mla_moe(streams[0].reshape(NP, D), streams[1].reshape(NS, D), m, ng, j,
                                                     cache_kv_latent, cache_k_rope, p)
            kv_l.append(kv_new)
            kr_l.append(kr_new)
    return (streams[0], streams[1],
            jnp.stack(lru_l, axis=1), jnp.stack(s5r_l, axis=1), jnp.stack(s5i_l, axis=1),
            jnp.stack(kv_l, axis=1), jnp.stack(kr_l, axis=1))
```
